```python
import jax
import jax.numpy as jnp
from jax import lax
import numpy as np

D_MODEL = 1024
BATCH = 32
SEQ = 2048
DEPTH = 2

CTX_LEN = 256
GRID_W = 64
HEAD_DIM = 64
NA_HEADS = 4
NA_ROWS = 8
NA_COLS = 16
GQA_HEADS = 8
GQA_KV_HEADS = 2
MLA_HEADS = 4
MLA_Q_RANK = 256
MLA_KV_RANK = 128
MLA_NOPE = 64
MLA_ROPE = 32
MLA_V = 64
N_BRANCH = 3
N_MOD = 9
D_FF = ((8 * D_MODEL // 3 + 127) // 128) * 128
Q_BLOCK = 128
ROPE_THETA = 10000.0
EPS = 1e-6
NEG_BIG = -1e30

NA_WIDTH = NA_HEADS * HEAD_DIM
GQA_Q_WIDTH = GQA_HEADS * HEAD_DIM
GQA_KV_WIDTH = GQA_KV_HEADS * HEAD_DIM
MLA_OUT_WIDTH = MLA_HEADS * MLA_V
IN_SIZES = (NA_WIDTH, NA_WIDTH, NA_WIDTH, GQA_Q_WIDTH, GQA_KV_WIDTH, GQA_KV_WIDTH,
            MLA_Q_RANK, MLA_KV_RANK, MLA_ROPE, N_BRANCH * D_MODEL)
IN_SPLITS = tuple(int(s) for s in np.cumsum(IN_SIZES)[:-1])
IN_WIDTH = sum(IN_SIZES)

kernel_name = "hybrid_na_gqa_mla_macaron_dit"


def rmsnorm(x, g):
    xf = x.astype(jnp.float32)
    y = xf * lax.rsqrt(jnp.mean(xf * xf, axis=-1, keepdims=True) + EPS)
    return (y * g.astype(jnp.float32)).astype(x.dtype)


def modulate(h, shift, scale):
    return h * (1.0 + scale) + shift


def swiglu(h, w_gate, w_up, w_down):
    return (jax.nn.silu(h @ w_gate) * (h @ w_up)) @ w_down


def half_ffn(h, shift, scale, gate, g_norm, w_gate, w_up, w_down):
    n = modulate(rmsnorm(h, g_norm), shift, scale)
    return h + 0.5 * gate * swiglu(n, w_gate, w_up, w_down)


def axial_rope(n_tok, d_rot):
    half = d_rot // 2
    freqs = ROPE_THETA ** (-jnp.arange(0, half, 2, dtype=jnp.float32) / half)
    t = jnp.arange(n_tok)
    row = (t // GRID_W).astype(jnp.float32)[:, None] * freqs
    col = (t % GRID_W).astype(jnp.float32)[:, None] * freqs
    ang = jnp.concatenate([row, row, col, col], axis=-1)
    return jnp.cos(ang), jnp.sin(ang)


def apply_rope(x, cos, sin):
    xr = x.reshape(x.shape[:-1] + (2, 2, -1))
    rot = jnp.concatenate([-xr[..., 1:, :], xr[..., :1, :]], axis=-2).reshape(x.shape)
    return x * cos.astype(x.dtype) + rot * sin.astype(x.dtype)


def project_heads(n, w_in, q_norm, k_norm, mq_norm, mkv_norm, w_uq, w_ukv, rope):
    bn, t_len, _ = n.shape
    a_q, a_k, a_v, b_q, b_k, b_v, c_q, c_kv, c_kr, gates = jnp.split(n @ w_in, IN_SPLITS, axis=-1)

    def heads(t, h):
        return t.reshape(bn, t_len, h, -1).transpose(0, 2, 1, 3)

    attn_a = (heads(a_q, NA_HEADS), heads(a_k, NA_HEADS), heads(a_v, NA_HEADS))
    b_q = rmsnorm(heads(b_q, GQA_HEADS), q_norm)
    b_k = rmsnorm(heads(b_k, GQA_KV_HEADS), k_norm)
    q_lat = heads(rmsnorm(c_q, mq_norm) @ w_uq, MLA_HEADS)
    kv_lat = heads(rmsnorm(c_kv, mkv_norm) @ w_ukv, MLA_HEADS)
    q_nope, q_rope = q_lat[..., :MLA_NOPE], q_lat[..., MLA_NOPE:]
    k_nope, m_v = kv_lat[..., :MLA_NOPE], kv_lat[..., MLA_NOPE:]
    k_rope = c_kr[:, None]
    if rope is not None:
        cos_b, sin_b, cos_m, sin_m = rope
        b_q = apply_rope(b_q, cos_b, sin_b)
        b_k = apply_rope(b_k, cos_b, sin_b)
        q_rope = apply_rope(q_rope, cos_m, sin_m)
        k_rope = apply_rope(k_rope, cos_m, sin_m)
    m_q = jnp.concatenate([q_nope, q_rope], axis=-1)
    m_k = jnp.concatenate([k_nope, jnp.broadcast_to(k_rope, k_nope.shape[:-1] + (MLA_ROPE,))], axis=-1)
    return attn_a, (b_q, b_k, heads(b_v, GQA_KV_HEADS)), (m_q, m_k, m_v), gates


def block_attention(q, k, v, scale):
    bn, hq, t_len, dk = q.shape
    hkv = k.shape[1]
    grp = hq // hkv
    nblk = t_len // Q_BLOCK
    qb = q.reshape(bn, hkv, grp, nblk, Q_BLOCK, dk).transpose(3, 0, 1, 2, 4, 5)

    def one_block(qi):
        s = jnp.einsum('bhgqd,bhkd->bhgqk', qi, k).astype(jnp.float32) * scale
        p = jax.nn.softmax(s, axis=-1).astype(v.dtype)
        return jnp.einsum('bhgqk,bhkd->bhgqd', p, v)

    o = lax.map(one_block, qb)
    return o.transpose(1, 2, 3, 0, 4, 5).reshape(bn, hq, t_len, v.shape[-1])


def neighbourhood_attention(q, k, v, k_ctx, v_ctx, rel_bias):
    bn, h, s_len, d = q.shape
    rows = s_len // GRID_W
    wr = min(NA_ROWS, rows)
    scale = d ** -0.5

    def grid(t):
        return t.reshape(bn, h, rows, GRID_W, d)

    kg, vg = grid(k), grid(v)
    q_rows = jnp.moveaxis(grid(q), 2, 0)
    cols = jnp.arange(GRID_W)
    c0 = jnp.clip(cols - NA_COLS // 2, 0, GRID_W - NA_COLS)
    col_in = (cols[None, :] >= c0[:, None]) & (cols[None, :] < c0[:, None] + NA_COLS)
    col_idx = jnp.clip(cols[None, :] - cols[:, None] + NA_COLS - 1, 0, 2 * NA_COLS - 2)
    mask = jnp.broadcast_to(col_in[:, None, :], (GRID_W, wr, GRID_W)).reshape(GRID_W, wr * GRID_W)
    n_loc = wr * GRID_W

    def row_block(args):
        qr, r = args
        r0 = jnp.clip(r - wr // 2, 0, rows - wr)
        kr = lax.dynamic_slice_in_dim(kg, r0, wr, axis=2).reshape(bn, h, n_loc, d)
        vr = lax.dynamic_slice_in_dim(vg, r0, wr, axis=2).reshape(bn, h, n_loc, d)
        dr_idx = r0 + jnp.arange(wr) - r + (NA_ROWS - 1)
        bias = rel_bias[:, dr_idx[None, :, None], col_idx[:, None, :]].reshape(h, GRID_W, n_loc)
        s_loc = jnp.einsum('bhqd,bhkd->bhqk', qr, kr).astype(jnp.float32) * scale + bias.astype(jnp.float32)
        s_loc = jnp.where(mask, s_loc, NEG_BIG)
        s_ctx = jnp.einsum('bhqd,bhkd->bhqk', qr, k_ctx).astype(jnp.float32) * scale
        p = jax.nn.softmax(jnp.concatenate([s_loc, s_ctx], axis=-1), axis=-1).astype(v.dtype)
        return (jnp.einsum('bhqk,bhkd->bhqd', p[..., :n_loc], vr)
                + jnp.einsum('bhqk,bhkd->bhqd', p[..., n_loc:], v_ctx))

    o = lax.map(row_block, (q_rows, jnp.arange(rows)))
    return jnp.moveaxis(o, 0, 2).reshape(bn, h, s_len, d)


def merge_branches(o_a, o_b, o_m, gate_logits, w_a, w_b, w_m, w_o):
    def flat(o):
        return o.transpose(0, 2, 1, 3).reshape(o.shape[0], o.shape[2], -1)

    g_a, g_b, g_m = jnp.split(jax.nn.sigmoid(gate_logits), N_BRANCH, axis=-1)
    y = g_a * (flat(o_a) @ w_a) + g_b * (flat(o_b) @ w_b) + g_m * (flat(o_m) @ w_m)
    return y @ w_o


def setup_inputs(seed: int = 0) -> dict:
    key = jax.random.key(seed)
    ks = iter(jax.random.split(key, 32))
    d = D_MODEL

    def nrm(shape, s):
        return jax.random.normal(next(ks), shape, jnp.float32) * s

    def lin(shape):
        return nrm(shape, shape[-2] ** -0.5)

    def gain(shape):
        return 1.0 + nrm(shape, 0.1)

    return {
        "x": nrm((BATCH, SEQ, d), 1.0),
        "c": nrm((BATCH, d), 1.0),
        "ctx": nrm((BATCH, CTX_LEN, d), 1.0),
        "c_ctx": nrm((d,), 1.0),
        "w_ada": nrm((DEPTH, d, N_MOD * d), 0.5 * d ** -0.5),
        "b_ada": nrm((DEPTH, N_MOD * d), 0.02),
        "ffn1_norm": gain((DEPTH, d)),
        "ffn1_w_gate": lin((DEPTH, d, D_FF)),
        "ffn1_w_up": lin((DEPTH, d, D_FF)),
        "ffn1_w_down": lin((DEPTH, D_FF, d)),
        "mix_norm": gain((DEPTH, d)),
        "w_in": lin((DEPTH, d, IN_WIDTH)),
        "na_rel_bias": nrm((DEPTH, NA_HEADS, 2 * NA_ROWS - 1, 2 * NA_COLS - 1), 0.2),
        "gqa_q_norm": gain((DEPTH, HEAD_DIM)),
        "gqa_k_norm": gain((DEPTH, HEAD_DIM)),
        "mla_q_norm": gain((DEPTH, MLA_Q_RANK)),
        "mla_kv_norm": gain((DEPTH, MLA_KV_RANK)),
        "mla_w_uq": lin((DEPTH, MLA_Q_RANK, MLA_HEADS * (MLA_NOPE + MLA_ROPE))),
        "mla_w_ukv": lin((DEPTH, MLA_KV_RANK, MLA_HEADS * (MLA_NOPE + MLA_V))),
        "w_branch_a": lin((DEPTH, NA_WIDTH, d)),
        "w_branch_b": lin((DEPTH, GQA_Q_WIDTH, d)),
        "w_branch_c": lin((DEPTH, MLA_OUT_WIDTH, d)),
        "w_out": lin((DEPTH, d, d)),
        "ffn2_norm": gain((DEPTH, d)),
        "ffn2_w_gate": lin((DEPTH, d, D_FF)),
        "ffn2_w_up": lin((DEPTH, d, D_FF)),
        "ffn2_w_down": lin((DEPTH, D_FF, d)),
        "final_norm": gain((d,)),
    }


def reference(x, c, ctx, c_ctx, w_ada, b_ada, ffn1_norm, ffn1_w_gate, ffn1_w_up, ffn1_w_down,
              mix_norm, w_in, na_rel_bias, gqa_q_norm, gqa_k_norm, mla_q_norm, mla_kv_norm,
              mla_w_uq, mla_w_ukv, w_branch_a, w_branch_b, w_branch_c, w_out,
              ffn2_norm, ffn2_w_gate, ffn2_w_up, ffn2_w_down, final_norm):
    n_tok = x.shape[1]
    cos_b, sin_b = axial_rope(n_tok, HEAD_DIM)
    cos_m, sin_m = axial_rope(n_tok, MLA_ROPE)
    rope = (cos_b, sin_b, cos_m, sin_m)
    na_scale = HEAD_DIM ** -0.5
    gqa_scale = HEAD_DIM ** -0.5
    mla_scale = (MLA_NOPE + MLA_ROPE) ** -0.5

    for i in range(DEPTH):
        mod_x = [m[:, None, :] for m in jnp.split(jax.nn.silu(c) @ w_ada[i] + b_ada[i], N_MOD, axis=-1)]
        mod_c = jnp.split(jax.nn.silu(c_ctx) @ w_ada[i] + b_ada[i], N_MOD, axis=-1)
        ffn1 = (ffn1_norm[i], ffn1_w_gate[i], ffn1_w_up[i], ffn1_w_down[i])
        ffn2 = (ffn2_norm[i], ffn2_w_gate[i], ffn2_w_up[i], ffn2_w_down[i])
        proj = (w_in[i], gqa_q_norm[i], gqa_k_norm[i], mla_q_norm[i], mla_kv_norm[i], mla_w_uq[i], mla_w_ukv[i])
        merge = (w_branch_a[i], w_branch_b[i], w_branch_c[i], w_out[i])

        x = half_ffn(x, mod_x[0], mod_x[1], mod_x[2], *ffn1)
        ctx = half_ffn(ctx, mod_c[0], mod_c[1], mod_c[2], *ffn1)

        nx = modulate(rmsnorm(x, mix_norm[i]), mod_x[3], mod_x[4])
        nc = modulate(rmsnorm(ctx, mix_norm[i]), mod_c[3], mod_c[4])
        (aq, ak, av), (bq, bk, bv), (mq, mk, mv), gates = project_heads(nx, *proj, rope)
        (caq, cak, cav), (cbq, cbk, cbv), (cmq, cmk, cmv), cgates = project_heads(nc, *proj, None)

        o_a = neighbourhood_attention(aq, ak, av, cak, cav, na_rel_bias[i])
        o_b = block_attention(bq, jnp.concatenate([bk, cbk], axis=2), jnp.concatenate([bv, cbv], axis=2), gqa_scale)
        o_m = block_attention(mq, jnp.concatenate([mk, cmk], axis=2), jnp.concatenate([mv, cmv], axis=2), mla_scale)
        x = x + mod_x[5] * merge_branches(o_a, o_b, o_m, gates, *merge)
        x = half_ffn(x, mod_x[6], mod_x[7], mod_x[8], *ffn2)

        if i < DEPTH - 1:
            oc_a = block_attention(caq, cak, cav, na_scale)
            oc_b = block_attention(cbq, cbk, cbv, gqa_scale)
            oc_m = block_attention(cmq, cmk, cmv, mla_scale)
            ctx = ctx + mod_c[5] * merge_branches(oc_a, oc_b, oc_m, cgates, *merge)
            ctx = half_ffn(ctx, mod_c[6], mod_c[7], mod_c[8], *ffn2)

    return rmsnorm(x, final_norm)
```

```python
import functools

import numpy as np
import jax
import jax.numpy as jnp
from jax import lax
from jax.experimental import pallas as pl
from jax.experimental.pallas import tpu as pltpu

D_MODEL = 1024
BATCH = 32
SEQ = 2048
DEPTH = 2
CTX_LEN = 256
GRID_W = 64
HEAD_DIM = 64
NA_HEADS = 4
NA_ROWS = 8
NA_COLS = 16
GQA_HEADS = 8
GQA_KV_HEADS = 2
MLA_HEADS = 4
MLA_Q_RANK = 256
MLA_KV_RANK = 128
MLA_NOPE = 64
MLA_ROPE = 32
MLA_V = 64
N_MOD = 9
D_FF = ((8 * D_MODEL // 3 + 127) // 128) * 128
ROPE_THETA = 10000.0
EPS = 1e-6
NEG_BIG = -1e30

NA_WIDTH = NA_HEADS * HEAD_DIM
GQA_Q_WIDTH = GQA_HEADS * HEAD_DIM
GQA_KV_WIDTH = GQA_KV_HEADS * HEAD_DIM
GATE_WIDTH = 3 * D_MODEL
ROWS = SEQ // GRID_W

LANES = 128
TOK = SEQ + CTX_LEN
TM = 256
NT_ALL = TOK // TM
NT_LAT = SEQ // TM
MOD_ROWS = 40
CTX_ROW = BATCH
ADA_BLOCK = 1152
MLA_PAD = LANES
MLA_QK_WIDTH = MLA_HEADS * MLA_PAD
QKV_WIDTH = 2048
NA_QROWS = TM // GRID_W
NA_KROWS = 12
NA_KBLK = NA_KROWS * GRID_W // TM
NA_NKEY = NA_KROWS * GRID_W + CTX_LEN
VMEM_LIMIT = 56 * 1024 * 1024

F32 = jnp.float32
BF16 = jnp.bfloat16

_OFF_AQ, _OFF_AK, _OFF_AV = 0, 256, 512
_OFF_BQ, _OFF_BK, _OFF_BV = 768, 1280, 1408
_OFF_CQ, _OFF_CKV, _OFF_CKR = 1536, 1792, 1920


def _params(n_axes):
    return pltpu.CompilerParams(dimension_semantics=("arbitrary",) * n_axes,
                                vmem_limit_bytes=VMEM_LIMIT)


def _const_spec(shape):
    nd = len(shape)
    return pl.BlockSpec(shape, lambda *_: (0,) * nd, pipeline_mode=pl.Buffered(1))


def _mod_spec(sub):
    return pl.BlockSpec((None, None, 3, D_MODEL),
                        lambda t, b: (jnp.where(t >= NT_LAT, CTX_ROW, b), sub, 0, 0))


def _tok_spec(width):
    return pl.BlockSpec((None, TM, width), lambda t, b: (b, t, 0))


def _sigmoid(x):
    return 1.0 / (1.0 + jnp.exp(-x))


def _rmsnorm(x, g):
    return (x * lax.rsqrt(jnp.mean(x * x, axis=-1, keepdims=True) + EPS)) * g


def _bdot(a, b):
    return jnp.dot(a, b, preferred_element_type=F32)


def _ada_body(c_ref, w_ref, b_ref, o_ref):
    c = c_ref[...]
    s = c * _sigmoid(c)
    o_ref[...] = _bdot(s.astype(BF16), w_ref[...].astype(BF16)) + b_ref[...]


def _ada(c_all, w_ada, b_ada):
    nblk = (N_MOD * D_MODEL) // ADA_BLOCK
    return pl.pallas_call(
        _ada_body,
        grid=(DEPTH, nblk),
        in_specs=[
            pl.BlockSpec((MOD_ROWS, D_MODEL), lambda i, j: (0, 0)),
            pl.BlockSpec((None, D_MODEL, ADA_BLOCK), lambda i, j: (i, 0, j)),
            pl.BlockSpec((None, 1, ADA_BLOCK), lambda i, j: (i, 0, j)),
        ],
        out_specs=pl.BlockSpec((None, MOD_ROWS, ADA_BLOCK), lambda i, j: (i, 0, j)),
        out_shape=jax.ShapeDtypeStruct((DEPTH, MOD_ROWS, N_MOD * D_MODEL), F32),
        compiler_params=_params(2),
        name="ada_mod",
    )(c_all, w_ada, b_ada.reshape(DEPTH, 1, N_MOD * D_MODEL))


def _ffn_body(h_ref, mod_ref, g_ref, wg_ref, wu_ref, wd_ref, fin_ref, o_ref, *, final):
    h = h_ref[...]
    mod = mod_ref[...]
    n = _rmsnorm(h, g_ref[...]) * (1.0 + mod[1:2]) + mod[0:1]
    nb = n.astype(BF16)
    g = _bdot(nb, wg_ref[...])
    u = _bdot(nb, wu_ref[...])
    a = (g * _sigmoid(g)) * u
    d = _bdot(a.astype(BF16), wd_ref[...])
    out = h + 0.5 * mod[2:3] * d
    if final:
        out = _rmsnorm(out, fin_ref[...])
    o_ref[...] = out


def _ffn(h, mod, sub, g_norm, wg, wu, wd, fin, *, n_tiles, final):
    return pl.pallas_call(
        functools.partial(_ffn_body, final=final),
        grid=(n_tiles, BATCH),
        in_specs=[
            _tok_spec(D_MODEL),
            _mod_spec(sub),
            _const_spec((1, D_MODEL)),
            _const_spec((D_MODEL, D_FF)),
            _const_spec((D_MODEL, D_FF)),
            _const_spec((D_FF, D_MODEL)),
            _const_spec((1, D_MODEL)),
        ],
        out_specs=_tok_spec(D_MODEL),
        out_shape=jax.ShapeDtypeStruct((BATCH, n_tiles * TM, D_MODEL), F32),
        compiler_params=_params(2),
        name="half_ffn",
    )(h, mod, g_norm, wg, wu, wd, fin)


def _rope(x, cos, sin_a, sin_b, half):
    w = x.shape[-1]
    return x * cos + pltpu.roll(x, w - half, 1) * sin_a + pltpu.roll(x, half, 1) * sin_b


def _seg_mean_sq(x, seg):
    x2 = x * x
    hi = x2.astype(BF16)
    lo = (x2 - hi.astype(F32)).astype(BF16)
    return _bdot(hi, seg) + _bdot(lo, seg)


def _proj_body(h_ref, mod_ref, g_ref, w_ref, segq_ref, segk_ref, qn_ref, kn_ref, mqn_ref, mkvn_ref,
               wuq_ref, wukv_ref, place_ref,
               cb_ref, sab_ref, sbb_ref, cm_ref, sam_ref, sbm_ref, ck_ref, sak_ref, sbk_ref,
               qa_ref, ka_ref, va_ref, qb_ref, kb_ref, vb_ref, qm_ref, km_ref, vm_ref):
    mod = mod_ref[...]
    n = _rmsnorm(h_ref[...], g_ref[...]) * (1.0 + mod[1:2]) + mod[0:1]
    z = _bdot(n.astype(BF16), w_ref[...])

    scale = HEAD_DIM ** -0.5
    qa_ref[...] = (z[:, _OFF_AQ:_OFF_AQ + NA_WIDTH] * scale).astype(BF16)
    ka_ref[...] = z[:, _OFF_AK:_OFF_AK + NA_WIDTH].astype(BF16)
    va_ref[...] = z[:, _OFF_AV:_OFF_AV + NA_WIDTH].astype(BF16)

    cb, sab, sbb = cb_ref[...], sab_ref[...], sbb_ref[...]
    bq = z[:, _OFF_BQ:_OFF_BQ + GQA_Q_WIDTH]
    bq = (bq * lax.rsqrt(_seg_mean_sq(bq, segq_ref[...]) + EPS)) * qn_ref[...]
    qb_ref[...] = (_rope(bq, cb, sab, sbb, HEAD_DIM // 4) * scale).astype(BF16)
    bk = z[:, _OFF_BK:_OFF_BK + GQA_KV_WIDTH]
    bk = (bk * lax.rsqrt(_seg_mean_sq(bk, segk_ref[...]) + EPS)) * kn_ref[...]
    kb_ref[...] = _rope(bk, cb[:, :GQA_KV_WIDTH], sab[:, :GQA_KV_WIDTH], sbb[:, :GQA_KV_WIDTH],
                        HEAD_DIM // 4).astype(BF16)
    vb_ref[...] = z[:, _OFF_BV:_OFF_BV + GQA_KV_WIDTH].astype(BF16)

    mla_scale = (MLA_NOPE + MLA_ROPE) ** -0.5
    cq = _rmsnorm(z[:, _OFF_CQ:_OFF_CQ + MLA_Q_RANK], mqn_ref[...])
    q_lat = _bdot(cq.astype(BF16), wuq_ref[...])
    qm_ref[...] = (_rope(q_lat, cm_ref[...], sam_ref[...], sbm_ref[...], MLA_ROPE // 4)
                   * mla_scale).astype(BF16)
    ckv = _rmsnorm(z[:, _OFF_CKV:_OFF_CKV + MLA_KV_RANK], mkvn_ref[...])
    kv_lat = _bdot(ckv.astype(BF16), wukv_ref[...])
    vm_ref[...] = kv_lat[:, MLA_QK_WIDTH:].astype(BF16)
    kr = _rope(z[:, _OFF_CKR:_OFF_CKR + LANES], ck_ref[...], sak_ref[...], sbk_ref[...],
               MLA_ROPE // 4).astype(BF16)
    km_ref[...] = (kv_lat[:, :MLA_QK_WIDTH] + _bdot(kr, place_ref[...])).astype(BF16)


def _proj(h, mod, g_norm, w_qkv, consts, tables):
    widths = (NA_WIDTH, NA_WIDTH, NA_WIDTH, GQA_Q_WIDTH, GQA_KV_WIDTH, GQA_KV_WIDTH,
              MLA_QK_WIDTH, MLA_QK_WIDTH, MLA_HEADS * MLA_V)
    tab_specs = [pl.BlockSpec((TM, t.shape[-1]), lambda t_, b: (t_, 0)) for t in tables]
    return pl.pallas_call(
        _proj_body,
        grid=(NT_ALL, BATCH),
        in_specs=[_tok_spec(D_MODEL), _mod_spec(1), _const_spec((1, D_MODEL)),
                  _const_spec((D_MODEL, QKV_WIDTH))]
                 + [_const_spec(c.shape) for c in consts] + tab_specs,
        out_specs=[_tok_spec(w) for w in widths],
        out_shape=[jax.ShapeDtypeStruct((BATCH, TOK, w), BF16) for w in widths],
        compiler_params=_params(2),
        name="mix_proj",
    )(h, mod, g_norm, w_qkv, *consts, *tables)


def _attend(q, k, v, *, hkv, grp, dqk, qstride, kstride, dv, bias=None):
    tq = q.shape[0]
    outs = []
    for g in range(hkv):
        kg = k[:, g * kstride:g * kstride + dqk]
        vg = v[:, g * dv:(g + 1) * dv]
        qs = [q[:, (g * grp + j) * qstride:(g * grp + j) * qstride + dqk] for j in range(grp)]
        qg = qs[0] if grp == 1 else jnp.concatenate(qs, axis=0)
        s = lax.dot_general(qg, kg, (((1,), (1,)), ((), ())), preferred_element_type=F32)
        if bias is not None:
            s = s + bias[g]
        m = jnp.max(s, axis=-1, keepdims=True)
        p = jnp.exp(s - m)
        l = jnp.sum(p, axis=-1, keepdims=True)
        o = _bdot(p.astype(BF16), vg) * (1.0 / l)
        outs.extend(o[j * tq:(j + 1) * tq] for j in range(grp))
    return outs


def _block_attn_body(q_ref, k_ref, v_ref, o_ref, *, cfg, with_ctx):
    def run(nk_lo, nk_hi):
        outs = _attend(q_ref[...], k_ref[nk_lo:nk_hi, :], v_ref[nk_lo:nk_hi, :], **cfg)
        o_ref[...] = jnp.concatenate(outs, axis=1).astype(BF16)

    if not with_ctx:
        run(0, TOK)
    else:
        t = pl.program_id(1)

        @pl.when(t < NT_LAT)
        def _():
            run(0, TOK)

        @pl.when(t >= NT_LAT)
        def _():
            run(SEQ, TOK)


def _block_attn(q, k, v, *, cfg, with_ctx, name):
    n_tiles = NT_ALL if with_ctx else NT_LAT
    wo = cfg["hkv"] * cfg["grp"] * cfg["dv"]
    return pl.pallas_call(
        functools.partial(_block_attn_body, cfg=cfg, with_ctx=with_ctx),
        grid=(BATCH, n_tiles),
        in_specs=[
            pl.BlockSpec((None, TM, q.shape[-1]), lambda b, t: (b, t, 0)),
            pl.BlockSpec((None, TOK, k.shape[-1]), lambda b, t: (b, 0, 0)),
            pl.BlockSpec((None, TOK, v.shape[-1]), lambda b, t: (b, 0, 0)),
        ],
        out_specs=pl.BlockSpec((None, TM, wo), lambda b, t: (b, t, 0)),
        out_shape=jax.ShapeDtypeStruct((BATCH, n_tiles * TM, wo), BF16),
        compiler_params=_params(2),
        name=name,
    )(q, k, v)


_GQA_CFG = dict(hkv=GQA_KV_HEADS, grp=GQA_HEADS // GQA_KV_HEADS, dqk=HEAD_DIM,
                qstride=HEAD_DIM, kstride=HEAD_DIM, dv=HEAD_DIM)
_MLA_CFG = dict(hkv=MLA_HEADS, grp=1, dqk=MLA_PAD, qstride=MLA_PAD, kstride=MLA_PAD, dv=MLA_V)
_NA_CFG = dict(hkv=NA_HEADS, grp=1, dqk=HEAD_DIM, qstride=HEAD_DIM, kstride=HEAD_DIM, dv=HEAD_DIM)


def _na_start_block(t):
    return jnp.clip(t - 1, 0, NT_LAT - NA_KBLK)


def _na_body(q_ref, k0_ref, k1_ref, k2_ref, kc_ref, v0_ref, v1_ref, v2_ref, vc_ref, bias_ref, o_ref,
             *, with_ctx):
    def latent():
        k = jnp.concatenate([k0_ref[...], k1_ref[...], k2_ref[...], kc_ref[...]], axis=0)
        v = jnp.concatenate([v0_ref[...], v1_ref[...], v2_ref[...], vc_ref[...]], axis=0)
        outs = _attend(q_ref[...], k, v, bias=bias_ref, **_NA_CFG)
        o_ref[...] = jnp.concatenate(outs, axis=1).astype(BF16)

    def context():
        outs = _attend(q_ref[...], kc_ref[...], vc_ref[...], **_NA_CFG)
        o_ref[...] = jnp.concatenate(outs, axis=1).astype(BF16)

    if not with_ctx:
        latent()
    else:
        t = pl.program_id(0)
        pl.when(t < NT_LAT)(latent)
        pl.when(t >= NT_LAT)(context)


def _na_attn(q, k, v, bias, *, with_ctx):
    n_tiles = NT_ALL if with_ctx else NT_LAT

    def kv_spec(j):
        return pl.BlockSpec((None, TM, NA_WIDTH), lambda t, b: (b, _na_start_block(t) + j, 0))

    ctx_spec = pl.BlockSpec((None, TM, NA_WIDTH), lambda t, b: (b, NT_LAT, 0))
    bias_spec = pl.BlockSpec(
        (NA_HEADS, None, TM, NA_NKEY),
        lambda t, b: (0, jnp.where(t == 0, 0, jnp.where(t == NT_LAT - 1, 2, 1)), 0, 0))
    return pl.pallas_call(
        functools.partial(_na_body, with_ctx=with_ctx),
        grid=(n_tiles, BATCH),
        in_specs=[_tok_spec(NA_WIDTH), kv_spec(0), kv_spec(1), kv_spec(2), ctx_spec,
                  kv_spec(0), kv_spec(1), kv_spec(2), ctx_spec, bias_spec],
        out_specs=_tok_spec(NA_WIDTH),
        out_shape=jax.ShapeDtypeStruct((BATCH, n_tiles * TM, NA_WIDTH), BF16),
        compiler_params=_params(2),
        name="na_attn",
    )(q, k, k, k, k, v, v, v, v, bias)


def _na_bias_table(rel_bias):
    cols = np.arange(GRID_W)
    c0 = np.clip(cols - NA_COLS // 2, 0, GRID_W - NA_COLS)
    col_in = (cols[None, :] >= c0[:, None]) & (cols[None, :] < c0[:, None] + NA_COLS)
    dc = np.clip(cols[None, :] - cols[:, None] + NA_COLS - 1, 0, 2 * NA_COLS - 2)
    valid = np.zeros((3, NA_QROWS, GRID_W, NA_KROWS, GRID_W), bool)
    dr_idx = np.zeros((3, NA_QROWS, GRID_W, NA_KROWS, GRID_W), np.int32)
    for cls, tile in enumerate((0, 2, NT_LAT - 1)):
        start = int(np.clip(tile - 1, 0, NT_LAT - NA_KBLK)) * NA_QROWS
        for qr in range(NA_QROWS):
            r = tile * NA_QROWS + qr
            r0 = int(np.clip(r - NA_ROWS // 2, 0, ROWS - NA_ROWS))
            for kj in range(NA_KROWS):
                kr = start + kj
                if r0 <= kr < r0 + NA_ROWS:
                    valid[cls, qr, :, kj, :] = col_in
                    dr_idx[cls, qr, :, kj, :] = kr - r + NA_ROWS - 1
    dc_idx = np.broadcast_to(dc[None, None, :, None, :], valid.shape)
    shape = (3, TM, NA_KROWS * GRID_W)
    vals = rel_bias[:, dr_idx.reshape(shape), dc_idx.reshape(shape)]
    local = jnp.where(valid.reshape(shape)[None], vals, NEG_BIG).astype(F32)
    return jnp.concatenate([local, jnp.zeros((NA_HEADS, 3, TM, CTX_LEN), F32)], axis=-1)


def _merge_body(h_ref, mod_ref, g_ref, wgate_ref, oa_ref, ob_ref, om_ref, wa_ref, wb_ref, wm_ref, wo_ref,
                o_ref):
    h = h_ref[...]
    mod = mod_ref[...]
    n = _rmsnorm(h, g_ref[...]) * (1.0 + mod[1:2]) + mod[0:1]
    gates = _sigmoid(_bdot(n.astype(BF16), wgate_ref[...]))
    y = (gates[:, :D_MODEL] * _bdot(oa_ref[...], wa_ref[...])
         + gates[:, D_MODEL:2 * D_MODEL] * _bdot(ob_ref[...], wb_ref[...])
         + gates[:, 2 * D_MODEL:] * _bdot(om_ref[...], wm_ref[...]))
    o_ref[...] = h + mod[2:3] * _bdot(y.astype(BF16), wo_ref[...])


def _merge(h, mod, g_norm, w_gate, oa, ob, om, wa, wb, wm, wo, *, n_tiles):
    return pl.pallas_call(
        _merge_body,
        grid=(n_tiles, BATCH),
        in_specs=[_tok_spec(D_MODEL), _mod_spec(1), _const_spec((1, D_MODEL)),
                  _const_spec((D_MODEL, GATE_WIDTH)),
                  _tok_spec(NA_WIDTH), _tok_spec(GQA_Q_WIDTH), _tok_spec(MLA_HEADS * MLA_V),
                  _const_spec((NA_WIDTH, D_MODEL)), _const_spec((GQA_Q_WIDTH, D_MODEL)),
                  _const_spec((MLA_HEADS * MLA_V, D_MODEL)), _const_spec((D_MODEL, D_MODEL))],
        out_specs=_tok_spec(D_MODEL),
        out_shape=jax.ShapeDtypeStruct((BATCH, n_tiles * TM, D_MODEL), F32),
        compiler_params=_params(2),
        name="gated_merge",
    )(h, mod, g_norm, w_gate, oa, ob, om, wa, wb, wm, wo)


def _rope_tables():
    t = np.arange(SEQ)
    row = (t // GRID_W).astype(np.float32)[:, None]
    col = (t % GRID_W).astype(np.float32)[:, None]

    def head_tables(d_rot):
        half = d_rot // 2
        freqs = jnp.asarray(ROPE_THETA, F32) ** (-jnp.arange(0, half, 2, dtype=F32) / half)
        r, c = jnp.asarray(row) * freqs, jnp.asarray(col) * freqs
        ang = jnp.concatenate([r, r, c, c], axis=-1)
        first = (np.arange(d_rot) % half) < (half // 2)
        cos, sin = jnp.cos(ang), jnp.sin(ang)
        return cos, jnp.where(first, -sin, 0.0), jnp.where(first, 0.0, sin)

    def place(tabs, reps, lo, group):
        out = []
        for k, tab in enumerate(tabs):
            fill = 1.0 if k == 0 else 0.0
            d_rot = tab.shape[-1]
            g = jnp.concatenate([jnp.full((SEQ, lo), fill, F32), tab,
                                 jnp.full((SEQ, group - lo - d_rot), fill, F32)], axis=-1)
            g = jnp.tile(g, (1, reps))
            ctx = jnp.full((CTX_LEN, reps * group), fill, F32)
            out.append(jnp.concatenate([g, ctx], axis=0))
        return out

    tb = head_tables(HEAD_DIM)
    tm = head_tables(MLA_ROPE)
    return (place(tb, GQA_HEADS, 0, HEAD_DIM)
            + place(tm, MLA_HEADS, MLA_NOPE, MLA_PAD)
            + place(tm, 1, 0, LANES))


def _seg_matrix(width):
    idx = np.arange(width) // HEAD_DIM
    return jnp.asarray((idx[:, None] == idx[None, :]).astype(np.float32) / HEAD_DIM, BF16)


def _place_matrix():
    m = np.zeros((LANES, MLA_QK_WIDTH), np.float32)
    for hd in range(MLA_HEADS):
        for j in range(MLA_ROPE):
            m[j, hd * MLA_PAD + MLA_NOPE + j] = 1.0
    return jnp.asarray(m, BF16)


def _pad_heads(w, n_heads, per_head, keep_lo, keep_hi):
    k = w.shape[0]
    wh = w.reshape(k, n_heads, per_head)[:, :, keep_lo:keep_hi]
    wh = jnp.pad(wh, ((0, 0), (0, 0), (0, MLA_PAD - (keep_hi - keep_lo))))
    return wh.reshape(k, n_heads * MLA_PAD)


def kernel(x, c, ctx, c_ctx, w_ada, b_ada, ffn1_norm, ffn1_w_gate, ffn1_w_up, ffn1_w_down, mix_norm, w_in,
           na_rel_bias, gqa_q_norm, gqa_k_norm, mla_q_norm, mla_kv_norm, mla_w_uq, mla_w_ukv,
           w_branch_a, w_branch_b, w_branch_c, w_out, ffn2_norm, ffn2_w_gate, ffn2_w_up, ffn2_w_down,
           final_norm):
    assert x.shape == (BATCH, SEQ, D_MODEL) and ctx.shape == (BATCH, CTX_LEN, D_MODEL)

    c_all = jnp.concatenate([c, c_ctx[None, :], jnp.zeros((MOD_ROWS - BATCH - 1, D_MODEL), F32)], axis=0)
    mod_all = _ada(c_all, w_ada, b_ada).reshape(DEPTH, MOD_ROWS, 3, 3, D_MODEL)

    tables = _rope_tables()
    seg_q, seg_k, place = _seg_matrix(GQA_Q_WIDTH), _seg_matrix(GQA_KV_WIDTH), _place_matrix()
    fin = final_norm.reshape(1, D_MODEL)

    h = jnp.concatenate([x, ctx], axis=1)
    for i in range(DEPTH):
        last = i == DEPTH - 1
        mod = mod_all[i]
        w_qkv = jnp.pad(w_in[i][:, :QKV_WIDTH - 96], ((0, 0), (0, 96))).astype(BF16)
        w_gate = w_in[i][:, QKV_WIDTH - 96:].astype(BF16)
        w_uq = _pad_heads(mla_w_uq[i], MLA_HEADS, MLA_NOPE + MLA_ROPE, 0, MLA_NOPE + MLA_ROPE).astype(BF16)
        w_uk = _pad_heads(mla_w_ukv[i], MLA_HEADS, MLA_NOPE + MLA_V, 0, MLA_NOPE)
        w_uv = mla_w_ukv[i].reshape(MLA_KV_RANK, MLA_HEADS, MLA_NOPE + MLA_V)[:, :, MLA_NOPE:]
        w_ukv = jnp.concatenate([w_uk, w_uv.reshape(MLA_KV_RANK, MLA_HEADS * MLA_V)], axis=1).astype(BF16)
        consts = (seg_q, seg_k,
                  jnp.tile(gqa_q_norm[i], GQA_HEADS)[None, :], jnp.tile(gqa_k_norm[i], GQA_KV_HEADS)[None, :],
                  mla_q_norm[i][None, :], mla_kv_norm[i][None, :], w_uq, w_ukv, place)

        h = _ffn(h, mod, 0, ffn1_norm[i][None, :], ffn1_w_gate[i].astype(BF16), ffn1_w_up[i].astype(BF16),
                 ffn1_w_down[i].astype(BF16), fin, n_tiles=NT_ALL, final=False)
        qa, ka, va, qb, kb, vb, qm, km, vm = _proj(h, mod, mix_norm[i][None, :], w_qkv, consts, tables)

        with_ctx = not last
        oa = _na_attn(qa, ka, va, _na_bias_table(na_rel_bias[i]), with_ctx=with_ctx)
        ob = _block_attn(qb, kb, vb, cfg=_GQA_CFG, with_ctx=with_ctx, name="gqa_attn")
        om = _block_attn(qm, km, vm, cfg=_MLA_CFG, with_ctx=with_ctx, name="mla_attn")

        n_tiles = NT_LAT if last else NT_ALL
        h = _merge(h, mod, mix_norm[i][None, :], w_gate, oa, ob, om, w_branch_a[i].astype(BF16),
                   w_branch_b[i].astype(BF16), w_branch_c[i].astype(BF16), w_out[i].astype(BF16),
                   n_tiles=n_tiles)
        h = _ffn(h, mod, 2, ffn2_norm[i][None, :], ffn2_w_gate[i].astype(BF16), ffn2_w_up[i].astype(BF16),
                 ffn2_w_down[i].astype(BF16), fin, n_tiles=n_tiles, final=last)
    return h
```

```python
import functools

import numpy as np
import jax
import jax.numpy as jnp
from jax import lax
from jax.experimental import pallas as pl
from jax.experimental.pallas import tpu as pltpu

D_MODEL = 1024
BATCH = 32
SEQ = 2048
DEPTH = 2
CTX_LEN = 256
GRID_W = 64
HEAD_DIM = 64
NA_HEADS = 4
NA_ROWS = 8
NA_COLS = 16
GQA_HEADS = 8
GQA_KV_HEADS = 2
MLA_HEADS = 4
MLA_Q_RANK = 256
MLA_KV_RANK = 128
MLA_NOPE = 64
MLA_ROPE = 32
MLA_V = 64
N_MOD = 9
D_FF = ((8 * D_MODEL // 3 + 127) // 128) * 128
ROPE_THETA = 10000.0
EPS = 1e-6
NEG_BIG = -1e30

NA_WIDTH = NA_HEADS * HEAD_DIM
GQA_Q_WIDTH = GQA_HEADS * HEAD_DIM
GQA_KV_WIDTH = GQA_KV_HEADS * HEAD_DIM
GATE_WIDTH = 3 * D_MODEL
ROWS = SEQ // GRID_W

LANES = 128
TOK = SEQ + CTX_LEN
TM = 256
NT_ALL = TOK // TM
NT_LAT = SEQ // TM
MOD_ROWS = 40
CTX_ROW = BATCH
ADA_BLOCK = 1152
MLA_PAD = LANES
MLA_QK_WIDTH = MLA_HEADS * MLA_PAD
QKV_WIDTH = 2048
NA_QROWS = TM // GRID_W
NA_KROWS = 12
NA_KBLK = NA_KROWS * GRID_W // TM
NA_NKEY = NA_KROWS * GRID_W + CTX_LEN
VMEM_LIMIT = 56 * 1024 * 1024

F32 = jnp.float32
BF16 = jnp.bfloat16

_OFF_AQ, _OFF_AK, _OFF_AV = 0, 256, 512
_OFF_BQ, _OFF_BK, _OFF_BV = 768, 1280, 1408
_OFF_CQ, _OFF_CKV, _OFF_CKR = 1536, 1792, 1920


def _params(n_axes):
    return pltpu.CompilerParams(dimension_semantics=("arbitrary",) * n_axes,
                                vmem_limit_bytes=VMEM_LIMIT)


def _const_spec(shape):
    nd = len(shape)
    return pl.BlockSpec(shape, lambda *_: (0,) * nd, pipeline_mode=pl.Buffered(1))


def _mod_spec(sub):
    return pl.BlockSpec((None, None, 3, D_MODEL),
                        lambda t, b: (jnp.where(t >= NT_LAT, CTX_ROW, b), sub, 0, 0))


def _tok_spec(width):
    return pl.BlockSpec((None, TM, width), lambda t, b: (b, t, 0))


def _sigmoid(x):
    return 1.0 / (1.0 + jnp.exp(-x))


def _rmsnorm(x, g):
    return (x * lax.rsqrt(jnp.mean(x * x, axis=-1, keepdims=True) + EPS)) * g


def _bdot(a, b):
    return jnp.dot(a, b, preferred_element_type=F32)


def _ada_body(c_ref, w_ref, b_ref, o_ref):
    c = c_ref[...]
    s = c * _sigmoid(c)
    o_ref[...] = _bdot(s.astype(BF16), w_ref[...].astype(BF16)) + b_ref[...]


def _ada(c_all, w_ada, b_ada):
    nblk = (N_MOD * D_MODEL) // ADA_BLOCK
    return pl.pallas_call(
        _ada_body,
        grid=(DEPTH, nblk),
        in_specs=[
            pl.BlockSpec((MOD_ROWS, D_MODEL), lambda i, j: (0, 0)),
            pl.BlockSpec((None, D_MODEL, ADA_BLOCK), lambda i, j: (i, 0, j)),
            pl.BlockSpec((None, 1, ADA_BLOCK), lambda i, j: (i, 0, j)),
        ],
        out_specs=pl.BlockSpec((None, MOD_ROWS, ADA_BLOCK), lambda i, j: (i, 0, j)),
        out_shape=jax.ShapeDtypeStruct((DEPTH, MOD_ROWS, N_MOD * D_MODEL), F32),
        compiler_params=_params(2),
        name="ada_mod",
    )(c_all, w_ada, b_ada.reshape(DEPTH, 1, N_MOD * D_MODEL))


def _ffn_body(h_ref, mod_ref, g_ref, wg_ref, wu_ref, wd_ref, fin_ref, o_ref, *, final):
    h = h_ref[...]
    mod = mod_ref[...]
    n = _rmsnorm(h, g_ref[...]) * (1.0 + mod[1:2]) + mod[0:1]
    nb = n.astype(BF16)
    g = _bdot(nb, wg_ref[...])
    u = _bdot(nb, wu_ref[...])
    a = (g * _sigmoid(g)) * u
    d = _bdot(a.astype(BF16), wd_ref[...])
    out = h + 0.5 * mod[2:3] * d
    if final:
        out = _rmsnorm(out, fin_ref[...])
    o_ref[...] = out


def _ffn(h, mod, sub, g_norm, wg, wu, wd, fin, *, n_tiles, final):
    return pl.pallas_call(
        functools.partial(_ffn_body, final=final),
        grid=(n_tiles, BATCH),
        in_specs=[
            _tok_spec(D_MODEL),
            _mod_spec(sub),
            _const_spec((1, D_MODEL)),
            _const_spec((D_MODEL, D_FF)),
            _const_spec((D_MODEL, D_FF)),
            _const_spec((D_FF, D_MODEL)),
            _const_spec((1, D_MODEL)),
        ],
        out_specs=_tok_spec(D_MODEL),
        out_shape=jax.ShapeDtypeStruct((BATCH, n_tiles * TM, D_MODEL), F32),
        compiler_params=_params(2),
        name="half_ffn",
    )(h, mod, g_norm, wg, wu, wd, fin)


def _rope(x, cos, sin_a, sin_b, half):
    w = x.shape[-1]
    return x * cos + pltpu.roll(x, w - half, 1) * sin_a + pltpu.roll(x, half, 1) * sin_b


def _seg_mean_sq(x, seg):
    x2 = x * x
    hi = x2.astype(BF16)
    lo = (x2 - hi.astype(F32)).astype(BF16)
    return _bdot(hi, seg) + _bdot(lo, seg)


def _proj_body(h_ref, mod_ref, g_ref, w_ref, segq_ref, segk_ref, qn_ref, kn_ref, mqn_ref, mkvn_ref,
               wuq_ref, wukv_ref, place_ref,
               cb_ref, sab_ref, sbb_ref, cm_ref, sam_ref, sbm_ref, ck_ref, sak_ref, sbk_ref,
               qa_ref, ka_ref, va_ref, qb_ref, kb_ref, vb_ref, qm_ref, km_ref, vm_ref):
    mod = mod_ref[...]
    n = _rmsnorm(h_ref[...], g_ref[...]) * (1.0 + mod[1:2]) + mod[0:1]
    z = _bdot(n.astype(BF16), w_ref[...])

    scale = HEAD_DIM ** -0.5
    qa_ref[...] = (z[:, _OFF_AQ:_OFF_AQ + NA_WIDTH] * scale).astype(BF16)
    ka_ref[...] = z[:, _OFF_AK:_OFF_AK + NA_WIDTH].astype(BF16)
    va_ref[...] = z[:, _OFF_AV:_OFF_AV + NA_WIDTH].astype(BF16)

    cb, sab, sbb = cb_ref[...], sab_ref[...], sbb_ref[...]
    bq = z[:, _OFF_BQ:_OFF_BQ + GQA_Q_WIDTH]
    bq = (bq * lax.rsqrt(_seg_mean_sq(bq, segq_ref[...]) + EPS)) * qn_ref[...]
    qb_ref[...] = (_rope(bq, cb, sab, sbb, HEAD_DIM // 4) * scale).astype(BF16)
    bk = z[:, _OFF_BK:_OFF_BK + GQA_KV_WIDTH]
    bk = (bk * lax.rsqrt(_seg_mean_sq(bk, segk_ref[...]) + EPS)) * kn_ref[...]
    kb_ref[...] = _rope(bk, cb[:, :GQA_KV_WIDTH], sab[:, :GQA_KV_WIDTH], sbb[:, :GQA_KV_WIDTH],
                        HEAD_DIM // 4).astype(BF16)
    vb_ref[...] = z[:, _OFF_BV:_OFF_BV + GQA_KV_WIDTH].astype(BF16)

    mla_scale = (MLA_NOPE + MLA_ROPE) ** -0.5
    cq = _rmsnorm(z[:, _OFF_CQ:_OFF_CQ + MLA_Q_RANK], mqn_ref[...])
    q_lat = _bdot(cq.astype(BF16), wuq_ref[...])
    qm_ref[...] = (_rope(q_lat, cm_ref[...], sam_ref[...], sbm_ref[...], MLA_ROPE // 4)
                   * mla_scale).astype(BF16)
    ckv = _rmsnorm(z[:, _OFF_CKV:_OFF_CKV + MLA_KV_RANK], mkvn_ref[...])
    kv_lat = _bdot(ckv.astype(BF16), wukv_ref[...])
    vm_ref[...] = kv_lat[:, MLA_QK_WIDTH:].astype(BF16)
    kr = _rope(z[:, _OFF_CKR:_OFF_CKR + LANES], ck_ref[...], sak_ref[...], sbk_ref[...],
               MLA_ROPE // 4).astype(BF16)
    km_ref[...] = (kv_lat[:, :MLA_QK_WIDTH] + _bdot(kr, place_ref[...])).astype(BF16)


def _proj(h, mod, g_norm, w_qkv, consts, tables):
    widths = (NA_WIDTH, NA_WIDTH, NA_WIDTH, GQA_Q_WIDTH, GQA_KV_WIDTH, GQA_KV_WIDTH,
              MLA_QK_WIDTH, MLA_QK_WIDTH, MLA_HEADS * MLA_V)
    tab_specs = [pl.BlockSpec((TM, t.shape[-1]), lambda t_, b: (t_, 0)) for t in tables]
    return pl.pallas_call(
        _proj_body,
        grid=(NT_ALL, BATCH),
        in_specs=[_tok_spec(D_MODEL), _mod_spec(1), _const_spec((1, D_MODEL)),
                  _const_spec((D_MODEL, QKV_WIDTH))]
                 + [_const_spec(c.shape) for c in consts] + tab_specs,
        out_specs=[_tok_spec(w) for w in widths],
        out_shape=[jax.ShapeDtypeStruct((BATCH, TOK, w), BF16) for w in widths],
        compiler_params=_params(2),
        name="mix_proj",
    )(h, mod, g_norm, w_qkv, *consts, *tables)


def _attend(q, k, v, *, hkv, grp, dqk, qstride, kstride, dv, bias=None):
    tq = q.shape[0]
    outs = []
    for g in range(hkv):
        kg = k[:, g * kstride:g * kstride + dqk]
        vg = v[:, g * dv:(g + 1) * dv]
        qs = [q[:, (g * grp + j) * qstride:(g * grp + j) * qstride + dqk] for j in range(grp)]
        qg = qs[0] if grp == 1 else jnp.concatenate(qs, axis=0)
        s = lax.dot_general(qg, kg, (((1,), (1,)), ((), ())), preferred_element_type=F32)
        if bias is not None:
            s = s + bias[g]
        m = jnp.max(s, axis=-1, keepdims=True)
        p = jnp.exp(s - m)
        l = jnp.sum(p, axis=-1, keepdims=True)
        o = _bdot(p.astype(BF16), vg) * (1.0 / l)
        outs.extend(o[j * tq:(j + 1) * tq] for j in range(grp))
    return outs


def _block_attn_body(q_ref, k_ref, v_ref, o_ref, *, cfg, with_ctx):
    def run(nk_lo, nk_hi):
        outs = _attend(q_ref[...], k_ref[nk_lo:nk_hi, :], v_ref[nk_lo:nk_hi, :], **cfg)
        o_ref[...] = jnp.concatenate(outs, axis=1).astype(BF16)

    if not with_ctx:
        run(0, TOK)
    else:
        t = pl.program_id(1)

        @pl.when(t < NT_LAT)
        def _():
            run(0, TOK)

        @pl.when(t >= NT_LAT)
        def _():
            run(SEQ, TOK)


def _block_attn(q, k, v, *, cfg, with_ctx, name):
    n_tiles = NT_ALL if with_ctx else NT_LAT
    wo = cfg["hkv"] * cfg["grp"] * cfg["dv"]
    return pl.pallas_call(
        functools.partial(_block_attn_body, cfg=cfg, with_ctx=with_ctx),
        grid=(BATCH, n_tiles),
        in_specs=[
            pl.BlockSpec((None, TM, q.shape[-1]), lambda b, t: (b, t, 0)),
            pl.BlockSpec((None, TOK, k.shape[-1]), lambda b, t: (b, 0, 0)),
            pl.BlockSpec((None, TOK, v.shape[-1]), lambda b, t: (b, 0, 0)),
        ],
        out_specs=pl.BlockSpec((None, TM, wo), lambda b, t: (b, t, 0)),
        out_shape=jax.ShapeDtypeStruct((BATCH, n_tiles * TM, wo), BF16),
        compiler_params=_params(2),
        name=name,
    )(q, k, v)


_GQA_CFG = dict(hkv=GQA_KV_HEADS, grp=GQA_HEADS // GQA_KV_HEADS, dqk=HEAD_DIM,
                qstride=HEAD_DIM, kstride=HEAD_DIM, dv=HEAD_DIM)
_MLA_CFG = dict(hkv=MLA_HEADS, grp=1, dqk=MLA_PAD, qstride=MLA_PAD, kstride=MLA_PAD, dv=MLA_V)
_NA_CFG = dict(hkv=NA_HEADS, grp=1, dqk=HEAD_DIM, qstride=HEAD_DIM, kstride=HEAD_DIM, dv=HEAD_DIM)


def _na_start_block(t):
    return jnp.clip(t - 1, 0, NT_LAT - NA_KBLK)


def _na_body(q_ref, k0_ref, k1_ref, k2_ref, kc_ref, v0_ref, v1_ref, v2_ref, vc_ref, bias_ref, o_ref,
             *, with_ctx):
    def latent():
        k = jnp.concatenate([k0_ref[...], k1_ref[...], k2_ref[...], kc_ref[...]], axis=0)
        v = jnp.concatenate([v0_ref[...], v1_ref[...], v2_ref[...], vc_ref[...]], axis=0)
        outs = _attend(q_ref[...], k, v, bias=bias_ref, **_NA_CFG)
        o_ref[...] = jnp.concatenate(outs, axis=1).astype(BF16)

    def context():
        outs = _attend(q_ref[...], kc_ref[...], vc_ref[...], **_NA_CFG)
        o_ref[...] = jnp.concatenate(outs, axis=1).astype(BF16)

    if not with_ctx:
        latent()
    else:
        t = pl.program_id(0)
        pl.when(t < NT_LAT)(latent)
        pl.when(t >= NT_LAT)(context)


def _na_attn(q, k, v, bias, *, with_ctx):
    n_tiles = NT_ALL if with_ctx else NT_LAT

    def kv_spec(j):
        return pl.BlockSpec((None, TM, NA_WIDTH), lambda t, b: (b, _na_start_block(t) + j, 0))

    ctx_spec = pl.BlockSpec((None, TM, NA_WIDTH), lambda t, b: (b, NT_LAT, 0))
    bias_spec = pl.BlockSpec(
        (NA_HEADS, None, TM, NA_NKEY),
        lambda t, b: (0, jnp.where(t == 0, 0, jnp.where(t == NT_LAT - 1, 2, 1)), 0, 0))
    return pl.pallas_call(
        functools.partial(_na_body, with_ctx=with_ctx),
        grid=(n_tiles, BATCH),
        in_specs=[_tok_spec(NA_WIDTH), kv_spec(0), kv_spec(1), kv_spec(2), ctx_spec,
                  kv_spec(0), kv_spec(1), kv_spec(2), ctx_spec, bias_spec],
        out_specs=_tok_spec(NA_WIDTH),
        out_shape=jax.ShapeDtypeStruct((BATCH, n_tiles * TM, NA_WIDTH), BF16),
        compiler_params=_params(2),
        name="na_attn",
    )(q, k, k, k, k, v, v, v, v, bias)


def _na_bias_table(rel_bias):
    cols = np.arange(GRID_W)
    c0 = np.clip(cols - NA_COLS // 2, 0, GRID_W - NA_COLS)
    col_in = (cols[None, :] >= c0[:, None]) & (cols[None, :] < c0[:, None] + NA_COLS)
    n_dr = 2 * NA_ROWS - 1
    pad = GRID_W - NA_COLS
    rbp = jnp.pad(rel_bias.astype(F32), ((0, 0), (0, 0), (pad, pad)), mode="edge")
    tiles = jnp.stack([rbp[:, :, GRID_W - 1 - cq:2 * GRID_W - 1 - cq] for cq in range(GRID_W)], axis=2)
    tiles = jnp.where(col_in[None, None], tiles, NEG_BIG)
    tiles = jnp.concatenate([tiles, jnp.full((NA_HEADS, 1, GRID_W, GRID_W), NEG_BIG, F32)], axis=1)
    dr_blk = np.full((3, NA_QROWS, NA_KROWS), n_dr, np.int32)
    for cls, tile in enumerate((0, 2, NT_LAT - 1)):
        start = int(np.clip(tile - 1, 0, NT_LAT - NA_KBLK)) * NA_QROWS
        for qr in range(NA_QROWS):
            r = tile * NA_QROWS + qr
            r0 = int(np.clip(r - NA_ROWS // 2, 0, ROWS - NA_ROWS))
            for kj in range(NA_KROWS):
                kr = start + kj
                if r0 <= kr < r0 + NA_ROWS:
                    dr_blk[cls, qr, kj] = kr - r + NA_ROWS - 1
    local = jnp.take(tiles, jnp.asarray(dr_blk.reshape(-1)), axis=1)
    local = local.reshape(NA_HEADS, 3, NA_QROWS, NA_KROWS, GRID_W, GRID_W).transpose(0, 1, 2, 4, 3, 5)
    local = local.reshape(NA_HEADS, 3, TM, NA_KROWS * GRID_W)
    return jnp.concatenate([local, jnp.zeros((NA_HEADS, 3, TM, CTX_LEN), F32)], axis=-1)


def _merge_body(h_ref, mod_ref, g_ref, wgate_ref, oa_ref, ob_ref, om_ref, wa_ref, wb_ref, wm_ref, wo_ref,
                o_ref):
    h = h_ref[...]
    mod = mod_ref[...]
    n = _rmsnorm(h, g_ref[...]) * (1.0 + mod[1:2]) + mod[0:1]
    gates = _sigmoid(_bdot(n.astype(BF16), wgate_ref[...]))
    y = (gates[:, :D_MODEL] * _bdot(oa_ref[...], wa_ref[...])
         + gates[:, D_MODEL:2 * D_MODEL] * _bdot(ob_ref[...], wb_ref[...])
         + gates[:, 2 * D_MODEL:] * _bdot(om_ref[...], wm_ref[...]))
    o_ref[...] = h + mod[2:3] * _bdot(y.astype(BF16), wo_ref[...])


def _merge(h, mod, g_norm, w_gate, oa, ob, om, wa, wb, wm, wo, *, n_tiles):
    return pl.pallas_call(
        _merge_body,
        grid=(n_tiles, BATCH),
        in_specs=[_tok_spec(D_MODEL), _mod_spec(1), _const_spec((1, D_MODEL)),
                  _const_spec((D_MODEL, GATE_WIDTH)),
                  _tok_spec(NA_WIDTH), _tok_spec(GQA_Q_WIDTH), _tok_spec(MLA_HEADS * MLA_V),
                  _const_spec((NA_WIDTH, D_MODEL)), _const_spec((GQA_Q_WIDTH, D_MODEL)),
                  _const_spec((MLA_HEADS * MLA_V, D_MODEL)), _const_spec((D_MODEL, D_MODEL))],
        out_specs=_tok_spec(D_MODEL),
        out_shape=jax.ShapeDtypeStruct((BATCH, n_tiles * TM, D_MODEL), F32),
        compiler_params=_params(2),
        name="gated_merge",
    )(h, mod, g_norm, w_gate, oa, ob, om, wa, wb, wm, wo)


def _rope_tables():
    t = np.arange(SEQ)
    row = (t // GRID_W).astype(np.float32)[:, None]
    col = (t % GRID_W).astype(np.float32)[:, None]

    def head_tables(d_rot):
        half = d_rot // 2
        freqs = jnp.asarray(ROPE_THETA, F32) ** (-jnp.arange(0, half, 2, dtype=F32) / half)
        r, c = jnp.asarray(row) * freqs, jnp.asarray(col) * freqs
        ang = jnp.concatenate([r, r, c, c], axis=-1)
        first = (np.arange(d_rot) % half) < (half // 2)
        cos, sin = jnp.cos(ang), jnp.sin(ang)
        return cos, jnp.where(first, -sin, 0.0), jnp.where(first, 0.0, sin)

    def place(tabs, reps, lo, group):
        out = []
        for k, tab in enumerate(tabs):
            fill = 1.0 if k == 0 else 0.0
            d_rot = tab.shape[-1]
            g = jnp.concatenate([jnp.full((SEQ, lo), fill, F32), tab,
                                 jnp.full((SEQ, group - lo - d_rot), fill, F32)], axis=-1)
            g = jnp.tile(g, (1, reps))
            ctx = jnp.full((CTX_LEN, reps * group), fill, F32)
            out.append(jnp.concatenate([g, ctx], axis=0))
        return out

    tb = head_tables(HEAD_DIM)
    tm = head_tables(MLA_ROPE)
    return (place(tb, GQA_HEADS, 0, HEAD_DIM)
            + place(tm, MLA_HEADS, MLA_NOPE, MLA_PAD)
            + place(tm, 1, 0, LANES))


def _seg_matrix(width):
    idx = np.arange(width) // HEAD_DIM
    return jnp.asarray((idx[:, None] == idx[None, :]).astype(np.float32) / HEAD_DIM, BF16)


def _place_matrix():
    m = np.zeros((LANES, MLA_QK_WIDTH), np.float32)
    for hd in range(MLA_HEADS):
        for j in range(MLA_ROPE):
            m[j, hd * MLA_PAD + MLA_NOPE + j] = 1.0
    return jnp.asarray(m, BF16)


def _pad_heads(w, n_heads, per_head, keep_lo, keep_hi):
    k = w.shape[0]
    wh = w.reshape(k, n_heads, per_head)[:, :, keep_lo:keep_hi]
    wh = jnp.pad(wh, ((0, 0), (0, 0), (0, MLA_PAD - (keep_hi - keep_lo))))
    return wh.reshape(k, n_heads * MLA_PAD)


def kernel(x, c, ctx, c_ctx, w_ada, b_ada, ffn1_norm, ffn1_w_gate, ffn1_w_up, ffn1_w_down, mix_norm, w_in,
           na_rel_bias, gqa_q_norm, gqa_k_norm, mla_q_norm, mla_kv_norm, mla_w_uq, mla_w_ukv,
           w_branch_a, w_branch_b, w_branch_c, w_out, ffn2_norm, ffn2_w_gate, ffn2_w_up, ffn2_w_down,
           final_norm):
    assert x.shape == (BATCH, SEQ, D_MODEL) and ctx.shape == (BATCH, CTX_LEN, D_MODEL)

    c_all = jnp.concatenate([c, c_ctx[None, :], jnp.zeros((MOD_ROWS - BATCH - 1, D_MODEL), F32)], axis=0)
    mod_all = _ada(c_all, w_ada, b_ada).reshape(DEPTH, MOD_ROWS, 3, 3, D_MODEL)

    tables = _rope_tables()
    seg_q, seg_k, place = _seg_matrix(GQA_Q_WIDTH), _seg_matrix(GQA_KV_WIDTH), _place_matrix()
    fin = final_norm.reshape(1, D_MODEL)

    h = jnp.concatenate([x, ctx], axis=1)
    for i in range(DEPTH):
        last = i == DEPTH - 1
        mod = mod_all[i]
        w_qkv = jnp.pad(w_in[i][:, :QKV_WIDTH - 96], ((0, 0), (0, 96))).astype(BF16)
        w_gate = w_in[i][:, QKV_WIDTH - 96:].astype(BF16)
        w_uq = _pad_heads(mla_w_uq[i], MLA_HEADS, MLA_NOPE + MLA_ROPE, 0, MLA_NOPE + MLA_ROPE).astype(BF16)
        w_uk = _pad_heads(mla_w_ukv[i], MLA_HEADS, MLA_NOPE + MLA_V, 0, MLA_NOPE)
        w_uv = mla_w_ukv[i].reshape(MLA_KV_RANK, MLA_HEADS, MLA_NOPE + MLA_V)[:, :, MLA_NOPE:]
        w_ukv = jnp.concatenate([w_uk, w_uv.reshape(MLA_KV_RANK, MLA_HEADS * MLA_V)], axis=1).astype(BF16)
        consts = (seg_q, seg_k,
                  jnp.tile(gqa_q_norm[i], GQA_HEADS)[None, :], jnp.tile(gqa_k_norm[i], GQA_KV_HEADS)[None, :],
                  mla_q_norm[i][None, :], mla_kv_norm[i][None, :], w_uq, w_ukv, place)

        h = _ffn(h, mod, 0, ffn1_norm[i][None, :], ffn1_w_gate[i].astype(BF16), ffn1_w_up[i].astype(BF16),
                 ffn1_w_down[i].astype(BF16), fin, n_tiles=NT_ALL, final=False)
        qa, ka, va, qb, kb, vb, qm, km, vm = _proj(h, mod, mix_norm[i][None, :], w_qkv, consts, tables)

        with_ctx = not last
        oa = _na_attn(qa, ka, va, _na_bias_table(na_rel_bias[i]), with_ctx=with_ctx)
        ob = _block_attn(qb, kb, vb, cfg=_GQA_CFG, with_ctx=with_ctx, name="gqa_attn")
        om = _block_attn(qm, km, vm, cfg=_MLA_CFG, with_ctx=with_ctx, name="mla_attn")

        n_tiles = NT_LAT if last else NT_ALL
        h = _merge(h, mod, mix_norm[i][None, :], w_gate, oa, ob, om, w_branch_a[i].astype(BF16),
                   w_branch_b[i].astype(BF16), w_branch_c[i].astype(BF16), w_out[i].astype(BF16),
                   n_tiles=n_tiles)
        h = _ffn(h, mod, 2, ffn2_norm[i][None, :], ffn2_w_gate[i].astype(BF16), ffn2_w_up[i].astype(BF16),
                 ffn2_w_down[i].astype(BF16), fin, n_tiles=n_tiles, final=last)
    return h
```

```python
import functools

import numpy as np
import jax
import jax.numpy as jnp
from jax import lax
from jax.experimental import pallas as pl
from jax.experimental.pallas import tpu as pltpu

D_MODEL = 1024
BATCH = 32
SEQ = 2048
DEPTH = 2
CTX_LEN = 256
GRID_W = 64
HEAD_DIM = 64
NA_HEADS = 4
NA_ROWS = 8
NA_COLS = 16
GQA_HEADS = 8
GQA_KV_HEADS = 2
MLA_HEADS = 4
MLA_Q_RANK = 256
MLA_KV_RANK = 128
MLA_NOPE = 64
MLA_ROPE = 32
MLA_V = 64
N_MOD = 9
D_FF = ((8 * D_MODEL // 3 + 127) // 128) * 128
ROPE_THETA = 10000.0
EPS = 1e-6
NEG_BIG = -1e30

NA_WIDTH = NA_HEADS * HEAD_DIM
GQA_Q_WIDTH = GQA_HEADS * HEAD_DIM
GQA_KV_WIDTH = GQA_KV_HEADS * HEAD_DIM
GATE_WIDTH = 3 * D_MODEL
ROWS = SEQ // GRID_W

LANES = 128
TOK = SEQ + CTX_LEN
TM = 256
NT_ALL = TOK // TM
NT_LAT = SEQ // TM
MOD_ROWS = 40
CTX_ROW = BATCH
ADA_BLOCK = 1152
MLA_PAD = LANES
MLA_QK_WIDTH = MLA_HEADS * MLA_PAD
QKV_WIDTH = 2048
NA_QROWS = TM // GRID_W
NA_KROWS = 12
NA_KBLK = NA_KROWS * GRID_W // TM
NA_NKEY = NA_KROWS * GRID_W + CTX_LEN
VMEM_LIMIT = 56 * 1024 * 1024

F32 = jnp.float32
BF16 = jnp.bfloat16

_OFF_AQ, _OFF_AK, _OFF_AV = 0, 256, 512
_OFF_BQ, _OFF_BK, _OFF_BV = 768, 1280, 1408
_OFF_CQ, _OFF_CKV, _OFF_CKR = 1536, 1792, 1920
_VB_OUT = 5


def _params(n_axes):
    return pltpu.CompilerParams(dimension_semantics=("arbitrary",) * n_axes,
                                vmem_limit_bytes=VMEM_LIMIT)


def _const_spec(shape):
    nd = len(shape)
    return pl.BlockSpec(shape, lambda *_: (0,) * nd, pipeline_mode=pl.Buffered(1))


def _mod_spec(sub):
    return pl.BlockSpec((None, None, 3, D_MODEL),
                        lambda t, b: (jnp.where(t >= NT_LAT, CTX_ROW, b), sub, 0, 0))


def _tok_spec(width):
    return pl.BlockSpec((None, TM, width), lambda t, b: (b, t, 0))


def _sigmoid(x):
    return 1.0 / (1.0 + jnp.exp(-x))


def _rmsnorm(x, g):
    return (x * lax.rsqrt(jnp.mean(x * x, axis=-1, keepdims=True) + EPS)) * g


def _bdot(a, b):
    return jnp.dot(a, b, preferred_element_type=F32)


def _ada_body(c_ref, w_ref, b_ref, o_ref):
    c = c_ref[...]
    s = c * _sigmoid(c)
    o_ref[...] = _bdot(s.astype(BF16), w_ref[...].astype(BF16)) + b_ref[...]


def _ada(c_all, w_ada, b_ada):
    nblk = (N_MOD * D_MODEL) // ADA_BLOCK
    return pl.pallas_call(
        _ada_body,
        grid=(DEPTH, nblk),
        in_specs=[
            pl.BlockSpec((MOD_ROWS, D_MODEL), lambda i, j: (0, 0)),
            pl.BlockSpec((None, D_MODEL, ADA_BLOCK), lambda i, j: (i, 0, j)),
            pl.BlockSpec((None, 1, ADA_BLOCK), lambda i, j: (i, 0, j)),
        ],
        out_specs=pl.BlockSpec((None, MOD_ROWS, ADA_BLOCK), lambda i, j: (i, 0, j)),
        out_shape=jax.ShapeDtypeStruct((DEPTH, MOD_ROWS, N_MOD * D_MODEL), F32),
        compiler_params=_params(2),
        name="ada_mod",
    )(c_all, w_ada, b_ada.reshape(DEPTH, 1, N_MOD * D_MODEL))


def _ffn_body(h_ref, mod_ref, g_ref, wg_ref, wu_ref, wd_ref, fin_ref, o_ref, *, final):
    h = h_ref[...]
    mod = mod_ref[...]
    n = _rmsnorm(h, g_ref[...]) * (1.0 + mod[1:2]) + mod[0:1]
    nb = n.astype(BF16)
    g = _bdot(nb, wg_ref[...])
    u = _bdot(nb, wu_ref[...])
    a = (g * _sigmoid(g)) * u
    d = _bdot(a.astype(BF16), wd_ref[...])
    out = h + 0.5 * mod[2:3] * d
    if final:
        out = _rmsnorm(out, fin_ref[...])
    o_ref[...] = out


def _ffn(h, mod, sub, g_norm, wg, wu, wd, fin, *, n_tiles, final):
    return pl.pallas_call(
        functools.partial(_ffn_body, final=final),
        grid=(n_tiles, BATCH),
        in_specs=[
            _tok_spec(D_MODEL),
            _mod_spec(sub),
            _const_spec((1, D_MODEL)),
            _const_spec((D_MODEL, D_FF)),
            _const_spec((D_MODEL, D_FF)),
            _const_spec((D_FF, D_MODEL)),
            _const_spec((1, D_MODEL)),
        ],
        out_specs=_tok_spec(D_MODEL),
        out_shape=jax.ShapeDtypeStruct((BATCH, n_tiles * TM, D_MODEL), F32),
        compiler_params=_params(2),
        name="half_ffn",
    )(h, mod, g_norm, wg, wu, wd, fin)


def _rope(x, cos, sin_a, sin_b, half):
    w = x.shape[-1]
    return x * cos + pltpu.roll(x, w - half, 1) * sin_a + pltpu.roll(x, half, 1) * sin_b


def _seg_mean_sq(x, seg):
    x2 = x * x
    hi = x2.astype(BF16)
    lo = (x2 - hi.astype(F32)).astype(BF16)
    return _bdot(hi, seg) + _bdot(lo, seg)


def _proj_body(h_ref, mod_ref, g_ref, w_ref, segq_ref, segk_ref, qn_ref, kn_ref, mqn_ref, mkvn_ref,
               wuq_ref, wukv_ref, place_ref,
               cb_ref, sab_ref, sbb_ref, cm_ref, sam_ref, sbm_ref, ck_ref, sak_ref, sbk_ref,
               qa_ref, ka_ref, va_ref, qb_ref, kb_ref, vb_ref, qm_ref, km_ref, vm_ref):
    mod = mod_ref[...]
    n = _rmsnorm(h_ref[...], g_ref[...]) * (1.0 + mod[1:2]) + mod[0:1]
    z = _bdot(n.astype(BF16), w_ref[...])

    scale = HEAD_DIM ** -0.5
    qa_ref[...] = (z[:, _OFF_AQ:_OFF_AQ + NA_WIDTH] * scale).astype(BF16)
    ka_ref[...] = z[:, _OFF_AK:_OFF_AK + NA_WIDTH].astype(BF16)
    va_ref[...] = z[:, _OFF_AV:_OFF_AV + NA_WIDTH].astype(BF16)

    cb, sab, sbb = cb_ref[...], sab_ref[...], sbb_ref[...]
    bq = z[:, _OFF_BQ:_OFF_BQ + GQA_Q_WIDTH]
    bq = (bq * lax.rsqrt(_seg_mean_sq(bq, segq_ref[...]) + EPS)) * qn_ref[...]
    qb_ref[...] = (_rope(bq, cb, sab, sbb, HEAD_DIM // 4) * scale).astype(BF16)
    bk = z[:, _OFF_BK:_OFF_BK + GQA_KV_WIDTH]
    bk = (bk * lax.rsqrt(_seg_mean_sq(bk, segk_ref[...]) + EPS)) * kn_ref[...]
    kb_ref[...] = _rope(bk, cb[:, :GQA_KV_WIDTH], sab[:, :GQA_KV_WIDTH], sbb[:, :GQA_KV_WIDTH],
                        HEAD_DIM // 4).astype(BF16)
    vb_ref[...] = z[:, _OFF_BV:_OFF_BV + GQA_KV_WIDTH].T.astype(BF16)

    mla_scale = (MLA_NOPE + MLA_ROPE) ** -0.5
    cq = _rmsnorm(z[:, _OFF_CQ:_OFF_CQ + MLA_Q_RANK], mqn_ref[...])
    q_lat = _bdot(cq.astype(BF16), wuq_ref[...])
    qm_ref[...] = (_rope(q_lat, cm_ref[...], sam_ref[...], sbm_ref[...], MLA_ROPE // 4)
                   * mla_scale).astype(BF16)
    ckv = _rmsnorm(z[:, _OFF_CKV:_OFF_CKV + MLA_KV_RANK], mkvn_ref[...])
    kv_lat = _bdot(ckv.astype(BF16), wukv_ref[...])
    vm_ref[...] = kv_lat[:, MLA_QK_WIDTH:].astype(BF16)
    kr = _rope(z[:, _OFF_CKR:_OFF_CKR + LANES], ck_ref[...], sak_ref[...], sbk_ref[...],
               MLA_ROPE // 4).astype(BF16)
    km_ref[...] = (kv_lat[:, :MLA_QK_WIDTH] + _bdot(kr, place_ref[...])).astype(BF16)


def _proj(h, mod, g_norm, w_qkv, consts, tables):
    widths = (NA_WIDTH, NA_WIDTH, NA_WIDTH, GQA_Q_WIDTH, GQA_KV_WIDTH, GQA_KV_WIDTH,
              MLA_QK_WIDTH, MLA_QK_WIDTH, MLA_HEADS * MLA_V)
    tab_specs = [pl.BlockSpec((TM, t.shape[-1]), lambda t_, b: (t_, 0)) for t in tables]
    return pl.pallas_call(
        _proj_body,
        grid=(NT_ALL, BATCH),
        in_specs=[_tok_spec(D_MODEL), _mod_spec(1), _const_spec((1, D_MODEL)),
                  _const_spec((D_MODEL, QKV_WIDTH))]
                 + [_const_spec(c.shape) for c in consts] + tab_specs,
        out_specs=[pl.BlockSpec((None, w, TM), lambda t, b: (b, 0, t)) if k == _VB_OUT else _tok_spec(w)
                   for k, w in enumerate(widths)],
        out_shape=[jax.ShapeDtypeStruct((BATCH, w, TOK) if k == _VB_OUT else (BATCH, TOK, w), BF16)
                   for k, w in enumerate(widths)],
        compiler_params=_params(2),
        name="mix_proj",
    )(h, mod, g_norm, w_qkv, *consts, *tables)


def _attend(q, k, v, *, hkv, grp, chain, dqk, qstride, kstride, dv, bias_ref=None):
    tq = q.shape[0]
    outs = []
    for g in range(hkv):
        kg = k[:, g * kstride:g * kstride + dqk]
        vg = v[:, g * dv:(g + 1) * dv]
        for j0 in range(0, grp, chain):
            qs = [q[:, (g * grp + j0 + j) * qstride:(g * grp + j0 + j) * qstride + dqk] for j in range(chain)]
            qg = qs[0] if chain == 1 else jnp.concatenate(qs, axis=0)
            s = lax.dot_general(qg, kg, (((1,), (1,)), ((), ())), preferred_element_type=F32)
            if bias_ref is not None:
                s = s + bias_ref[g]
            m = jnp.max(s, axis=-1, keepdims=True)
            p = jnp.exp(s - m)
            l = jnp.sum(p, axis=-1, keepdims=True)
            o = _bdot(p.astype(BF16), vg) * (1.0 / l)
            outs.extend(o[j * tq:(j + 1) * tq] for j in range(chain))
    return jnp.concatenate(outs, axis=1)


def _attend_t(q, k, vt, *, hkv, grp, chain, dqk, qstride, kstride, dv):
    tq = q.shape[0]
    outs = []
    for g in range(hkv):
        kg = k[:, g * kstride:g * kstride + dqk]
        for j0 in range(0, grp, chain):
            qs = [q[:, (g * grp + j0 + j) * qstride:(g * grp + j0 + j) * qstride + dqk] for j in range(chain)]
            qg = qs[0] if chain == 1 else jnp.concatenate(qs, axis=0)
            st = lax.dot_general(kg, qg, (((1,), (1,)), ((), ())), preferred_element_type=F32)
            m = jnp.max(st, axis=0, keepdims=True)
            p = jnp.exp(st - m)
            l = jnp.sum(p, axis=0, keepdims=True)
            ot = _bdot(vt[g * dv:(g + 1) * dv, :], p.astype(BF16)) * (1.0 / l)
            outs.extend(ot[:, j * tq:(j + 1) * tq] for j in range(chain))
    return jnp.concatenate(outs, axis=0).T


def _block_attn_body(q_ref, k_ref, v_ref, o_ref, *, cfg, transposed, with_ctx):
    def run(nk_lo, nk_hi):
        if transposed:
            o = _attend_t(q_ref[...], k_ref[nk_lo:nk_hi, :], v_ref[:, nk_lo:nk_hi], **cfg)
        else:
            o = _attend(q_ref[...], k_ref[nk_lo:nk_hi, :], v_ref[nk_lo:nk_hi, :], **cfg)
        o_ref[...] = o.astype(BF16)

    if not with_ctx:
        run(0, TOK)
    else:
        t = pl.program_id(1)

        @pl.when(t < NT_LAT)
        def _():
            run(0, TOK)

        @pl.when(t >= NT_LAT)
        def _():
            run(SEQ, TOK)


def _block_attn(q, k, v, *, cfg, transposed, with_ctx, name):
    n_tiles = NT_ALL if with_ctx else NT_LAT
    wo = cfg["hkv"] * cfg["grp"] * cfg["dv"]
    return pl.pallas_call(
        functools.partial(_block_attn_body, cfg=cfg, transposed=transposed, with_ctx=with_ctx),
        grid=(BATCH, n_tiles),
        in_specs=[
            pl.BlockSpec((None, TM, q.shape[-1]), lambda b, t: (b, t, 0)),
            pl.BlockSpec((None, TOK, k.shape[-1]), lambda b, t: (b, 0, 0)),
            pl.BlockSpec((None,) + v.shape[1:], lambda b, t: (b, 0, 0)),
        ],
        out_specs=pl.BlockSpec((None, TM, wo), lambda b, t: (b, t, 0)),
        out_shape=jax.ShapeDtypeStruct((BATCH, n_tiles * TM, wo), BF16),
        compiler_params=_params(2),
        name=name,
    )(q, k, v)


_GQA_CFG = dict(hkv=GQA_KV_HEADS, grp=GQA_HEADS // GQA_KV_HEADS, chain=4, dqk=HEAD_DIM,
                qstride=HEAD_DIM, kstride=HEAD_DIM, dv=HEAD_DIM)
_MLA_CFG = dict(hkv=MLA_HEADS, grp=1, chain=1, dqk=MLA_PAD, qstride=MLA_PAD, kstride=MLA_PAD, dv=MLA_V)
_NA_CFG = dict(hkv=NA_HEADS, grp=1, chain=1, dqk=HEAD_DIM, qstride=HEAD_DIM, kstride=HEAD_DIM, dv=HEAD_DIM)


def _na_start_block(t):
    return jnp.clip(t - 1, 0, NT_LAT - NA_KBLK)


def _na_body(q_ref, k0_ref, k1_ref, k2_ref, kc_ref, v0_ref, v1_ref, v2_ref, vc_ref, bias_ref, o_ref,
             *, with_ctx):
    def latent():
        k = jnp.concatenate([k0_ref[...], k1_ref[...], k2_ref[...], kc_ref[...]], axis=0)
        v = jnp.concatenate([v0_ref[...], v1_ref[...], v2_ref[...], vc_ref[...]], axis=0)
        o_ref[...] = _attend(q_ref[...], k, v, bias_ref=bias_ref, **_NA_CFG).astype(BF16)

    def context():
        o_ref[...] = _attend(q_ref[...], kc_ref[...], vc_ref[...], **_NA_CFG).astype(BF16)

    if not with_ctx:
        latent()
    else:
        t = pl.program_id(0)
        pl.when(t < NT_LAT)(latent)
        pl.when(t >= NT_LAT)(context)


def _na_attn(q, k, v, bias, *, with_ctx):
    n_tiles = NT_ALL if with_ctx else NT_LAT

    def kv_spec(j):
        return pl.BlockSpec((None, TM, NA_WIDTH), lambda t, b: (b, _na_start_block(t) + j, 0))

    ctx_spec = pl.BlockSpec((None, TM, NA_WIDTH), lambda t, b: (b, NT_LAT, 0))
    bias_spec = pl.BlockSpec(
        (NA_HEADS, None, TM, NA_NKEY),
        lambda t, b: (0, jnp.where(t == 0, 0, jnp.where(t == NT_LAT - 1, 2, 1)), 0, 0))
    return pl.pallas_call(
        functools.partial(_na_body, with_ctx=with_ctx),
        grid=(n_tiles, BATCH),
        in_specs=[_tok_spec(NA_WIDTH), kv_spec(0), kv_spec(1), kv_spec(2), ctx_spec,
                  kv_spec(0), kv_spec(1), kv_spec(2), ctx_spec, bias_spec],
        out_specs=_tok_spec(NA_WIDTH),
        out_shape=jax.ShapeDtypeStruct((BATCH, n_tiles * TM, NA_WIDTH), BF16),
        compiler_params=_params(2),
        name="na_attn",
    )(q, k, k, k, k, v, v, v, v, bias)


def _na_bias_table(rel_bias):
    cols = np.arange(GRID_W)
    c0 = np.clip(cols - NA_COLS // 2, 0, GRID_W - NA_COLS)
    col_in = (cols[None, :] >= c0[:, None]) & (cols[None, :] < c0[:, None] + NA_COLS)
    n_dr = 2 * NA_ROWS - 1
    pad = GRID_W - NA_COLS
    rbp = jnp.pad(rel_bias.astype(F32), ((0, 0), (0, 0), (pad, pad)), mode="edge")
    tiles = jnp.stack([rbp[:, :, GRID_W - 1 - cq:2 * GRID_W - 1 - cq] for cq in range(GRID_W)], axis=2)
    tiles = jnp.where(col_in[None, None], tiles, NEG_BIG)
    tiles = jnp.concatenate([tiles, jnp.full((NA_HEADS, 1, GRID_W, GRID_W), NEG_BIG, F32)], axis=1)
    dr_blk = np.full((3, NA_QROWS, NA_KROWS), n_dr, np.int32)
    for cls, tile in enumerate((0, 2, NT_LAT - 1)):
        start = int(np.clip(tile - 1, 0, NT_LAT - NA_KBLK)) * NA_QROWS
        for qr in range(NA_QROWS):
            r = tile * NA_QROWS + qr
            r0 = int(np.clip(r - NA_ROWS // 2, 0, ROWS - NA_ROWS))
            for kj in range(NA_KROWS):
                kr = start + kj
                if r0 <= kr < r0 + NA_ROWS:
                    dr_blk[cls, qr, kj] = kr - r + NA_ROWS - 1
    local = jnp.take(tiles, jnp.asarray(dr_blk.reshape(-1)), axis=1)
    local = local.reshape(NA_HEADS, 3, NA_QROWS, NA_KROWS, GRID_W, GRID_W).transpose(0, 1, 2, 4, 3, 5)
    local = local.reshape(NA_HEADS, 3, TM, NA_KROWS * GRID_W)
    return jnp.concatenate([local, jnp.zeros((NA_HEADS, 3, TM, CTX_LEN), F32)], axis=-1)


def _merge_body(h_ref, mod_ref, g_ref, wgate_ref, oa_ref, ob_ref, om_ref, wa_ref, wb_ref, wm_ref, wo_ref,
                o_ref):
    h = h_ref[...]
    mod = mod_ref[...]
    n = _rmsnorm(h, g_ref[...]) * (1.0 + mod[1:2]) + mod[0:1]
    gates = _sigmoid(_bdot(n.astype(BF16), wgate_ref[...]))
    y = (gates[:, :D_MODEL] * _bdot(oa_ref[...], wa_ref[...])
         + gates[:, D_MODEL:2 * D_MODEL] * _bdot(ob_ref[...], wb_ref[...])
         + gates[:, 2 * D_MODEL:] * _bdot(om_ref[...], wm_ref[...]))
    o_ref[...] = h + mod[2:3] * _bdot(y.astype(BF16), wo_ref[...])


def _merge(h, mod, g_norm, w_gate, oa, ob, om, wa, wb, wm, wo, *, n_tiles):
    return pl.pallas_call(
        _merge_body,
        grid=(n_tiles, BATCH),
        in_specs=[_tok_spec(D_MODEL), _mod_spec(1), _const_spec((1, D_MODEL)),
                  _const_spec((D_MODEL, GATE_WIDTH)),
                  _tok_spec(NA_WIDTH), _tok_spec(GQA_Q_WIDTH), _tok_spec(MLA_HEADS * MLA_V),
                  _const_spec((NA_WIDTH, D_MODEL)), _const_spec((GQA_Q_WIDTH, D_MODEL)),
                  _const_spec((MLA_HEADS * MLA_V, D_MODEL)), _const_spec((D_MODEL, D_MODEL))],
        out_specs=_tok_spec(D_MODEL),
        out_shape=jax.ShapeDtypeStruct((BATCH, n_tiles * TM, D_MODEL), F32),
        compiler_params=_params(2),
        name="gated_merge",
    )(h, mod, g_norm, w_gate, oa, ob, om, wa, wb, wm, wo)


def _rope_tables():
    t = np.arange(SEQ)
    row = (t // GRID_W).astype(np.float32)[:, None]
    col = (t % GRID_W).astype(np.float32)[:, None]

    def head_tables(d_rot):
        half = d_rot // 2
        freqs = jnp.asarray(ROPE_THETA, F32) ** (-jnp.arange(0, half, 2, dtype=F32) / half)
        r, c = jnp.asarray(row) * freqs, jnp.asarray(col) * freqs
        ang = jnp.concatenate([r, r, c, c], axis=-1)
        first = (np.arange(d_rot) % half) < (half // 2)
        cos, sin = jnp.cos(ang), jnp.sin(ang)
        return cos, jnp.where(first, -sin, 0.0), jnp.where(first, 0.0, sin)

    def place(tabs, reps, lo, group):
        out = []
        for k, tab in enumerate(tabs):
            fill = 1.0 if k == 0 else 0.0
            d_rot = tab.shape[-1]
            g = jnp.concatenate([jnp.full((SEQ, lo), fill, F32), tab,
                                 jnp.full((SEQ, group - lo - d_rot), fill, F32)], axis=-1)
            g = jnp.tile(g, (1, reps))
            ctx = jnp.full((CTX_LEN, reps * group), fill, F32)
            out.append(jnp.concatenate([g, ctx], axis=0))
        return out

    tb = head_tables(HEAD_DIM)
    tm = head_tables(MLA_ROPE)
    return (place(tb, GQA_HEADS, 0, HEAD_DIM)
            + place(tm, MLA_HEADS, MLA_NOPE, MLA_PAD)
            + place(tm, 1, 0, LANES))


def _seg_matrix(width):
    idx = np.arange(width) // HEAD_DIM
    return jnp.asarray((idx[:, None] == idx[None, :]).astype(np.float32) / HEAD_DIM, BF16)


def _place_matrix():
    m = np.zeros((LANES, MLA_QK_WIDTH), np.float32)
    for hd in range(MLA_HEADS):
        for j in range(MLA_ROPE):
            m[j, hd * MLA_PAD + MLA_NOPE + j] = 1.0
    return jnp.asarray(m, BF16)


def _pad_heads(w, n_heads, per_head, keep_lo, keep_hi):
    k = w.shape[0]
    wh = w.reshape(k, n_heads, per_head)[:, :, keep_lo:keep_hi]
    wh = jnp.pad(wh, ((0, 0), (0, 0), (0, MLA_PAD - (keep_hi - keep_lo))))
    return wh.reshape(k, n_heads * MLA_PAD)


def kernel(x, c, ctx, c_ctx, w_ada, b_ada, ffn1_norm, ffn1_w_gate, ffn1_w_up, ffn1_w_down, mix_norm, w_in,
           na_rel_bias, gqa_q_norm, gqa_k_norm, mla_q_norm, mla_kv_norm, mla_w_uq, mla_w_ukv,
           w_branch_a, w_branch_b, w_branch_c, w_out, ffn2_norm, ffn2_w_gate, ffn2_w_up, ffn2_w_down,
           final_norm):
    assert x.shape == (BATCH, SEQ, D_MODEL) and ctx.shape == (BATCH, CTX_LEN, D_MODEL)

    c_all = jnp.concatenate([c, c_ctx[None, :], jnp.zeros((MOD_ROWS - BATCH - 1, D_MODEL), F32)], axis=0)
    mod_all = _ada(c_all, w_ada, b_ada).reshape(DEPTH, MOD_ROWS, 3, 3, D_MODEL)

    tables = _rope_tables()
    seg_q, seg_k, place = _seg_matrix(GQA_Q_WIDTH), _seg_matrix(GQA_KV_WIDTH), _place_matrix()
    fin = final_norm.reshape(1, D_MODEL)

    h = jnp.concatenate([x, ctx], axis=1)
    for i in range(DEPTH):
        last = i == DEPTH - 1
        mod = mod_all[i]
        w_qkv = jnp.pad(w_in[i][:, :QKV_WIDTH - 96], ((0, 0), (0, 96))).astype(BF16)
        w_gate = w_in[i][:, QKV_WIDTH - 96:].astype(BF16)
        w_uq = _pad_heads(mla_w_uq[i], MLA_HEADS, MLA_NOPE + MLA_ROPE, 0, MLA_NOPE + MLA_ROPE).astype(BF16)
        w_uk = _pad_heads(mla_w_ukv[i], MLA_HEADS, MLA_NOPE + MLA_V, 0, MLA_NOPE)
        w_uv = mla_w_ukv[i].reshape(MLA_KV_RANK, MLA_HEADS, MLA_NOPE + MLA_V)[:, :, MLA_NOPE:]
        w_ukv = jnp.concatenate([w_uk, w_uv.reshape(MLA_KV_RANK, MLA_HEADS * MLA_V)], axis=1).astype(BF16)
        consts = (seg_q, seg_k,
                  jnp.tile(gqa_q_norm[i], GQA_HEADS)[None, :], jnp.tile(gqa_k_norm[i], GQA_KV_HEADS)[None, :],
                  mla_q_norm[i][None, :], mla_kv_norm[i][None, :], w_uq, w_ukv, place)

        h = _ffn(h, mod, 0, ffn1_norm[i][None, :], ffn1_w_gate[i].astype(BF16), ffn1_w_up[i].astype(BF16),
                 ffn1_w_down[i].astype(BF16), fin, n_tiles=NT_ALL, final=False)
        qa, ka, va, qb, kb, vb, qm, km, vm = _proj(h, mod, mix_norm[i][None, :], w_qkv, consts, tables)

        with_ctx = not last
        oa = _na_attn(qa, ka, va, _na_bias_table(na_rel_bias[i]), with_ctx=with_ctx)
        ob = _block_attn(qb, kb, vb, cfg=_GQA_CFG, transposed=True, with_ctx=with_ctx, name="gqa_attn")
        om = _block_attn(qm, km, vm, cfg=_MLA_CFG, transposed=False, with_ctx=with_ctx, name="mla_attn")

        n_tiles = NT_LAT if last else NT_ALL
        h = _merge(h, mod, mix_norm[i][None, :], w_gate, oa, ob, om, w_branch_a[i].astype(BF16),
                   w_branch_b[i].astype(BF16), w_branch_c[i].astype(BF16), w_out[i].astype(BF16),
                   n_tiles=n_tiles)
        h = _ffn(h, mod, 2, ffn2_norm[i][None, :], ffn2_w_gate[i].astype(BF16), ffn2_w_up[i].astype(BF16),
                 ffn2_w_down[i].astype(BF16), fin, n_tiles=n_tiles, final=last)
    return h
```

```python
import functools

import numpy as np
import jax
import jax.numpy as jnp
from jax import lax
from jax.experimental import pallas as pl
from jax.experimental.pallas import tpu as pltpu

D_MODEL = 1024
BATCH = 32
SEQ = 2048
DEPTH = 2
CTX_LEN = 256
GRID_W = 64
HEAD_DIM = 64
NA_HEADS = 4
NA_ROWS = 8
NA_COLS = 16
GQA_HEADS = 8
GQA_KV_HEADS = 2
MLA_HEADS = 4
MLA_Q_RANK = 256
MLA_KV_RANK = 128
MLA_NOPE = 64
MLA_ROPE = 32
MLA_V = 64
N_MOD = 9
D_FF = ((8 * D_MODEL // 3 + 127) // 128) * 128
ROPE_THETA = 10000.0
EPS = 1e-6
NEG_BIG = -1e30

NA_WIDTH = NA_HEADS * HEAD_DIM
GQA_Q_WIDTH = GQA_HEADS * HEAD_DIM
GQA_KV_WIDTH = GQA_KV_HEADS * HEAD_DIM
GATE_WIDTH = 3 * D_MODEL
ROWS = SEQ // GRID_W

LANES = 128
BF16_SUBLANES = 16
LOG2_E = 1.4426950408889634
TOK = SEQ + CTX_LEN
TM = 256
NT_ALL = TOK // TM
NT_LAT = SEQ // TM
MOD_ROWS = 40
CTX_ROW = BATCH
ADA_BLOCK = 1152
MLA_PAD = LANES
MLA_QK_WIDTH = MLA_HEADS * MLA_PAD
QKV_WIDTH = 2048
NA_QROWS = TM // GRID_W
NA_KROWS = 12
NA_KBLK = NA_KROWS * GRID_W // TM
NA_NKEY = NA_KROWS * GRID_W + CTX_LEN
KEY_CHUNK = 256
VMEM_LIMIT = 56 * 1024 * 1024

F32 = jnp.float32
BF16 = jnp.bfloat16

_OFF_AQ, _OFF_AK, _OFF_AV = 0, 256, 512
_OFF_BQ, _OFF_BK, _OFF_BV = 768, 1280, 1408
_OFF_CQ, _OFF_CKV, _OFF_CKR = 1536, 1792, 1920
_VB_OUT = 5


def _params(n_axes):
    return pltpu.CompilerParams(dimension_semantics=("arbitrary",) * n_axes,
                                vmem_limit_bytes=VMEM_LIMIT)


def _const_spec(shape):
    nd = len(shape)
    return pl.BlockSpec(shape, lambda *_: (0,) * nd, pipeline_mode=pl.Buffered(1))


def _mod_spec(sub):
    return pl.BlockSpec((None, None, 3, D_MODEL),
                        lambda t, b: (jnp.where(t >= NT_LAT, CTX_ROW, b), sub, 0, 0))


def _tok_spec(width):
    return pl.BlockSpec((None, TM, width), lambda t, b: (b, t, 0))


def _sigmoid(x):
    return 1.0 / (1.0 + jnp.exp(-x))


def _rmsnorm(x, g):
    return (x * lax.rsqrt(jnp.mean(x * x, axis=-1, keepdims=True) + EPS)) * g


def _bdot(a, b):
    return jnp.dot(a, b, preferred_element_type=F32)


def _ada_body(c_ref, w_ref, b_ref, o_ref):
    c = c_ref[...]
    s = c * _sigmoid(c)
    o_ref[...] = _bdot(s.astype(BF16), w_ref[...].astype(BF16)) + b_ref[...]


def _ada(c_all, w_ada, b_ada):
    nblk = (N_MOD * D_MODEL) // ADA_BLOCK
    return pl.pallas_call(
        _ada_body,
        grid=(DEPTH, nblk),
        in_specs=[
            pl.BlockSpec((MOD_ROWS, D_MODEL), lambda i, j: (0, 0)),
            pl.BlockSpec((None, D_MODEL, ADA_BLOCK), lambda i, j: (i, 0, j)),
            pl.BlockSpec((None, 1, ADA_BLOCK), lambda i, j: (i, 0, j)),
        ],
        out_specs=pl.BlockSpec((None, MOD_ROWS, ADA_BLOCK), lambda i, j: (i, 0, j)),
        out_shape=jax.ShapeDtypeStruct((DEPTH, MOD_ROWS, N_MOD * D_MODEL), F32),
        compiler_params=_params(2),
        name="ada_mod",
    )(c_all, w_ada, b_ada.reshape(DEPTH, 1, N_MOD * D_MODEL))


def _ffn_body(h_ref, mod_ref, g_ref, wg_ref, wu_ref, wd_ref, fin_ref, o_ref, *, final):
    h = h_ref[...]
    mod = mod_ref[...]
    n = _rmsnorm(h, g_ref[...]) * (1.0 + mod[1:2]) + mod[0:1]
    nb = n.astype(BF16)
    g = _bdot(nb, wg_ref[...])
    u = _bdot(nb, wu_ref[...])
    a = (g * _sigmoid(g)) * u
    d = _bdot(a.astype(BF16), wd_ref[...])
    out = h + 0.5 * mod[2:3] * d
    if final:
        out = _rmsnorm(out, fin_ref[...])
    o_ref[...] = out


def _ffn(h, mod, sub, g_norm, wg, wu, wd, fin, *, n_tiles, final):
    return pl.pallas_call(
        functools.partial(_ffn_body, final=final),
        grid=(n_tiles, BATCH),
        in_specs=[
            _tok_spec(D_MODEL),
            _mod_spec(sub),
            _const_spec((1, D_MODEL)),
            _const_spec((D_MODEL, D_FF)),
            _const_spec((D_MODEL, D_FF)),
            _const_spec((D_FF, D_MODEL)),
            _const_spec((1, D_MODEL)),
        ],
        out_specs=_tok_spec(D_MODEL),
        out_shape=jax.ShapeDtypeStruct((BATCH, n_tiles * TM, D_MODEL), F32),
        compiler_params=_params(2),
        name="half_ffn",
    )(h, mod, g_norm, wg, wu, wd, fin)


def _rope(x, cos, sin_a, sin_b, half):
    w = x.shape[-1]
    return x * cos + pltpu.roll(x, w - half, 1) * sin_a + pltpu.roll(x, half, 1) * sin_b


def _seg_mean_sq(x, seg):
    x2 = x * x
    hi = x2.astype(BF16)
    lo = (x2 - hi.astype(F32)).astype(BF16)
    return _bdot(hi, seg) + _bdot(lo, seg)


def _proj_body(h_ref, mod_ref, g_ref, w_ref, segq_ref, segk_ref, qn_ref, kn_ref, mqn_ref, mkvn_ref,
               wuq_ref, wukv_ref, place_ref,
               cb_ref, sab_ref, sbb_ref, cm_ref, sam_ref, sbm_ref, ck_ref, sak_ref, sbk_ref,
               qa_ref, ka_ref, va_ref, qb_ref, kb_ref, vb_ref, qm_ref, km_ref, vm_ref):
    mod = mod_ref[...]
    n = _rmsnorm(h_ref[...], g_ref[...]) * (1.0 + mod[1:2]) + mod[0:1]
    z = _bdot(n.astype(BF16), w_ref[...])

    scale = LOG2_E * HEAD_DIM ** -0.5
    qa_ref[...] = (z[:, _OFF_AQ:_OFF_AQ + NA_WIDTH] * scale).astype(BF16)
    ka_ref[...] = z[:, _OFF_AK:_OFF_AK + NA_WIDTH].astype(BF16)
    va_ref[...] = z[:, _OFF_AV:_OFF_AV + NA_WIDTH].astype(BF16)

    cb, sab, sbb = cb_ref[...], sab_ref[...], sbb_ref[...]
    bq = z[:, _OFF_BQ:_OFF_BQ + GQA_Q_WIDTH]
    bq = (bq * lax.rsqrt(_seg_mean_sq(bq, segq_ref[...]) + EPS)) * qn_ref[...]
    qb_ref[...] = (_rope(bq, cb, sab, sbb, HEAD_DIM // 4) * scale).astype(BF16)
    bk = z[:, _OFF_BK:_OFF_BK + GQA_KV_WIDTH]
    bk = (bk * lax.rsqrt(_seg_mean_sq(bk, segk_ref[...]) + EPS)) * kn_ref[...]
    kb_ref[...] = _rope(bk, cb[:, :GQA_KV_WIDTH], sab[:, :GQA_KV_WIDTH], sbb[:, :GQA_KV_WIDTH],
                        HEAD_DIM // 4).astype(BF16)
    vb_ref[...] = z[:, _OFF_BV:_OFF_BV + GQA_KV_WIDTH].T.astype(BF16)

    mla_scale = LOG2_E * (MLA_NOPE + MLA_ROPE) ** -0.5
    cq = _rmsnorm(z[:, _OFF_CQ:_OFF_CQ + MLA_Q_RANK], mqn_ref[...])
    q_lat = _bdot(cq.astype(BF16), wuq_ref[...])
    qm_ref[...] = (_rope(q_lat, cm_ref[...], sam_ref[...], sbm_ref[...], MLA_ROPE // 4)
                   * mla_scale).astype(BF16)
    ckv = _rmsnorm(z[:, _OFF_CKV:_OFF_CKV + MLA_KV_RANK], mkvn_ref[...])
    kv_lat = _bdot(ckv.astype(BF16), wukv_ref[...])
    vm_ref[...] = kv_lat[:, MLA_QK_WIDTH:].astype(BF16)
    kr = _rope(z[:, _OFF_CKR:_OFF_CKR + LANES], ck_ref[...], sak_ref[...], sbk_ref[...],
               MLA_ROPE // 4).astype(BF16)
    km_ref[...] = (kv_lat[:, :MLA_QK_WIDTH] + _bdot(kr, place_ref[...])).astype(BF16)


def _proj(h, mod, g_norm, w_qkv, consts, tables):
    widths = (NA_WIDTH, NA_WIDTH, NA_WIDTH, GQA_Q_WIDTH, GQA_KV_WIDTH, GQA_KV_WIDTH,
              MLA_QK_WIDTH, MLA_QK_WIDTH, MLA_HEADS * MLA_V)
    tab_specs = [pl.BlockSpec((TM, t.shape[-1]), lambda t_, b: (t_, 0)) for t in tables]
    return pl.pallas_call(
        _proj_body,
        grid=(NT_ALL, BATCH),
        in_specs=[_tok_spec(D_MODEL), _mod_spec(1), _const_spec((1, D_MODEL)),
                  _const_spec((D_MODEL, QKV_WIDTH))]
                 + [_const_spec(c.shape) for c in consts] + tab_specs,
        out_specs=[pl.BlockSpec((None, w, TM), lambda t, b: (b, 0, t)) if k == _VB_OUT else _tok_spec(w)
                   for k, w in enumerate(widths)],
        out_shape=[jax.ShapeDtypeStruct((BATCH, w, TOK) if k == _VB_OUT else (BATCH, TOK, w), BF16)
                   for k, w in enumerate(widths)],
        compiler_params=_params(2),
        name="mix_proj",
    )(h, mod, g_norm, w_qkv, *consts, *tables)


def _attend(q, k, v, *, hkv, grp, chain, dqk, qstride, kstride, dv, bias_ref=None):
    assert grp == 1 and chain == 1 and 2 * dv == LANES and hkv % 2 == 0
    lane = lax.broadcasted_iota(jnp.int32, (1, LANES), 1)
    groups = []
    for g0 in range(0, hkv, 2):
        vgrp = v[:, g0 * dv:(g0 + 2) * dv]
        halves = []
        for g in (g0, g0 + 1):
            own = (lane < dv) if g == g0 else (lane >= dv)
            qg = q[:, g * qstride:g * qstride + dqk]
            kg = k[:, g * kstride:g * kstride + dqk]
            s = lax.dot_general(qg, kg, (((1,), (1,)), ((), ())), preferred_element_type=F32)
            if bias_ref is not None:
                s = s + bias_ref[g]
            p = jnp.exp2(s - jnp.max(s, axis=-1, keepdims=True))
            o = _bdot(p.astype(BF16), jnp.where(own, vgrp, jnp.ones_like(vgrp)))
            l = o[:, dv:dv + 1] if g == g0 else o[:, 0:1]
            halves.append(o * (1.0 / l))
        groups.append(jnp.where(lane < dv, halves[0], halves[1]))
    return jnp.concatenate(groups, axis=1)


def _zero_of(x):
    u = pltpu.bitcast(x, jnp.uint32)
    return pltpu.bitcast((u >> 16) >> 16, F32)


def _attend_t(q, k, vt, *, hkv, grp, chain, dqk, qstride, kstride, dv):
    tq, nk = q.shape[0], k.shape[0]
    scores = []
    for g in range(hkv):
        kg = k[:, g * kstride:g * kstride + dqk]
        for j0 in range(0, grp, chain):
            qs = [q[:, (g * grp + j0 + j) * qstride:(g * grp + j0 + j) * qstride + dqk] for j in range(chain)]
            qg = qs[0] if chain == 1 else jnp.concatenate(qs, axis=0)
            scores.append((g, lax.dot_general(kg, qg, (((1,), (1,)), ((), ())), preferred_element_type=F32)))
    outs = []
    for n, (g, st) in enumerate(scores):
        p = jnp.exp2(st - jnp.max(st, axis=0, keepdims=True))
        ones = jnp.ones((BF16_SUBLANES, nk), BF16)
        if n + 1 < len(scores):
            ones = ones + jnp.tile(_zero_of(scores[n + 1][1][0:1, 0:LANES]), (1, nk // LANES)).astype(BF16)
        vg = jnp.concatenate([vt[g * dv:(g + 1) * dv, :], ones], axis=0)
        ot = _bdot(vg, p.astype(BF16))
        ot = ot[:dv] * (1.0 / ot[dv:dv + 1])
        outs.extend(ot[:, j * tq:(j + 1) * tq] for j in range(chain))
    return jnp.concatenate(outs, axis=0).T


def _block_attn_body(q_ref, k_ref, v_ref, o_ref, *, cfg, transposed, tiles):
    def run(nk_lo, nk_hi):
        if transposed:
            o = _attend_t(q_ref[...], k_ref[nk_lo:nk_hi, :], v_ref[:, nk_lo:nk_hi], **cfg)
        else:
            o = _attend(q_ref[...], k_ref[nk_lo:nk_hi, :], v_ref[nk_lo:nk_hi, :], **cfg)
        o_ref[...] = o.astype(BF16)

    if tiles == "latent":
        run(0, TOK)
    elif tiles == "ctx":
        run(SEQ, TOK)
    else:
        t = pl.program_id(1)

        @pl.when(t < NT_LAT)
        def _():
            run(0, TOK)

        @pl.when(t >= NT_LAT)
        def _():
            run(SEQ, TOK)


def _block_attn(q, k, v, *, cfg, transposed, tiles, name):
    n_tiles = {"latent": NT_LAT, "all": NT_ALL, "ctx": 1}[tiles]
    t0 = NT_LAT if tiles == "ctx" else 0
    wo = cfg["hkv"] * cfg["grp"] * cfg["dv"]
    return pl.pallas_call(
        functools.partial(_block_attn_body, cfg=cfg, transposed=transposed, tiles=tiles),
        grid=(BATCH, n_tiles),
        in_specs=[
            pl.BlockSpec((None, TM, q.shape[-1]), lambda b, t: (b, t + t0, 0)),
            pl.BlockSpec((None, TOK, k.shape[-1]), lambda b, t: (b, 0, 0)),
            pl.BlockSpec((None,) + v.shape[1:], lambda b, t: (b, 0, 0)),
        ],
        out_specs=pl.BlockSpec((None, TM, wo), lambda b, t: (b, t, 0)),
        out_shape=jax.ShapeDtypeStruct((BATCH, n_tiles * TM, wo), BF16),
        compiler_params=_params(2),
        name=name,
    )(q, k, v)


GQA_GRP = GQA_HEADS // GQA_KV_HEADS
GQA_CHAINS = BATCH * NT_LAT * GQA_KV_HEADS
GQA_M = GQA_GRP * TM


def _gqa_pipe_body(q_ref, k_ref, vt_ref, o_ref, s0_scr, s1_scr, m0_scr, m1_scr):
    i = pl.program_id(0)
    s_scr, m_scr = (s0_scr, s1_scr), (m0_scr, m1_scr)

    @pl.when(i == 0)
    def _():
        s1_scr[...] = jnp.zeros(s1_scr.shape, F32)
        m1_scr[...] = jnp.zeros(m1_scr.shape, F32)

    def step(g_prod, g_cons):
        q = q_ref[...]
        qg = jnp.concatenate([q[:, j * HEAD_DIM:(j + 1) * HEAD_DIM] for j in range(GQA_GRP)], axis=0)
        m_cons = m_scr[g_cons][0:1, :]
        ones = jnp.ones((BF16_SUBLANES, KEY_CHUNK), BF16)
        m_new, acc = None, None
        for c in range(TOK // KEY_CHUNK):
            rows = slice(c * KEY_CHUNK, (c + 1) * KEY_CHUNK)
            kg = k_ref[rows, g_prod * HEAD_DIM:(g_prod + 1) * HEAD_DIM]
            st = lax.dot_general(kg, qg, (((1,), (1,)), ((), ())), preferred_element_type=F32)
            s_scr[g_prod][rows, :] = st
            mc = jnp.max(st, axis=0, keepdims=True)
            m_new = mc if m_new is None else jnp.maximum(m_new, mc)

            p = jnp.exp2(s_scr[g_cons][rows, :] - m_cons).astype(BF16)
            vg = jnp.concatenate([vt_ref[g_cons * HEAD_DIM:(g_cons + 1) * HEAD_DIM, rows], ones], axis=0)
            part = _bdot(vg, p)
            acc = part if acc is None else acc + part
        m_scr[g_prod][0:1, :] = m_new
        ot = acc[:HEAD_DIM] * (1.0 / acc[HEAD_DIM:HEAD_DIM + 1])
        o = jnp.concatenate([ot[:, j * TM:(j + 1) * TM] for j in range(GQA_GRP)], axis=0)
        o_ref[...] = o.T.astype(BF16)

    pl.when(i % 2 == 0)(functools.partial(step, 0, 1))
    pl.when(i % 2 == 1)(functools.partial(step, 1, 0))


def _gqa_pipe(q, k, vt):
    per_b = NT_LAT * GQA_KV_HEADS

    def prod(i):
        c = jnp.minimum(i, GQA_CHAINS - 1)
        return c // per_b, (c // GQA_KV_HEADS) % NT_LAT, c % GQA_KV_HEADS

    def cons(i):
        c = jnp.maximum(i - 1, 0)
        return c // per_b, (c // GQA_KV_HEADS) % NT_LAT, c % GQA_KV_HEADS

    wq = GQA_GRP * HEAD_DIM
    return pl.pallas_call(
        _gqa_pipe_body,
        grid=(GQA_CHAINS + 1,),
        in_specs=[
            pl.BlockSpec((None, TM, wq), lambda i: prod(i)),
            pl.BlockSpec((None, TOK, GQA_KV_WIDTH), lambda i: (prod(i)[0], 0, 0)),
            pl.BlockSpec((None, GQA_KV_WIDTH, TOK), lambda i: (cons(i)[0], 0, 0)),
        ],
        out_specs=pl.BlockSpec((None, TM, wq), lambda i: cons(i)),
        out_shape=jax.ShapeDtypeStruct((BATCH, SEQ, GQA_Q_WIDTH), BF16),
        scratch_shapes=[pltpu.VMEM((TOK, GQA_M), F32), pltpu.VMEM((TOK, GQA_M), F32),
                        pltpu.VMEM((8, GQA_M), F32), pltpu.VMEM((8, GQA_M), F32)],
        compiler_params=_params(1),
        name="gqa_pipe",
    )(q, k, vt)


_GQA_CFG = dict(hkv=GQA_KV_HEADS, grp=GQA_HEADS // GQA_KV_HEADS, chain=4, dqk=HEAD_DIM,
                qstride=HEAD_DIM, kstride=HEAD_DIM, dv=HEAD_DIM)
_MLA_CFG = dict(hkv=MLA_HEADS, grp=1, chain=1, dqk=MLA_PAD, qstride=MLA_PAD, kstride=MLA_PAD, dv=MLA_V)
_NA_CFG = dict(hkv=NA_HEADS, grp=1, chain=1, dqk=HEAD_DIM, qstride=HEAD_DIM, kstride=HEAD_DIM, dv=HEAD_DIM)


def _na_start_block(t):
    return jnp.clip(t - 1, 0, NT_LAT - NA_KBLK)


def _na_body(q_ref, k0_ref, k1_ref, k2_ref, kc_ref, v0_ref, v1_ref, v2_ref, vc_ref, bias_ref, o_ref,
             *, with_ctx):
    def latent():
        k = jnp.concatenate([k0_ref[...], k1_ref[...], k2_ref[...], kc_ref[...]], axis=0)
        v = jnp.concatenate([v0_ref[...], v1_ref[...], v2_ref[...], vc_ref[...]], axis=0)
        o_ref[...] = _attend(q_ref[...], k, v, bias_ref=bias_ref, **_NA_CFG).astype(BF16)

    def context():
        o_ref[...] = _attend(q_ref[...], kc_ref[...], vc_ref[...], **_NA_CFG).astype(BF16)

    if not with_ctx:
        latent()
    else:
        t = pl.program_id(0)
        pl.when(t < NT_LAT)(latent)
        pl.when(t >= NT_LAT)(context)


def _na_attn(q, k, v, bias, *, with_ctx):
    n_tiles = NT_ALL if with_ctx else NT_LAT

    def kv_spec(j):
        return pl.BlockSpec((None, TM, NA_WIDTH), lambda t, b: (b, _na_start_block(t) + j, 0))

    ctx_spec = pl.BlockSpec((None, TM, NA_WIDTH), lambda t, b: (b, NT_LAT, 0))
    bias_spec = pl.BlockSpec(
        (NA_HEADS, None, TM, NA_NKEY),
        lambda t, b: (0, jnp.where(t == 0, 0, jnp.where(t == NT_LAT - 1, 2, 1)), 0, 0))
    return pl.pallas_call(
        functools.partial(_na_body, with_ctx=with_ctx),
        grid=(n_tiles, BATCH),
        in_specs=[_tok_spec(NA_WIDTH), kv_spec(0), kv_spec(1), kv_spec(2), ctx_spec,
                  kv_spec(0), kv_spec(1), kv_spec(2), ctx_spec, bias_spec],
        out_specs=_tok_spec(NA_WIDTH),
        out_shape=jax.ShapeDtypeStruct((BATCH, n_tiles * TM, NA_WIDTH), BF16),
        compiler_params=_params(2),
        name="na_attn",
    )(q, k, k, k, k, v, v, v, v, bias)


def _na_bias_table(rel_bias):
    cols = np.arange(GRID_W)
    c0 = np.clip(cols - NA_COLS // 2, 0, GRID_W - NA_COLS)
    col_in = (cols[None, :] >= c0[:, None]) & (cols[None, :] < c0[:, None] + NA_COLS)
    n_dr = 2 * NA_ROWS - 1
    pad = GRID_W - NA_COLS
    rbp = jnp.pad(rel_bias.astype(F32) * LOG2_E, ((0, 0), (0, 0), (pad, pad)), mode="edge")
    tiles = jnp.stack([rbp[:, :, GRID_W - 1 - cq:2 * GRID_W - 1 - cq] for cq in range(GRID_W)], axis=2)
    tiles = jnp.where(col_in[None, None], tiles, NEG_BIG)
    tiles = jnp.concatenate([tiles, jnp.full((NA_HEADS, 1, GRID_W, GRID_W), NEG_BIG, F32)], axis=1)
    dr_blk = np.full((3, NA_QROWS, NA_KROWS), n_dr, np.int32)
    for cls, tile in enumerate((0, 2, NT_LAT - 1)):
        start = int(np.clip(tile - 1, 0, NT_LAT - NA_KBLK)) * NA_QROWS
        for qr in range(NA_QROWS):
            r = tile * NA_QROWS + qr
            r0 = int(np.clip(r - NA_ROWS // 2, 0, ROWS - NA_ROWS))
            for kj in range(NA_KROWS):
                kr = start + kj
                if r0 <= kr < r0 + NA_ROWS:
                    dr_blk[cls, qr, kj] = kr - r + NA_ROWS - 1
    local = jnp.take(tiles, jnp.asarray(dr_blk.reshape(-1)), axis=1)
    local = local.reshape(NA_HEADS, 3, NA_QROWS, NA_KROWS, GRID_W, GRID_W).transpose(0, 1, 2, 4, 3, 5)
    local = local.reshape(NA_HEADS, 3, TM, NA_KROWS * GRID_W)
    return jnp.concatenate([local, jnp.zeros((NA_HEADS, 3, TM, CTX_LEN), F32)], axis=-1)


def _merge_body(h_ref, mod_ref, g_ref, wgate_ref, wa_ref, wb_ref, wm_ref, wo_ref, *rest, split):
    o_ref = rest[-1]
    branch, pos = [], 0
    for has_ctx in split:
        if has_ctx:
            is_ctx = pl.program_id(0) >= NT_LAT
            branch.append(jnp.where(is_ctx, rest[pos + 1][...], rest[pos][...]))
        else:
            branch.append(rest[pos][...])
        pos += 2 if has_ctx else 1
    h = h_ref[...]
    mod = mod_ref[...]
    n = _rmsnorm(h, g_ref[...]) * (1.0 + mod[1:2]) + mod[0:1]
    gates = _sigmoid(_bdot(n.astype(BF16), wgate_ref[...]))
    y = (gates[:, :D_MODEL] * _bdot(branch[0], wa_ref[...])
         + gates[:, D_MODEL:2 * D_MODEL] * _bdot(branch[1], wb_ref[...])
         + gates[:, 2 * D_MODEL:] * _bdot(branch[2], wm_ref[...]))
    o_ref[...] = h + mod[2:3] * _bdot(y.astype(BF16), wo_ref[...])


def _merge(h, mod, g_norm, w_gate, branches, wa, wb, wm, wo, *, n_tiles):
    split = tuple(isinstance(br, tuple) for br in branches)
    arrays, specs = [], []
    for br in branches:
        if isinstance(br, tuple):
            lat, ctx = br
            arrays += [lat, ctx]
            specs += [pl.BlockSpec((None, TM, lat.shape[-1]), lambda t, b: (b, jnp.minimum(t, NT_LAT - 1), 0)),
                      pl.BlockSpec((None, TM, ctx.shape[-1]), lambda t, b: (b, 0, 0))]
        else:
            arrays.append(br)
            specs.append(_tok_spec(br.shape[-1]))
    return pl.pallas_call(
        functools.partial(_merge_body, split=split),
        grid=(n_tiles, BATCH),
        in_specs=[_tok_spec(D_MODEL), _mod_spec(1), _const_spec((1, D_MODEL)),
                  _const_spec((D_MODEL, GATE_WIDTH)),
                  _const_spec((NA_WIDTH, D_MODEL)), _const_spec((GQA_Q_WIDTH, D_MODEL)),
                  _const_spec((MLA_HEADS * MLA_V, D_MODEL)), _const_spec((D_MODEL, D_MODEL))] + specs,
        out_specs=_tok_spec(D_MODEL),
        out_shape=jax.ShapeDtypeStruct((BATCH, n_tiles * TM, D_MODEL), F32),
        compiler_params=_params(2),
        name="gated_merge",
    )(h, mod, g_norm, w_gate, wa, wb, wm, wo, *arrays)


def _rope_tables():
    t = np.arange(SEQ)
    row = (t // GRID_W).astype(np.float32)[:, None]
    col = (t % GRID_W).astype(np.float32)[:, None]

    def head_tables(d_rot):
        half = d_rot // 2
        freqs = jnp.asarray(ROPE_THETA, F32) ** (-jnp.arange(0, half, 2, dtype=F32) / half)
        r, c = jnp.asarray(row) * freqs, jnp.asarray(col) * freqs
        ang = jnp.concatenate([r, r, c, c], axis=-1)
        first = (np.arange(d_rot) % half) < (half // 2)
        cos, sin = jnp.cos(ang), jnp.sin(ang)
        return cos, jnp.where(first, -sin, 0.0), jnp.where(first, 0.0, sin)

    def place(tabs, reps, lo, group):
        out = []
        for k, tab in enumerate(tabs):
            fill = 1.0 if k == 0 else 0.0
            d_rot = tab.shape[-1]
            g = jnp.concatenate([jnp.full((SEQ, lo), fill, F32), tab,
                                 jnp.full((SEQ, group - lo - d_rot), fill, F32)], axis=-1)
            g = jnp.tile(g, (1, reps))
            ctx = jnp.full((CTX_LEN, reps * group), fill, F32)
            out.append(jnp.concatenate([g, ctx], axis=0))
        return out

    tb = head_tables(HEAD_DIM)
    tm = head_tables(MLA_ROPE)
    return (place(tb, GQA_HEADS, 0, HEAD_DIM)
            + place(tm, MLA_HEADS, MLA_NOPE, MLA_PAD)
            + place(tm, 1, 0, LANES))


def _seg_matrix(width):
    idx = np.arange(width) // HEAD_DIM
    return jnp.asarray((idx[:, None] == idx[None, :]).astype(np.float32) / HEAD_DIM, BF16)


def _place_matrix():
    m = np.zeros((LANES, MLA_QK_WIDTH), np.float32)
    for hd in range(MLA_HEADS):
        for j in range(MLA_ROPE):
            m[j, hd * MLA_PAD + MLA_NOPE + j] = 1.0
    return jnp.asarray(m, BF16)


def _pad_heads(w, n_heads, per_head, keep_lo, keep_hi):
    k = w.shape[0]
    wh = w.reshape(k, n_heads, per_head)[:, :, keep_lo:keep_hi]
    wh = jnp.pad(wh, ((0, 0), (0, 0), (0, MLA_PAD - (keep_hi - keep_lo))))
    return wh.reshape(k, n_heads * MLA_PAD)


def kernel(x, c, ctx, c_ctx, w_ada, b_ada, ffn1_norm, ffn1_w_gate, ffn1_w_up, ffn1_w_down, mix_norm, w_in,
           na_rel_bias, gqa_q_norm, gqa_k_norm, mla_q_norm, mla_kv_norm, mla_w_uq, mla_w_ukv,
           w_branch_a, w_branch_b, w_branch_c, w_out, ffn2_norm, ffn2_w_gate, ffn2_w_up, ffn2_w_down,
           final_norm):
    assert x.shape == (BATCH, SEQ, D_MODEL) and ctx.shape == (BATCH, CTX_LEN, D_MODEL)

    c_all = jnp.concatenate([c, c_ctx[None, :], jnp.zeros((MOD_ROWS - BATCH - 1, D_MODEL), F32)], axis=0)
    mod_all = _ada(c_all, w_ada, b_ada).reshape(DEPTH, MOD_ROWS, 3, 3, D_MODEL)

    tables = _rope_tables()
    seg_q, seg_k, place = _seg_matrix(GQA_Q_WIDTH), _seg_matrix(GQA_KV_WIDTH), _place_matrix()
    fin = final_norm.reshape(1, D_MODEL)

    h = jnp.concatenate([x, ctx], axis=1)
    for i in range(DEPTH):
        last = i == DEPTH - 1
        mod = mod_all[i]
        w_qkv = jnp.pad(w_in[i][:, :QKV_WIDTH - 96], ((0, 0), (0, 96))).astype(BF16)
        w_gate = w_in[i][:, QKV_WIDTH - 96:].astype(BF16)
        w_uq = _pad_heads(mla_w_uq[i], MLA_HEADS, MLA_NOPE + MLA_ROPE, 0, MLA_NOPE + MLA_ROPE).astype(BF16)
        w_uk = _pad_heads(mla_w_ukv[i], MLA_HEADS, MLA_NOPE + MLA_V, 0, MLA_NOPE)
        w_uv = mla_w_ukv[i].reshape(MLA_KV_RANK, MLA_HEADS, MLA_NOPE + MLA_V)[:, :, MLA_NOPE:]
        w_ukv = jnp.concatenate([w_uk, w_uv.reshape(MLA_KV_RANK, MLA_HEADS * MLA_V)], axis=1).astype(BF16)
        consts = (seg_q, seg_k,
                  jnp.tile(gqa_q_norm[i], GQA_HEADS)[None, :], jnp.tile(gqa_k_norm[i], GQA_KV_HEADS)[None, :],
                  mla_q_norm[i][None, :], mla_kv_norm[i][None, :], w_uq, w_ukv, place)

        h = _ffn(h, mod, 0, ffn1_norm[i][None, :], ffn1_w_gate[i].astype(BF16), ffn1_w_up[i].astype(BF16),
                 ffn1_w_down[i].astype(BF16), fin, n_tiles=NT_ALL, final=False)
        qa, ka, va, qb, kb, vb, qm, km, vm = _proj(h, mod, mix_norm[i][None, :], w_qkv, consts, tables)

        with_ctx = not last
        oa = _na_attn(qa, ka, va, _na_bias_table(na_rel_bias[i]), with_ctx=with_ctx)
        ob = _gqa_pipe(qb, kb, vb)
        if with_ctx:
            ob = (ob, _block_attn(qb, kb, vb, cfg=_GQA_CFG, transposed=True, tiles="ctx", name="gqa_ctx"))
        om = _block_attn(qm, km, vm, cfg=_MLA_CFG, transposed=False, tiles="all" if with_ctx else "latent",
                         name="mla_attn")

        n_tiles = NT_LAT if last else NT_ALL
        h = _merge(h, mod, mix_norm[i][None, :], w_gate, (oa, ob, om), w_branch_a[i].astype(BF16),
                   w_branch_b[i].astype(BF16), w_branch_c[i].astype(BF16), w_out[i].astype(BF16),
                   n_tiles=n_tiles)
        h = _ffn(h, mod, 2, ffn2_norm[i][None, :], ffn2_w_gate[i].astype(BF16), ffn2_w_up[i].astype(BF16),
                 ffn2_w_down[i].astype(BF16), fin, n_tiles=n_tiles, final=last)
    return h
```

```python
import functools

import numpy as np
import jax
import jax.numpy as jnp
from jax import lax
from jax.experimental import pallas as pl
from jax.experimental.pallas import tpu as pltpu

D_MODEL = 1024
BATCH = 32
SEQ = 2048
DEPTH = 2
CTX_LEN = 256
GRID_W = 64
HEAD_DIM = 64
NA_HEADS = 4
NA_ROWS = 8
NA_COLS = 16
GQA_HEADS = 8
GQA_KV_HEADS = 2
MLA_HEADS = 4
MLA_Q_RANK = 256
MLA_KV_RANK = 128
MLA_NOPE = 64
MLA_ROPE = 32
MLA_V = 64
N_MOD = 9
D_FF = ((8 * D_MODEL // 3 + 127) // 128) * 128
ROPE_THETA = 10000.0
EPS = 1e-6
NEG_BIG = -1e30

NA_WIDTH = NA_HEADS * HEAD_DIM
GQA_Q_WIDTH = GQA_HEADS * HEAD_DIM
GQA_KV_WIDTH = GQA_KV_HEADS * HEAD_DIM
GQA_GRP = GQA_HEADS // GQA_KV_HEADS
MLA_V_WIDTH = MLA_HEADS * MLA_V
GATE_WIDTH = 3 * D_MODEL
ROWS = SEQ // GRID_W

LANES = 128
BF16_SUBLANES = 16
LOG2_E = 1.4426950408889634
TOK = SEQ + CTX_LEN
TM = 256
NT_ALL = TOK // TM
NT_LAT = SEQ // TM
MOD_ROWS = 40
CTX_ROW = BATCH
ADA_BLOCK = 1152
MLA_PAD = LANES
MLA_QK_WIDTH = MLA_HEADS * MLA_PAD
QKV_WIDTH = 2048
NA_QROWS = TM // GRID_W
NA_KROWS = 12
NA_KBLK = NA_KROWS * GRID_W // TM
NA_NKEY = NA_KROWS * GRID_W + CTX_LEN
KEY_CHUNK = 256
CHAINS_PER_TILE = 2
N_CHAINS = BATCH * NT_LAT * CHAINS_PER_TILE
VMEM_LIMIT = 56 * 1024 * 1024

F32 = jnp.float32
BF16 = jnp.bfloat16

_OFF_AQ, _OFF_AK, _OFF_AV = 0, 256, 512
_OFF_BQ, _OFF_BK, _OFF_BV = 768, 1280, 1408
_OFF_CQ, _OFF_CKV, _OFF_CKR = 1536, 1792, 1920


def _params(n_axes):
    return pltpu.CompilerParams(dimension_semantics=("arbitrary",) * n_axes,
                                vmem_limit_bytes=VMEM_LIMIT)


def _const_spec(shape):
    nd = len(shape)
    return pl.BlockSpec(shape, lambda *_: (0,) * nd, pipeline_mode=pl.Buffered(1))


def _mod_spec(sub):
    return pl.BlockSpec((None, None, 3, D_MODEL),
                        lambda t, b: (jnp.where(t >= NT_LAT, CTX_ROW, b), sub, 0, 0))


def _tok_spec(width):
    return pl.BlockSpec((None, TM, width), lambda t, b: (b, t, 0))


def _sigmoid(x):
    return 1.0 / (1.0 + jnp.exp(-x))


def _rmsnorm(x, g):
    return (x * lax.rsqrt(jnp.mean(x * x, axis=-1, keepdims=True) + EPS)) * g


def _bdot(a, b):
    return jnp.dot(a, b, preferred_element_type=F32)


def _dot_nt(a, b):
    return lax.dot_general(a, b, (((1,), (1,)), ((), ())), preferred_element_type=F32)


def _ada_body(c_ref, w_ref, b_ref, o_ref):
    c = c_ref[...]
    s = c * _sigmoid(c)
    o_ref[...] = _bdot(s.astype(BF16), w_ref[...].astype(BF16)) + b_ref[...]


def _ada(c_all, w_ada, b_ada):
    nblk = (N_MOD * D_MODEL) // ADA_BLOCK
    return pl.pallas_call(
        _ada_body,
        grid=(DEPTH, nblk),
        in_specs=[
            pl.BlockSpec((MOD_ROWS, D_MODEL), lambda i, j: (0, 0)),
            pl.BlockSpec((None, D_MODEL, ADA_BLOCK), lambda i, j: (i, 0, j)),
            pl.BlockSpec((None, 1, ADA_BLOCK), lambda i, j: (i, 0, j)),
        ],
        out_specs=pl.BlockSpec((None, MOD_ROWS, ADA_BLOCK), lambda i, j: (i, 0, j)),
        out_shape=jax.ShapeDtypeStruct((DEPTH, MOD_ROWS, N_MOD * D_MODEL), F32),
        compiler_params=_params(2),
        name="ada_mod",
    )(c_all, w_ada, b_ada.reshape(DEPTH, 1, N_MOD * D_MODEL))


def _ffn_body(h_ref, mod_ref, g_ref, wg_ref, wu_ref, wd_ref, fin_ref, o_ref, *, final):
    h = h_ref[...]
    mod = mod_ref[...]
    n = _rmsnorm(h, g_ref[...]) * (1.0 + mod[1:2]) + mod[0:1]
    nb = n.astype(BF16)
    g = _bdot(nb, wg_ref[...])
    u = _bdot(nb, wu_ref[...])
    a = (g * _sigmoid(g)) * u
    d = _bdot(a.astype(BF16), wd_ref[...])
    out = h + 0.5 * mod[2:3] * d
    if final:
        out = _rmsnorm(out, fin_ref[...])
    o_ref[...] = out


def _ffn(h, mod, sub, g_norm, wg, wu, wd, fin, *, n_tiles, final):
    return pl.pallas_call(
        functools.partial(_ffn_body, final=final),
        grid=(n_tiles, BATCH),
        in_specs=[
            _tok_spec(D_MODEL),
            _mod_spec(sub),
            _const_spec((1, D_MODEL)),
            _const_spec((D_MODEL, D_FF)),
            _const_spec((D_MODEL, D_FF)),
            _const_spec((D_FF, D_MODEL)),
            _const_spec((1, D_MODEL)),
        ],
        out_specs=_tok_spec(D_MODEL),
        out_shape=jax.ShapeDtypeStruct((BATCH, n_tiles * TM, D_MODEL), F32),
        compiler_params=_params(2),
        name="half_ffn",
    )(h, mod, g_norm, wg, wu, wd, fin)


def _rope(x, cos, sin_a, sin_b, half):
    w = x.shape[-1]
    return x * cos + pltpu.roll(x, w - half, 1) * sin_a + pltpu.roll(x, half, 1) * sin_b


def _seg_mean_sq(x, seg):
    x2 = x * x
    hi = x2.astype(BF16)
    lo = (x2 - hi.astype(F32)).astype(BF16)
    return _bdot(hi, seg) + _bdot(lo, seg)


def _proj_body(h_ref, mod_ref, g_ref, w_ref, segq_ref, segk_ref, qn_ref, kn_ref, mqn_ref, mkvn_ref,
               wuq_ref, wukv_ref, place_ref,
               cb_ref, sab_ref, sbb_ref, cm_ref, sam_ref, sbm_ref, ck_ref, sak_ref, sbk_ref,
               qa_ref, ka_ref, va_ref, qb_ref, kb_ref, vb_ref, qm_ref, km_ref, vm_ref):
    mod = mod_ref[...]
    n = _rmsnorm(h_ref[...], g_ref[...]) * (1.0 + mod[1:2]) + mod[0:1]
    z = _bdot(n.astype(BF16), w_ref[...])

    scale = LOG2_E * HEAD_DIM ** -0.5
    qa_ref[...] = (z[:, _OFF_AQ:_OFF_AQ + NA_WIDTH] * scale).astype(BF16)
    ka_ref[...] = z[:, _OFF_AK:_OFF_AK + NA_WIDTH].astype(BF16)
    va_ref[...] = z[:, _OFF_AV:_OFF_AV + NA_WIDTH].T.astype(BF16)

    cb, sab, sbb = cb_ref[...], sab_ref[...], sbb_ref[...]
    bq = z[:, _OFF_BQ:_OFF_BQ + GQA_Q_WIDTH]
    bq = (bq * lax.rsqrt(_seg_mean_sq(bq, segq_ref[...]) + EPS)) * qn_ref[...]
    qb_ref[...] = (_rope(bq, cb, sab, sbb, HEAD_DIM // 4) * scale).astype(BF16)
    bk = z[:, _OFF_BK:_OFF_BK + GQA_KV_WIDTH]
    bk = (bk * lax.rsqrt(_seg_mean_sq(bk, segk_ref[...]) + EPS)) * kn_ref[...]
    kb_ref[...] = _rope(bk, cb[:, :GQA_KV_WIDTH], sab[:, :GQA_KV_WIDTH], sbb[:, :GQA_KV_WIDTH],
                        HEAD_DIM // 4).astype(BF16)
    vb_ref[...] = z[:, _OFF_BV:_OFF_BV + GQA_KV_WIDTH].T.astype(BF16)

    mla_scale = LOG2_E * (MLA_NOPE + MLA_ROPE) ** -0.5
    cq = _rmsnorm(z[:, _OFF_CQ:_OFF_CQ + MLA_Q_RANK], mqn_ref[...])
    q_lat = _bdot(cq.astype(BF16), wuq_ref[...])
    qm_ref[...] = (_rope(q_lat, cm_ref[...], sam_ref[...], sbm_ref[...], MLA_ROPE // 4)
                   * mla_scale).astype(BF16)
    ckv = _rmsnorm(z[:, _OFF_CKV:_OFF_CKV + MLA_KV_RANK], mkvn_ref[...])
    kv_lat = _bdot(ckv.astype(BF16), wukv_ref[...])
    vm_ref[...] = kv_lat[:, MLA_QK_WIDTH:].T.astype(BF16)
    kr = _rope(z[:, _OFF_CKR:_OFF_CKR + LANES], ck_ref[...], sak_ref[...], sbk_ref[...],
               MLA_ROPE // 4).astype(BF16)
    km_ref[...] = (kv_lat[:, :MLA_QK_WIDTH] + _bdot(kr, place_ref[...])).astype(BF16)


def _proj(h, mod, g_norm, w_qkv, consts, tables):
    widths = (NA_WIDTH, NA_WIDTH, NA_WIDTH, GQA_Q_WIDTH, GQA_KV_WIDTH, GQA_KV_WIDTH,
              MLA_QK_WIDTH, MLA_QK_WIDTH, MLA_V_WIDTH)
    is_value = [k % 3 == 2 for k in range(len(widths))]
    tab_specs = [pl.BlockSpec((TM, t.shape[-1]), lambda t_, b: (t_, 0)) for t in tables]
    return pl.pallas_call(
        _proj_body,
        grid=(NT_ALL, BATCH),
        in_specs=[_tok_spec(D_MODEL), _mod_spec(1), _const_spec((1, D_MODEL)),
                  _const_spec((D_MODEL, QKV_WIDTH))]
                 + [_const_spec(c.shape) for c in consts] + tab_specs,
        out_specs=[pl.BlockSpec((None, w, TM), lambda t, b: (b, 0, t)) if v else _tok_spec(w)
                   for v, w in zip(is_value, widths)],
        out_shape=[jax.ShapeDtypeStruct((BATCH, w, TOK) if v else (BATCH, TOK, w), BF16)
                   for v, w in zip(is_value, widths)],
        compiler_params=_params(2),
        name="mix_proj",
    )(h, mod, g_norm, w_qkv, *consts, *tables)


def _ctx_attn_body(q_ref, k_ref, vt_ref, o_ref, *, heads, grp, dqk, dv):
    ones = jnp.ones((BF16_SUBLANES, TM), BF16)
    outs = []
    for hd in range(heads):
        g = hd // grp
        st = _dot_nt(k_ref[:, g * dqk:(g + 1) * dqk], q_ref[:, hd * dqk:(hd + 1) * dqk])
        p = jnp.exp2(st - jnp.max(st, axis=0, keepdims=True)).astype(BF16)
        acc = _bdot(jnp.concatenate([vt_ref[g * dv:(g + 1) * dv, :], ones], axis=0), p)
        outs.append(acc[:dv] * (1.0 / acc[dv:dv + 1]))
    o_ref[...] = jnp.concatenate(outs, axis=0).T.astype(BF16)


def _ctx_attn(q, k, vt, *, heads, grp, dqk, dv, name):
    return pl.pallas_call(
        functools.partial(_ctx_attn_body, heads=heads, grp=grp, dqk=dqk, dv=dv),
        grid=(BATCH,),
        in_specs=[pl.BlockSpec((None, TM, heads * dqk), lambda b: (b, NT_LAT, 0)),
                  pl.BlockSpec((None, TM, k.shape[-1]), lambda b: (b, NT_LAT, 0)),
                  pl.BlockSpec((None, vt.shape[1], TM), lambda b: (b, 0, NT_LAT))],
        out_specs=pl.BlockSpec((None, TM, heads * dv), lambda b: (b, 0, 0)),
        out_shape=jax.ShapeDtypeStruct((BATCH, TM, heads * dv), BF16),
        compiler_params=_params(1),
        name=name,
    )(q, k, vt)


def _pipe_step(prod, cons, n_chunk, dv):
    ones = jnp.ones((BF16_SUBLANES, KEY_CHUNK), BF16)
    m_cons = [m_ref[0:1, :] for _, _, m_ref in cons]
    m_new = [None] * len(prod)
    acc = [None] * len(cons)
    for c in range(n_chunk):
        rows = slice(c * KEY_CHUNK, (c + 1) * KEY_CHUNK)
        for d, (qg, k_chunk, bias_chunk, s_ref, _) in enumerate(prod):
            st = _dot_nt(k_chunk(c), qg)
            if bias_chunk is not None:
                st = st + bias_chunk(c)
            s_ref[rows, :] = st
            mc = jnp.max(st, axis=0, keepdims=True)
            m_new[d] = mc if m_new[d] is None else jnp.maximum(m_new[d], mc)
        for d, (vt_chunk, s_ref, _) in enumerate(cons):
            p = jnp.exp2(s_ref[rows, :] - m_cons[d]).astype(BF16)
            part = _bdot(jnp.concatenate([vt_chunk(c), ones], axis=0), p)
            acc[d] = part if acc[d] is None else acc[d] + part
    for d, (_, _, _, _, m_ref) in enumerate(prod):
        m_ref[0:1, :] = m_new[d]
    return [a[:dv] * (1.0 / a[dv:dv + 1]) for a in acc]


def _pipe_run(step, scratch, n_dots):
    i = pl.program_id(0)
    s = [scratch[slot * n_dots:(slot + 1) * n_dots] for slot in range(2)]
    m = [scratch[(2 + slot) * n_dots:(3 + slot) * n_dots] for slot in range(2)]

    @pl.when(i == 0)
    def _():
        for ref in s[1] + m[1]:
            ref[...] = jnp.zeros(ref.shape, F32)

    pl.when(i % 2 == 0)(functools.partial(step, 0, 1, s, m))
    pl.when(i % 2 == 1)(functools.partial(step, 1, 0, s, m))


def _prod_chain(i):
    c = jnp.minimum(i, N_CHAINS - 1)
    return c // (NT_LAT * CHAINS_PER_TILE), (c // CHAINS_PER_TILE) % NT_LAT, c % CHAINS_PER_TILE


def _cons_chain(i):
    c = jnp.maximum(i - 1, 0)
    return c // (NT_LAT * CHAINS_PER_TILE), (c // CHAINS_PER_TILE) % NT_LAT, c % CHAINS_PER_TILE


def _pipe_scratch(nk, m_len, n_dots):
    return ([pltpu.VMEM((nk, m_len), F32)] * (2 * n_dots)
            + [pltpu.VMEM((8, m_len), F32)] * (2 * n_dots))


def _gqa_body(q_ref, k_ref, vt_ref, o_ref, *scratch):
    def step(sp, sc, s, m):
        q = q_ref[...]
        qg = jnp.concatenate([q[:, j * HEAD_DIM:(j + 1) * HEAD_DIM] for j in range(GQA_GRP)], axis=0)

        def k_chunk(c):
            return k_ref[c * KEY_CHUNK:(c + 1) * KEY_CHUNK, sp * HEAD_DIM:(sp + 1) * HEAD_DIM]

        def vt_chunk(c):
            return vt_ref[sc * HEAD_DIM:(sc + 1) * HEAD_DIM, c * KEY_CHUNK:(c + 1) * KEY_CHUNK]

        (ot,) = _pipe_step([(qg, k_chunk, None, s[sp][0], m[sp][0])], [(vt_chunk, s[sc][0], m[sc][0])],
                           TOK // KEY_CHUNK, HEAD_DIM)
        o = jnp.concatenate([ot[:, j * TM:(j + 1) * TM] for j in range(GQA_GRP)], axis=0)
        o_ref[...] = o.T.astype(BF16)

    _pipe_run(step, scratch, 1)


def _gqa_attn(q, k, vt):
    wq = GQA_GRP * HEAD_DIM
    return pl.pallas_call(
        _gqa_body,
        grid=(N_CHAINS + 1,),
        in_specs=[
            pl.BlockSpec((None, TM, wq), lambda i: _prod_chain(i)),
            pl.BlockSpec((None, TOK, GQA_KV_WIDTH), lambda i: (_prod_chain(i)[0], 0, 0)),
            pl.BlockSpec((None, GQA_KV_WIDTH, TOK), lambda i: (_cons_chain(i)[0], 0, 0)),
        ],
        out_specs=pl.BlockSpec((None, TM, wq), lambda i: _cons_chain(i)),
        out_shape=jax.ShapeDtypeStruct((BATCH, SEQ, GQA_Q_WIDTH), BF16),
        scratch_shapes=_pipe_scratch(TOK, GQA_GRP * TM, 1),
        compiler_params=_params(1),
        name="gqa_attn",
    )(q, k, vt)


def _mla_body(q_ref, k_ref, vt_ref, o_ref, *scratch):
    def run(sp, sc, s, m):
        prod, cons = [], []
        for d in range(2):
            def k_chunk(c, d=d):
                return k_ref[c * KEY_CHUNK:(c + 1) * KEY_CHUNK, d * MLA_PAD:(d + 1) * MLA_PAD]

            def vt_chunk(c, d=d):
                return vt_ref[d * MLA_V:(d + 1) * MLA_V, c * KEY_CHUNK:(c + 1) * KEY_CHUNK]

            prod.append((q_ref[:, d * MLA_PAD:(d + 1) * MLA_PAD], k_chunk, None, s[sp][d], m[sp][d]))
            cons.append((vt_chunk, s[sc][d], m[sc][d]))
        outs = _pipe_step(prod, cons, TOK // KEY_CHUNK, MLA_V)
        o_ref[...] = jnp.concatenate(outs, axis=0).T.astype(BF16)

    _pipe_run(run, scratch, 2)


def _mla_attn(q, k, vt):
    return pl.pallas_call(
        _mla_body,
        grid=(N_CHAINS + 1,),
        in_specs=[
            pl.BlockSpec((None, TM, 2 * MLA_PAD), lambda i: _prod_chain(i)),
            pl.BlockSpec((None, TOK, 2 * MLA_PAD), lambda i: (_prod_chain(i)[0], 0, _prod_chain(i)[2])),
            pl.BlockSpec((None, 2 * MLA_V, TOK), lambda i: (_cons_chain(i)[0], _cons_chain(i)[2], 0)),
        ],
        out_specs=pl.BlockSpec((None, TM, 2 * MLA_V), lambda i: _cons_chain(i)),
        out_shape=jax.ShapeDtypeStruct((BATCH, SEQ, MLA_V_WIDTH), BF16),
        scratch_shapes=_pipe_scratch(TOK, TM, 2),
        compiler_params=_params(1),
        name="mla_attn",
    )(q, k, vt)


def _na_start_block(t):
    return jnp.clip(t - 1, 0, NT_LAT - NA_KBLK)


def _na_body(q_ref, k0_ref, k1_ref, k2_ref, kc_ref, v0_ref, v1_ref, v2_ref, vc_ref, bias_ref, o_ref, *scratch):
    k_blocks = (k0_ref, k1_ref, k2_ref, kc_ref)
    v_blocks = (v0_ref, v1_ref, v2_ref, vc_ref)

    def run(sp, sc, s, m):
        prod, cons = [], []
        for d in range(2):
            def k_chunk(c, d=d):
                return k_blocks[c][:, d * HEAD_DIM:(d + 1) * HEAD_DIM]

            def bias_chunk(c, d=d):
                return bias_ref[d, c * KEY_CHUNK:(c + 1) * KEY_CHUNK, :]

            def vt_chunk(c, d=d):
                return v_blocks[c][d * HEAD_DIM:(d + 1) * HEAD_DIM, :]

            prod.append((q_ref[:, d * HEAD_DIM:(d + 1) * HEAD_DIM], k_chunk, bias_chunk, s[sp][d], m[sp][d]))
            cons.append((vt_chunk, s[sc][d], m[sc][d]))
        outs = _pipe_step(prod, cons, NA_NKEY // KEY_CHUNK, HEAD_DIM)
        o_ref[...] = jnp.concatenate(outs, axis=0).T.astype(BF16)

    _pipe_run(run, scratch, 2)


def _na_attn(q, k, vt, bias):
    wpair = 2 * HEAD_DIM

    def k_spec(j):
        def index(i):
            b, t, par = _prod_chain(i)
            return b, (NT_LAT if j == NA_KBLK else _na_start_block(t) + j), par
        return pl.BlockSpec((None, TM, wpair), index)

    def vt_spec(j):
        def index(i):
            b, t, par = _cons_chain(i)
            return b, par, (NT_LAT if j == NA_KBLK else _na_start_block(t) + j)
        return pl.BlockSpec((None, wpair, TM), index)

    def bias_index(i):
        _, t, par = _prod_chain(i)
        return par, jnp.where(t == 0, 0, jnp.where(t == NT_LAT - 1, 2, 1)), 0, 0

    return pl.pallas_call(
        _na_body,
        grid=(N_CHAINS + 1,),
        in_specs=[pl.BlockSpec((None, TM, wpair), lambda i: _prod_chain(i))]
                 + [k_spec(j) for j in range(NA_KBLK + 1)] + [vt_spec(j) for j in range(NA_KBLK + 1)]
                 + [pl.BlockSpec((2, None, NA_NKEY, TM), bias_index)],
        out_specs=pl.BlockSpec((None, TM, wpair), lambda i: _cons_chain(i)),
        out_shape=jax.ShapeDtypeStruct((BATCH, SEQ, NA_WIDTH), BF16),
        scratch_shapes=_pipe_scratch(NA_NKEY, TM, 2),
        compiler_params=_params(1),
        name="na_attn",
    )(q, *([k] * (NA_KBLK + 1)), *([vt] * (NA_KBLK + 1)), bias)


def _na_bias_table(rel_bias):
    cols = np.arange(GRID_W)
    c0 = np.clip(cols - NA_COLS // 2, 0, GRID_W - NA_COLS)
    col_in = (cols[None, :] >= c0[:, None]) & (cols[None, :] < c0[:, None] + NA_COLS)
    n_dr = 2 * NA_ROWS - 1
    pad = GRID_W - NA_COLS
    rbp = jnp.pad(rel_bias.astype(F32) * LOG2_E, ((0, 0), (0, 0), (pad, pad)), mode="edge")
    tiles = jnp.stack([rbp[:, :, GRID_W - 1 - cq:2 * GRID_W - 1 - cq] for cq in range(GRID_W)], axis=2)
    tiles = jnp.where(col_in[None, None], tiles, NEG_BIG)
    tiles = jnp.concatenate([tiles, jnp.full((NA_HEADS, 1, GRID_W, GRID_W), NEG_BIG, F32)], axis=1)
    dr_blk = np.full((3, NA_QROWS, NA_KROWS), n_dr, np.int32)
    for cls, tile in enumerate((0, 2, NT_LAT - 1)):
        start = int(np.clip(tile - 1, 0, NT_LAT - NA_KBLK)) * NA_QROWS
        for qr in range(NA_QROWS):
            r = tile * NA_QROWS + qr
            r0 = int(np.clip(r - NA_ROWS // 2, 0, ROWS - NA_ROWS))
            for kj in range(NA_KROWS):
                kr = start + kj
                if r0 <= kr < r0 + NA_ROWS:
                    dr_blk[cls, qr, kj] = kr - r + NA_ROWS - 1
    local = jnp.take(tiles, jnp.asarray(dr_blk.reshape(-1)), axis=1)
    local = local.reshape(NA_HEADS, 3, NA_QROWS, NA_KROWS, GRID_W, GRID_W).transpose(0, 1, 2, 4, 3, 5)
    local = local.reshape(NA_HEADS, 3, TM, NA_KROWS * GRID_W)
    table = jnp.concatenate([local, jnp.zeros((NA_HEADS, 3, TM, CTX_LEN), F32)], axis=-1)
    return table.transpose(0, 1, 3, 2)


def _merge_body(h_ref, mod_ref, g_ref, wgate_ref, wa_ref, wb_ref, wm_ref, wo_ref, *rest, split):
    o_ref = rest[-1]
    branch, pos = [], 0
    for has_ctx in split:
        if has_ctx:
            is_ctx = pl.program_id(0) >= NT_LAT
            branch.append(jnp.where(is_ctx, rest[pos + 1][...], rest[pos][...]))
        else:
            branch.append(rest[pos][...])
        pos += 2 if has_ctx else 1
    h = h_ref[...]
    mod = mod_ref[...]
    n = _rmsnorm(h, g_ref[...]) * (1.0 + mod[1:2]) + mod[0:1]
    gates = _sigmoid(_bdot(n.astype(BF16), wgate_ref[...]))
    y = (gates[:, :D_MODEL] * _bdot(branch[0], wa_ref[...])
         + gates[:, D_MODEL:2 * D_MODEL] * _bdot(branch[1], wb_ref[...])
         + gates[:, 2 * D_MODEL:] * _bdot(branch[2], wm_ref[...]))
    o_ref[...] = h + mod[2:3] * _bdot(y.astype(BF16), wo_ref[...])


def _merge(h, mod, g_norm, w_gate, branches, wa, wb, wm, wo, *, n_tiles):
    split = tuple(isinstance(br, tuple) for br in branches)
    arrays, specs = [], []
    for br in branches:
        if isinstance(br, tuple):
            lat, ctx = br
            arrays += [lat, ctx]
            specs += [pl.BlockSpec((None, TM, lat.shape[-1]), lambda t, b: (b, jnp.minimum(t, NT_LAT - 1), 0)),
                      pl.BlockSpec((None, TM, ctx.shape[-1]), lambda t, b: (b, 0, 0))]
        else:
            arrays.append(br)
            specs.append(_tok_spec(br.shape[-1]))
    return pl.pallas_call(
        functools.partial(_merge_body, split=split),
        grid=(n_tiles, BATCH),
        in_specs=[_tok_spec(D_MODEL), _mod_spec(1), _const_spec((1, D_MODEL)),
                  _const_spec((D_MODEL, GATE_WIDTH)),
                  _const_spec((NA_WIDTH, D_MODEL)), _const_spec((GQA_Q_WIDTH, D_MODEL)),
                  _const_spec((MLA_V_WIDTH, D_MODEL)), _const_spec((D_MODEL, D_MODEL))] + specs,
        out_specs=_tok_spec(D_MODEL),
        out_shape=jax.ShapeDtypeStruct((BATCH, n_tiles * TM, D_MODEL), F32),
        compiler_params=_params(2),
        name="gated_merge",
    )(h, mod, g_norm, w_gate, wa, wb, wm, wo, *arrays)


def _rope_tables():
    t = np.arange(SEQ)
    row = (t // GRID_W).astype(np.float32)[:, None]
    col = (t % GRID_W).astype(np.float32)[:, None]

    def head_tables(d_rot):
        half = d_rot // 2
        freqs = jnp.asarray(ROPE_THETA, F32) ** (-jnp.arange(0, half, 2, dtype=F32) / half)
        r, c = jnp.asarray(row) * freqs, jnp.asarray(col) * freqs
        ang = jnp.concatenate([r, r, c, c], axis=-1)
        first = (np.arange(d_rot) % half) < (half // 2)
        cos, sin = jnp.cos(ang), jnp.sin(ang)
        return cos, jnp.where(first, -sin, 0.0), jnp.where(first, 0.0, sin)

    def place(tabs, reps, lo, group):
        out = []
        for k, tab in enumerate(tabs):
            fill = 1.0 if k == 0 else 0.0
            d_rot = tab.shape[-1]
            g = jnp.concatenate([jnp.full((SEQ, lo), fill, F32), tab,
                                 jnp.full((SEQ, group - lo - d_rot), fill, F32)], axis=-1)
            g = jnp.tile(g, (1, reps))
            ctx = jnp.full((CTX_LEN, reps * group), fill, F32)
            out.append(jnp.concatenate([g, ctx], axis=0))
        return out

    tb = head_tables(HEAD_DIM)
    tm = head_tables(MLA_ROPE)
    return (place(tb, GQA_HEADS, 0, HEAD_DIM)
            + place(tm, MLA_HEADS, MLA_NOPE, MLA_PAD)
            + place(tm, 1, 0, LANES))


def _seg_matrix(width):
    idx = np.arange(width) // HEAD_DIM
    return jnp.asarray((idx[:, None] == idx[None, :]).astype(np.float32) / HEAD_DIM, BF16)


def _place_matrix():
    m = np.zeros((LANES, MLA_QK_WIDTH), np.float32)
    for hd in range(MLA_HEADS):
        for j in range(MLA_ROPE):
            m[j, hd * MLA_PAD + MLA_NOPE + j] = 1.0
    return jnp.asarray(m, BF16)


def _pad_heads(w, n_heads, per_head, keep_lo, keep_hi):
    k = w.shape[0]
    wh = w.reshape(k, n_heads, per_head)[:, :, keep_lo:keep_hi]
    wh = jnp.pad(wh, ((0, 0), (0, 0), (0, MLA_PAD - (keep_hi - keep_lo))))
    return wh.reshape(k, n_heads * MLA_PAD)


def kernel(x, c, ctx, c_ctx, w_ada, b_ada, ffn1_norm, ffn1_w_gate, ffn1_w_up, ffn1_w_down, mix_norm, w_in,
           na_rel_bias, gqa_q_norm, gqa_k_norm, mla_q_norm, mla_kv_norm, mla_w_uq, mla_w_ukv,
           w_branch_a, w_branch_b, w_branch_c, w_out, ffn2_norm, ffn2_w_gate, ffn2_w_up, ffn2_w_down,
           final_norm):
    assert x.shape == (BATCH, SEQ, D_MODEL) and ctx.shape == (BATCH, CTX_LEN, D_MODEL)

    c_all = jnp.concatenate([c, c_ctx[None, :], jnp.zeros((MOD_ROWS - BATCH - 1, D_MODEL), F32)], axis=0)
    mod_all = _ada(c_all, w_ada, b_ada).reshape(DEPTH, MOD_ROWS, 3, 3, D_MODEL)

    tables = _rope_tables()
    seg_q, seg_k, place = _seg_matrix(GQA_Q_WIDTH), _seg_matrix(GQA_KV_WIDTH), _place_matrix()
    fin = final_norm.reshape(1, D_MODEL)

    h = jnp.concatenate([x, ctx], axis=1)
    for i in range(DEPTH):
        last = i == DEPTH - 1
        mod = mod_all[i]
        w_qkv = jnp.pad(w_in[i][:, :QKV_WIDTH - 96], ((0, 0), (0, 96))).astype(BF16)
        w_gate = w_in[i][:, QKV_WIDTH - 96:].astype(BF16)
        w_uq = _pad_heads(mla_w_uq[i], MLA_HEADS, MLA_NOPE + MLA_ROPE, 0, MLA_NOPE + MLA_ROPE).astype(BF16)
        w_uk = _pad_heads(mla_w_ukv[i], MLA_HEADS, MLA_NOPE + MLA_V, 0, MLA_NOPE)
        w_uv = mla_w_ukv[i].reshape(MLA_KV_RANK, MLA_HEADS, MLA_NOPE + MLA_V)[:, :, MLA_NOPE:]
        w_ukv = jnp.concatenate([w_uk, w_uv.reshape(MLA_KV_RANK, MLA_V_WIDTH)], axis=1).astype(BF16)
        consts = (seg_q, seg_k,
                  jnp.tile(gqa_q_norm[i], GQA_HEADS)[None, :], jnp.tile(gqa_k_norm[i], GQA_KV_HEADS)[None, :],
                  mla_q_norm[i][None, :], mla_kv_norm[i][None, :], w_uq, w_ukv, place)

        h = _ffn(h, mod, 0, ffn1_norm[i][None, :], ffn1_w_gate[i].astype(BF16), ffn1_w_up[i].astype(BF16),
                 ffn1_w_down[i].astype(BF16), fin, n_tiles=NT_ALL, final=False)
        qa, ka, va, qb, kb, vb, qm, km, vm = _proj(h, mod, mix_norm[i][None, :], w_qkv, consts, tables)

        oa = _na_attn(qa, ka, va, _na_bias_table(na_rel_bias[i]))
        ob = _gqa_attn(qb, kb, vb)
        om = _mla_attn(qm, km, vm)
        if not last:
            oa = (oa, _ctx_attn(qa, ka, va, heads=NA_HEADS, grp=1, dqk=HEAD_DIM, dv=HEAD_DIM, name="na_ctx"))
            ob = (ob, _ctx_attn(qb, kb, vb, heads=GQA_HEADS, grp=GQA_GRP, dqk=HEAD_DIM, dv=HEAD_DIM,
                                name="gqa_ctx"))
            om = (om, _ctx_attn(qm, km, vm, heads=MLA_HEADS, grp=1, dqk=MLA_PAD, dv=MLA_V, name="mla_ctx"))

        n_tiles = NT_LAT if last else NT_ALL
        h = _merge(h, mod, mix_norm[i][None, :], w_gate, (oa, ob, om), w_branch_a[i].astype(BF16),
                   w_branch_b[i].astype(BF16), w_branch_c[i].astype(BF16), w_out[i].astype(BF16),
                   n_tiles=n_tiles)
        h = _ffn(h, mod, 2, ffn2_norm[i][None, :], ffn2_w_gate[i].astype(BF16), ffn2_w_up[i].astype(BF16),
                 ffn2_w_down[i].astype(BF16), fin, n_tiles=n_tiles, final=last)
    return h
```

```python
import functools

import numpy as np
import jax
import jax.numpy as jnp
from jax import lax
from jax.experimental import pallas as pl
from jax.experimental.pallas import tpu as pltpu

D_MODEL = 1024
BATCH = 32
SEQ = 2048
DEPTH = 2
CTX_LEN = 256
GRID_W = 64
HEAD_DIM = 64
NA_HEADS = 4
NA_ROWS = 8
NA_COLS = 16
GQA_HEADS = 8
GQA_KV_HEADS = 2
MLA_HEADS = 4
MLA_Q_RANK = 256
MLA_KV_RANK = 128
MLA_NOPE = 64
MLA_ROPE = 32
MLA_V = 64
N_MOD = 9
D_FF = ((8 * D_MODEL // 3 + 127) // 128) * 128
ROPE_THETA = 10000.0
EPS = 1e-6
NEG_BIG = -1e30

NA_WIDTH = NA_HEADS * HEAD_DIM
GQA_Q_WIDTH = GQA_HEADS * HEAD_DIM
GQA_KV_WIDTH = GQA_KV_HEADS * HEAD_DIM
GQA_GRP = GQA_HEADS // GQA_KV_HEADS
MLA_V_WIDTH = MLA_HEADS * MLA_V
GATE_WIDTH = 3 * D_MODEL
ROWS = SEQ // GRID_W

LANES = 128
BF16_SUBLANES = 16
LOG2_E = 1.4426950408889634
TOK = SEQ + CTX_LEN
TM = 256
NT_ALL = TOK // TM
NT_LAT = SEQ // TM
PAIR = 2
MOD_ROWS = 40
CTX_ROW = BATCH
ADA_BLOCK = 1152
MLA_PAD = LANES
MLA_QK_WIDTH = MLA_HEADS * MLA_PAD
QKV_WIDTH = 2048
NA_QROWS = TM // GRID_W
NA_KROWS = 12
NA_KBLK = NA_KROWS * GRID_W // TM
NA_NKEY = NA_KROWS * GRID_W + CTX_LEN
KEY_CHUNK = 256
VMEM_LIMIT = 56 * 1024 * 1024

F32 = jnp.float32
BF16 = jnp.bfloat16

_OFF_AQ, _OFF_AK, _OFF_AV = 0, 256, 512
_OFF_BQ, _OFF_BK, _OFF_BV = 768, 1280, 1408
_OFF_CQ, _OFF_CKV, _OFF_CKR = 1536, 1792, 1920


def _params(n_axes):
    return pltpu.CompilerParams(dimension_semantics=("arbitrary",) * n_axes,
                                vmem_limit_bytes=VMEM_LIMIT)


def _const_spec(shape):
    nd = len(shape)
    return pl.BlockSpec(shape, lambda *_: (0,) * nd, pipeline_mode=pl.Buffered(1))


def _mod_spec(sub):
    return pl.BlockSpec((PAIR, None, 3, D_MODEL),
                        lambda t, b: (jnp.where(t >= NT_LAT, CTX_ROW // PAIR, b), sub, 0, 0))


def _tok_spec(width):
    return pl.BlockSpec((PAIR, TM, width), lambda t, b: (b, t, 0))


def _sigmoid(x):
    return 1.0 / (1.0 + jnp.exp(-x))


def _rmsnorm(x, g):
    return (x * lax.rsqrt(jnp.mean(x * x, axis=-1, keepdims=True) + EPS)) * g


def _bdot(a, b):
    return jnp.dot(a, b, preferred_element_type=F32)


def _dot_nt(a, b):
    return lax.dot_general(a, b, (((1,), (1,)), ((), ())), preferred_element_type=F32)


def _ada_body(c_ref, w_ref, b_ref, o_ref):
    c = c_ref[...]
    s = c * _sigmoid(c)
    o_ref[...] = _bdot(s.astype(BF16), w_ref[...].astype(BF16)) + b_ref[...]


def _ada(c_all, w_ada, b_ada):
    nblk = (N_MOD * D_MODEL) // ADA_BLOCK
    return pl.pallas_call(
        _ada_body,
        grid=(DEPTH, nblk),
        in_specs=[
            pl.BlockSpec((MOD_ROWS, D_MODEL), lambda i, j: (0, 0)),
            pl.BlockSpec((None, D_MODEL, ADA_BLOCK), lambda i, j: (i, 0, j)),
            pl.BlockSpec((None, 1, ADA_BLOCK), lambda i, j: (i, 0, j)),
        ],
        out_specs=pl.BlockSpec((None, MOD_ROWS, ADA_BLOCK), lambda i, j: (i, 0, j)),
        out_shape=jax.ShapeDtypeStruct((DEPTH, MOD_ROWS, N_MOD * D_MODEL), F32),
        compiler_params=_params(2),
        name="ada_mod",
    )(c_all, w_ada, b_ada.reshape(DEPTH, 1, N_MOD * D_MODEL))


def _ffn_body(h_ref, mod_ref, g_ref, wg_ref, wu_ref, wd_ref, fin_ref, o_ref, *, final):
    for r in range(PAIR):
        h = h_ref[r]
        mod = mod_ref[r]
        n = _rmsnorm(h, g_ref[...]) * (1.0 + mod[1:2]) + mod[0:1]
        nb = n.astype(BF16)
        g = _bdot(nb, wg_ref[...])
        u = _bdot(nb, wu_ref[...])
        a = (g * _sigmoid(g)) * u
        d = _bdot(a.astype(BF16), wd_ref[...])
        out = h + 0.5 * mod[2:3] * d
        if final:
            out = _rmsnorm(out, fin_ref[...])
        o_ref[r] = out


def _ffn(h, mod, sub, g_norm, wg, wu, wd, fin, *, n_tiles, final):
    return pl.pallas_call(
        functools.partial(_ffn_body, final=final),
        grid=(n_tiles, BATCH // PAIR),
        in_specs=[
            _tok_spec(D_MODEL),
            _mod_spec(sub),
            _const_spec((1, D_MODEL)),
            _const_spec((D_MODEL, D_FF)),
            _const_spec((D_MODEL, D_FF)),
            _const_spec((D_FF, D_MODEL)),
            _const_spec((1, D_MODEL)),
        ],
        out_specs=_tok_spec(D_MODEL),
        out_shape=jax.ShapeDtypeStruct((BATCH, n_tiles * TM, D_MODEL), F32),
        compiler_params=_params(2),
        name="half_ffn",
    )(h, mod, g_norm, wg, wu, wd, fin)


def _rope(x, cos, sin_a, sin_b, half):
    w = x.shape[-1]
    return x * cos + pltpu.roll(x, w - half, 1) * sin_a + pltpu.roll(x, half, 1) * sin_b


def _seg_mean_sq(x, seg):
    x2 = x * x
    hi = x2.astype(BF16)
    lo = (x2 - hi.astype(F32)).astype(BF16)
    return _bdot(hi, seg) + _bdot(lo, seg)


def _proj_body(h_ref, mod_ref, g_ref, w_ref, segq_ref, segk_ref, qn_ref, kn_ref, mqn_ref, mkvn_ref,
               wuq_ref, wukv_ref, place_ref,
               cb_ref, sab_ref, sbb_ref, cm_ref, sam_ref, sbm_ref, ck_ref, sak_ref, sbk_ref,
               qa_ref, ka_ref, va_ref, qb_ref, kb_ref, vb_ref, qm_ref, km_ref, vm_ref):
    scale = LOG2_E * HEAD_DIM ** -0.5
    mla_scale = LOG2_E * (MLA_NOPE + MLA_ROPE) ** -0.5
    cb, sab, sbb = cb_ref[...], sab_ref[...], sbb_ref[...]
    for r in range(PAIR):
        mod = mod_ref[r]
        n = _rmsnorm(h_ref[r], g_ref[...]) * (1.0 + mod[1:2]) + mod[0:1]
        z = _bdot(n.astype(BF16), w_ref[...])

        qa_ref[r] = (z[:, _OFF_AQ:_OFF_AQ + NA_WIDTH] * scale).astype(BF16)
        ka_ref[r] = z[:, _OFF_AK:_OFF_AK + NA_WIDTH].astype(BF16)
        va_ref[r] = z[:, _OFF_AV:_OFF_AV + NA_WIDTH].T.astype(BF16)

        bq = z[:, _OFF_BQ:_OFF_BQ + GQA_Q_WIDTH]
        bq = (bq * lax.rsqrt(_seg_mean_sq(bq, segq_ref[...]) + EPS)) * qn_ref[...]
        qb_ref[r] = (_rope(bq, cb, sab, sbb, HEAD_DIM // 4) * scale).astype(BF16)
        bk = z[:, _OFF_BK:_OFF_BK + GQA_KV_WIDTH]
        bk = (bk * lax.rsqrt(_seg_mean_sq(bk, segk_ref[...]) + EPS)) * kn_ref[...]
        kb_ref[r] = _rope(bk, cb[:, :GQA_KV_WIDTH], sab[:, :GQA_KV_WIDTH], sbb[:, :GQA_KV_WIDTH],
                          HEAD_DIM // 4).astype(BF16)
        vb_ref[r] = z[:, _OFF_BV:_OFF_BV + GQA_KV_WIDTH].T.astype(BF16)

        cq = _rmsnorm(z[:, _OFF_CQ:_OFF_CQ + MLA_Q_RANK], mqn_ref[...])
        q_lat = _bdot(cq.astype(BF16), wuq_ref[...])
        qm_ref[r] = (_rope(q_lat, cm_ref[...], sam_ref[...], sbm_ref[...], MLA_ROPE // 4)
                     * mla_scale).astype(BF16)
        ckv = _rmsnorm(z[:, _OFF_CKV:_OFF_CKV + MLA_KV_RANK], mkvn_ref[...])
        kv_lat = _bdot(ckv.astype(BF16), wukv_ref[...])
        vm_ref[r] = kv_lat[:, MLA_QK_WIDTH:].T.astype(BF16)
        kr = _rope(z[:, _OFF_CKR:_OFF_CKR + LANES], ck_ref[...], sak_ref[...], sbk_ref[...],
                   MLA_ROPE // 4).astype(BF16)
        km_ref[r] = (kv_lat[:, :MLA_QK_WIDTH] + _bdot(kr, place_ref[...])).astype(BF16)


def _proj(h, mod, g_norm, w_qkv, consts, tables):
    widths = (NA_WIDTH, NA_WIDTH, NA_WIDTH, GQA_Q_WIDTH, GQA_KV_WIDTH, GQA_KV_WIDTH,
              MLA_QK_WIDTH, MLA_QK_WIDTH, MLA_V_WIDTH)
    is_value = [k % 3 == 2 for k in range(len(widths))]
    tab_specs = [pl.BlockSpec((TM, t.shape[-1]), lambda t_, b: (t_, 0)) for t in tables]
    return pl.pallas_call(
        _proj_body,
        grid=(NT_ALL, BATCH // PAIR),
        in_specs=[_tok_spec(D_MODEL), _mod_spec(1), _const_spec((1, D_MODEL)),
                  _const_spec((D_MODEL, QKV_WIDTH))]
                 + [_const_spec(c.shape) for c in consts] + tab_specs,
        out_specs=[pl.BlockSpec((PAIR, w, TM), lambda t, b: (b, 0, t)) if v else _tok_spec(w)
                   for v, w in zip(is_value, widths)],
        out_shape=[jax.ShapeDtypeStruct((BATCH, w, TOK) if v else (BATCH, TOK, w), BF16)
                   for v, w in zip(is_value, widths)],
        compiler_params=_params(2),
        name="mix_proj",
    )(h, mod, g_norm, w_qkv, *consts, *tables)


def _ctx_attn_body(q_ref, k_ref, vt_ref, o_ref, *, heads, grp, dqk, dv):
    ones = jnp.ones((BF16_SUBLANES, TM), BF16)
    outs = []
    for hd in range(heads):
        g = hd // grp
        st = _dot_nt(k_ref[:, g * dqk:(g + 1) * dqk], q_ref[:, hd * dqk:(hd + 1) * dqk])
        p = jnp.exp2(st - jnp.max(st, axis=0, keepdims=True)).astype(BF16)
        acc = _bdot(jnp.concatenate([vt_ref[g * dv:(g + 1) * dv, :], ones], axis=0), p)
        outs.append(acc[:dv] * (1.0 / acc[dv:dv + 1]))
    o_ref[...] = jnp.concatenate(outs, axis=0).T.astype(BF16)


def _ctx_attn(q, k, vt, *, heads, grp, dqk, dv, name):
    return pl.pallas_call(
        functools.partial(_ctx_attn_body, heads=heads, grp=grp, dqk=dqk, dv=dv),
        grid=(BATCH,),
        in_specs=[pl.BlockSpec((None, TM, heads * dqk), lambda b: (b, NT_LAT, 0)),
                  pl.BlockSpec((None, TM, k.shape[-1]), lambda b: (b, NT_LAT, 0)),
                  pl.BlockSpec((None, vt.shape[1], TM), lambda b: (b, 0, NT_LAT))],
        out_specs=pl.BlockSpec((None, TM, heads * dv), lambda b: (b, 0, 0)),
        out_shape=jax.ShapeDtypeStruct((BATCH, TM, heads * dv), BF16),
        compiler_params=_params(1),
        name=name,
    )(q, k, vt)


def _pipe_step(prod, cons, n_chunk, dv):
    ones = jnp.ones((BF16_SUBLANES, KEY_CHUNK), BF16)
    m_cons = [m_ref[0:1, :] for _, _, m_ref in cons]
    m_new = [None] * len(prod)
    acc = [None] * len(cons)
    for c in range(n_chunk):
        rows = slice(c * KEY_CHUNK, (c + 1) * KEY_CHUNK)
        for d, (qg, k_chunk, bias_chunk, s_ref, _) in enumerate(prod):
            st = _dot_nt(k_chunk(c), qg)
            if bias_chunk is not None:
                st = st + bias_chunk(c)
            s_ref[rows, :] = st
            mc = jnp.max(st, axis=0, keepdims=True)
            m_new[d] = mc if m_new[d] is None else jnp.maximum(m_new[d], mc)
        for d, (vt_chunk, s_ref, _) in enumerate(cons):
            p = jnp.exp2(s_ref[rows, :] - m_cons[d]).astype(BF16)
            part = _bdot(jnp.concatenate([vt_chunk(c), ones], axis=0), p)
            acc[d] = part if acc[d] is None else acc[d] + part
    for d, (_, _, _, _, m_ref) in enumerate(prod):
        m_ref[0:1, :] = m_new[d]
    return [a[:dv] * (1.0 / a[dv:dv + 1]) for a in acc]


def _pipe_run(step, scratch, n_dots):
    i = pl.program_id(0)
    s = [scratch[slot * n_dots:(slot + 1) * n_dots] for slot in range(2)]
    m = [scratch[(2 + slot) * n_dots:(3 + slot) * n_dots] for slot in range(2)]

    @pl.when(i == 0)
    def _():
        for ref in s[1] + m[1]:
            ref[...] = jnp.zeros(ref.shape, F32)

    pl.when(i % 2 == 0)(functools.partial(step, 0, 1, s, m))
    pl.when(i % 2 == 1)(functools.partial(step, 1, 0, s, m))


def _n_chains(cpt):
    return BATCH * NT_LAT * cpt


def _prod_chain(i, cpt):
    c = jnp.minimum(i, _n_chains(cpt) - 1)
    return c // (NT_LAT * cpt), (c // cpt) % NT_LAT, c % cpt


def _cons_chain(i, cpt):
    c = jnp.maximum(i - 1, 0)
    return c // (NT_LAT * cpt), (c // cpt) % NT_LAT, c % cpt


def _pipe_scratch(nk, m_len, n_dots):
    return ([pltpu.VMEM((nk, m_len), F32)] * (2 * n_dots)
            + [pltpu.VMEM((8, m_len), F32)] * (2 * n_dots))


def _gqa_body(q_ref, k_ref, vt_ref, o_ref, *scratch):
    def step(sp, sc, s, m):
        q = q_ref[...]
        qg = jnp.concatenate([q[:, j * HEAD_DIM:(j + 1) * HEAD_DIM] for j in range(GQA_GRP)], axis=0)

        def k_chunk(c):
            return k_ref[c * KEY_CHUNK:(c + 1) * KEY_CHUNK, sp * HEAD_DIM:(sp + 1) * HEAD_DIM]

        def vt_chunk(c):
            return vt_ref[sc * HEAD_DIM:(sc + 1) * HEAD_DIM, c * KEY_CHUNK:(c + 1) * KEY_CHUNK]

        (ot,) = _pipe_step([(qg, k_chunk, None, s[sp][0], m[sp][0])], [(vt_chunk, s[sc][0], m[sc][0])],
                           TOK // KEY_CHUNK, HEAD_DIM)
        o = jnp.concatenate([ot[:, j * TM:(j + 1) * TM] for j in range(GQA_GRP)], axis=0)
        o_ref[...] = o.T.astype(BF16)

    _pipe_run(step, scratch, 1)


def _gqa_attn(q, k, vt):
    wq = GQA_GRP * HEAD_DIM
    return pl.pallas_call(
        _gqa_body,
        grid=(_n_chains(GQA_KV_HEADS) + 1,),
        in_specs=[
            pl.BlockSpec((None, TM, wq), lambda i: _prod_chain(i, GQA_KV_HEADS)),
            pl.BlockSpec((None, TOK, GQA_KV_WIDTH), lambda i: (_prod_chain(i, GQA_KV_HEADS)[0], 0, 0)),
            pl.BlockSpec((None, GQA_KV_WIDTH, TOK), lambda i: (_cons_chain(i, GQA_KV_HEADS)[0], 0, 0)),
        ],
        out_specs=pl.BlockSpec((None, TM, wq), lambda i: _cons_chain(i, GQA_KV_HEADS)),
        out_shape=jax.ShapeDtypeStruct((BATCH, SEQ, GQA_Q_WIDTH), BF16),
        scratch_shapes=_pipe_scratch(TOK, GQA_GRP * TM, 1),
        compiler_params=_params(1),
        name="gqa_attn",
    )(q, k, vt)


def _mla_body(q_ref, k_ref, vt_ref, o_ref, *scratch):
    def run(sp, sc, s, m):
        prod, cons = [], []
        for d in range(MLA_HEADS):
            def k_chunk(c, d=d):
                return k_ref[c * KEY_CHUNK:(c + 1) * KEY_CHUNK, d * MLA_PAD:(d + 1) * MLA_PAD]

            def vt_chunk(c, d=d):
                return vt_ref[d * MLA_V:(d + 1) * MLA_V, c * KEY_CHUNK:(c + 1) * KEY_CHUNK]

            prod.append((q_ref[:, d * MLA_PAD:(d + 1) * MLA_PAD], k_chunk, None, s[sp][d], m[sp][d]))
            cons.append((vt_chunk, s[sc][d], m[sc][d]))
        outs = _pipe_step(prod, cons, TOK // KEY_CHUNK, MLA_V)
        o_ref[...] = jnp.concatenate(outs, axis=0).T.astype(BF16)

    _pipe_run(run, scratch, MLA_HEADS)


def _mla_attn(q, k, vt):
    return pl.pallas_call(
        _mla_body,
        grid=(_n_chains(1) + 1,),
        in_specs=[
            pl.BlockSpec((None, TM, MLA_QK_WIDTH), lambda i: _prod_chain(i, 1)),
            pl.BlockSpec((None, TOK, MLA_QK_WIDTH), lambda i: (_prod_chain(i, 1)[0], 0, 0)),
            pl.BlockSpec((None, MLA_V_WIDTH, TOK), lambda i: (_cons_chain(i, 1)[0], 0, 0)),
        ],
        out_specs=pl.BlockSpec((None, TM, MLA_V_WIDTH), lambda i: _cons_chain(i, 1)),
        out_shape=jax.ShapeDtypeStruct((BATCH, SEQ, MLA_V_WIDTH), BF16),
        scratch_shapes=_pipe_scratch(TOK, TM, MLA_HEADS),
        compiler_params=_params(1),
        name="mla_attn",
    )(q, k, vt)


def _na_start_block(t):
    return jnp.clip(t - 1, 0, NT_LAT - NA_KBLK)


def _na_body(q_ref, k0_ref, k1_ref, k2_ref, kc_ref, v0_ref, v1_ref, v2_ref, vc_ref, bias_ref, o_ref, *scratch):
    k_blocks = (k0_ref, k1_ref, k2_ref, kc_ref)
    v_blocks = (v0_ref, v1_ref, v2_ref, vc_ref)

    def run(sp, sc, s, m):
        prod, cons = [], []
        for d in range(NA_HEADS):
            def k_chunk(c, d=d):
                return k_blocks[c][:, d * HEAD_DIM:(d + 1) * HEAD_DIM]

            def bias_chunk(c, d=d):
                return bias_ref[d, c * KEY_CHUNK:(c + 1) * KEY_CHUNK, :]

            def vt_chunk(c, d=d):
                return v_blocks[c][d * HEAD_DIM:(d + 1) * HEAD_DIM, :]

            prod.append((q_ref[:, d * HEAD_DIM:(d + 1) * HEAD_DIM], k_chunk, bias_chunk, s[sp][d], m[sp][d]))
            cons.append((vt_chunk, s[sc][d], m[sc][d]))
        outs = _pipe_step(prod, cons, NA_NKEY // KEY_CHUNK, HEAD_DIM)
        o_ref[...] = jnp.concatenate(outs, axis=0).T.astype(BF16)

    _pipe_run(run, scratch, NA_HEADS)


def _na_attn(q, k, vt, bias):
    def k_spec(j):
        def index(i):
            b, t, _ = _prod_chain(i, 1)
            return b, (NT_LAT if j == NA_KBLK else _na_start_block(t) + j), 0
        return pl.BlockSpec((None, TM, NA_WIDTH), index)

    def vt_spec(j):
        def index(i):
            b, t, _ = _cons_chain(i, 1)
            return b, 0, (NT_LAT if j == NA_KBLK else _na_start_block(t) + j)
        return pl.BlockSpec((None, NA_WIDTH, TM), index)

    def bias_index(i):
        t = _prod_chain(i, 1)[1]
        return 0, jnp.where(t == 0, 0, jnp.where(t == NT_LAT - 1, 2, 1)), 0, 0

    return pl.pallas_call(
        _na_body,
        grid=(_n_chains(1) + 1,),
        in_specs=[pl.BlockSpec((None, TM, NA_WIDTH), lambda i: _prod_chain(i, 1))]
                 + [k_spec(j) for j in range(NA_KBLK + 1)] + [vt_spec(j) for j in range(NA_KBLK + 1)]
                 + [pl.BlockSpec((NA_HEADS, None, NA_NKEY, TM), bias_index)],
        out_specs=pl.BlockSpec((None, TM, NA_WIDTH), lambda i: _cons_chain(i, 1)),
        out_shape=jax.ShapeDtypeStruct((BATCH, SEQ, NA_WIDTH), BF16),
        scratch_shapes=_pipe_scratch(NA_NKEY, TM, NA_HEADS),
        compiler_params=_params(1),
        name="na_attn",
    )(q, *([k] * (NA_KBLK + 1)), *([vt] * (NA_KBLK + 1)), bias)


def _na_bias_table(rel_bias):
    cols = np.arange(GRID_W)
    c0 = np.clip(cols - NA_COLS // 2, 0, GRID_W - NA_COLS)
    col_in = (cols[None, :] >= c0[:, None]) & (cols[None, :] < c0[:, None] + NA_COLS)
    n_dr = 2 * NA_ROWS - 1
    pad = GRID_W - NA_COLS
    rbp = jnp.pad(rel_bias.astype(F32) * LOG2_E, ((0, 0), (0, 0), (pad, pad)), mode="edge")
    tiles = jnp.stack([rbp[:, :, GRID_W - 1 - cq:2 * GRID_W - 1 - cq] for cq in range(GRID_W)], axis=2)
    tiles = jnp.where(col_in[None, None], tiles, NEG_BIG)
    tiles = jnp.concatenate([tiles, jnp.full((NA_HEADS, 1, GRID_W, GRID_W), NEG_BIG, F32)], axis=1)
    dr_blk = np.full((3, NA_QROWS, NA_KROWS), n_dr, np.int32)
    for cls, tile in enumerate((0, 2, NT_LAT - 1)):
        start = int(np.clip(tile - 1, 0, NT_LAT - NA_KBLK)) * NA_QROWS
        for qr in range(NA_QROWS):
            r = tile * NA_QROWS + qr
            r0 = int(np.clip(r - NA_ROWS // 2, 0, ROWS - NA_ROWS))
            for kj in range(NA_KROWS):
                kr = start + kj
                if r0 <= kr < r0 + NA_ROWS:
                    dr_blk[cls, qr, kj] = kr - r + NA_ROWS - 1
    local = jnp.take(tiles, jnp.asarray(dr_blk.reshape(-1)), axis=1)
    local = local.reshape(NA_HEADS, 3, NA_QROWS, NA_KROWS, GRID_W, GRID_W).transpose(0, 1, 2, 4, 3, 5)
    local = local.reshape(NA_HEADS, 3, TM, NA_KROWS * GRID_W)
    table = jnp.concatenate([local, jnp.zeros((NA_HEADS, 3, TM, CTX_LEN), F32)], axis=-1)
    return table.transpose(0, 1, 3, 2)


def _merge_body(h_ref, mod_ref, g_ref, wgate_ref, wa_ref, wb_ref, wm_ref, wo_ref, *rest, split):
    o_ref = rest[-1]
    for r in range(PAIR):
        branch, pos = [], 0
        for has_ctx in split:
            if has_ctx:
                is_ctx = pl.program_id(0) >= NT_LAT
                branch.append(jnp.where(is_ctx, rest[pos + 1][r], rest[pos][r]))
            else:
                branch.append(rest[pos][r])
            pos += 2 if has_ctx else 1
        h = h_ref[r]
        mod = mod_ref[r]
        n = _rmsnorm(h, g_ref[...]) * (1.0 + mod[1:2]) + mod[0:1]
        gates = _sigmoid(_bdot(n.astype(BF16), wgate_ref[...]))
        y = (gates[:, :D_MODEL] * _bdot(branch[0], wa_ref[...])
             + gates[:, D_MODEL:2 * D_MODEL] * _bdot(branch[1], wb_ref[...])
             + gates[:, 2 * D_MODEL:] * _bdot(branch[2], wm_ref[...]))
        o_ref[r] = h + mod[2:3] * _bdot(y.astype(BF16), wo_ref[...])


def _merge(h, mod, g_norm, w_gate, branches, wa, wb, wm, wo, *, n_tiles):
    split = tuple(isinstance(br, tuple) for br in branches)
    arrays, specs = [], []
    for br in branches:
        if isinstance(br, tuple):
            lat, ctx = br
            arrays += [lat, ctx]
            specs += [pl.BlockSpec((PAIR, TM, lat.shape[-1]), lambda t, b: (b, jnp.minimum(t, NT_LAT - 1), 0)),
                      pl.BlockSpec((PAIR, TM, ctx.shape[-1]), lambda t, b: (b, 0, 0))]
        else:
            arrays.append(br)
            specs.append(_tok_spec(br.shape[-1]))
    return pl.pallas_call(
        functools.partial(_merge_body, split=split),
        grid=(n_tiles, BATCH // PAIR),
        in_specs=[_tok_spec(D_MODEL), _mod_spec(1), _const_spec((1, D_MODEL)),
                  _const_spec((D_MODEL, GATE_WIDTH)),
                  _const_spec((NA_WIDTH, D_MODEL)), _const_spec((GQA_Q_WIDTH, D_MODEL)),
                  _const_spec((MLA_V_WIDTH, D_MODEL)), _const_spec((D_MODEL, D_MODEL))] + specs,
        out_specs=_tok_spec(D_MODEL),
        out_shape=jax.ShapeDtypeStruct((BATCH, n_tiles * TM, D_MODEL), F32),
        compiler_params=_params(2),
        name="gated_merge",
    )(h, mod, g_norm, w_gate, wa, wb, wm, wo, *arrays)


def _rope_tables():
    t = np.arange(SEQ)
    row = (t // GRID_W).astype(np.float32)[:, None]
    col = (t % GRID_W).astype(np.float32)[:, None]

    def head_tables(d_rot):
        half = d_rot // 2
        freqs = jnp.asarray(ROPE_THETA, F32) ** (-jnp.arange(0, half, 2, dtype=F32) / half)
        r, c = jnp.asarray(row) * freqs, jnp.asarray(col) * freqs
        ang = jnp.concatenate([r, r, c, c], axis=-1)
        first = (np.arange(d_rot) % half) < (half // 2)
        cos, sin = jnp.cos(ang), jnp.sin(ang)
        return cos, jnp.where(first, -sin, 0.0), jnp.where(first, 0.0, sin)

    def place(tabs, reps, lo, group):
        out = []
        for k, tab in enumerate(tabs):
            fill = 1.0 if k == 0 else 0.0
            d_rot = tab.shape[-1]
            g = jnp.concatenate([jnp.full((SEQ, lo), fill, F32), tab,
                                 jnp.full((SEQ, group - lo - d_rot), fill, F32)], axis=-1)
            g = jnp.tile(g, (1, reps))
            ctx = jnp.full((CTX_LEN, reps * group), fill, F32)
            out.append(jnp.concatenate([g, ctx], axis=0))
        return out

    tb = head_tables(HEAD_DIM)
    tm = head_tables(MLA_ROPE)
    return (place(tb, GQA_HEADS, 0, HEAD_DIM)
            + place(tm, MLA_HEADS, MLA_NOPE, MLA_PAD)
            + place(tm, 1, 0, LANES))


def _seg_matrix(width):
    idx = np.arange(width) // HEAD_DIM
    return jnp.asarray((idx[:, None] == idx[None, :]).astype(np.float32) / HEAD_DIM, BF16)


def _place_matrix():
    m = np.zeros((LANES, MLA_QK_WIDTH), np.float32)
    for hd in range(MLA_HEADS):
        for j in range(MLA_ROPE):
            m[j, hd * MLA_PAD + MLA_NOPE + j] = 1.0
    return jnp.asarray(m, BF16)


def _pad_heads(w, n_heads, per_head, keep_lo, keep_hi):
    k = w.shape[0]
    wh = w.reshape(k, n_heads, per_head)[:, :, keep_lo:keep_hi]
    wh = jnp.pad(wh, ((0, 0), (0, 0), (0, MLA_PAD - (keep_hi - keep_lo))))
    return wh.reshape(k, n_heads * MLA_PAD)


def kernel(x, c, ctx, c_ctx, w_ada, b_ada, ffn1_norm, ffn1_w_gate, ffn1_w_up, ffn1_w_down, mix_norm, w_in,
           na_rel_bias, gqa_q_norm, gqa_k_norm, mla_q_norm, mla_kv_norm, mla_w_uq, mla_w_ukv,
           w_branch_a, w_branch_b, w_branch_c, w_out, ffn2_norm, ffn2_w_gate, ffn2_w_up, ffn2_w_down,
           final_norm):
    assert x.shape == (BATCH, SEQ, D_MODEL) and ctx.shape == (BATCH, CTX_LEN, D_MODEL)

    c_all = jnp.concatenate([c, jnp.tile(c_ctx[None, :], (PAIR, 1)),
                             jnp.zeros((MOD_ROWS - BATCH - PAIR, D_MODEL), F32)], axis=0)
    mod_all = _ada(c_all, w_ada, b_ada).reshape(DEPTH, MOD_ROWS, 3, 3, D_MODEL)

    tables = _rope_tables()
    seg_q, seg_k, place = _seg_matrix(GQA_Q_WIDTH), _seg_matrix(GQA_KV_WIDTH), _place_matrix()
    fin = final_norm.reshape(1, D_MODEL)

    h = jnp.concatenate([x, ctx], axis=1)
    for i in range(DEPTH):
        last = i == DEPTH - 1
        mod = mod_all[i]
        w_qkv = jnp.pad(w_in[i][:, :QKV_WIDTH - 96], ((0, 0), (0, 96))).astype(BF16)
        w_gate = w_in[i][:, QKV_WIDTH - 96:].astype(BF16)
        w_uq = _pad_heads(mla_w_uq[i], MLA_HEADS, MLA_NOPE + MLA_ROPE, 0, MLA_NOPE + MLA_ROPE).astype(BF16)
        w_uk = _pad_heads(mla_w_ukv[i], MLA_HEADS, MLA_NOPE + MLA_V, 0, MLA_NOPE)
        w_uv = mla_w_ukv[i].reshape(MLA_KV_RANK, MLA_HEADS, MLA_NOPE + MLA_V)[:, :, MLA_NOPE:]
        w_ukv = jnp.concatenate([w_uk, w_uv.reshape(MLA_KV_RANK, MLA_V_WIDTH)], axis=1).astype(BF16)
        consts = (seg_q, seg_k,
                  jnp.tile(gqa_q_norm[i], GQA_HEADS)[None, :], jnp.tile(gqa_k_norm[i], GQA_KV_HEADS)[None, :],
                  mla_q_norm[i][None, :], mla_kv_norm[i][None, :], w_uq, w_ukv, place)

        h = _ffn(h, mod, 0, ffn1_norm[i][None, :], ffn1_w_gate[i].astype(BF16), ffn1_w_up[i].astype(BF16),
                 ffn1_w_down[i].astype(BF16), fin, n_tiles=NT_ALL, final=False)
        qa, ka, va, qb, kb, vb, qm, km, vm = _proj(h, mod, mix_norm[i][None, :], w_qkv, consts, tables)

        oa = _na_attn(qa, ka, va, _na_bias_table(na_rel_bias[i]))
        ob = _gqa_attn(qb, kb, vb)
        om = _mla_attn(qm, km, vm)
        if not last:
            oa = (oa, _ctx_attn(qa, ka, va, heads=NA_HEADS, grp=1, dqk=HEAD_DIM, dv=HEAD_DIM, name="na_ctx"))
            ob = (ob, _ctx_attn(qb, kb, vb, heads=GQA_HEADS, grp=GQA_GRP, dqk=HEAD_DIM, dv=HEAD_DIM,
                                name="gqa_ctx"))
            om = (om, _ctx_attn(qm, km, vm, heads=MLA_HEADS, grp=1, dqk=MLA_PAD, dv=MLA_V, name="mla_ctx"))

        n_tiles = NT_LAT if last else NT_ALL
        h = _merge(h, mod, mix_norm[i][None, :], w_gate, (oa, ob, om), w_branch_a[i].astype(BF16),
                   w_branch_b[i].astype(BF16), w_branch_c[i].astype(BF16), w_out[i].astype(BF16),
                   n_tiles=n_tiles)
        h = _ffn(h, mod, 2, ffn2_norm[i][None, :], ffn2_w_gate[i].astype(BF16), ffn2_w_up[i].astype(BF16),
                 ffn2_w_down[i].astype(BF16), fin, n_tiles=n_tiles, final=last)
    return h
```

```python
import functools

import numpy as np
import jax
import jax.numpy as jnp
from jax import lax
from jax.experimental import pallas as pl
from jax.experimental.pallas import tpu as pltpu

D_MODEL = 1024
BATCH = 32
SEQ = 2048
DEPTH = 2
CTX_LEN = 256
GRID_W = 64
HEAD_DIM = 64
NA_HEADS = 4
NA_ROWS = 8
NA_COLS = 16
GQA_HEADS = 8
GQA_KV_HEADS = 2
MLA_HEADS = 4
MLA_Q_RANK = 256
MLA_KV_RANK = 128
MLA_NOPE = 64
MLA_ROPE = 32
MLA_V = 64
N_MOD = 9
D_FF = ((8 * D_MODEL // 3 + 127) // 128) * 128
ROPE_THETA = 10000.0
EPS = 1e-6
NEG_BIG = -1e30

NA_WIDTH = NA_HEADS * HEAD_DIM
GQA_Q_WIDTH = GQA_HEADS * HEAD_DIM
GQA_KV_WIDTH = GQA_KV_HEADS * HEAD_DIM
GQA_GRP = GQA_HEADS // GQA_KV_HEADS
MLA_V_WIDTH = MLA_HEADS * MLA_V
GATE_WIDTH = 3 * D_MODEL
ROWS = SEQ // GRID_W

LANES = 128
BF16_SUBLANES = 16
LOG2_E = 1.4426950408889634
TOK = SEQ + CTX_LEN
TM = 256
NT_ALL = TOK // TM
NT_LAT = SEQ // TM
BGROUP = 4
MOD_ROWS = 40
CTX_ROW = BATCH
ADA_BLOCK = 1152
MLA_PAD = LANES
MLA_QK_WIDTH = MLA_HEADS * MLA_PAD
QKV_WIDTH = 2048
NA_QROWS = TM // GRID_W
NA_KROWS = 12
NA_KBLK = NA_KROWS * GRID_W // TM
NA_NKEY = NA_KROWS * GRID_W + CTX_LEN
KEY_CHUNK = 256
VMEM_LIMIT = 56 * 1024 * 1024

F32 = jnp.float32
BF16 = jnp.bfloat16

_OFF_AQ, _OFF_AK, _OFF_AV = 0, 256, 512
_OFF_BQ, _OFF_BK, _OFF_BV = 768, 1280, 1408
_OFF_CQ, _OFF_CKV, _OFF_CKR = 1536, 1792, 1920


def _params(n_axes):
    return pltpu.CompilerParams(dimension_semantics=("arbitrary",) * n_axes,
                                vmem_limit_bytes=VMEM_LIMIT)


def _const_spec(shape):
    nd = len(shape)
    return pl.BlockSpec(shape, lambda *_: (0,) * nd, pipeline_mode=pl.Buffered(1))


def _mod_spec(sub):
    return pl.BlockSpec((BGROUP, None, 3, D_MODEL),
                        lambda t, b: (jnp.where(t >= NT_LAT, CTX_ROW // BGROUP, b), sub, 0, 0))


def _tok_spec(width):
    return pl.BlockSpec((BGROUP, TM, width), lambda t, b: (b, t, 0))


def _sigmoid(x):
    return 1.0 / (1.0 + jnp.exp(-x))


def _rmsnorm(x, g):
    return (x * lax.rsqrt(jnp.mean(x * x, axis=-1, keepdims=True) + EPS)) * g


def _bdot(a, b):
    return jnp.dot(a, b, preferred_element_type=F32)


def _dot_nt(a, b):
    return lax.dot_general(a, b, (((1,), (1,)), ((), ())), preferred_element_type=F32)


def _ada_body(c_ref, w_ref, b_ref, o_ref):
    c = c_ref[...]
    s = c * _sigmoid(c)
    o_ref[...] = _bdot(s.astype(BF16), w_ref[...].astype(BF16)) + b_ref[...]


def _ada(c_all, w_ada, b_ada):
    nblk = (N_MOD * D_MODEL) // ADA_BLOCK
    return pl.pallas_call(
        _ada_body,
        grid=(DEPTH, nblk),
        in_specs=[
            pl.BlockSpec((MOD_ROWS, D_MODEL), lambda i, j: (0, 0)),
            pl.BlockSpec((None, D_MODEL, ADA_BLOCK), lambda i, j: (i, 0, j)),
            pl.BlockSpec((None, 1, ADA_BLOCK), lambda i, j: (i, 0, j)),
        ],
        out_specs=pl.BlockSpec((None, MOD_ROWS, ADA_BLOCK), lambda i, j: (i, 0, j)),
        out_shape=jax.ShapeDtypeStruct((DEPTH, MOD_ROWS, N_MOD * D_MODEL), F32),
        compiler_params=_params(2),
        name="ada_mod",
    )(c_all, w_ada, b_ada.reshape(DEPTH, 1, N_MOD * D_MODEL))


def _ffn_body(*refs, final, split):
    if split:
        h_ref, hc_ref, mod_ref, g_ref, wg_ref, wu_ref, wd_ref, fin_ref, o_ref = refs
    else:
        h_ref, mod_ref, g_ref, wg_ref, wu_ref, wd_ref, fin_ref, o_ref = refs
    for r in range(BGROUP):
        h = h_ref[r]
        if split:
            h = jnp.where(pl.program_id(0) >= NT_LAT, hc_ref[r], h)
        mod = mod_ref[r]
        n = _rmsnorm(h, g_ref[...]) * (1.0 + mod[1:2]) + mod[0:1]
        nb = n.astype(BF16)
        g = _bdot(nb, wg_ref[...])
        u = _bdot(nb, wu_ref[...])
        a = (g * _sigmoid(g)) * u
        d = _bdot(a.astype(BF16), wd_ref[...])
        out = h + 0.5 * mod[2:3] * d
        if final:
            out = _rmsnorm(out, fin_ref[...])
        o_ref[r] = out


def _ffn(h, mod, sub, g_norm, wg, wu, wd, fin, *, n_tiles, final):
    split = isinstance(h, tuple)
    if split:
        h_arrays = list(h)
        h_specs = [pl.BlockSpec((BGROUP, TM, D_MODEL), lambda t, b: (b, jnp.minimum(t, NT_LAT - 1), 0)),
                   pl.BlockSpec((BGROUP, TM, D_MODEL), lambda t, b: (b, 0, 0))]
    else:
        h_arrays, h_specs = [h], [_tok_spec(D_MODEL)]
    return pl.pallas_call(
        functools.partial(_ffn_body, final=final, split=split),
        grid=(n_tiles, BATCH // BGROUP),
        in_specs=h_specs + [
            _mod_spec(sub),
            _const_spec((1, D_MODEL)),
            _const_spec((D_MODEL, D_FF)),
            _const_spec((D_MODEL, D_FF)),
            _const_spec((D_FF, D_MODEL)),
            _const_spec((1, D_MODEL)),
        ],
        out_specs=_tok_spec(D_MODEL),
        out_shape=jax.ShapeDtypeStruct((BATCH, n_tiles * TM, D_MODEL), F32),
        compiler_params=_params(2),
        name="half_ffn",
    )(*h_arrays, mod, g_norm, wg, wu, wd, fin)


def _rope(x, cos, sin_a, sin_b, half):
    w = x.shape[-1]
    return x * cos + pltpu.roll(x, w - half, 1) * sin_a + pltpu.roll(x, half, 1) * sin_b


def _seg_mean_sq(x, seg):
    w = seg.shape[0]
    parts = []
    for lo_col in range(0, x.shape[-1], w):
        x2 = x[:, lo_col:lo_col + w]
        x2 = x2 * x2
        hi = x2.astype(BF16)
        lo = (x2 - hi.astype(F32)).astype(BF16)
        parts.append(_bdot(hi, seg) + _bdot(lo, seg))
    return parts[0] if len(parts) == 1 else jnp.concatenate(parts, axis=1)


def _proj_body(h_ref, mod_ref, g_ref, w_ref, segq_ref, segk_ref, qn_ref, kn_ref, mqn_ref, mkvn_ref,
               wuq_ref, wukv_ref, place_ref,
               cb_ref, sab_ref, sbb_ref, cm_ref, sam_ref, sbm_ref, ck_ref, sak_ref, sbk_ref,
               qa_ref, ka_ref, va_ref, qb_ref, kb_ref, vb_ref, qm_ref, km_ref, vm_ref):
    scale = LOG2_E * HEAD_DIM ** -0.5
    mla_scale = LOG2_E * (MLA_NOPE + MLA_ROPE) ** -0.5
    cb, sab, sbb = cb_ref[...], sab_ref[...], sbb_ref[...]
    for r in range(BGROUP):
        mod = mod_ref[r]
        n = _rmsnorm(h_ref[r], g_ref[...]) * (1.0 + mod[1:2]) + mod[0:1]
        z = _bdot(n.astype(BF16), w_ref[...])

        qa_ref[r] = (z[:, _OFF_AQ:_OFF_AQ + NA_WIDTH] * scale).astype(BF16)
        ka_ref[r] = z[:, _OFF_AK:_OFF_AK + NA_WIDTH].astype(BF16)
        va_ref[r] = z[:, _OFF_AV:_OFF_AV + NA_WIDTH].T.astype(BF16)

        bq = z[:, _OFF_BQ:_OFF_BQ + GQA_Q_WIDTH]
        bq = (bq * lax.rsqrt(_seg_mean_sq(bq, segq_ref[...]) + EPS)) * qn_ref[...]
        qb_ref[r] = (_rope(bq, cb, sab, sbb, HEAD_DIM // 4) * scale).astype(BF16)
        bk = z[:, _OFF_BK:_OFF_BK + GQA_KV_WIDTH]
        bk = (bk * lax.rsqrt(_seg_mean_sq(bk, segk_ref[...]) + EPS)) * kn_ref[...]
        kb_ref[r] = _rope(bk, cb[:, :GQA_KV_WIDTH], sab[:, :GQA_KV_WIDTH], sbb[:, :GQA_KV_WIDTH],
                          HEAD_DIM // 4).astype(BF16)
        vb_ref[r] = z[:, _OFF_BV:_OFF_BV + GQA_KV_WIDTH].T.astype(BF16)

        cq = _rmsnorm(z[:, _OFF_CQ:_OFF_CQ + MLA_Q_RANK], mqn_ref[...])
        q_lat = _bdot(cq.astype(BF16), wuq_ref[...])
        qm_ref[r] = (_rope(q_lat, cm_ref[...], sam_ref[...], sbm_ref[...], MLA_ROPE // 4)
                     * mla_scale).astype(BF16)
        ckv = _rmsnorm(z[:, _OFF_CKV:_OFF_CKV + MLA_KV_RANK], mkvn_ref[...])
        kv_lat = _bdot(ckv.astype(BF16), wukv_ref[...])
        vm_ref[r] = kv_lat[:, MLA_QK_WIDTH:].T.astype(BF16)
        kr = _rope(z[:, _OFF_CKR:_OFF_CKR + LANES], ck_ref[...], sak_ref[...], sbk_ref[...],
                   MLA_ROPE // 4).astype(BF16)
        km_ref[r] = (kv_lat[:, :MLA_QK_WIDTH] + _bdot(kr, place_ref[...])).astype(BF16)


def _proj(h, mod, g_norm, w_qkv, consts, tables):
    widths = (NA_WIDTH, NA_WIDTH, NA_WIDTH, GQA_Q_WIDTH, GQA_KV_WIDTH, GQA_KV_WIDTH,
              MLA_QK_WIDTH, MLA_QK_WIDTH, MLA_V_WIDTH)
    is_value = [k % 3 == 2 for k in range(len(widths))]
    tab_specs = [pl.BlockSpec((TM, t.shape[-1]), lambda t_, b: (t_, 0)) for t in tables]
    return pl.pallas_call(
        _proj_body,
        grid=(NT_ALL, BATCH // BGROUP),
        in_specs=[_tok_spec(D_MODEL), _mod_spec(1), _const_spec((1, D_MODEL)),
                  _const_spec((D_MODEL, QKV_WIDTH))]
                 + [_const_spec(c.shape) for c in consts] + tab_specs,
        out_specs=[pl.BlockSpec((BGROUP, w, TM), lambda t, b: (b, 0, t)) if v else _tok_spec(w)
                   for v, w in zip(is_value, widths)],
        out_shape=[jax.ShapeDtypeStruct((BATCH, w, TOK) if v else (BATCH, TOK, w), BF16)
                   for v, w in zip(is_value, widths)],
        compiler_params=_params(2),
        name="mix_proj",
    )(h, mod, g_norm, w_qkv, *consts, *tables)


def _ctx_attn_body(q_ref, k_ref, vt_ref, o_ref, *, heads, grp, dqk, dv):
    ones = jnp.ones((BF16_SUBLANES, TM), BF16)
    outs = []
    for hd in range(heads):
        g = hd // grp
        st = _dot_nt(k_ref[:, g * dqk:(g + 1) * dqk], q_ref[:, hd * dqk:(hd + 1) * dqk])
        p = jnp.exp2(st - jnp.max(st, axis=0, keepdims=True)).astype(BF16)
        acc = _bdot(jnp.concatenate([vt_ref[g * dv:(g + 1) * dv, :], ones], axis=0), p)
        outs.append(acc[:dv] * (1.0 / acc[dv:dv + 1]))
    o_ref[...] = jnp.concatenate(outs, axis=0).T.astype(BF16)


def _ctx_attn(q, k, vt, *, heads, grp, dqk, dv, name):
    return pl.pallas_call(
        functools.partial(_ctx_attn_body, heads=heads, grp=grp, dqk=dqk, dv=dv),
        grid=(BATCH,),
        in_specs=[pl.BlockSpec((None, TM, heads * dqk), lambda b: (b, NT_LAT, 0)),
                  pl.BlockSpec((None, TM, k.shape[-1]), lambda b: (b, NT_LAT, 0)),
                  pl.BlockSpec((None, vt.shape[1], TM), lambda b: (b, 0, NT_LAT))],
        out_specs=pl.BlockSpec((None, TM, heads * dv), lambda b: (b, 0, 0)),
        out_shape=jax.ShapeDtypeStruct((BATCH, TM, heads * dv), BF16),
        compiler_params=_params(1),
        name=name,
    )(q, k, vt)


def _pipe_step(prod, cons, n_chunk, dv):
    ones = jnp.ones((BF16_SUBLANES, KEY_CHUNK), BF16)
    m_cons = [m_ref[0:1, :] for _, _, m_ref in cons]
    m_new = [None] * len(prod)
    acc = [None] * len(cons)
    for c in range(n_chunk):
        rows = slice(c * KEY_CHUNK, (c + 1) * KEY_CHUNK)
        for d, (qg, k_chunk, bias_chunk, s_ref, _) in enumerate(prod):
            st = _dot_nt(k_chunk(c), qg)
            if bias_chunk is not None:
                st = st + bias_chunk(c)
            s_ref[rows, :] = st
            mc = jnp.max(st, axis=0, keepdims=True)
            m_new[d] = mc if m_new[d] is None else jnp.maximum(m_new[d], mc)
        for d, (vt_chunk, s_ref, _) in enumerate(cons):
            p = jnp.exp2(s_ref[rows, :] - m_cons[d]).astype(BF16)
            part = _bdot(jnp.concatenate([vt_chunk(c), ones], axis=0), p)
            acc[d] = part if acc[d] is None else acc[d] + part
    for d, (_, _, _, _, m_ref) in enumerate(prod):
        m_ref[0:1, :] = m_new[d]
    return [a[:dv] * (1.0 / a[dv:dv + 1]) for a in acc]


def _pipe_run(step, scratch, n_dots):
    i = pl.program_id(0)
    s = [scratch[slot * n_dots:(slot + 1) * n_dots] for slot in range(2)]
    m = [scratch[(2 + slot) * n_dots:(3 + slot) * n_dots] for slot in range(2)]

    @pl.when(i == 0)
    def _():
        for ref in s[1] + m[1]:
            ref[...] = jnp.zeros(ref.shape, F32)

    pl.when(i % 2 == 0)(functools.partial(step, 0, 1, s, m))
    pl.when(i % 2 == 1)(functools.partial(step, 1, 0, s, m))


def _n_chains(cpt):
    return BATCH * NT_LAT * cpt


def _prod_chain(i, cpt):
    c = jnp.minimum(i, _n_chains(cpt) - 1)
    return c // (NT_LAT * cpt), (c // cpt) % NT_LAT, c % cpt


def _cons_chain(i, cpt):
    c = jnp.maximum(i - 1, 0)
    return c // (NT_LAT * cpt), (c // cpt) % NT_LAT, c % cpt


def _pipe_scratch(nk, m_len, n_dots):
    return ([pltpu.VMEM((nk, m_len), F32)] * (2 * n_dots)
            + [pltpu.VMEM((8, m_len), F32)] * (2 * n_dots))


def _gqa_body(q_ref, k_ref, vt_ref, o_ref, *scratch):
    def step(sp, sc, s, m):
        q = q_ref[...]
        qg = jnp.concatenate([q[:, j * HEAD_DIM:(j + 1) * HEAD_DIM] for j in range(GQA_GRP)], axis=0)

        def k_chunk(c):
            return k_ref[c * KEY_CHUNK:(c + 1) * KEY_CHUNK, sp * HEAD_DIM:(sp + 1) * HEAD_DIM]

        def vt_chunk(c):
            return vt_ref[sc * HEAD_DIM:(sc + 1) * HEAD_DIM, c * KEY_CHUNK:(c + 1) * KEY_CHUNK]

        (ot,) = _pipe_step([(qg, k_chunk, None, s[sp][0], m[sp][0])], [(vt_chunk, s[sc][0], m[sc][0])],
                           TOK // KEY_CHUNK, HEAD_DIM)
        o = jnp.concatenate([ot[:, j * TM:(j + 1) * TM] for j in range(GQA_GRP)], axis=0)
        o_ref[...] = o.T.astype(BF16)

    _pipe_run(step, scratch, 1)


def _gqa_attn(q, k, vt):
    wq = GQA_GRP * HEAD_DIM
    return pl.pallas_call(
        _gqa_body,
        grid=(_n_chains(GQA_KV_HEADS) + 1,),
        in_specs=[
            pl.BlockSpec((None, TM, wq), lambda i: _prod_chain(i, GQA_KV_HEADS)),
            pl.BlockSpec((None, TOK, GQA_KV_WIDTH), lambda i: (_prod_chain(i, GQA_KV_HEADS)[0], 0, 0)),
            pl.BlockSpec((None, GQA_KV_WIDTH, TOK), lambda i: (_cons_chain(i, GQA_KV_HEADS)[0], 0, 0)),
        ],
        out_specs=pl.BlockSpec((None, TM, wq), lambda i: _cons_chain(i, GQA_KV_HEADS)),
        out_shape=jax.ShapeDtypeStruct((BATCH, SEQ, GQA_Q_WIDTH), BF16),
        scratch_shapes=_pipe_scratch(TOK, GQA_GRP * TM, 1),
        compiler_params=_params(1),
        name="gqa_attn",
    )(q, k, vt)


def _mla_body(q_ref, k_ref, vt_ref, o_ref, *scratch):
    def run(sp, sc, s, m):
        prod, cons = [], []
        for d in range(MLA_HEADS):
            def k_chunk(c, d=d):
                return k_ref[c * KEY_CHUNK:(c + 1) * KEY_CHUNK, d * MLA_PAD:(d + 1) * MLA_PAD]

            def vt_chunk(c, d=d):
                return vt_ref[d * MLA_V:(d + 1) * MLA_V, c * KEY_CHUNK:(c + 1) * KEY_CHUNK]

            prod.append((q_ref[:, d * MLA_PAD:(d + 1) * MLA_PAD], k_chunk, None, s[sp][d], m[sp][d]))
            cons.append((vt_chunk, s[sc][d], m[sc][d]))
        outs = _pipe_step(prod, cons, TOK // KEY_CHUNK, MLA_V)
        o_ref[...] = jnp.concatenate(outs, axis=0).T.astype(BF16)

    _pipe_run(run, scratch, MLA_HEADS)


def _mla_attn(q, k, vt):
    return pl.pallas_call(
        _mla_body,
        grid=(_n_chains(1) + 1,),
        in_specs=[
            pl.BlockSpec((None, TM, MLA_QK_WIDTH), lambda i: _prod_chain(i, 1)),
            pl.BlockSpec((None, TOK, MLA_QK_WIDTH), lambda i: (_prod_chain(i, 1)[0], 0, 0)),
            pl.BlockSpec((None, MLA_V_WIDTH, TOK), lambda i: (_cons_chain(i, 1)[0], 0, 0)),
        ],
        out_specs=pl.BlockSpec((None, TM, MLA_V_WIDTH), lambda i: _cons_chain(i, 1)),
        out_shape=jax.ShapeDtypeStruct((BATCH, SEQ, MLA_V_WIDTH), BF16),
        scratch_shapes=_pipe_scratch(TOK, TM, MLA_HEADS),
        compiler_params=_params(1),
        name="mla_attn",
    )(q, k, vt)


def _na_start_block(t):
    return jnp.clip(t - 1, 0, NT_LAT - NA_KBLK)


def _na_body(q_ref, k0_ref, k1_ref, k2_ref, kc_ref, v0_ref, v1_ref, v2_ref, vc_ref, bias_ref, o_ref, *scratch):
    k_blocks = (k0_ref, k1_ref, k2_ref, kc_ref)
    v_blocks = (v0_ref, v1_ref, v2_ref, vc_ref)

    def run(sp, sc, s, m):
        prod, cons = [], []
        for d in range(NA_HEADS):
            def k_chunk(c, d=d):
                return k_blocks[c][:, d * HEAD_DIM:(d + 1) * HEAD_DIM]

            def bias_chunk(c, d=d):
                return bias_ref[d, c * KEY_CHUNK:(c + 1) * KEY_CHUNK, :]

            def vt_chunk(c, d=d):
                return v_blocks[c][d * HEAD_DIM:(d + 1) * HEAD_DIM, :]

            prod.append((q_ref[:, d * HEAD_DIM:(d + 1) * HEAD_DIM], k_chunk, bias_chunk, s[sp][d], m[sp][d]))
            cons.append((vt_chunk, s[sc][d], m[sc][d]))
        outs = _pipe_step(prod, cons, NA_NKEY // KEY_CHUNK, HEAD_DIM)
        o_ref[...] = jnp.concatenate(outs, axis=0).T.astype(BF16)

    _pipe_run(run, scratch, NA_HEADS)


def _na_attn(q, k, vt, bias):
    def k_spec(j):
        def index(i):
            b, t, _ = _prod_chain(i, 1)
            return b, (NT_LAT if j == NA_KBLK else _na_start_block(t) + j), 0
        return pl.BlockSpec((None, TM, NA_WIDTH), index)

    def vt_spec(j):
        def index(i):
            b, t, _ = _cons_chain(i, 1)
            return b, 0, (NT_LAT if j == NA_KBLK else _na_start_block(t) + j)
        return pl.BlockSpec((None, NA_WIDTH, TM), index)

    def bias_index(i):
        t = _prod_chain(i, 1)[1]
        return 0, jnp.where(t == 0, 0, jnp.where(t == NT_LAT - 1, 2, 1)), 0, 0

    return pl.pallas_call(
        _na_body,
        grid=(_n_chains(1) + 1,),
        in_specs=[pl.BlockSpec((None, TM, NA_WIDTH), lambda i: _prod_chain(i, 1))]
                 + [k_spec(j) for j in range(NA_KBLK + 1)] + [vt_spec(j) for j in range(NA_KBLK + 1)]
                 + [pl.BlockSpec((NA_HEADS, None, NA_NKEY, TM), bias_index)],
        out_specs=pl.BlockSpec((None, TM, NA_WIDTH), lambda i: _cons_chain(i, 1)),
        out_shape=jax.ShapeDtypeStruct((BATCH, SEQ, NA_WIDTH), BF16),
        scratch_shapes=_pipe_scratch(NA_NKEY, TM, NA_HEADS),
        compiler_params=_params(1),
        name="na_attn",
    )(q, *([k] * (NA_KBLK + 1)), *([vt] * (NA_KBLK + 1)), bias)


def _na_bias_table(rel_bias):
    cols = np.arange(GRID_W)
    c0 = np.clip(cols - NA_COLS // 2, 0, GRID_W - NA_COLS)
    col_in = (cols[None, :] >= c0[:, None]) & (cols[None, :] < c0[:, None] + NA_COLS)
    n_dr = 2 * NA_ROWS - 1
    pad = GRID_W - NA_COLS
    rbp = jnp.pad(rel_bias.astype(F32) * LOG2_E, ((0, 0), (0, 0), (pad, pad)), mode="edge")
    tiles = jnp.stack([rbp[:, :, GRID_W - 1 - cq:2 * GRID_W - 1 - cq] for cq in range(GRID_W)], axis=2)
    tiles = jnp.where(col_in[None, None], tiles, NEG_BIG)
    tiles = jnp.concatenate([tiles, jnp.full((NA_HEADS, 1, GRID_W, GRID_W), NEG_BIG, F32)], axis=1)
    dr_blk = np.full((3, NA_QROWS, NA_KROWS), n_dr, np.int32)
    for cls, tile in enumerate((0, 2, NT_LAT - 1)):
        start = int(np.clip(tile - 1, 0, NT_LAT - NA_KBLK)) * NA_QROWS
        for qr in range(NA_QROWS):
            r = tile * NA_QROWS + qr
            r0 = int(np.clip(r - NA_ROWS // 2, 0, ROWS - NA_ROWS))
            for kj in range(NA_KROWS):
                kr = start + kj
                if r0 <= kr < r0 + NA_ROWS:
                    dr_blk[cls, qr, kj] = kr - r + NA_ROWS - 1
    local = jnp.take(tiles, jnp.asarray(dr_blk.reshape(-1)), axis=1)
    local = local.reshape(NA_HEADS, 3, NA_QROWS, NA_KROWS, GRID_W, GRID_W).transpose(0, 1, 2, 4, 3, 5)
    local = local.reshape(NA_HEADS, 3, TM, NA_KROWS * GRID_W)
    table = jnp.concatenate([local, jnp.zeros((NA_HEADS, 3, TM, CTX_LEN), F32)], axis=-1)
    return table.transpose(0, 1, 3, 2)


def _merge_body(h_ref, mod_ref, g_ref, wgate_ref, wa_ref, wb_ref, wm_ref, wo_ref, *rest, split):
    o_ref = rest[-1]
    for r in range(BGROUP):
        branch, pos = [], 0
        for has_ctx in split:
            if has_ctx:
                is_ctx = pl.program_id(0) >= NT_LAT
                branch.append(jnp.where(is_ctx, rest[pos + 1][r], rest[pos][r]))
            else:
                branch.append(rest[pos][r])
            pos += 2 if has_ctx else 1
        h = h_ref[r]
        mod = mod_ref[r]
        n = _rmsnorm(h, g_ref[...]) * (1.0 + mod[1:2]) + mod[0:1]
        gates = _sigmoid(_bdot(n.astype(BF16), wgate_ref[...]))
        y = (gates[:, :D_MODEL] * _bdot(branch[0], wa_ref[...])
             + gates[:, D_MODEL:2 * D_MODEL] * _bdot(branch[1], wb_ref[...])
             + gates[:, 2 * D_MODEL:] * _bdot(branch[2], wm_ref[...]))
        o_ref[r] = h + mod[2:3] * _bdot(y.astype(BF16), wo_ref[...])


def _merge(h, mod, g_norm, w_gate, branches, wa, wb, wm, wo, *, n_tiles):
    split = tuple(isinstance(br, tuple) for br in branches)
    arrays, specs = [], []
    for br in branches:
        if isinstance(br, tuple):
            lat, ctx = br
            arrays += [lat, ctx]
            specs += [pl.BlockSpec((BGROUP, TM, lat.shape[-1]), lambda t, b: (b, jnp.minimum(t, NT_LAT - 1), 0)),
                      pl.BlockSpec((BGROUP, TM, ctx.shape[-1]), lambda t, b: (b, 0, 0))]
        else:
            arrays.append(br)
            specs.append(_tok_spec(br.shape[-1]))
    return pl.pallas_call(
        functools.partial(_merge_body, split=split),
        grid=(n_tiles, BATCH // BGROUP),
        in_specs=[_tok_spec(D_MODEL), _mod_spec(1), _const_spec((1, D_MODEL)),
                  _const_spec((D_MODEL, GATE_WIDTH)),
                  _const_spec((NA_WIDTH, D_MODEL)), _const_spec((GQA_Q_WIDTH, D_MODEL)),
                  _const_spec((MLA_V_WIDTH, D_MODEL)), _const_spec((D_MODEL, D_MODEL))] + specs,
        out_specs=_tok_spec(D_MODEL),
        out_shape=jax.ShapeDtypeStruct((BATCH, n_tiles * TM, D_MODEL), F32),
        compiler_params=_params(2),
        name="gated_merge",
    )(h, mod, g_norm, w_gate, wa, wb, wm, wo, *arrays)


def _rope_tables():
    t = np.arange(SEQ)
    row = (t // GRID_W).astype(np.float32)[:, None]
    col = (t % GRID_W).astype(np.float32)[:, None]

    def head_tables(d_rot):
        half = d_rot // 2
        freqs = jnp.asarray(ROPE_THETA, F32) ** (-jnp.arange(0, half, 2, dtype=F32) / half)
        r, c = jnp.asarray(row) * freqs, jnp.asarray(col) * freqs
        ang = jnp.concatenate([r, r, c, c], axis=-1)
        first = (np.arange(d_rot) % half) < (half // 2)
        cos, sin = jnp.cos(ang), jnp.sin(ang)
        return cos, jnp.where(first, -sin, 0.0), jnp.where(first, 0.0, sin)

    def place(tabs, reps, lo, group):
        out = []
        for k, tab in enumerate(tabs):
            fill = 1.0 if k == 0 else 0.0
            d_rot = tab.shape[-1]
            g = jnp.concatenate([jnp.full((SEQ, lo), fill, F32), tab,
                                 jnp.full((SEQ, group - lo - d_rot), fill, F32)], axis=-1)
            g = jnp.tile(g, (1, reps))
            ctx = jnp.full((CTX_LEN, reps * group), fill, F32)
            out.append(jnp.concatenate([g, ctx], axis=0))
        return out

    tb = head_tables(HEAD_DIM)
    tm = head_tables(MLA_ROPE)
    return (place(tb, GQA_HEADS, 0, HEAD_DIM)
            + place(tm, MLA_HEADS, MLA_NOPE, MLA_PAD)
            + place(tm, 1, 0, LANES))


def _seg_matrix(width):
    idx = np.arange(width) // HEAD_DIM
    return jnp.asarray((idx[:, None] == idx[None, :]).astype(np.float32) / HEAD_DIM, BF16)


def _place_matrix():
    m = np.zeros((LANES, MLA_QK_WIDTH), np.float32)
    for hd in range(MLA_HEADS):
        for j in range(MLA_ROPE):
            m[j, hd * MLA_PAD + MLA_NOPE + j] = 1.0
    return jnp.asarray(m, BF16)


def _pad_heads(w, n_heads, per_head, keep_lo, keep_hi):
    k = w.shape[0]
    wh = w.reshape(k, n_heads, per_head)[:, :, keep_lo:keep_hi]
    wh = jnp.pad(wh, ((0, 0), (0, 0), (0, MLA_PAD - (keep_hi - keep_lo))))
    return wh.reshape(k, n_heads * MLA_PAD)


def kernel(x, c, ctx, c_ctx, w_ada, b_ada, ffn1_norm, ffn1_w_gate, ffn1_w_up, ffn1_w_down, mix_norm, w_in,
           na_rel_bias, gqa_q_norm, gqa_k_norm, mla_q_norm, mla_kv_norm, mla_w_uq, mla_w_ukv,
           w_branch_a, w_branch_b, w_branch_c, w_out, ffn2_norm, ffn2_w_gate, ffn2_w_up, ffn2_w_down,
           final_norm):
    assert x.shape == (BATCH, SEQ, D_MODEL) and ctx.shape == (BATCH, CTX_LEN, D_MODEL)

    c_all = jnp.concatenate([c, jnp.tile(c_ctx[None, :], (BGROUP, 1)),
                             jnp.zeros((MOD_ROWS - BATCH - BGROUP, D_MODEL), F32)], axis=0)
    mod_all = _ada(c_all, w_ada, b_ada).reshape(DEPTH, MOD_ROWS, 3, 3, D_MODEL)

    tables = _rope_tables()
    seg_q, seg_k, place = _seg_matrix(2 * LANES), _seg_matrix(GQA_KV_WIDTH), _place_matrix()
    fin = final_norm.reshape(1, D_MODEL)

    h = (x, ctx)
    for i in range(DEPTH):
        last = i == DEPTH - 1
        mod = mod_all[i]
        w_qkv = jnp.pad(w_in[i][:, :QKV_WIDTH - 96], ((0, 0), (0, 96))).astype(BF16)
        w_gate = w_in[i][:, QKV_WIDTH - 96:].astype(BF16)
        w_uq = _pad_heads(mla_w_uq[i], MLA_HEADS, MLA_NOPE + MLA_ROPE, 0, MLA_NOPE + MLA_ROPE).astype(BF16)
        w_uk = _pad_heads(mla_w_ukv[i], MLA_HEADS, MLA_NOPE + MLA_V, 0, MLA_NOPE)
        w_uv = mla_w_ukv[i].reshape(MLA_KV_RANK, MLA_HEADS, MLA_NOPE + MLA_V)[:, :, MLA_NOPE:]
        w_ukv = jnp.concatenate([w_uk, w_uv.reshape(MLA_KV_RANK, MLA_V_WIDTH)], axis=1).astype(BF16)
        consts = (seg_q, seg_k,
                  jnp.tile(gqa_q_norm[i], GQA_HEADS)[None, :], jnp.tile(gqa_k_norm[i], GQA_KV_HEADS)[None, :],
                  mla_q_norm[i][None, :], mla_kv_norm[i][None, :], w_uq, w_ukv, place)

        h = _ffn(h, mod, 0, ffn1_norm[i][None, :], ffn1_w_gate[i].astype(BF16), ffn1_w_up[i].astype(BF16),
                 ffn1_w_down[i].astype(BF16), fin, n_tiles=NT_ALL, final=False)
        qa, ka, va, qb, kb, vb, qm, km, vm = _proj(h, mod, mix_norm[i][None, :], w_qkv, consts, tables)

        oa = _na_attn(qa, ka, va, _na_bias_table(na_rel_bias[i]))
        ob = _gqa_attn(qb, kb, vb)
        om = _mla_attn(qm, km, vm)
        if not last:
            oa = (oa, _ctx_attn(qa, ka, va, heads=NA_HEADS, grp=1, dqk=HEAD_DIM, dv=HEAD_DIM, name="na_ctx"))
            ob = (ob, _ctx_attn(qb, kb, vb, heads=GQA_HEADS, grp=GQA_GRP, dqk=HEAD_DIM, dv=HEAD_DIM,
                                name="gqa_ctx"))
            om = (om, _ctx_attn(qm, km, vm, heads=MLA_HEADS, grp=1, dqk=MLA_PAD, dv=MLA_V, name="mla_ctx"))

        n_tiles = NT_LAT if last else NT_ALL
        h = _merge(h, mod, mix_norm[i][None, :], w_gate, (oa, ob, om), w_branch_a[i].astype(BF16),
                   w_branch_b[i].astype(BF16), w_branch_c[i].astype(BF16), w_out[i].astype(BF16),
                   n_tiles=n_tiles)
        h = _ffn(h, mod, 2, ffn2_norm[i][None, :], ffn2_w_gate[i].astype(BF16), ffn2_w_up[i].astype(BF16),
                 ffn2_w_down[i].astype(BF16), fin, n_tiles=n_tiles, final=last)
    return h
```

```python
import functools

import numpy as np
import jax
import jax.numpy as jnp
from jax import lax
from jax.experimental import pallas as pl
from jax.experimental.pallas import tpu as pltpu

D_MODEL = 1024
BATCH = 32
SEQ = 2048
DEPTH = 2
CTX_LEN = 256
GRID_W = 64
HEAD_DIM = 64
NA_HEADS = 4
NA_ROWS = 8
NA_COLS = 16
GQA_HEADS = 8
GQA_KV_HEADS = 2
MLA_HEADS = 4
MLA_Q_RANK = 256
MLA_KV_RANK = 128
MLA_NOPE = 64
MLA_ROPE = 32
MLA_V = 64
N_MOD = 9
D_FF = ((8 * D_MODEL // 3 + 127) // 128) * 128
ROPE_THETA = 10000.0
EPS = 1e-6
NEG_BIG = -1e30

NA_WIDTH = NA_HEADS * HEAD_DIM
GQA_Q_WIDTH = GQA_HEADS * HEAD_DIM
GQA_KV_WIDTH = GQA_KV_HEADS * HEAD_DIM
GQA_GRP = GQA_HEADS // GQA_KV_HEADS
MLA_V_WIDTH = MLA_HEADS * MLA_V
GATE_WIDTH = 3 * D_MODEL
ROWS = SEQ // GRID_W

LANES = 128
BF16_SUBLANES = 16
LOG2_E = 1.4426950408889634
TOK = SEQ + CTX_LEN
TM = 256
NT_ALL = TOK // TM
NT_LAT = SEQ // TM
BGROUP = 4
MOD_ROWS = 40
CTX_ROW = BATCH
ADA_BLOCK = 1152
MLA_PAD = LANES
MLA_QK_WIDTH = MLA_HEADS * MLA_PAD
QKV_WIDTH = 2048
NA_QROWS = TM // GRID_W
NA_KROWS = 12
NA_KBLK = NA_KROWS * GRID_W // TM
NA_NKEY = NA_KROWS * GRID_W + CTX_LEN
KEY_CHUNK = 256
VMEM_LIMIT = 56 * 1024 * 1024

F32 = jnp.float32
BF16 = jnp.bfloat16

_OFF_AQ, _OFF_AK, _OFF_AV = 0, 256, 512
_OFF_BQ, _OFF_BK, _OFF_BV = 768, 1280, 1408
_OFF_CQ, _OFF_CKV, _OFF_CKR = 1536, 1792, 1920


def _params(n_axes):
    return pltpu.CompilerParams(dimension_semantics=("arbitrary",) * n_axes,
                                vmem_limit_bytes=VMEM_LIMIT)


def _const_spec(shape):
    nd = len(shape)
    return pl.BlockSpec(shape, lambda *_: (0,) * nd, pipeline_mode=pl.Buffered(1))


def _layer_spec(arr, layer):
    nd = arr.ndim - 1
    return pl.BlockSpec((None,) + arr.shape[1:], lambda *_: (layer,) + (0,) * nd, pipeline_mode=pl.Buffered(1))


def _mod_spec(layer, sub):
    return pl.BlockSpec((None, BGROUP, None, 3, D_MODEL),
                        lambda t, b: (layer, jnp.where(t >= NT_LAT, CTX_ROW // BGROUP, b), sub, 0, 0))


def _tok_spec(width):
    return pl.BlockSpec((BGROUP, TM, width), lambda t, b: (b, t, 0))


def _sigmoid(x):
    return 1.0 / (1.0 + jnp.exp(-x))


def _rmsnorm(x, g):
    return (x * lax.rsqrt(jnp.mean(x * x, axis=-1, keepdims=True) + EPS)) * g


def _bdot(a, b):
    return jnp.dot(a, b, preferred_element_type=F32)


def _dot_nt(a, b):
    return lax.dot_general(a, b, (((1,), (1,)), ((), ())), preferred_element_type=F32)


def _ada_body(c_ref, w_ref, b_ref, o_ref):
    c = c_ref[...]
    s = c * _sigmoid(c)
    o_ref[...] = _bdot(s.astype(BF16), w_ref[...].astype(BF16)) + b_ref[...]


def _ada(c_all, w_ada, b_ada):
    nblk = (N_MOD * D_MODEL) // ADA_BLOCK
    return pl.pallas_call(
        _ada_body,
        grid=(DEPTH, nblk),
        in_specs=[
            pl.BlockSpec((MOD_ROWS, D_MODEL), lambda i, j: (0, 0)),
            pl.BlockSpec((None, D_MODEL, ADA_BLOCK), lambda i, j: (i, 0, j)),
            pl.BlockSpec((None, 1, ADA_BLOCK), lambda i, j: (i, 0, j)),
        ],
        out_specs=pl.BlockSpec((None, MOD_ROWS, ADA_BLOCK), lambda i, j: (i, 0, j)),
        out_shape=jax.ShapeDtypeStruct((DEPTH, MOD_ROWS, N_MOD * D_MODEL), F32),
        compiler_params=_params(2),
        name="ada_mod",
    )(c_all, w_ada, b_ada.reshape(DEPTH, 1, N_MOD * D_MODEL))


def _ffn_body(*refs, final, split):
    if split:
        h_ref, hc_ref, mod_ref, g_ref, wg_ref, wu_ref, wd_ref, fin_ref, o_ref = refs
    else:
        h_ref, mod_ref, g_ref, wg_ref, wu_ref, wd_ref, fin_ref, o_ref = refs
    for r in range(BGROUP):
        h = h_ref[r]
        if split:
            h = jnp.where(pl.program_id(0) >= NT_LAT, hc_ref[r], h)
        mod = mod_ref[r]
        n = _rmsnorm(h, g_ref[...]) * (1.0 + mod[1:2]) + mod[0:1]
        nb = n.astype(BF16)
        g = _bdot(nb, wg_ref[...])
        u = _bdot(nb, wu_ref[...])
        a = (g * _sigmoid(g)) * u
        d = _bdot(a.astype(BF16), wd_ref[...])
        out = h + 0.5 * mod[2:3] * d
        if final:
            out = _rmsnorm(out, fin_ref[...])
        o_ref[r] = out


def _ffn(h, mod, layer, sub, g_norm, wg, wu, wd, fin, *, n_tiles, final):
    split = isinstance(h, tuple)
    if split:
        h_arrays = list(h)
        h_specs = [pl.BlockSpec((BGROUP, TM, D_MODEL), lambda t, b: (b, jnp.minimum(t, NT_LAT - 1), 0)),
                   pl.BlockSpec((BGROUP, TM, D_MODEL), lambda t, b: (b, 0, 0))]
    else:
        h_arrays, h_specs = [h], [_tok_spec(D_MODEL)]
    return pl.pallas_call(
        functools.partial(_ffn_body, final=final, split=split),
        grid=(n_tiles, BATCH // BGROUP),
        in_specs=h_specs + [_mod_spec(layer, sub)] + [_layer_spec(a, layer) for a in (g_norm, wg, wu, wd)]
                 + [_const_spec((1, D_MODEL))],
        out_specs=_tok_spec(D_MODEL),
        out_shape=jax.ShapeDtypeStruct((BATCH, n_tiles * TM, D_MODEL), F32),
        compiler_params=_params(2),
        name="half_ffn",
    )(*h_arrays, mod, g_norm, wg, wu, wd, fin)


def _rope(x, cos, sin_a, sin_b, half):
    w = x.shape[-1]
    return x * cos + pltpu.roll(x, w - half, 1) * sin_a + pltpu.roll(x, half, 1) * sin_b


def _seg_mean_sq(x, seg):
    w = seg.shape[0]
    parts = []
    for lo_col in range(0, x.shape[-1], w):
        x2 = x[:, lo_col:lo_col + w]
        x2 = x2 * x2
        hi = x2.astype(BF16)
        lo = (x2 - hi.astype(F32)).astype(BF16)
        parts.append(_bdot(hi, seg) + _bdot(lo, seg))
    return parts[0] if len(parts) == 1 else jnp.concatenate(parts, axis=1)


def _proj_body(h_ref, mod_ref, g_ref, w_ref, segq_ref, segk_ref, qn_ref, kn_ref, mqn_ref, mkvn_ref,
               wuq_ref, wukv_ref, place_ref,
               cb_ref, sab_ref, sbb_ref, cm_ref, sam_ref, sbm_ref, ck_ref, sak_ref, sbk_ref,
               qa_ref, ka_ref, va_ref, qb_ref, kb_ref, vb_ref, qm_ref, km_ref, vm_ref):
    scale = LOG2_E * HEAD_DIM ** -0.5
    mla_scale = LOG2_E * (MLA_NOPE + MLA_ROPE) ** -0.5
    cb, sab, sbb = cb_ref[...], sab_ref[...], sbb_ref[...]
    for r in range(BGROUP):
        mod = mod_ref[r]
        n = _rmsnorm(h_ref[r], g_ref[...]) * (1.0 + mod[1:2]) + mod[0:1]
        z = _bdot(n.astype(BF16), w_ref[...])

        qa_ref[r] = (z[:, _OFF_AQ:_OFF_AQ + NA_WIDTH] * scale).astype(BF16)
        ka_ref[r] = z[:, _OFF_AK:_OFF_AK + NA_WIDTH].astype(BF16)
        va_ref[r] = z[:, _OFF_AV:_OFF_AV + NA_WIDTH].T.astype(BF16)

        bq = z[:, _OFF_BQ:_OFF_BQ + GQA_Q_WIDTH]
        bq = (bq * lax.rsqrt(_seg_mean_sq(bq, segq_ref[...]) + EPS)) * qn_ref[...]
        qb_ref[r] = (_rope(bq, cb, sab, sbb, HEAD_DIM // 4) * scale).astype(BF16)
        bk = z[:, _OFF_BK:_OFF_BK + GQA_KV_WIDTH]
        bk = (bk * lax.rsqrt(_seg_mean_sq(bk, segk_ref[...]) + EPS)) * kn_ref[...]
        kb_ref[r] = _rope(bk, cb[:, :GQA_KV_WIDTH], sab[:, :GQA_KV_WIDTH], sbb[:, :GQA_KV_WIDTH],
                          HEAD_DIM // 4).astype(BF16)
        vb_ref[r] = z[:, _OFF_BV:_OFF_BV + GQA_KV_WIDTH].T.astype(BF16)

        cq = _rmsnorm(z[:, _OFF_CQ:_OFF_CQ + MLA_Q_RANK], mqn_ref[...])
        q_lat = _bdot(cq.astype(BF16), wuq_ref[...])
        qm_ref[r] = (_rope(q_lat, cm_ref[...], sam_ref[...], sbm_ref[...], MLA_ROPE // 4)
                     * mla_scale).astype(BF16)
        ckv = _rmsnorm(z[:, _OFF_CKV:_OFF_CKV + MLA_KV_RANK], mkvn_ref[...])
        kv_lat = _bdot(ckv.astype(BF16), wukv_ref[...])
        vm_ref[r] = kv_lat[:, MLA_QK_WIDTH:].T.astype(BF16)
        kr = _rope(z[:, _OFF_CKR:_OFF_CKR + LANES], ck_ref[...], sak_ref[...], sbk_ref[...],
                   MLA_ROPE // 4).astype(BF16)
        km_ref[r] = (kv_lat[:, :MLA_QK_WIDTH] + _bdot(kr, place_ref[...])).astype(BF16)


def _proj(h, mod, layer, g_norm, w_qkv, consts, tables):
    widths = (NA_WIDTH, NA_WIDTH, NA_WIDTH, GQA_Q_WIDTH, GQA_KV_WIDTH, GQA_KV_WIDTH,
              MLA_QK_WIDTH, MLA_QK_WIDTH, MLA_V_WIDTH)
    is_value = [k % 3 == 2 for k in range(len(widths))]
    tab_specs = [pl.BlockSpec((TM, t.shape[-1]), lambda t_, b: (t_, 0)) for t in tables]
    return pl.pallas_call(
        _proj_body,
        grid=(NT_ALL, BATCH // BGROUP),
        in_specs=[_tok_spec(D_MODEL), _mod_spec(layer, 1), _layer_spec(g_norm, layer), _layer_spec(w_qkv, layer)]
                 + [_layer_spec(c, layer) if c.ndim == 3 else _const_spec(c.shape) for c in consts] + tab_specs,
        out_specs=[pl.BlockSpec((BGROUP, w, TM), lambda t, b: (b, 0, t)) if v else _tok_spec(w)
                   for v, w in zip(is_value, widths)],
        out_shape=[jax.ShapeDtypeStruct((BATCH, w, TOK) if v else (BATCH, TOK, w), BF16)
                   for v, w in zip(is_value, widths)],
        compiler_params=_params(2),
        name="mix_proj",
    )(h, mod, g_norm, w_qkv, *consts, *tables)


_CTX_MIXERS = ((NA_HEADS, 1, HEAD_DIM, HEAD_DIM), (GQA_HEADS, GQA_GRP, HEAD_DIM, HEAD_DIM),
               (MLA_HEADS, 1, MLA_PAD, MLA_V))


def _ctx_attn_body(*refs):
    ones = jnp.ones((BF16_SUBLANES, TM), BF16)
    n_mix = len(_CTX_MIXERS)
    for mix, (heads, grp, dqk, dv) in enumerate(_CTX_MIXERS):
        q_ref, k_ref, vt_ref = refs[3 * mix:3 * mix + 3]
        outs = []
        for hd in range(heads):
            g = hd // grp
            st = _dot_nt(k_ref[:, g * dqk:(g + 1) * dqk], q_ref[:, hd * dqk:(hd + 1) * dqk])
            p = jnp.exp2(st - jnp.max(st, axis=0, keepdims=True)).astype(BF16)
            acc = _bdot(jnp.concatenate([vt_ref[g * dv:(g + 1) * dv, :], ones], axis=0), p)
            outs.append(acc[:dv] * (1.0 / acc[dv:dv + 1]))
        refs[3 * n_mix + mix][...] = jnp.concatenate(outs, axis=0).T.astype(BF16)


def _ctx_attn(qkv):
    in_specs, arrays = [], []
    for q, k, vt in qkv:
        arrays += [q, k, vt]
        in_specs += [pl.BlockSpec((None, TM, q.shape[-1]), lambda b: (b, NT_LAT, 0)),
                     pl.BlockSpec((None, TM, k.shape[-1]), lambda b: (b, NT_LAT, 0)),
                     pl.BlockSpec((None, vt.shape[1], TM), lambda b: (b, 0, NT_LAT))]
    widths = [heads * dv for heads, _, _, dv in _CTX_MIXERS]
    return pl.pallas_call(
        _ctx_attn_body,
        grid=(BATCH,),
        in_specs=in_specs,
        out_specs=[pl.BlockSpec((None, TM, w), lambda b: (b, 0, 0)) for w in widths],
        out_shape=[jax.ShapeDtypeStruct((BATCH, TM, w), BF16) for w in widths],
        compiler_params=_params(1),
        name="ctx_attn",
    )(*arrays)


def _pipe_step(prod, cons, n_chunk, dv):
    ones = jnp.ones((BF16_SUBLANES, KEY_CHUNK), BF16)
    m_cons = [m_ref[0:1, :] for _, _, m_ref in cons]
    m_new = [None] * len(prod)
    acc = [None] * len(cons)
    for c in range(n_chunk):
        rows = slice(c * KEY_CHUNK, (c + 1) * KEY_CHUNK)
        for d, (qg, k_chunk, bias_chunk, s_ref, _) in enumerate(prod):
            st = _dot_nt(k_chunk(c), qg)
            if bias_chunk is not None:
                st = st + bias_chunk(c)
            s_ref[rows, :] = st
            mc = jnp.max(st, axis=0, keepdims=True)
            m_new[d] = mc if m_new[d] is None else jnp.maximum(m_new[d], mc)
        for d, (vt_chunk, s_ref, _) in enumerate(cons):
            p = jnp.exp2(s_ref[rows, :] - m_cons[d]).astype(BF16)
            part = _bdot(jnp.concatenate([vt_chunk(c), ones], axis=0), p)
            acc[d] = part if acc[d] is None else acc[d] + part
    for d, (_, _, _, _, m_ref) in enumerate(prod):
        m_ref[0:1, :] = m_new[d]
    return [a[:dv] * (1.0 / a[dv:dv + 1]) for a in acc]


def _pipe_run(step, scratch, n_dots):
    i = pl.program_id(0)
    s = [scratch[slot * n_dots:(slot + 1) * n_dots] for slot in range(2)]
    m = [scratch[(2 + slot) * n_dots:(3 + slot) * n_dots] for slot in range(2)]

    @pl.when(i == 0)
    def _():
        for ref in s[1] + m[1]:
            ref[...] = jnp.zeros(ref.shape, F32)

    pl.when(i % 2 == 0)(functools.partial(step, 0, 1, s, m))
    pl.when(i % 2 == 1)(functools.partial(step, 1, 0, s, m))


N_CHAINS = BATCH * NT_LAT


def _prod_chain(i):
    c = jnp.minimum(i, N_CHAINS - 1)
    return c // NT_LAT, c % NT_LAT, 0


def _cons_chain(i):
    c = jnp.maximum(i - 1, 0)
    return c // NT_LAT, c % NT_LAT, 0


def _pipe_scratch(nk, m_len, n_dots):
    return ([pltpu.VMEM((nk, m_len), F32)] * (2 * n_dots)
            + [pltpu.VMEM((8, m_len), F32)] * (2 * n_dots))


def _gqa_body(q_ref, k_ref, vt_ref, o_ref, *scratch):
    def run(sp, sc, s, m):
        q = q_ref[...]
        prod, cons = [], []
        for g in range(GQA_KV_HEADS):
            qg = jnp.concatenate([q[:, (g * GQA_GRP + j) * HEAD_DIM:(g * GQA_GRP + j + 1) * HEAD_DIM]
                                  for j in range(GQA_GRP)], axis=0)

            def k_chunk(c, g=g):
                return k_ref[c * KEY_CHUNK:(c + 1) * KEY_CHUNK, g * HEAD_DIM:(g + 1) * HEAD_DIM]

            def vt_chunk(c, g=g):
                return vt_ref[g * HEAD_DIM:(g + 1) * HEAD_DIM, c * KEY_CHUNK:(c + 1) * KEY_CHUNK]

            prod.append((qg, k_chunk, None, s[sp][g], m[sp][g]))
            cons.append((vt_chunk, s[sc][g], m[sc][g]))
        outs = _pipe_step(prod, cons, TOK // KEY_CHUNK, HEAD_DIM)
        o = jnp.concatenate([ot[:, j * TM:(j + 1) * TM] for ot in outs for j in range(GQA_GRP)], axis=0)
        o_ref[...] = o.T.astype(BF16)

    _pipe_run(run, scratch, GQA_KV_HEADS)


def _full_key_attn(body, q, k, vt, *, w_out, n_dots, m_len, name):
    return pl.pallas_call(
        body,
        grid=(N_CHAINS + 1,),
        in_specs=[
            pl.BlockSpec((None, TM, q.shape[-1]), _prod_chain),
            pl.BlockSpec((None, TOK, k.shape[-1]), lambda i: (_prod_chain(i)[0], 0, 0)),
            pl.BlockSpec((None, vt.shape[1], TOK), lambda i: (_cons_chain(i)[0], 0, 0)),
        ],
        out_specs=pl.BlockSpec((None, TM, w_out), _cons_chain),
        out_shape=jax.ShapeDtypeStruct((BATCH, SEQ, w_out), BF16),
        scratch_shapes=_pipe_scratch(TOK, m_len, n_dots),
        compiler_params=_params(1),
        name=name,
    )(q, k, vt)


def _gqa_attn(q, k, vt):
    return _full_key_attn(_gqa_body, q, k, vt, w_out=GQA_Q_WIDTH, n_dots=GQA_KV_HEADS, m_len=GQA_GRP * TM,
                          name="gqa_attn")


def _mla_body(q_ref, k_ref, vt_ref, o_ref, *scratch):
    def run(sp, sc, s, m):
        prod, cons = [], []
        for d in range(MLA_HEADS):
            def k_chunk(c, d=d):
                return k_ref[c * KEY_CHUNK:(c + 1) * KEY_CHUNK, d * MLA_PAD:(d + 1) * MLA_PAD]

            def vt_chunk(c, d=d):
                return vt_ref[d * MLA_V:(d + 1) * MLA_V, c * KEY_CHUNK:(c + 1) * KEY_CHUNK]

            prod.append((q_ref[:, d * MLA_PAD:(d + 1) * MLA_PAD], k_chunk, None, s[sp][d], m[sp][d]))
            cons.append((vt_chunk, s[sc][d], m[sc][d]))
        outs = _pipe_step(prod, cons, TOK // KEY_CHUNK, MLA_V)
        o_ref[...] = jnp.concatenate(outs, axis=0).T.astype(BF16)

    _pipe_run(run, scratch, MLA_HEADS)


def _mla_attn(q, k, vt):
    return _full_key_attn(_mla_body, q, k, vt, w_out=MLA_V_WIDTH, n_dots=MLA_HEADS, m_len=TM, name="mla_attn")


def _na_start_block(t):
    return jnp.clip(t - 1, 0, NT_LAT - NA_KBLK)


def _na_body(q_ref, k0_ref, k1_ref, k2_ref, kc_ref, v0_ref, v1_ref, v2_ref, vc_ref, bias_ref, o_ref, *scratch):
    k_blocks = (k0_ref, k1_ref, k2_ref, kc_ref)
    v_blocks = (v0_ref, v1_ref, v2_ref, vc_ref)

    def run(sp, sc, s, m):
        prod, cons = [], []
        for d in range(NA_HEADS):
            def k_chunk(c, d=d):
                return k_blocks[c][:, d * HEAD_DIM:(d + 1) * HEAD_DIM]

            def bias_chunk(c, d=d):
                return bias_ref[d, c * KEY_CHUNK:(c + 1) * KEY_CHUNK, :]

            def vt_chunk(c, d=d):
                return v_blocks[c][d * HEAD_DIM:(d + 1) * HEAD_DIM, :]

            prod.append((q_ref[:, d * HEAD_DIM:(d + 1) * HEAD_DIM], k_chunk, bias_chunk, s[sp][d], m[sp][d]))
            cons.append((vt_chunk, s[sc][d], m[sc][d]))
        outs = _pipe_step(prod, cons, NA_NKEY // KEY_CHUNK, HEAD_DIM)
        o_ref[...] = jnp.concatenate(outs, axis=0).T.astype(BF16)

    _pipe_run(run, scratch, NA_HEADS)


def _na_attn(q, k, vt, bias, layer):
    def k_spec(j):
        def index(i):
            b, t, _ = _prod_chain(i)
            return b, (NT_LAT if j == NA_KBLK else _na_start_block(t) + j), 0
        return pl.BlockSpec((None, TM, NA_WIDTH), index)

    def vt_spec(j):
        def index(i):
            b, t, _ = _cons_chain(i)
            return b, 0, (NT_LAT if j == NA_KBLK else _na_start_block(t) + j)
        return pl.BlockSpec((None, NA_WIDTH, TM), index)

    def bias_index(i):
        t = _prod_chain(i)[1]
        return layer, 0, jnp.where(t == 0, 0, jnp.where(t == NT_LAT - 1, 2, 1)), 0, 0

    return pl.pallas_call(
        _na_body,
        grid=(N_CHAINS + 1,),
        in_specs=[pl.BlockSpec((None, TM, NA_WIDTH), _prod_chain)]
                 + [k_spec(j) for j in range(NA_KBLK + 1)] + [vt_spec(j) for j in range(NA_KBLK + 1)]
                 + [pl.BlockSpec((None, NA_HEADS, None, NA_NKEY, TM), bias_index)],
        out_specs=pl.BlockSpec((None, TM, NA_WIDTH), _cons_chain),
        out_shape=jax.ShapeDtypeStruct((BATCH, SEQ, NA_WIDTH), BF16),
        scratch_shapes=_pipe_scratch(NA_NKEY, TM, NA_HEADS),
        compiler_params=_params(1),
        name="na_attn",
    )(q, *([k] * (NA_KBLK + 1)), *([vt] * (NA_KBLK + 1)), bias)


def _na_bias_table(rel_bias):
    lead = rel_bias.shape[:-2]
    rel_bias = rel_bias.reshape((-1,) + rel_bias.shape[-2:])
    nh = rel_bias.shape[0]
    cols = np.arange(GRID_W)
    c0 = np.clip(cols - NA_COLS // 2, 0, GRID_W - NA_COLS)
    col_in = (cols[None, :] >= c0[:, None]) & (cols[None, :] < c0[:, None] + NA_COLS)
    n_dr = 2 * NA_ROWS - 1
    pad = GRID_W - NA_COLS
    rbp = jnp.pad(rel_bias.astype(F32) * LOG2_E, ((0, 0), (0, 0), (pad, pad)), mode="edge")
    tiles = jnp.stack([rbp[:, :, GRID_W - 1 - cq:2 * GRID_W - 1 - cq] for cq in range(GRID_W)], axis=2)
    tiles = jnp.where(col_in[None, None], tiles, NEG_BIG)
    tiles = jnp.concatenate([tiles, jnp.full((nh, 1, GRID_W, GRID_W), NEG_BIG, F32)], axis=1)
    dr_blk = np.full((3, NA_QROWS, NA_KROWS), n_dr, np.int32)
    for cls, tile in enumerate((0, 2, NT_LAT - 1)):
        start = int(np.clip(tile - 1, 0, NT_LAT - NA_KBLK)) * NA_QROWS
        for qr in range(NA_QROWS):
            r = tile * NA_QROWS + qr
            r0 = int(np.clip(r - NA_ROWS // 2, 0, ROWS - NA_ROWS))
            for kj in range(NA_KROWS):
                kr = start + kj
                if r0 <= kr < r0 + NA_ROWS:
                    dr_blk[cls, qr, kj] = kr - r + NA_ROWS - 1
    local = jnp.take(tiles, jnp.asarray(dr_blk.reshape(-1)), axis=1)
    local = local.reshape(nh, 3, NA_QROWS, NA_KROWS, GRID_W, GRID_W).transpose(0, 1, 2, 4, 3, 5)
    local = local.reshape(nh, 3, TM, NA_KROWS * GRID_W)
    table = jnp.concatenate([local, jnp.zeros((nh, 3, TM, CTX_LEN), F32)], axis=-1)
    return table.transpose(0, 1, 3, 2).reshape(lead + (3, NA_NKEY, TM))


def _merge_body(h_ref, mod_ref, g_ref, wgate_ref, wa_ref, wb_ref, wm_ref, wo_ref, *rest, split):
    o_ref = rest[-1]
    for r in range(BGROUP):
        branch, pos = [], 0
        for has_ctx in split:
            if has_ctx:
                is_ctx = pl.program_id(0) >= NT_LAT
                branch.append(jnp.where(is_ctx, rest[pos + 1][r], rest[pos][r]))
            else:
                branch.append(rest[pos][r])
            pos += 2 if has_ctx else 1
        h = h_ref[r]
        mod = mod_ref[r]
        n = _rmsnorm(h, g_ref[...]) * (1.0 + mod[1:2]) + mod[0:1]
        gates = _sigmoid(_bdot(n.astype(BF16), wgate_ref[...]))
        y = (gates[:, :D_MODEL] * _bdot(branch[0], wa_ref[...])
             + gates[:, D_MODEL:2 * D_MODEL] * _bdot(branch[1], wb_ref[...])
             + gates[:, 2 * D_MODEL:] * _bdot(branch[2], wm_ref[...]))
        o_ref[r] = h + mod[2:3] * _bdot(y.astype(BF16), wo_ref[...])


def _merge(h, mod, layer, g_norm, w_gate, branches, wa, wb, wm, wo, *, n_tiles):
    split = tuple(isinstance(br, tuple) for br in branches)
    arrays, specs = [], []
    for br in branches:
        if isinstance(br, tuple):
            lat, ctx = br
            arrays += [lat, ctx]
            specs += [pl.BlockSpec((BGROUP, TM, lat.shape[-1]), lambda t, b: (b, jnp.minimum(t, NT_LAT - 1), 0)),
                      pl.BlockSpec((BGROUP, TM, ctx.shape[-1]), lambda t, b: (b, 0, 0))]
        else:
            arrays.append(br)
            specs.append(_tok_spec(br.shape[-1]))
    return pl.pallas_call(
        functools.partial(_merge_body, split=split),
        grid=(n_tiles, BATCH // BGROUP),
        in_specs=[_tok_spec(D_MODEL), _mod_spec(layer, 1)]
                 + [_layer_spec(a, layer) for a in (g_norm, w_gate, wa, wb, wm, wo)] + specs,
        out_specs=_tok_spec(D_MODEL),
        out_shape=jax.ShapeDtypeStruct((BATCH, n_tiles * TM, D_MODEL), F32),
        compiler_params=_params(2),
        name="gated_merge",
    )(h, mod, g_norm, w_gate, wa, wb, wm, wo, *arrays)


def _rope_tables():
    t = np.arange(SEQ)
    row = (t // GRID_W).astype(np.float32)[:, None]
    col = (t % GRID_W).astype(np.float32)[:, None]

    def head_tables(d_rot):
        half = d_rot // 2
        freqs = np.float32(ROPE_THETA) ** (-np.arange(0, half, 2, dtype=np.float32) / np.float32(half))
        r, c = row * freqs, col * freqs
        ang = np.concatenate([r, r, c, c], axis=-1).astype(np.float32)
        first = (np.arange(d_rot) % half) < (half // 2)
        cos, sin = np.cos(ang), np.sin(ang)
        zero = np.zeros_like(sin)
        return cos, np.where(first, -sin, zero), np.where(first, zero, sin)

    def place(tabs, reps, lo, group):
        out = []
        for k, tab in enumerate(tabs):
            fill = 1.0 if k == 0 else 0.0
            d_rot = tab.shape[-1]
            g = np.concatenate([np.full((SEQ, lo), fill, np.float32), tab,
                                np.full((SEQ, group - lo - d_rot), fill, np.float32)], axis=-1)
            g = np.tile(g, (1, reps))
            ctx = np.full((CTX_LEN, reps * group), fill, np.float32)
            out.append(jnp.asarray(np.concatenate([g, ctx], axis=0), F32))
        return out

    tb = head_tables(HEAD_DIM)
    tm = head_tables(MLA_ROPE)
    return (place(tb, GQA_HEADS, 0, HEAD_DIM)
            + place(tm, MLA_HEADS, MLA_NOPE, MLA_PAD)
            + place(tm, 1, 0, LANES))


def _seg_matrix(width):
    idx = np.arange(width) // HEAD_DIM
    return jnp.asarray((idx[:, None] == idx[None, :]).astype(np.float32) / HEAD_DIM, BF16)


def _place_matrix():
    m = np.zeros((LANES, MLA_QK_WIDTH), np.float32)
    for hd in range(MLA_HEADS):
        for j in range(MLA_ROPE):
            m[j, hd * MLA_PAD + MLA_NOPE + j] = 1.0
    return jnp.asarray(m, BF16)


def _pad_heads(w, n_heads, per_head, keep_lo, keep_hi):
    d, k = w.shape[:2]
    wh = w.reshape(d, k, n_heads, per_head)[..., keep_lo:keep_hi]
    wh = jnp.pad(wh, ((0, 0), (0, 0), (0, 0), (0, MLA_PAD - (keep_hi - keep_lo))))
    return wh.reshape(d, k, n_heads * MLA_PAD)


def kernel(x, c, ctx, c_ctx, w_ada, b_ada, ffn1_norm, ffn1_w_gate, ffn1_w_up, ffn1_w_down, mix_norm, w_in,
           na_rel_bias, gqa_q_norm, gqa_k_norm, mla_q_norm, mla_kv_norm, mla_w_uq, mla_w_ukv,
           w_branch_a, w_branch_b, w_branch_c, w_out, ffn2_norm, ffn2_w_gate, ffn2_w_up, ffn2_w_down,
           final_norm):
    assert x.shape == (BATCH, SEQ, D_MODEL) and ctx.shape == (BATCH, CTX_LEN, D_MODEL)

    c_all = jnp.concatenate([c, jnp.tile(c_ctx[None, :], (BGROUP, 1)),
                             jnp.zeros((MOD_ROWS - BATCH - BGROUP, D_MODEL), F32)], axis=0)
    mod_all = _ada(c_all, w_ada, b_ada).reshape(DEPTH, MOD_ROWS, 3, 3, D_MODEL)

    tables = _rope_tables()
    seg_q, seg_k, place = _seg_matrix(2 * LANES), _seg_matrix(GQA_KV_WIDTH), _place_matrix()
    fin = final_norm.reshape(1, D_MODEL)

    def row(v):
        return v[:, None, :]

    w_qkv = jnp.pad(w_in[:, :, :QKV_WIDTH - 96], ((0, 0), (0, 0), (0, 96))).astype(BF16)
    w_gate = w_in[:, :, QKV_WIDTH - 96:].astype(BF16)
    w_uq = _pad_heads(mla_w_uq, MLA_HEADS, MLA_NOPE + MLA_ROPE, 0, MLA_NOPE + MLA_ROPE).astype(BF16)
    w_uk = _pad_heads(mla_w_ukv, MLA_HEADS, MLA_NOPE + MLA_V, 0, MLA_NOPE)
    w_uv = mla_w_ukv.reshape(DEPTH, MLA_KV_RANK, MLA_HEADS, MLA_NOPE + MLA_V)[..., MLA_NOPE:]
    w_ukv = jnp.concatenate([w_uk, w_uv.reshape(DEPTH, MLA_KV_RANK, MLA_V_WIDTH)], axis=-1).astype(BF16)
    consts = (seg_q, seg_k, row(jnp.tile(gqa_q_norm, (1, GQA_HEADS))), row(jnp.tile(gqa_k_norm, (1, GQA_KV_HEADS))),
              row(mla_q_norm), row(mla_kv_norm), w_uq, w_ukv, place)
    ffn1 = (row(ffn1_norm), ffn1_w_gate.astype(BF16), ffn1_w_up.astype(BF16), ffn1_w_down.astype(BF16))
    ffn2 = (row(ffn2_norm), ffn2_w_gate.astype(BF16), ffn2_w_up.astype(BF16), ffn2_w_down.astype(BF16))
    branch_w = (w_branch_a.astype(BF16), w_branch_b.astype(BF16), w_branch_c.astype(BF16), w_out.astype(BF16))
    na_bias = _na_bias_table(na_rel_bias)
    g_mix = row(mix_norm)

    h = (x, ctx)
    for i in range(DEPTH):
        last = i == DEPTH - 1
        h = _ffn(h, mod_all, i, 0, *ffn1, fin, n_tiles=NT_ALL, final=False)
        qa, ka, va, qb, kb, vb, qm, km, vm = _proj(h, mod_all, i, g_mix, w_qkv, consts, tables)

        oa = _na_attn(qa, ka, va, na_bias, i)
        ob = _gqa_attn(qb, kb, vb)
        om = _mla_attn(qm, km, vm)
        if not last:
            oa_c, ob_c, om_c = _ctx_attn(((qa, ka, va), (qb, kb, vb), (qm, km, vm)))
            oa, ob, om = (oa, oa_c), (ob, ob_c), (om, om_c)

        n_tiles = NT_LAT if last else NT_ALL
        h = _merge(h, mod_all, i, g_mix, w_gate, (oa, ob, om), *branch_w, n_tiles=n_tiles)
        h = _ffn(h, mod_all, i, 2, *ffn2, fin, n_tiles=n_tiles, final=last)
    return h
```

```python
import functools

import numpy as np
import jax
import jax.numpy as jnp
from jax import lax
from jax.experimental import pallas as pl
from jax.experimental.pallas import tpu as pltpu

D_MODEL = 1024
BATCH = 32
SEQ = 2048
DEPTH = 2
CTX_LEN = 256
GRID_W = 64
HEAD_DIM = 64
NA_HEADS = 4
NA_ROWS = 8
NA_COLS = 16
GQA_HEADS = 8
GQA_KV_HEADS = 2
MLA_HEADS = 4
MLA_Q_RANK = 256
MLA_KV_RANK = 128
MLA_NOPE = 64
MLA_ROPE = 32
MLA_V = 64
N_MOD = 9
D_FF = ((8 * D_MODEL // 3 + 127) // 128) * 128
ROPE_THETA = 10000.0
EPS = 1e-6
NEG_BIG = -1e30

NA_WIDTH = NA_HEADS * HEAD_DIM
GQA_Q_WIDTH = GQA_HEADS * HEAD_DIM
GQA_KV_WIDTH = GQA_KV_HEADS * HEAD_DIM
GQA_GRP = GQA_HEADS // GQA_KV_HEADS
MLA_V_WIDTH = MLA_HEADS * MLA_V
GATE_WIDTH = 3 * D_MODEL
ROWS = SEQ // GRID_W

LANES = 128
BF16_SUBLANES = 16
LOG2_E = 1.4426950408889634
TOK = SEQ + CTX_LEN
TM = 256
NT_ALL = TOK // TM
NT_LAT = SEQ // TM
BGROUP = 4
MOD_ROWS = 40
CTX_ROW = BATCH
ADA_BLOCK = 1152
MLA_PAD = LANES
MLA_QK_WIDTH = MLA_HEADS * MLA_PAD
N_QKV_COLS = 1952
QKV_WIDTH = 2048
NA_QROWS = TM // GRID_W
NA_KROWS = 12
NA_KBLK = NA_KROWS * GRID_W // TM
NA_NLOCAL = NA_KROWS * GRID_W
NA_NKEY = NA_NLOCAL + CTX_LEN
KEY_CHUNK = 256
VMEM_LIMIT = 56 * 1024 * 1024

F32 = jnp.float32
BF16 = jnp.bfloat16

_OFF_AQ, _OFF_AK, _OFF_AV = 0, 256, 512
_OFF_BQ, _OFF_BK, _OFF_BV = 768, 1280, 1408
_OFF_CQ, _OFF_CKV, _OFF_CKR = 1536, 1792, 1920


def _params(n_axes):
    return pltpu.CompilerParams(dimension_semantics=("arbitrary",) * n_axes,
                                vmem_limit_bytes=VMEM_LIMIT)


def _const_spec(shape):
    nd = len(shape)
    return pl.BlockSpec(shape, lambda *_: (0,) * nd, pipeline_mode=pl.Buffered(1))


def _layer_spec(arr, layer):
    nd = arr.ndim - 1
    return pl.BlockSpec((None,) + arr.shape[1:], lambda *_: (layer,) + (0,) * nd, pipeline_mode=pl.Buffered(1))


def _mod_spec(layer, sub):
    return pl.BlockSpec((None, BGROUP, None, 3, D_MODEL),
                        lambda t, b: (layer, jnp.where(t >= NT_LAT, CTX_ROW // BGROUP, b), sub, 0, 0))


def _tok_spec(width):
    return pl.BlockSpec((BGROUP, TM, width), lambda t, b: (b, t, 0))


def _sigmoid(x):
    return 1.0 / (1.0 + jnp.exp(-x))


def _rmsnorm(x, g):
    return (x * lax.rsqrt(jnp.mean(x * x, axis=-1, keepdims=True) + EPS)) * g


def _bdot(a, b):
    return jnp.dot(a, b, preferred_element_type=F32)


def _dot_nt(a, b):
    return lax.dot_general(a, b, (((1,), (1,)), ((), ())), preferred_element_type=F32)


def _ada_body(c_ref, w_ref, b_ref, o_ref):
    c = c_ref[...]
    s = c * _sigmoid(c)
    o_ref[...] = _bdot(s.astype(BF16), w_ref[...].astype(BF16)) + b_ref[...]


def _ada(c_all, w_ada, b_ada):
    nblk = (N_MOD * D_MODEL) // ADA_BLOCK
    return pl.pallas_call(
        _ada_body,
        grid=(DEPTH, nblk),
        in_specs=[
            pl.BlockSpec((MOD_ROWS, D_MODEL), lambda i, j: (0, 0)),
            pl.BlockSpec((None, D_MODEL, ADA_BLOCK), lambda i, j: (i, 0, j)),
            pl.BlockSpec((None, 1, ADA_BLOCK), lambda i, j: (i, 0, j)),
        ],
        out_specs=pl.BlockSpec((None, MOD_ROWS, ADA_BLOCK), lambda i, j: (i, 0, j)),
        out_shape=jax.ShapeDtypeStruct((DEPTH, MOD_ROWS, N_MOD * D_MODEL), F32),
        compiler_params=_params(2),
        name="ada_mod",
    )(c_all, w_ada, b_ada.reshape(DEPTH, 1, N_MOD * D_MODEL))


def _ffn_body(*refs, final, split):
    if split:
        h_ref, hc_ref, mod_ref, g_ref, wg_ref, wu_ref, wd_ref, fin_ref, o_ref = refs
    else:
        h_ref, mod_ref, g_ref, wg_ref, wu_ref, wd_ref, fin_ref, o_ref = refs
    for r in range(BGROUP):
        h = h_ref[r]
        if split:
            h = jnp.where(pl.program_id(0) >= NT_LAT, hc_ref[r], h)
        mod = mod_ref[r]
        n = _rmsnorm(h, g_ref[...]) * (1.0 + mod[1:2]) + mod[0:1]
        nb = n.astype(BF16)
        g = _bdot(nb, wg_ref[...])
        u = _bdot(nb, wu_ref[...])
        a = (g * _sigmoid(g)) * u
        d = _bdot(a.astype(BF16), wd_ref[...])
        out = h + 0.5 * mod[2:3] * d
        if final:
            out = _rmsnorm(out, fin_ref[...])
        o_ref[r] = out


def _ffn(h, mod, layer, sub, g_norm, wg, wu, wd, fin, *, n_tiles, final):
    split = isinstance(h, tuple)
    if split:
        h_arrays = list(h)
        h_specs = [pl.BlockSpec((BGROUP, TM, D_MODEL), lambda t, b: (b, jnp.minimum(t, NT_LAT - 1), 0)),
                   pl.BlockSpec((BGROUP, TM, D_MODEL), lambda t, b: (b, 0, 0))]
    else:
        h_arrays, h_specs = [h], [_tok_spec(D_MODEL)]
    return pl.pallas_call(
        functools.partial(_ffn_body, final=final, split=split),
        grid=(n_tiles, BATCH // BGROUP),
        in_specs=h_specs + [_mod_spec(layer, sub)] + [_layer_spec(a, layer) for a in (g_norm, wg, wu, wd)]
                 + [_const_spec((1, D_MODEL))],
        out_specs=_tok_spec(D_MODEL),
        out_shape=jax.ShapeDtypeStruct((BATCH, n_tiles * TM, D_MODEL), F32),
        compiler_params=_params(2),
        name="half_ffn",
    )(*h_arrays, mod, g_norm, wg, wu, wd, fin)


def _rope(x, cos, sin_a, sin_b, half):
    w = x.shape[-1]
    return x * cos + pltpu.roll(x, w - half, 1) * sin_a + pltpu.roll(x, half, 1) * sin_b


def _seg_mean_sq(x, seg):
    w = seg.shape[0]
    parts = []
    for lo_col in range(0, x.shape[-1], w):
        x2 = x[:, lo_col:lo_col + w]
        x2 = x2 * x2
        hi = x2.astype(BF16)
        lo = (x2 - hi.astype(F32)).astype(BF16)
        parts.append(_bdot(hi, seg) + _bdot(lo, seg))
    return parts[0] if len(parts) == 1 else jnp.concatenate(parts, axis=1)


def _proj_body(h_ref, mod_ref, g_ref, w_ref, segq_ref, segk_ref, qn_ref, kn_ref, mqn_ref, mkvn_ref,
               wuq_ref, wukv_ref, place_ref,
               cb_ref, sab_ref, sbb_ref, cm_ref, sam_ref, sbm_ref, ck_ref, sak_ref, sbk_ref,
               qa_ref, ka_ref, va_ref, qb_ref, kb_ref, vb_ref, qm_ref, km_ref, vm_ref):
    scale = LOG2_E * HEAD_DIM ** -0.5
    mla_scale = LOG2_E * (MLA_NOPE + MLA_ROPE) ** -0.5
    cb, sab, sbb = cb_ref[...], sab_ref[...], sbb_ref[...]
    for r in range(BGROUP):
        mod = mod_ref[r]
        n = _rmsnorm(h_ref[r], g_ref[...]) * (1.0 + mod[1:2]) + mod[0:1]
        z = _bdot(n.astype(BF16), w_ref[...])

        qa_ref[r] = (z[:, _OFF_AQ:_OFF_AQ + NA_WIDTH] * scale).astype(BF16)
        ka_ref[r] = z[:, _OFF_AK:_OFF_AK + NA_WIDTH].astype(BF16)
        va_ref[r] = z[:, _OFF_AV:_OFF_AV + NA_WIDTH].T.astype(BF16)

        bq = z[:, _OFF_BQ:_OFF_BQ + GQA_Q_WIDTH]
        bq = (bq * lax.rsqrt(_seg_mean_sq(bq, segq_ref[...]) + EPS)) * qn_ref[...]
        qb_ref[r] = (_rope(bq, cb, sab, sbb, HEAD_DIM // 4) * scale).astype(BF16)
        bk = z[:, _OFF_BK:_OFF_BK + GQA_KV_WIDTH]
        bk = (bk * lax.rsqrt(_seg_mean_sq(bk, segk_ref[...]) + EPS)) * kn_ref[...]
        kb_ref[r] = _rope(bk, cb[:, :GQA_KV_WIDTH], sab[:, :GQA_KV_WIDTH], sbb[:, :GQA_KV_WIDTH],
                          HEAD_DIM // 4).astype(BF16)
        vb_ref[r] = z[:, _OFF_BV:_OFF_BV + GQA_KV_WIDTH].T.astype(BF16)

        cq = _rmsnorm(z[:, _OFF_CQ:_OFF_CQ + MLA_Q_RANK], mqn_ref[...])
        q_lat = _bdot(cq.astype(BF16), wuq_ref[...])
        qm_ref[r] = (_rope(q_lat, cm_ref[...], sam_ref[...], sbm_ref[...], MLA_ROPE // 4)
                     * mla_scale).astype(BF16)
        ckv = _rmsnorm(z[:, _OFF_CKV:_OFF_CKV + MLA_KV_RANK], mkvn_ref[...])
        kv_lat = _bdot(ckv.astype(BF16), wukv_ref[...])
        vm_ref[r] = kv_lat[:, MLA_QK_WIDTH:].T.astype(BF16)
        lane = lax.broadcasted_iota(jnp.int32, (1, LANES), 1)
        kr = jnp.where(lane < MLA_ROPE, z[:, _OFF_CKR:_OFF_CKR + LANES], 0.0)
        kr = _rope(kr, ck_ref[...], sak_ref[...], sbk_ref[...], MLA_ROPE // 4).astype(BF16)
        km_ref[r] = (kv_lat[:, :MLA_QK_WIDTH] + _bdot(kr, place_ref[...])).astype(BF16)


def _proj(h, mod, layer, g_norm, w_qkv, consts, tables):
    widths = (NA_WIDTH, NA_WIDTH, NA_WIDTH, GQA_Q_WIDTH, GQA_KV_WIDTH, GQA_KV_WIDTH,
              MLA_QK_WIDTH, MLA_QK_WIDTH, MLA_V_WIDTH)
    is_value = [k % 3 == 2 for k in range(len(widths))]
    tab_specs = [pl.BlockSpec((TM, t.shape[-1]), lambda t_, b: (t_, 0)) for t in tables]
    return pl.pallas_call(
        _proj_body,
        grid=(NT_ALL, BATCH // BGROUP),
        in_specs=[_tok_spec(D_MODEL), _mod_spec(layer, 1), _layer_spec(g_norm, layer), _layer_spec(w_qkv, layer)]
                 + [_layer_spec(c, layer) if c.ndim == 3 else _const_spec(c.shape) for c in consts] + tab_specs,
        out_specs=[pl.BlockSpec((BGROUP, w, TM), lambda t, b: (b, 0, t)) if v else _tok_spec(w)
                   for v, w in zip(is_value, widths)],
        out_shape=[jax.ShapeDtypeStruct((BATCH, w, TOK) if v else (BATCH, TOK, w), BF16)
                   for v, w in zip(is_value, widths)],
        compiler_params=_params(2),
        name="mix_proj",
    )(h, mod, g_norm, w_qkv, *consts, *tables)


_CTX_MIXERS = ((NA_HEADS, 1, HEAD_DIM, HEAD_DIM), (GQA_HEADS, GQA_GRP, HEAD_DIM, HEAD_DIM),
               (MLA_HEADS, 1, MLA_PAD, MLA_V))


def _ctx_attn_body(*refs):
    ones = jnp.ones((BF16_SUBLANES, TM), BF16)
    n_mix = len(_CTX_MIXERS)
    for mix, (heads, grp, dqk, dv) in enumerate(_CTX_MIXERS):
        q_ref, k_ref, vt_ref = refs[3 * mix:3 * mix + 3]
        outs = []
        for hd in range(heads):
            g = hd // grp
            st = _dot_nt(k_ref[:, g * dqk:(g + 1) * dqk], q_ref[:, hd * dqk:(hd + 1) * dqk])
            p = jnp.exp2(st - jnp.max(st, axis=0, keepdims=True)).astype(BF16)
            acc = _bdot(jnp.concatenate([vt_ref[g * dv:(g + 1) * dv, :], ones], axis=0), p)
            outs.append(acc[:dv] * (1.0 / acc[dv:dv + 1]))
        refs[3 * n_mix + mix][...] = jnp.concatenate(outs, axis=0).T.astype(BF16)


def _ctx_attn(qkv):
    in_specs, arrays = [], []
    for q, k, vt in qkv:
        arrays += [q, k, vt]
        in_specs += [pl.BlockSpec((None, TM, q.shape[-1]), lambda b: (b, NT_LAT, 0)),
                     pl.BlockSpec((None, TM, k.shape[-1]), lambda b: (b, NT_LAT, 0)),
                     pl.BlockSpec((None, vt.shape[1], TM), lambda b: (b, 0, NT_LAT))]
    widths = [heads * dv for heads, _, _, dv in _CTX_MIXERS]
    return pl.pallas_call(
        _ctx_attn_body,
        grid=(BATCH,),
        in_specs=in_specs,
        out_specs=[pl.BlockSpec((None, TM, w), lambda b: (b, 0, 0)) for w in widths],
        out_shape=[jax.ShapeDtypeStruct((BATCH, TM, w), BF16) for w in widths],
        compiler_params=_params(1),
        name="ctx_attn",
    )(*arrays)


def _pipe_step(prod, cons, n_chunk, dv):
    ones = jnp.ones((BF16_SUBLANES, KEY_CHUNK), BF16)
    m_cons = [m_ref[0:1, :] for _, _, m_ref in cons]
    m_new = [None] * len(prod)
    acc = [None] * len(cons)
    for c in range(n_chunk):
        rows = slice(c * KEY_CHUNK, (c + 1) * KEY_CHUNK)
        for d, (qg, k_chunk, bias_chunk, s_ref, _) in enumerate(prod):
            st = _dot_nt(k_chunk(c), qg)
            bias = None if bias_chunk is None else bias_chunk(c)
            if bias is not None:
                st = st + bias
            s_ref[rows, :] = st
            mc = jnp.max(st, axis=0, keepdims=True)
            m_new[d] = mc if m_new[d] is None else jnp.maximum(m_new[d], mc)
        for d, (vt_chunk, s_ref, _) in enumerate(cons):
            p = jnp.exp2(s_ref[rows, :] - m_cons[d]).astype(BF16)
            part = _bdot(jnp.concatenate([vt_chunk(c), ones], axis=0), p)
            acc[d] = part if acc[d] is None else acc[d] + part
    for d, (_, _, _, _, m_ref) in enumerate(prod):
        m_ref[0:1, :] = m_new[d]
    return [a[:dv] * (1.0 / a[dv:dv + 1]) for a in acc]


def _pipe_run(step, scratch, n_dots):
    i = pl.program_id(0)
    s = [scratch[slot * n_dots:(slot + 1) * n_dots] for slot in range(2)]
    m = [scratch[(2 + slot) * n_dots:(3 + slot) * n_dots] for slot in range(2)]

    @pl.when(i == 0)
    def _():
        for ref in s[1] + m[1]:
            ref[...] = jnp.zeros(ref.shape, F32)

    pl.when(i % 2 == 0)(functools.partial(step, 0, 1, s, m))
    pl.when(i % 2 == 1)(functools.partial(step, 1, 0, s, m))


N_CHAINS = BATCH * NT_LAT


def _prod_chain(i):
    c = jnp.minimum(i, N_CHAINS - 1)
    return c // NT_LAT, c % NT_LAT, 0


def _cons_chain(i):
    c = jnp.maximum(i - 1, 0)
    return c // NT_LAT, c % NT_LAT, 0


def _pipe_scratch(nk, m_len, n_dots):
    return ([pltpu.VMEM((nk, m_len), F32)] * (2 * n_dots)
            + [pltpu.VMEM((8, m_len), F32)] * (2 * n_dots))


def _gqa_body(q_ref, k_ref, vt_ref, o_ref, *scratch):
    def run(sp, sc, s, m):
        q = q_ref[...]
        prod, cons = [], []
        for g in range(GQA_KV_HEADS):
            qg = jnp.concatenate([q[:, (g * GQA_GRP + j) * HEAD_DIM:(g * GQA_GRP + j + 1) * HEAD_DIM]
                                  for j in range(GQA_GRP)], axis=0)

            def k_chunk(c, g=g):
                return k_ref[c * KEY_CHUNK:(c + 1) * KEY_CHUNK, g * HEAD_DIM:(g + 1) * HEAD_DIM]

            def vt_chunk(c, g=g):
                return vt_ref[g * HEAD_DIM:(g + 1) * HEAD_DIM, c * KEY_CHUNK:(c + 1) * KEY_CHUNK]

            prod.append((qg, k_chunk, None, s[sp][g], m[sp][g]))
            cons.append((vt_chunk, s[sc][g], m[sc][g]))
        outs = _pipe_step(prod, cons, TOK // KEY_CHUNK, HEAD_DIM)
        o = jnp.concatenate([ot[:, j * TM:(j + 1) * TM] for ot in outs for j in range(GQA_GRP)], axis=0)
        o_ref[...] = o.T.astype(BF16)

    _pipe_run(run, scratch, GQA_KV_HEADS)


def _full_key_attn(body, q, k, vt, *, w_out, n_dots, m_len, name):
    return pl.pallas_call(
        body,
        grid=(N_CHAINS + 1,),
        in_specs=[
            pl.BlockSpec((None, TM, q.shape[-1]), _prod_chain),
            pl.BlockSpec((None, TOK, k.shape[-1]), lambda i: (_prod_chain(i)[0], 0, 0)),
            pl.BlockSpec((None, vt.shape[1], TOK), lambda i: (_cons_chain(i)[0], 0, 0)),
        ],
        out_specs=pl.BlockSpec((None, TM, w_out), _cons_chain),
        out_shape=jax.ShapeDtypeStruct((BATCH, SEQ, w_out), BF16),
        scratch_shapes=_pipe_scratch(TOK, m_len, n_dots),
        compiler_params=_params(1),
        name=name,
    )(q, k, vt)


def _gqa_attn(q, k, vt):
    return _full_key_attn(_gqa_body, q, k, vt, w_out=GQA_Q_WIDTH, n_dots=GQA_KV_HEADS, m_len=GQA_GRP * TM,
                          name="gqa_attn")


def _mla_body(q_ref, k_ref, vt_ref, o_ref, *scratch):
    def run(sp, sc, s, m):
        prod, cons = [], []
        for d in range(MLA_HEADS):
            def k_chunk(c, d=d):
                return k_ref[c * KEY_CHUNK:(c + 1) * KEY_CHUNK, d * MLA_PAD:(d + 1) * MLA_PAD]

            def vt_chunk(c, d=d):
                return vt_ref[d * MLA_V:(d + 1) * MLA_V, c * KEY_CHUNK:(c + 1) * KEY_CHUNK]

            prod.append((q_ref[:, d * MLA_PAD:(d + 1) * MLA_PAD], k_chunk, None, s[sp][d], m[sp][d]))
            cons.append((vt_chunk, s[sc][d], m[sc][d]))
        outs = _pipe_step(prod, cons, TOK // KEY_CHUNK, MLA_V)
        o_ref[...] = jnp.concatenate(outs, axis=0).T.astype(BF16)

    _pipe_run(run, scratch, MLA_HEADS)


def _mla_attn(q, k, vt):
    return _full_key_attn(_mla_body, q, k, vt, w_out=MLA_V_WIDTH, n_dots=MLA_HEADS, m_len=TM, name="mla_attn")


def _na_start_block(t):
    return jnp.clip(t - 1, 0, NT_LAT - NA_KBLK)


def _na_body(q_ref, k0_ref, k1_ref, k2_ref, kc_ref, v0_ref, v1_ref, v2_ref, vc_ref, bias_ref, o_ref, *scratch):
    k_blocks = (k0_ref, k1_ref, k2_ref, kc_ref)
    v_blocks = (v0_ref, v1_ref, v2_ref, vc_ref)

    def run(sp, sc, s, m):
        prod, cons = [], []
        for d in range(NA_HEADS):
            def k_chunk(c, d=d):
                return k_blocks[c][:, d * HEAD_DIM:(d + 1) * HEAD_DIM]

            def bias_chunk(c, d=d):
                return None if c == NA_KBLK else bias_ref[d, c * KEY_CHUNK:(c + 1) * KEY_CHUNK, :]

            def vt_chunk(c, d=d):
                return v_blocks[c][d * HEAD_DIM:(d + 1) * HEAD_DIM, :]

            prod.append((q_ref[:, d * HEAD_DIM:(d + 1) * HEAD_DIM], k_chunk, bias_chunk, s[sp][d], m[sp][d]))
            cons.append((vt_chunk, s[sc][d], m[sc][d]))
        outs = _pipe_step(prod, cons, NA_NKEY // KEY_CHUNK, HEAD_DIM)
        o_ref[...] = jnp.concatenate(outs, axis=0).T.astype(BF16)

    _pipe_run(run, scratch, NA_HEADS)


def _na_attn(q, k, vt, bias, layer):
    def k_spec(j):
        def index(i):
            b, t, _ = _prod_chain(i)
            return b, (NT_LAT if j == NA_KBLK else _na_start_block(t) + j), 0
        return pl.BlockSpec((None, TM, NA_WIDTH), index)

    def vt_spec(j):
        def index(i):
            b, t, _ = _cons_chain(i)
            return b, 0, (NT_LAT if j == NA_KBLK else _na_start_block(t) + j)
        return pl.BlockSpec((None, NA_WIDTH, TM), index)

    def bias_index(i):
        t = _prod_chain(i)[1]
        return layer, 0, jnp.where(t == 0, 0, jnp.where(t == NT_LAT - 1, 2, 1)), 0, 0

    return pl.pallas_call(
        _na_body,
        grid=(N_CHAINS + 1,),
        in_specs=[pl.BlockSpec((None, TM, NA_WIDTH), _prod_chain)]
                 + [k_spec(j) for j in range(NA_KBLK + 1)] + [vt_spec(j) for j in range(NA_KBLK + 1)]
                 + [pl.BlockSpec((None, NA_HEADS, None, NA_NLOCAL, TM), bias_index)],
        out_specs=pl.BlockSpec((None, TM, NA_WIDTH), _cons_chain),
        out_shape=jax.ShapeDtypeStruct((BATCH, SEQ, NA_WIDTH), BF16),
        scratch_shapes=_pipe_scratch(NA_NKEY, TM, NA_HEADS),
        compiler_params=_params(1),
        name="na_attn",
    )(q, *([k] * (NA_KBLK + 1)), *([vt] * (NA_KBLK + 1)), bias)


def _na_bias_table(rel_bias):
    lead = rel_bias.shape[:-2]
    rel_bias = rel_bias.reshape((-1,) + rel_bias.shape[-2:])
    nh = rel_bias.shape[0]
    cols = np.arange(GRID_W)
    c0 = np.clip(cols - NA_COLS // 2, 0, GRID_W - NA_COLS)
    col_in = (cols[None, :] >= c0[:, None]) & (cols[None, :] < c0[:, None] + NA_COLS)
    n_dr = 2 * NA_ROWS - 1
    pad = GRID_W - NA_COLS
    rbp = jnp.pad(rel_bias.astype(F32) * LOG2_E, ((0, 0), (0, 0), (pad, pad)), mode="edge")
    tiles = jnp.stack([rbp[:, :, GRID_W - 1 - cq:2 * GRID_W - 1 - cq] for cq in range(GRID_W)], axis=2)
    tiles = jnp.where(col_in[None, None], tiles, NEG_BIG)
    tiles = jnp.concatenate([tiles, jnp.full((nh, 1, GRID_W, GRID_W), NEG_BIG, F32)], axis=1)
    dr_blk = np.full((3, NA_QROWS, NA_KROWS), n_dr, np.int32)
    for cls, tile in enumerate((0, 2, NT_LAT - 1)):
        start = int(np.clip(tile - 1, 0, NT_LAT - NA_KBLK)) * NA_QROWS
        for qr in range(NA_QROWS):
            r = tile * NA_QROWS + qr
            r0 = int(np.clip(r - NA_ROWS // 2, 0, ROWS - NA_ROWS))
            for kj in range(NA_KROWS):
                kr = start + kj
                if r0 <= kr < r0 + NA_ROWS:
                    dr_blk[cls, qr, kj] = kr - r + NA_ROWS - 1
    local = jnp.take(tiles, jnp.asarray(dr_blk.reshape(-1)), axis=1)
    local = local.reshape(nh, 3, NA_QROWS, NA_KROWS, GRID_W, GRID_W).transpose(0, 1, 2, 4, 3, 5)
    local = local.reshape(nh, 3, TM, NA_NLOCAL)
    return local.transpose(0, 1, 3, 2).reshape(lead + (3, NA_NLOCAL, TM))


def _merge_body(h_ref, mod_ref, g_ref, wgate_ref, wa_ref, wb_ref, wm_ref, wo_ref, *rest, split):
    o_ref = rest[-1]
    for r in range(BGROUP):
        branch, pos = [], 0
        for has_ctx in split:
            if has_ctx:
                is_ctx = pl.program_id(0) >= NT_LAT
                branch.append(jnp.where(is_ctx, rest[pos + 1][r], rest[pos][r]))
            else:
                branch.append(rest[pos][r])
            pos += 2 if has_ctx else 1
        h = h_ref[r]
        mod = mod_ref[r]
        n = _rmsnorm(h, g_ref[...]) * (1.0 + mod[1:2]) + mod[0:1]
        gates = _sigmoid(_bdot(n.astype(BF16), wgate_ref[...]))
        y = (gates[:, :D_MODEL] * _bdot(branch[0], wa_ref[...])
             + gates[:, D_MODEL:2 * D_MODEL] * _bdot(branch[1], wb_ref[...])
             + gates[:, 2 * D_MODEL:] * _bdot(branch[2], wm_ref[...]))
        o_ref[r] = h + mod[2:3] * _bdot(y.astype(BF16), wo_ref[...])


def _merge(h, mod, layer, g_norm, w_gate, branches, wa, wb, wm, wo, *, n_tiles):
    split = tuple(isinstance(br, tuple) for br in branches)
    arrays, specs = [], []
    for br in branches:
        if isinstance(br, tuple):
            lat, ctx = br
            arrays += [lat, ctx]
            specs += [pl.BlockSpec((BGROUP, TM, lat.shape[-1]), lambda t, b: (b, jnp.minimum(t, NT_LAT - 1), 0)),
                      pl.BlockSpec((BGROUP, TM, ctx.shape[-1]), lambda t, b: (b, 0, 0))]
        else:
            arrays.append(br)
            specs.append(_tok_spec(br.shape[-1]))
    return pl.pallas_call(
        functools.partial(_merge_body, split=split),
        grid=(n_tiles, BATCH // BGROUP),
        in_specs=[_tok_spec(D_MODEL), _mod_spec(layer, 1)]
                 + [_layer_spec(a, layer) for a in (g_norm, w_gate, wa, wb, wm, wo)] + specs,
        out_specs=_tok_spec(D_MODEL),
        out_shape=jax.ShapeDtypeStruct((BATCH, n_tiles * TM, D_MODEL), F32),
        compiler_params=_params(2),
        name="gated_merge",
    )(h, mod, g_norm, w_gate, wa, wb, wm, wo, *arrays)


def _rope_tables():
    t = np.arange(SEQ)
    row = (t // GRID_W).astype(np.float32)[:, None]
    col = (t % GRID_W).astype(np.float32)[:, None]

    def head_tables(d_rot):
        half = d_rot // 2
        freqs = np.float32(ROPE_THETA) ** (-np.arange(0, half, 2, dtype=np.float32) / np.float32(half))
        r, c = row * freqs, col * freqs
        ang = np.concatenate([r, r, c, c], axis=-1).astype(np.float32)
        first = (np.arange(d_rot) % half) < (half // 2)
        cos, sin = np.cos(ang), np.sin(ang)
        zero = np.zeros_like(sin)
        return cos, np.where(first, -sin, zero), np.where(first, zero, sin)

    def place(tabs, reps, lo, group):
        out = []
        for k, tab in enumerate(tabs):
            fill = 1.0 if k == 0 else 0.0
            d_rot = tab.shape[-1]
            g = np.concatenate([np.full((SEQ, lo), fill, np.float32), tab,
                                np.full((SEQ, group - lo - d_rot), fill, np.float32)], axis=-1)
            g = np.tile(g, (1, reps))
            ctx = np.full((CTX_LEN, reps * group), fill, np.float32)
            out.append(jnp.asarray(np.concatenate([g, ctx], axis=0), F32))
        return out

    tb = head_tables(HEAD_DIM)
    tm = head_tables(MLA_ROPE)
    return (place(tb, GQA_HEADS, 0, HEAD_DIM)
            + place(tm, MLA_HEADS, MLA_NOPE, MLA_PAD)
            + place(tm, 1, 0, LANES))


def _seg_matrix(width):
    idx = np.arange(width) // HEAD_DIM
    return jnp.asarray((idx[:, None] == idx[None, :]).astype(np.float32) / HEAD_DIM, BF16)


def _place_matrix():
    m = np.zeros((LANES, MLA_QK_WIDTH), np.float32)
    for hd in range(MLA_HEADS):
        for j in range(MLA_ROPE):
            m[j, hd * MLA_PAD + MLA_NOPE + j] = 1.0
    return jnp.asarray(m, BF16)


def _pad_heads(w, n_heads, per_head, keep_lo, keep_hi):
    d, k = w.shape[:2]
    wh = w.reshape(d, k, n_heads, per_head)[..., keep_lo:keep_hi]
    wh = jnp.pad(wh, ((0, 0), (0, 0), (0, 0), (0, MLA_PAD - (keep_hi - keep_lo))))
    return wh.reshape(d, k, n_heads * MLA_PAD)


def kernel(x, c, ctx, c_ctx, w_ada, b_ada, ffn1_norm, ffn1_w_gate, ffn1_w_up, ffn1_w_down, mix_norm, w_in,
           na_rel_bias, gqa_q_norm, gqa_k_norm, mla_q_norm, mla_kv_norm, mla_w_uq, mla_w_ukv,
           w_branch_a, w_branch_b, w_branch_c, w_out, ffn2_norm, ffn2_w_gate, ffn2_w_up, ffn2_w_down,
           final_norm):
    assert x.shape == (BATCH, SEQ, D_MODEL) and ctx.shape == (BATCH, CTX_LEN, D_MODEL)

    c_all = jnp.concatenate([c, jnp.tile(c_ctx[None, :], (BGROUP, 1)),
                             jnp.zeros((MOD_ROWS - BATCH - BGROUP, D_MODEL), F32)], axis=0)
    mod_all = _ada(c_all, w_ada, b_ada).reshape(DEPTH, MOD_ROWS, 3, 3, D_MODEL)

    tables = _rope_tables()
    seg_q, seg_k, place = _seg_matrix(2 * LANES), _seg_matrix(GQA_KV_WIDTH), _place_matrix()
    fin = final_norm.reshape(1, D_MODEL)

    def row(v):
        return v[:, None, :]

    w_qkv = w_in[:, :, :QKV_WIDTH].astype(BF16)
    w_gate = w_in[:, :, N_QKV_COLS:].astype(BF16)
    w_uq = _pad_heads(mla_w_uq, MLA_HEADS, MLA_NOPE + MLA_ROPE, 0, MLA_NOPE + MLA_ROPE).astype(BF16)
    w_uk = _pad_heads(mla_w_ukv, MLA_HEADS, MLA_NOPE + MLA_V, 0, MLA_NOPE)
    w_uv = mla_w_ukv.reshape(DEPTH, MLA_KV_RANK, MLA_HEADS, MLA_NOPE + MLA_V)[..., MLA_NOPE:]
    w_ukv = jnp.concatenate([w_uk, w_uv.reshape(DEPTH, MLA_KV_RANK, MLA_V_WIDTH)], axis=-1).astype(BF16)
    consts = (seg_q, seg_k, row(jnp.tile(gqa_q_norm, (1, GQA_HEADS))), row(jnp.tile(gqa_k_norm, (1, GQA_KV_HEADS))),
              row(mla_q_norm), row(mla_kv_norm), w_uq, w_ukv, place)
    ffn1 = (row(ffn1_norm), ffn1_w_gate.astype(BF16), ffn1_w_up.astype(BF16), ffn1_w_down.astype(BF16))
    ffn2 = (row(ffn2_norm), ffn2_w_gate.astype(BF16), ffn2_w_up.astype(BF16), ffn2_w_down.astype(BF16))
    branch_w = (w_branch_a.astype(BF16), w_branch_b.astype(BF16), w_branch_c.astype(BF16), w_out.astype(BF16))
    na_bias = _na_bias_table(na_rel_bias)
    g_mix = row(mix_norm)

    h = (x, ctx)
    for i in range(DEPTH):
        last = i == DEPTH - 1
        h = _ffn(h, mod_all, i, 0, *ffn1, fin, n_tiles=NT_ALL, final=False)
        qa, ka, va, qb, kb, vb, qm, km, vm = _proj(h, mod_all, i, g_mix, w_qkv, consts, tables)

        oa = _na_attn(qa, ka, va, na_bias, i)
        ob = _gqa_attn(qb, kb, vb)
        om = _mla_attn(qm, km, vm)
        if not last:
            oa_c, ob_c, om_c = _ctx_attn(((qa, ka, va), (qb, kb, vb), (qm, km, vm)))
            oa, ob, om = (oa, oa_c), (ob, ob_c), (om, om_c)

        n_tiles = NT_LAT if last else NT_ALL
        h = _merge(h, mod_all, i, g_mix, w_gate, (oa, ob, om), *branch_w, n_tiles=n_tiles)
        h = _ffn(h, mod_all, i, 2, *ffn2, fin, n_tiles=n_tiles, final=last)
    return h
```

```python
import functools

import numpy as np
import jax
import jax.numpy as jnp
from jax import lax
from jax.experimental import pallas as pl
from jax.experimental.pallas import tpu as pltpu

D_MODEL = 1024
BATCH = 32
SEQ = 2048
DEPTH = 2
CTX_LEN = 256
GRID_W = 64
HEAD_DIM = 64
NA_HEADS = 4
NA_ROWS = 8
NA_COLS = 16
GQA_HEADS = 8
GQA_KV_HEADS = 2
MLA_HEADS = 4
MLA_Q_RANK = 256
MLA_KV_RANK = 128
MLA_NOPE = 64
MLA_ROPE = 32
MLA_V = 64
N_MOD = 9
D_FF = ((8 * D_MODEL // 3 + 127) // 128) * 128
ROPE_THETA = 10000.0
EPS = 1e-6
NEG_BIG = -1e30

NA_WIDTH = NA_HEADS * HEAD_DIM
GQA_Q_WIDTH = GQA_HEADS * HEAD_DIM
GQA_KV_WIDTH = GQA_KV_HEADS * HEAD_DIM
GQA_GRP = GQA_HEADS // GQA_KV_HEADS
MLA_V_WIDTH = MLA_HEADS * MLA_V
GATE_WIDTH = 3 * D_MODEL
ROWS = SEQ // GRID_W

LANES = 128
SUBLANES = 8
BF16_SUBLANES = 16
LOG2_E = 1.4426950408889634
TOK = SEQ + CTX_LEN
TM = 256
NT_ALL = TOK // TM
NT_LAT = SEQ // TM
BGROUP = 4
MOD_ROWS = 40
CTX_ROW = BATCH
ADA_BLOCK = 1152
MLA_PAD = LANES
MLA_QK_WIDTH = MLA_HEADS * MLA_PAD
N_QKV_COLS = 1952
QKV_WIDTH = 2048
NA_QROWS = TM // GRID_W
NA_KROWS = 12
NA_KBLK = NA_KROWS * GRID_W // TM
NA_NLOCAL = NA_KROWS * GRID_W
NA_NKEY = NA_NLOCAL + CTX_LEN
KEY_CHUNK = 256
VMEM_LIMIT = 56 * 1024 * 1024

F32 = jnp.float32
BF16 = jnp.bfloat16

_OFF_AQ, _OFF_AK, _OFF_AV = 0, 256, 512
_OFF_BQ, _OFF_BK, _OFF_BV = 768, 1280, 1408
_OFF_CQ, _OFF_CKV, _OFF_CKR = 1536, 1792, 1920


def _params(n_axes):
    return pltpu.CompilerParams(dimension_semantics=("arbitrary",) * n_axes,
                                vmem_limit_bytes=VMEM_LIMIT)


def _const_spec(shape):
    nd = len(shape)
    return pl.BlockSpec(shape, lambda *_: (0,) * nd, pipeline_mode=pl.Buffered(1))


def _layer_spec(arr, layer):
    nd = arr.ndim - 1
    return pl.BlockSpec((None,) + arr.shape[1:], lambda *_: (layer,) + (0,) * nd, pipeline_mode=pl.Buffered(1))


def _mod_spec(layer, sub):
    return pl.BlockSpec((None, SUBLANES, 3 * D_MODEL),
                        lambda t, b: (layer, jnp.where(t >= NT_LAT, CTX_ROW, b * BGROUP) // SUBLANES, sub))


def _mod_vectors(mod_ref, r):
    first = jnp.where(pl.program_id(0) >= NT_LAT, CTX_ROW, pl.program_id(1) * BGROUP) % SUBLANES
    row = mod_ref[pl.ds(first + r, 1), :]
    return row[:, :D_MODEL], row[:, D_MODEL:2 * D_MODEL], row[:, 2 * D_MODEL:]


def _tok_spec(width):
    return pl.BlockSpec((BGROUP, TM, width), lambda t, b: (b, t, 0))


def _sigmoid(x):
    return 1.0 / (1.0 + jnp.exp(-x))


def _rmsnorm(x, g):
    return (x * lax.rsqrt(jnp.mean(x * x, axis=-1, keepdims=True) + EPS)) * g


def _bdot(a, b):
    return jnp.dot(a, b, preferred_element_type=F32)


def _dot_nt(a, b):
    return lax.dot_general(a, b, (((1,), (1,)), ((), ())), preferred_element_type=F32)


def _ada_body(c_ref, w_ref, b_ref, o_ref):
    c = c_ref[...]
    s = c * _sigmoid(c)
    o_ref[...] = _bdot(s.astype(BF16), w_ref[...].astype(BF16)) + b_ref[...]


def _ada(c_all, w_ada, b_ada):
    nblk = (N_MOD * D_MODEL) // ADA_BLOCK
    return pl.pallas_call(
        _ada_body,
        grid=(DEPTH, nblk),
        in_specs=[
            pl.BlockSpec((MOD_ROWS, D_MODEL), lambda i, j: (0, 0)),
            pl.BlockSpec((None, D_MODEL, ADA_BLOCK), lambda i, j: (i, 0, j)),
            pl.BlockSpec((None, 1, ADA_BLOCK), lambda i, j: (i, 0, j)),
        ],
        out_specs=pl.BlockSpec((None, MOD_ROWS, ADA_BLOCK), lambda i, j: (i, 0, j)),
        out_shape=jax.ShapeDtypeStruct((DEPTH, MOD_ROWS, N_MOD * D_MODEL), F32),
        compiler_params=_params(2),
        name="ada_mod",
    )(c_all, w_ada, b_ada.reshape(DEPTH, 1, N_MOD * D_MODEL))


def _ffn_body(*refs, final, split):
    if split:
        h_ref, hc_ref, mod_ref, g_ref, wg_ref, wu_ref, wd_ref, fin_ref, o_ref = refs
    else:
        h_ref, mod_ref, g_ref, wg_ref, wu_ref, wd_ref, fin_ref, o_ref = refs
    for r in range(BGROUP):
        h = h_ref[r]
        if split:
            h = jnp.where(pl.program_id(0) >= NT_LAT, hc_ref[r], h)
        shift, scale, gate = _mod_vectors(mod_ref, r)
        n = _rmsnorm(h, g_ref[...]) * (1.0 + scale) + shift
        nb = n.astype(BF16)
        g = _bdot(nb, wg_ref[...])
        u = _bdot(nb, wu_ref[...])
        a = (g * _sigmoid(g)) * u
        d = _bdot(a.astype(BF16), wd_ref[...])
        out = h + 0.5 * gate * d
        if final:
            out = _rmsnorm(out, fin_ref[...])
        o_ref[r] = out


def _ffn(h, mod, layer, sub, g_norm, wg, wu, wd, fin, *, n_tiles, final):
    split = isinstance(h, tuple)
    if split:
        h_arrays = list(h)
        h_specs = [pl.BlockSpec((BGROUP, TM, D_MODEL), lambda t, b: (b, jnp.minimum(t, NT_LAT - 1), 0)),
                   pl.BlockSpec((BGROUP, TM, D_MODEL), lambda t, b: (b, 0, 0))]
    else:
        h_arrays, h_specs = [h], [_tok_spec(D_MODEL)]
    return pl.pallas_call(
        functools.partial(_ffn_body, final=final, split=split),
        grid=(n_tiles, BATCH // BGROUP),
        in_specs=h_specs + [_mod_spec(layer, sub)] + [_layer_spec(a, layer) for a in (g_norm, wg, wu, wd)]
                 + [_const_spec((1, D_MODEL))],
        out_specs=_tok_spec(D_MODEL),
        out_shape=jax.ShapeDtypeStruct((BATCH, n_tiles * TM, D_MODEL), F32),
        compiler_params=_params(2),
        name="half_ffn",
    )(*h_arrays, mod, g_norm, wg, wu, wd, fin)


def _rope(x, cos, sin_a, sin_b, half):
    w = x.shape[-1]
    return x * cos + pltpu.roll(x, w - half, 1) * sin_a + pltpu.roll(x, half, 1) * sin_b


def _seg_mean_sq(x, seg):
    w = seg.shape[0]
    parts = []
    for lo_col in range(0, x.shape[-1], w):
        x2 = x[:, lo_col:lo_col + w]
        x2 = x2 * x2
        hi = x2.astype(BF16)
        lo = (x2 - hi.astype(F32)).astype(BF16)
        parts.append(_bdot(hi, seg) + _bdot(lo, seg))
    return parts[0] if len(parts) == 1 else jnp.concatenate(parts, axis=1)


def _proj_body(h_ref, mod_ref, g_ref, w_ref, segq_ref, segk_ref, qn_ref, kn_ref, mqn_ref, mkvn_ref,
               wuq_ref, wukv_ref, place_ref,
               cb_ref, sab_ref, sbb_ref, cm_ref, sam_ref, sbm_ref, ck_ref, sak_ref, sbk_ref,
               qa_ref, ka_ref, va_ref, qb_ref, kb_ref, vb_ref, qm_ref, km_ref, vm_ref):
    scale = LOG2_E * HEAD_DIM ** -0.5
    mla_scale = LOG2_E * (MLA_NOPE + MLA_ROPE) ** -0.5
    cb, sab, sbb = cb_ref[...], sab_ref[...], sbb_ref[...]
    for r in range(BGROUP):
        shift, mscale, _ = _mod_vectors(mod_ref, r)
        n = _rmsnorm(h_ref[r], g_ref[...]) * (1.0 + mscale) + shift
        z = _bdot(n.astype(BF16), w_ref[...])

        qa_ref[r] = (z[:, _OFF_AQ:_OFF_AQ + NA_WIDTH] * scale).astype(BF16)
        ka_ref[r] = z[:, _OFF_AK:_OFF_AK + NA_WIDTH].astype(BF16)
        va_ref[r] = z[:, _OFF_AV:_OFF_AV + NA_WIDTH].T.astype(BF16)

        bq = z[:, _OFF_BQ:_OFF_BQ + GQA_Q_WIDTH]
        bq = (bq * lax.rsqrt(_seg_mean_sq(bq, segq_ref[...]) + EPS)) * qn_ref[...]
        qb_ref[r] = (_rope(bq, cb, sab, sbb, HEAD_DIM // 4) * scale).astype(BF16)
        bk = z[:, _OFF_BK:_OFF_BK + GQA_KV_WIDTH]
        bk = (bk * lax.rsqrt(_seg_mean_sq(bk, segk_ref[...]) + EPS)) * kn_ref[...]
        kb_ref[r] = _rope(bk, cb[:, :GQA_KV_WIDTH], sab[:, :GQA_KV_WIDTH], sbb[:, :GQA_KV_WIDTH],
                          HEAD_DIM // 4).astype(BF16)
        vb_ref[r] = z[:, _OFF_BV:_OFF_BV + GQA_KV_WIDTH].T.astype(BF16)

        cq = _rmsnorm(z[:, _OFF_CQ:_OFF_CQ + MLA_Q_RANK], mqn_ref[...])
        q_lat = _bdot(cq.astype(BF16), wuq_ref[...])
        qm_ref[r] = (_rope(q_lat, cm_ref[...], sam_ref[...], sbm_ref[...], MLA_ROPE // 4)
                     * mla_scale).astype(BF16)
        ckv = _rmsnorm(z[:, _OFF_CKV:_OFF_CKV + MLA_KV_RANK], mkvn_ref[...])
        kv_lat = _bdot(ckv.astype(BF16), wukv_ref[...])
        vm_ref[r] = kv_lat[:, MLA_QK_WIDTH:].T.astype(BF16)
        lane = lax.broadcasted_iota(jnp.int32, (1, LANES), 1)
        kr = jnp.where(lane < MLA_ROPE, z[:, _OFF_CKR:_OFF_CKR + LANES], 0.0)
        kr = _rope(kr, ck_ref[...], sak_ref[...], sbk_ref[...], MLA_ROPE // 4).astype(BF16)
        km_ref[r] = (kv_lat[:, :MLA_QK_WIDTH] + _bdot(kr, place_ref[...])).astype(BF16)


def _proj(h, mod, layer, g_norm, w_qkv, consts, tables):
    widths = (NA_WIDTH, NA_WIDTH, NA_WIDTH, GQA_Q_WIDTH, GQA_KV_WIDTH, GQA_KV_WIDTH,
              MLA_QK_WIDTH, MLA_QK_WIDTH, MLA_V_WIDTH)
    is_value = [k % 3 == 2 for k in range(len(widths))]
    tab_specs = [pl.BlockSpec((TM, t.shape[-1]), lambda t_, b: (t_, 0)) for t in tables]
    return pl.pallas_call(
        _proj_body,
        grid=(NT_ALL, BATCH // BGROUP),
        in_specs=[_tok_spec(D_MODEL), _mod_spec(layer, 1), _layer_spec(g_norm, layer), _layer_spec(w_qkv, layer)]
                 + [_layer_spec(c, layer) if c.ndim == 3 else _const_spec(c.shape) for c in consts] + tab_specs,
        out_specs=[pl.BlockSpec((BGROUP, w, TM), lambda t, b: (b, 0, t)) if v else _tok_spec(w)
                   for v, w in zip(is_value, widths)],
        out_shape=[jax.ShapeDtypeStruct((BATCH, w, TOK) if v else (BATCH, TOK, w), BF16)
                   for v, w in zip(is_value, widths)],
        compiler_params=_params(2),
        name="mix_proj",
    )(h, mod, g_norm, w_qkv, *consts, *tables)


_CTX_MIXERS = ((NA_HEADS, 1, HEAD_DIM, HEAD_DIM), (GQA_HEADS, GQA_GRP, HEAD_DIM, HEAD_DIM),
               (MLA_HEADS, 1, MLA_PAD, MLA_V))


def _ctx_attn_body(*refs):
    ones = jnp.ones((BF16_SUBLANES, TM), BF16)
    n_mix = len(_CTX_MIXERS)
    for r in range(BGROUP):
        for mix, (heads, grp, dqk, dv) in enumerate(_CTX_MIXERS):
            q_ref, k_ref, vt_ref = refs[3 * mix:3 * mix + 3]
            outs = []
            for hd in range(heads):
                g = hd // grp
                st = _dot_nt(k_ref[r, :, g * dqk:(g + 1) * dqk], q_ref[r, :, hd * dqk:(hd + 1) * dqk])
                p = jnp.exp2(st - jnp.max(st, axis=0, keepdims=True)).astype(BF16)
                acc = _bdot(jnp.concatenate([vt_ref[r, g * dv:(g + 1) * dv, :], ones], axis=0), p)
                outs.append(acc[:dv] * (1.0 / acc[dv:dv + 1]))
            refs[3 * n_mix + mix][r] = jnp.concatenate(outs, axis=0).T.astype(BF16)


def _ctx_attn(qkv):
    in_specs, arrays = [], []
    for q, k, vt in qkv:
        arrays += [q, k, vt]
        in_specs += [pl.BlockSpec((BGROUP, TM, q.shape[-1]), lambda b: (b, NT_LAT, 0)),
                     pl.BlockSpec((BGROUP, TM, k.shape[-1]), lambda b: (b, NT_LAT, 0)),
                     pl.BlockSpec((BGROUP, vt.shape[1], TM), lambda b: (b, 0, NT_LAT))]
    widths = [heads * dv for heads, _, _, dv in _CTX_MIXERS]
    return pl.pallas_call(
        _ctx_attn_body,
        grid=(BATCH // BGROUP,),
        in_specs=in_specs,
        out_specs=[pl.BlockSpec((BGROUP, TM, w), lambda b: (b, 0, 0)) for w in widths],
        out_shape=[jax.ShapeDtypeStruct((BATCH, TM, w), BF16) for w in widths],
        compiler_params=_params(1),
        name="ctx_attn",
    )(*arrays)


def _pipe_step(prod, cons, n_chunk, dv):
    ones = jnp.ones((BF16_SUBLANES, KEY_CHUNK), BF16)
    m_cons = [m_ref[0:1, :] for _, _, m_ref in cons]
    m_new = [None] * len(prod)
    acc = [None] * len(cons)
    for c in range(n_chunk):
        rows = slice(c * KEY_CHUNK, (c + 1) * KEY_CHUNK)
        for d, (qg, k_chunk, bias_chunk, s_ref, _) in enumerate(prod):
            st = _dot_nt(k_chunk(c), qg)
            bias = None if bias_chunk is None else bias_chunk(c)
            if bias is not None:
                st = st + bias
            s_ref[rows, :] = st
            mc = jnp.max(st, axis=0, keepdims=True)
            m_new[d] = mc if m_new[d] is None else jnp.maximum(m_new[d], mc)
        for d, (vt_chunk, s_ref, _) in enumerate(cons):
            p = jnp.exp2(s_ref[rows, :] - m_cons[d]).astype(BF16)
            part = _bdot(jnp.concatenate([vt_chunk(c), ones], axis=0), p)
            acc[d] = part if acc[d] is None else acc[d] + part
    for d, (_, _, _, _, m_ref) in enumerate(prod):
        m_ref[0:1, :] = m_new[d]
    return [a[:dv] * (1.0 / a[dv:dv + 1]) for a in acc]


def _pipe_run(step, scratch, n_dots):
    i = pl.program_id(0)
    s = [scratch[slot * n_dots:(slot + 1) * n_dots] for slot in range(2)]
    m = [scratch[(2 + slot) * n_dots:(3 + slot) * n_dots] for slot in range(2)]

    @pl.when(i == 0)
    def _():
        for ref in s[1] + m[1]:
            ref[...] = jnp.zeros(ref.shape, F32)

    pl.when(i % 2 == 0)(functools.partial(step, 0, 1, s, m))
    pl.when(i % 2 == 1)(functools.partial(step, 1, 0, s, m))


N_CHAINS = BATCH * NT_LAT


def _prod_chain(i):
    c = jnp.minimum(i, N_CHAINS - 1)
    return c // NT_LAT, c % NT_LAT, 0


def _cons_chain(i):
    c = jnp.maximum(i - 1, 0)
    return c // NT_LAT, c % NT_LAT, 0


def _pipe_scratch(nk, m_len, n_dots):
    return ([pltpu.VMEM((nk, m_len), F32)] * (2 * n_dots)
            + [pltpu.VMEM((8, m_len), F32)] * (2 * n_dots))


def _gqa_body(q_ref, k_ref, vt_ref, o_ref, *scratch):
    def run(sp, sc, s, m):
        q = q_ref[...]
        prod, cons = [], []
        for g in range(GQA_KV_HEADS):
            qg = jnp.concatenate([q[:, (g * GQA_GRP + j) * HEAD_DIM:(g * GQA_GRP + j + 1) * HEAD_DIM]
                                  for j in range(GQA_GRP)], axis=0)

            def k_chunk(c, g=g):
                return k_ref[c * KEY_CHUNK:(c + 1) * KEY_CHUNK, g * HEAD_DIM:(g + 1) * HEAD_DIM]

            def vt_chunk(c, g=g):
                return vt_ref[g * HEAD_DIM:(g + 1) * HEAD_DIM, c * KEY_CHUNK:(c + 1) * KEY_CHUNK]

            prod.append((qg, k_chunk, None, s[sp][g], m[sp][g]))
            cons.append((vt_chunk, s[sc][g], m[sc][g]))
        outs = _pipe_step(prod, cons, TOK // KEY_CHUNK, HEAD_DIM)
        o = jnp.concatenate([ot[:, j * TM:(j + 1) * TM] for ot in outs for j in range(GQA_GRP)], axis=0)
        o_ref[...] = o.T.astype(BF16)

    _pipe_run(run, scratch, GQA_KV_HEADS)


def _full_key_attn(body, q, k, vt, *, w_out, n_dots, m_len, name):
    return pl.pallas_call(
        body,
        grid=(N_CHAINS + 1,),
        in_specs=[
            pl.BlockSpec((None, TM, q.shape[-1]), _prod_chain),
            pl.BlockSpec((None, TOK, k.shape[-1]), lambda i: (_prod_chain(i)[0], 0, 0)),
            pl.BlockSpec((None, vt.shape[1], TOK), lambda i: (_cons_chain(i)[0], 0, 0)),
        ],
        out_specs=pl.BlockSpec((None, TM, w_out), _cons_chain),
        out_shape=jax.ShapeDtypeStruct((BATCH, SEQ, w_out), BF16),
        scratch_shapes=_pipe_scratch(TOK, m_len, n_dots),
        compiler_params=_params(1),
        name=name,
    )(q, k, vt)


def _gqa_attn(q, k, vt):
    return _full_key_attn(_gqa_body, q, k, vt, w_out=GQA_Q_WIDTH, n_dots=GQA_KV_HEADS, m_len=GQA_GRP * TM,
                          name="gqa_attn")


def _mla_body(q_ref, k_ref, vt_ref, o_ref, *scratch):
    def run(sp, sc, s, m):
        prod, cons = [], []
        for d in range(MLA_HEADS):
            def k_chunk(c, d=d):
                return k_ref[c * KEY_CHUNK:(c + 1) * KEY_CHUNK, d * MLA_PAD:(d + 1) * MLA_PAD]

            def vt_chunk(c, d=d):
                return vt_ref[d * MLA_V:(d + 1) * MLA_V, c * KEY_CHUNK:(c + 1) * KEY_CHUNK]

            prod.append((q_ref[:, d * MLA_PAD:(d + 1) * MLA_PAD], k_chunk, None, s[sp][d], m[sp][d]))
            cons.append((vt_chunk, s[sc][d], m[sc][d]))
        outs = _pipe_step(prod, cons, TOK // KEY_CHUNK, MLA_V)
        o_ref[...] = jnp.concatenate(outs, axis=0).T.astype(BF16)

    _pipe_run(run, scratch, MLA_HEADS)


def _mla_attn(q, k, vt):
    return _full_key_attn(_mla_body, q, k, vt, w_out=MLA_V_WIDTH, n_dots=MLA_HEADS, m_len=TM, name="mla_attn")


def _na_start_block(t):
    return jnp.clip(t - 1, 0, NT_LAT - NA_KBLK)


def _na_body(q_ref, k0_ref, k1_ref, k2_ref, kc_ref, v0_ref, v1_ref, v2_ref, vc_ref, bias_ref, o_ref, *scratch):
    k_blocks = (k0_ref, k1_ref, k2_ref, kc_ref)
    v_blocks = (v0_ref, v1_ref, v2_ref, vc_ref)

    def run(sp, sc, s, m):
        prod, cons = [], []
        for d in range(NA_HEADS):
            def k_chunk(c, d=d):
                return k_blocks[c][:, d * HEAD_DIM:(d + 1) * HEAD_DIM]

            def bias_chunk(c, d=d):
                return None if c == NA_KBLK else bias_ref[d, c * KEY_CHUNK:(c + 1) * KEY_CHUNK, :]

            def vt_chunk(c, d=d):
                return v_blocks[c][d * HEAD_DIM:(d + 1) * HEAD_DIM, :]

            prod.append((q_ref[:, d * HEAD_DIM:(d + 1) * HEAD_DIM], k_chunk, bias_chunk, s[sp][d], m[sp][d]))
            cons.append((vt_chunk, s[sc][d], m[sc][d]))
        outs = _pipe_step(prod, cons, NA_NKEY // KEY_CHUNK, HEAD_DIM)
        o_ref[...] = jnp.concatenate(outs, axis=0).T.astype(BF16)

    _pipe_run(run, scratch, NA_HEADS)


def _na_attn(q, k, vt, bias, layer):
    def k_spec(j):
        def index(i):
            b, t, _ = _prod_chain(i)
            return b, (NT_LAT if j == NA_KBLK else _na_start_block(t) + j), 0
        return pl.BlockSpec((None, TM, NA_WIDTH), index)

    def vt_spec(j):
        def index(i):
            b, t, _ = _cons_chain(i)
            return b, 0, (NT_LAT if j == NA_KBLK else _na_start_block(t) + j)
        return pl.BlockSpec((None, NA_WIDTH, TM), index)

    def bias_index(i):
        t = _prod_chain(i)[1]
        return layer, 0, jnp.where(t == 0, 0, jnp.where(t == NT_LAT - 1, 2, 1)), 0, 0

    return pl.pallas_call(
        _na_body,
        grid=(N_CHAINS + 1,),
        in_specs=[pl.BlockSpec((None, TM, NA_WIDTH), _prod_chain)]
                 + [k_spec(j) for j in range(NA_KBLK + 1)] + [vt_spec(j) for j in range(NA_KBLK + 1)]
                 + [pl.BlockSpec((None, NA_HEADS, None, NA_NLOCAL, TM), bias_index)],
        out_specs=pl.BlockSpec((None, TM, NA_WIDTH), _cons_chain),
        out_shape=jax.ShapeDtypeStruct((BATCH, SEQ, NA_WIDTH), BF16),
        scratch_shapes=_pipe_scratch(NA_NKEY, TM, NA_HEADS),
        compiler_params=_params(1),
        name="na_attn",
    )(q, *([k] * (NA_KBLK + 1)), *([vt] * (NA_KBLK + 1)), bias)


def _na_bias_table(rel_bias):
    lead = rel_bias.shape[:-2]
    rel_bias = rel_bias.reshape((-1,) + rel_bias.shape[-2:])
    nh = rel_bias.shape[0]
    cols = np.arange(GRID_W)
    c0 = np.clip(cols - NA_COLS // 2, 0, GRID_W - NA_COLS)
    col_in = (cols[None, :] >= c0[:, None]) & (cols[None, :] < c0[:, None] + NA_COLS)
    n_dr = 2 * NA_ROWS - 1
    pad = GRID_W - NA_COLS
    rbp = jnp.pad(rel_bias.astype(F32) * LOG2_E, ((0, 0), (0, 0), (pad, pad)), mode="edge")
    tiles = jnp.stack([rbp[:, :, GRID_W - 1 - cq:2 * GRID_W - 1 - cq] for cq in range(GRID_W)], axis=2)
    tiles = jnp.where(col_in[None, None], tiles, NEG_BIG)
    tiles = jnp.concatenate([tiles, jnp.full((nh, 1, GRID_W, GRID_W), NEG_BIG, F32)], axis=1)
    dr_blk = np.full((3, NA_QROWS, NA_KROWS), n_dr, np.int32)
    for cls, tile in enumerate((0, 2, NT_LAT - 1)):
        start = int(np.clip(tile - 1, 0, NT_LAT - NA_KBLK)) * NA_QROWS
        for qr in range(NA_QROWS):
            r = tile * NA_QROWS + qr
            r0 = int(np.clip(r - NA_ROWS // 2, 0, ROWS - NA_ROWS))
            for kj in range(NA_KROWS):
                kr = start + kj
                if r0 <= kr < r0 + NA_ROWS:
                    dr_blk[cls, qr, kj] = kr - r + NA_ROWS - 1
    local = jnp.take(tiles, jnp.asarray(dr_blk.reshape(-1)), axis=1)
    local = local.reshape(nh, 3, NA_QROWS, NA_KROWS, GRID_W, GRID_W).transpose(0, 1, 2, 4, 3, 5)
    local = local.reshape(nh, 3, TM, NA_NLOCAL)
    return local.transpose(0, 1, 3, 2).reshape(lead + (3, NA_NLOCAL, TM))


def _merge_body(h_ref, mod_ref, g_ref, wgate_ref, wa_ref, wb_ref, wm_ref, wo_ref, *rest, split):
    o_ref = rest[-1]
    for r in range(BGROUP):
        branch, pos = [], 0
        for has_ctx in split:
            if has_ctx:
                is_ctx = pl.program_id(0) >= NT_LAT
                branch.append(jnp.where(is_ctx, rest[pos + 1][r], rest[pos][r]))
            else:
                branch.append(rest[pos][r])
            pos += 2 if has_ctx else 1
        h = h_ref[r]
        shift, scale, gate = _mod_vectors(mod_ref, r)
        n = _rmsnorm(h, g_ref[...]) * (1.0 + scale) + shift
        gates =_sigmoid(_bdot(n.astype(BF16), wgate_ref[...]))
        y = (gates[:, :D_MODEL] * _bdot(branch[0], wa_ref[...])
             + gates[:, D_MODEL:2 * D_MODEL] * _bdot(branch[1], wb_ref[...])
             + gates[:, 2 * D_MODEL:] * _bdot(branch[2], wm_ref[...]))
        o_ref[r] = h + gate * _bdot(y.astype(BF16), wo_ref[...])


def _merge(h, mod, layer, g_norm, w_gate, branches, wa, wb, wm, wo, *, n_tiles):
    split = tuple(isinstance(br, tuple) for br in branches)
    arrays, specs = [], []
    for br in branches:
        if isinstance(br, tuple):
            lat, ctx = br
            arrays += [lat, ctx]
            specs += [pl.BlockSpec((BGROUP, TM, lat.shape[-1]), lambda t, b: (b, jnp.minimum(t, NT_LAT - 1), 0)),
                      pl.BlockSpec((BGROUP, TM, ctx.shape[-1]), lambda t, b: (b, 0, 0))]
        else:
            arrays.append(br)
            specs.append(_tok_spec(br.shape[-1]))
    return pl.pallas_call(
        functools.partial(_merge_body, split=split),
        grid=(n_tiles, BATCH // BGROUP),
        in_specs=[_tok_spec(D_MODEL), _mod_spec(layer, 1)]
                 + [_layer_spec(a, layer) for a in (g_norm, w_gate, wa, wb, wm, wo)] + specs,
        out_specs=_tok_spec(D_MODEL),
        out_shape=jax.ShapeDtypeStruct((BATCH, n_tiles * TM, D_MODEL), F32),
        compiler_params=_params(2),
        name="gated_merge",
    )(h, mod, g_norm, w_gate, wa, wb, wm, wo, *arrays)


def _rope_tables():
    t = np.arange(SEQ)
    row = (t // GRID_W).astype(np.float32)[:, None]
    col = (t % GRID_W).astype(np.float32)[:, None]

    def head_tables(d_rot):
        half = d_rot // 2
        freqs = np.float32(ROPE_THETA) ** (-np.arange(0, half, 2, dtype=np.float32) / np.float32(half))
        r, c = row * freqs, col * freqs
        ang = np.concatenate([r, r, c, c], axis=-1).astype(np.float32)
        first = (np.arange(d_rot) % half) < (half // 2)
        cos, sin = np.cos(ang), np.sin(ang)
        zero = np.zeros_like(sin)
        return cos, np.where(first, -sin, zero), np.where(first, zero, sin)

    def place(tabs, reps, lo, group):
        out = []
        for k, tab in enumerate(tabs):
            fill = 1.0 if k == 0 else 0.0
            d_rot = tab.shape[-1]
            g = np.concatenate([np.full((SEQ, lo), fill, np.float32), tab,
                                np.full((SEQ, group - lo - d_rot), fill, np.float32)], axis=-1)
            g = np.tile(g, (1, reps))
            ctx = np.full((CTX_LEN, reps * group), fill, np.float32)
            out.append(jnp.asarray(np.concatenate([g, ctx], axis=0), F32))
        return out

    tb = head_tables(HEAD_DIM)
    tm = head_tables(MLA_ROPE)
    return (place(tb, GQA_HEADS, 0, HEAD_DIM)
            + place(tm, MLA_HEADS, MLA_NOPE, MLA_PAD)
            + place(tm, 1, 0, LANES))


def _seg_matrix(width):
    idx = np.arange(width) // HEAD_DIM
    return jnp.asarray((idx[:, None] == idx[None, :]).astype(np.float32) / HEAD_DIM, BF16)


def _place_matrix():
    m = np.zeros((LANES, MLA_QK_WIDTH), np.float32)
    for hd in range(MLA_HEADS):
        for j in range(MLA_ROPE):
            m[j, hd * MLA_PAD + MLA_NOPE + j] = 1.0
    return jnp.asarray(m, BF16)


def _pad_heads(w, n_heads, per_head, keep_lo, keep_hi):
    d, k = w.shape[:2]
    wh = w.reshape(d, k, n_heads, per_head)[..., keep_lo:keep_hi]
    wh = jnp.pad(wh, ((0, 0), (0, 0), (0, 0), (0, MLA_PAD - (keep_hi - keep_lo))))
    return wh.reshape(d, k, n_heads * MLA_PAD)


def kernel(x, c, ctx, c_ctx, w_ada, b_ada, ffn1_norm, ffn1_w_gate, ffn1_w_up, ffn1_w_down, mix_norm, w_in,
           na_rel_bias, gqa_q_norm, gqa_k_norm, mla_q_norm, mla_kv_norm, mla_w_uq, mla_w_ukv,
           w_branch_a, w_branch_b, w_branch_c, w_out, ffn2_norm, ffn2_w_gate, ffn2_w_up, ffn2_w_down,
           final_norm):
    assert x.shape == (BATCH, SEQ, D_MODEL) and ctx.shape == (BATCH, CTX_LEN, D_MODEL)

    c_all = jnp.concatenate([c, jnp.tile(c_ctx[None, :], (BGROUP, 1)),
                             jnp.zeros((MOD_ROWS - BATCH - BGROUP, D_MODEL), F32)], axis=0)
    mod_all = _ada(c_all, w_ada, b_ada)

    tables = _rope_tables()
    seg_q, seg_k, place = _seg_matrix(2 * LANES), _seg_matrix(GQA_KV_WIDTH), _place_matrix()
    fin = final_norm.reshape(1, D_MODEL)

    def row(v):
        return v[:, None, :]

    w_qkv = w_in[:, :, :QKV_WIDTH].astype(BF16)
    w_gate = w_in[:, :, N_QKV_COLS:].astype(BF16)
    w_uq = _pad_heads(mla_w_uq, MLA_HEADS, MLA_NOPE + MLA_ROPE, 0, MLA_NOPE + MLA_ROPE).astype(BF16)
    w_uk = _pad_heads(mla_w_ukv, MLA_HEADS, MLA_NOPE + MLA_V, 0, MLA_NOPE)
    w_uv = mla_w_ukv.reshape(DEPTH, MLA_KV_RANK, MLA_HEADS, MLA_NOPE + MLA_V)[..., MLA_NOPE:]
    w_ukv = jnp.concatenate([w_uk, w_uv.reshape(DEPTH, MLA_KV_RANK, MLA_V_WIDTH)], axis=-1).astype(BF16)
    consts = (seg_q, seg_k, row(jnp.tile(gqa_q_norm, (1, GQA_HEADS))), row(jnp.tile(gqa_k_norm, (1, GQA_KV_HEADS))),
              row(mla_q_norm), row(mla_kv_norm), w_uq, w_ukv, place)
    ffn1 = (row(ffn1_norm), ffn1_w_gate.astype(BF16), ffn1_w_up.astype(BF16), ffn1_w_down.astype(BF16))
    ffn2 = (row(ffn2_norm), ffn2_w_gate.astype(BF16), ffn2_w_up.astype(BF16), ffn2_w_down.astype(BF16))
    branch_w = (w_branch_a.astype(BF16), w_branch_b.astype(BF16), w_branch_c.astype(BF16), w_out.astype(BF16))
    na_bias = _na_bias_table(na_rel_bias)
    g_mix = row(mix_norm)

    h = (x, ctx)
    for i in range(DEPTH):
        last = i == DEPTH - 1
        h = _ffn(h, mod_all, i, 0, *ffn1, fin, n_tiles=NT_ALL, final=False)
        qa, ka, va, qb, kb, vb, qm, km, vm = _proj(h, mod_all, i, g_mix, w_qkv, consts, tables)

        oa = _na_attn(qa, ka, va, na_bias, i)
        ob = _gqa_attn(qb, kb, vb)
        om = _mla_attn(qm, km, vm)
        if not last:
            oa_c, ob_c, om_c = _ctx_attn(((qa, ka, va), (qb, kb, vb), (qm, km, vm)))
            oa, ob, om = (oa, oa_c), (ob, ob_c), (om, om_c)

        n_tiles = NT_LAT if last else NT_ALL
        h = _merge(h, mod_all, i, g_mix, w_gate, (oa, ob, om), *branch_w, n_tiles=n_tiles)
        h = _ffn(h, mod_all, i, 2, *ffn2, fin, n_tiles=n_tiles, final=last)
    return h
```

```python
import functools

import numpy as np
import jax
import jax.numpy as jnp
from jax import lax
from jax.experimental import pallas as pl
from jax.experimental.pallas import tpu as pltpu

D_MODEL = 1024
BATCH = 32
SEQ = 2048
DEPTH = 2
CTX_LEN = 256
GRID_W = 64
HEAD_DIM = 64
NA_HEADS = 4
NA_ROWS = 8
NA_COLS = 16
GQA_HEADS = 8
GQA_KV_HEADS = 2
MLA_HEADS = 4
MLA_Q_RANK = 256
MLA_KV_RANK = 128
MLA_NOPE = 64
MLA_ROPE = 32
MLA_V = 64
N_MOD = 9
D_FF = ((8 * D_MODEL // 3 + 127) // 128) * 128
ROPE_THETA = 10000.0
EPS = 1e-6
NEG_BIG = -1e30

NA_WIDTH = NA_HEADS * HEAD_DIM
GQA_Q_WIDTH = GQA_HEADS * HEAD_DIM
GQA_KV_WIDTH = GQA_KV_HEADS * HEAD_DIM
GQA_GRP = GQA_HEADS // GQA_KV_HEADS
MLA_V_WIDTH = MLA_HEADS * MLA_V
GATE_WIDTH = 3 * D_MODEL
ROWS = SEQ // GRID_W

LANES = 128
SUBLANES = 8
BF16_SUBLANES = 16
LOG2_E = 1.4426950408889634
TOK = SEQ + CTX_LEN
TM = 256
NT_ALL = TOK // TM
NT_LAT = SEQ // TM
BGROUP = 4
MOD_ROWS = 40
CTX_ROW = BATCH
ADA_BLOCK = 1152
MLA_PAD = LANES
MLA_QK_WIDTH = MLA_HEADS * MLA_PAD
N_QKV_COLS = 1952
QKV_WIDTH = 2048
NA_QROWS = TM // GRID_W
NA_KROWS = 12
NA_KBLK = NA_KROWS * GRID_W // TM
NA_NLOCAL = NA_KROWS * GRID_W
NA_NKEY = NA_NLOCAL + CTX_LEN
KEY_CHUNK = 256
VMEM_LIMIT = 56 * 1024 * 1024

F32 = jnp.float32
BF16 = jnp.bfloat16

_OFF_AQ, _OFF_AK, _OFF_AV = 0, 256, 512
_OFF_BQ, _OFF_BK, _OFF_BV = 768, 1280, 1408
_OFF_CQ, _OFF_CKV, _OFF_CKR = 1536, 1792, 1920


def _params(n_axes):
    return pltpu.CompilerParams(dimension_semantics=("arbitrary",) * n_axes,
                                vmem_limit_bytes=VMEM_LIMIT)


def _const_spec(shape):
    nd = len(shape)
    return pl.BlockSpec(shape, lambda *_: (0,) * nd, pipeline_mode=pl.Buffered(1))


def _layer_spec(arr, layer):
    nd = arr.ndim - 1
    return pl.BlockSpec((None,) + arr.shape[1:], lambda *_: (layer,) + (0,) * nd, pipeline_mode=pl.Buffered(1))


def _mod_spec(layer, sub):
    return pl.BlockSpec((None, SUBLANES, 3 * D_MODEL),
                        lambda t, b: (layer, jnp.where(t >= NT_LAT, CTX_ROW, b * BGROUP) // SUBLANES, sub))


def _mod_vectors(mod_ref, r):
    first = jnp.where(pl.program_id(0) >= NT_LAT, CTX_ROW, pl.program_id(1) * BGROUP) % SUBLANES
    row = mod_ref[pl.ds(first + r, 1), :]
    return row[:, :D_MODEL], row[:, D_MODEL:2 * D_MODEL], row[:, 2 * D_MODEL:]


def _tok_spec(width):
    return pl.BlockSpec((BGROUP, TM, width), lambda t, b: (b, t, 0))


def _sigmoid(x):
    return 1.0 / (1.0 + jnp.exp(-x))


def _rmsnorm(x, g):
    return (x * lax.rsqrt(jnp.mean(x * x, axis=-1, keepdims=True) + EPS)) * g


def _bdot(a, b):
    return jnp.dot(a, b, preferred_element_type=F32)


def _dot_tn(a, b):
    return lax.dot_general(a, b, (((0,), (0,)), ((), ())), preferred_element_type=F32)


def _dot_nt(a, b):
    return lax.dot_general(a, b, (((1,), (1,)), ((), ())), preferred_element_type=F32)


def _ada_body(c_ref, w_ref, b_ref, o_ref):
    c = c_ref[...]
    s = c * _sigmoid(c)
    o_ref[...] = _bdot(s.astype(BF16), w_ref[...].astype(BF16)) + b_ref[...]


def _ada(c_all, w_ada, b_ada):
    nblk = (N_MOD * D_MODEL) // ADA_BLOCK
    return pl.pallas_call(
        _ada_body,
        grid=(DEPTH, nblk),
        in_specs=[
            pl.BlockSpec((MOD_ROWS, D_MODEL), lambda i, j: (0, 0)),
            pl.BlockSpec((None, D_MODEL, ADA_BLOCK), lambda i, j: (i, 0, j)),
            pl.BlockSpec((None, 1, ADA_BLOCK), lambda i, j: (i, 0, j)),
        ],
        out_specs=pl.BlockSpec((None, MOD_ROWS, ADA_BLOCK), lambda i, j: (i, 0, j)),
        out_shape=jax.ShapeDtypeStruct((DEPTH, MOD_ROWS, N_MOD * D_MODEL), F32),
        compiler_params=_params(2),
        name="ada_mod",
    )(c_all, w_ada, b_ada.reshape(DEPTH, 1, N_MOD * D_MODEL))


def _ffn_body(*refs, final, split):
    if split:
        h_ref, hc_ref, mod_ref, g_ref, wg_ref, wu_ref, wd_ref, fin_ref, o_ref = refs
    else:
        h_ref, mod_ref, g_ref, wg_ref, wu_ref, wd_ref, fin_ref, o_ref = refs
    for r in range(BGROUP):
        h = h_ref[r]
        if split:
            h = jnp.where(pl.program_id(0) >= NT_LAT, hc_ref[r], h)
        shift, scale, gate = _mod_vectors(mod_ref, r)
        n = _rmsnorm(h, g_ref[...]) * (1.0 + scale) + shift
        nb = n.astype(BF16)
        g = _bdot(nb, wg_ref[...])
        u = _bdot(nb, wu_ref[...])
        a = (g * _sigmoid(g)) * u
        d = _bdot(a.astype(BF16), wd_ref[...])
        out = h + 0.5 * gate * d
        if final:
            out = _rmsnorm(out, fin_ref[...])
        o_ref[r] = out


def _ffn(h, mod, layer, sub, g_norm, wg, wu, wd, fin, *, n_tiles, final):
    split = isinstance(h, tuple)
    if split:
        h_arrays = list(h)
        h_specs = [pl.BlockSpec((BGROUP, TM, D_MODEL), lambda t, b: (b, jnp.minimum(t, NT_LAT - 1), 0)),
                   pl.BlockSpec((BGROUP, TM, D_MODEL), lambda t, b: (b, 0, 0))]
    else:
        h_arrays, h_specs = [h], [_tok_spec(D_MODEL)]
    return pl.pallas_call(
        functools.partial(_ffn_body, final=final, split=split),
        grid=(n_tiles, BATCH // BGROUP),
        in_specs=h_specs + [_mod_spec(layer, sub)] + [_layer_spec(a, layer) for a in (g_norm, wg, wu, wd)]
                 + [_const_spec((1, D_MODEL))],
        out_specs=_tok_spec(D_MODEL),
        out_shape=jax.ShapeDtypeStruct((BATCH, n_tiles * TM, D_MODEL), F32),
        compiler_params=_params(2),
        name="half_ffn",
    )(*h_arrays, mod, g_norm, wg, wu, wd, fin)


def _rope(x, cos, sin_a, sin_b, half):
    w = x.shape[-1]
    return x * cos + pltpu.roll(x, w - half, 1) * sin_a + pltpu.roll(x, half, 1) * sin_b


def _seg_mean_sq(x, seg):
    w = seg.shape[0]
    parts = []
    for lo_col in range(0, x.shape[-1], w):
        x2 = x[:, lo_col:lo_col + w]
        x2 = x2 * x2
        hi = x2.astype(BF16)
        lo = (x2 - hi.astype(F32)).astype(BF16)
        parts.append(_bdot(hi, seg) + _bdot(lo, seg))
    return parts[0] if len(parts) == 1 else jnp.concatenate(parts, axis=1)


def _proj_body(h_ref, mod_ref, g_ref, w_ref, segq_ref, segk_ref, qn_ref, kn_ref, mqn_ref, mkvn_ref,
               wuq_ref, wukv_ref, place_ref,
               cb_ref, sab_ref, sbb_ref, cm_ref, sam_ref, sbm_ref, ck_ref, sak_ref, sbk_ref,
               qa_ref, ka_ref, va_ref, qb_ref, kb_ref, vb_ref, qm_ref, km_ref, vm_ref):
    scale = LOG2_E * HEAD_DIM ** -0.5
    mla_scale = LOG2_E * (MLA_NOPE + MLA_ROPE) ** -0.5
    cb, sab, sbb = cb_ref[...], sab_ref[...], sbb_ref[...]
    for r in range(BGROUP):
        shift, mscale, _ = _mod_vectors(mod_ref, r)
        n = _rmsnorm(h_ref[r], g_ref[...]) * (1.0 + mscale) + shift
        z = _bdot(n.astype(BF16), w_ref[...])

        qa_ref[r] = (z[:, _OFF_AQ:_OFF_AQ + NA_WIDTH] * scale).astype(BF16)
        ka_ref[r] = z[:, _OFF_AK:_OFF_AK + NA_WIDTH].astype(BF16)
        va_ref[r] = z[:, _OFF_AV:_OFF_AV + NA_WIDTH].T.astype(BF16)

        bq = z[:, _OFF_BQ:_OFF_BQ + GQA_Q_WIDTH]
        bq = (bq * lax.rsqrt(_seg_mean_sq(bq, segq_ref[...]) + EPS)) * qn_ref[...]
        qb_ref[r] = (_rope(bq, cb, sab, sbb, HEAD_DIM // 4) * scale).astype(BF16)
        bk = z[:, _OFF_BK:_OFF_BK + GQA_KV_WIDTH]
        bk = (bk * lax.rsqrt(_seg_mean_sq(bk, segk_ref[...]) + EPS)) * kn_ref[...]
        kb_ref[r] = _rope(bk, cb[:, :GQA_KV_WIDTH], sab[:, :GQA_KV_WIDTH], sbb[:, :GQA_KV_WIDTH],
                          HEAD_DIM // 4).astype(BF16)
        vb_ref[r] = z[:, _OFF_BV:_OFF_BV + GQA_KV_WIDTH].T.astype(BF16)

        cq = _rmsnorm(z[:, _OFF_CQ:_OFF_CQ + MLA_Q_RANK], mqn_ref[...])
        q_lat = _bdot(cq.astype(BF16), wuq_ref[...])
        qm_ref[r] = (_rope(q_lat, cm_ref[...], sam_ref[...], sbm_ref[...], MLA_ROPE // 4)
                     * mla_scale).astype(BF16)
        ckv = _rmsnorm(z[:, _OFF_CKV:_OFF_CKV + MLA_KV_RANK], mkvn_ref[...])
        kv_lat = _bdot(ckv.astype(BF16), wukv_ref[...])
        vm_ref[r] = kv_lat[:, MLA_QK_WIDTH:].T.astype(BF16)
        lane = lax.broadcasted_iota(jnp.int32, (1, LANES), 1)
        kr = jnp.where(lane < MLA_ROPE, z[:, _OFF_CKR:_OFF_CKR + LANES], 0.0)
        kr = _rope(kr, ck_ref[...], sak_ref[...], sbk_ref[...], MLA_ROPE // 4).astype(BF16)
        km_ref[r] = (kv_lat[:, :MLA_QK_WIDTH] + _bdot(kr, place_ref[...])).astype(BF16)


def _proj(h, mod, layer, g_norm, w_qkv, consts, tables):
    widths = (NA_WIDTH, NA_WIDTH, NA_WIDTH, GQA_Q_WIDTH, GQA_KV_WIDTH, GQA_KV_WIDTH,
              MLA_QK_WIDTH, MLA_QK_WIDTH, MLA_V_WIDTH)
    is_value = [k % 3 == 2 for k in range(len(widths))]
    tab_specs = [pl.BlockSpec((TM, t.shape[-1]), lambda t_, b: (t_, 0)) for t in tables]
    return pl.pallas_call(
        _proj_body,
        grid=(NT_ALL, BATCH // BGROUP),
        in_specs=[_tok_spec(D_MODEL), _mod_spec(layer, 1), _layer_spec(g_norm, layer), _layer_spec(w_qkv, layer)]
                 + [_layer_spec(c, layer) if c.ndim == 3 else _const_spec(c.shape) for c in consts] + tab_specs,
        out_specs=[pl.BlockSpec((BGROUP, w, TM), lambda t, b: (b, 0, t)) if v else _tok_spec(w)
                   for v, w in zip(is_value, widths)],
        out_shape=[jax.ShapeDtypeStruct((BATCH, w, TOK) if v else (BATCH, TOK, w), BF16)
                   for v, w in zip(is_value, widths)],
        compiler_params=_params(2),
        name="mix_proj",
    )(h, mod, g_norm, w_qkv, *consts, *tables)


_CTX_MIXERS = ((NA_HEADS, 1, HEAD_DIM, HEAD_DIM), (GQA_HEADS, GQA_GRP, HEAD_DIM, HEAD_DIM),
               (MLA_HEADS, 1, MLA_PAD, MLA_V))


def _ctx_attn_body(*refs):
    ones = jnp.ones((BF16_SUBLANES, TM), BF16)
    n_mix = len(_CTX_MIXERS)
    for r in range(BGROUP):
        for mix, (heads, grp, dqk, dv) in enumerate(_CTX_MIXERS):
            q_ref, k_ref, vt_ref = refs[3 * mix:3 * mix + 3]
            outs = []
            for hd in range(heads):
                g = hd // grp
                st = _dot_nt(k_ref[r, :, g * dqk:(g + 1) * dqk], q_ref[r, :, hd * dqk:(hd + 1) * dqk])
                p = jnp.exp2(st - jnp.max(st, axis=0, keepdims=True)).astype(BF16)
                acc = _bdot(jnp.concatenate([vt_ref[r, g * dv:(g + 1) * dv, :], ones], axis=0), p)
                outs.append(acc[:dv] * (1.0 / acc[dv:dv + 1]))
            refs[3 * n_mix + mix][r] = jnp.concatenate(outs, axis=0).astype(BF16)


def _ctx_attn(qkv):
    in_specs, arrays = [], []
    for q, k, vt in qkv:
        arrays += [q, k, vt]
        in_specs += [pl.BlockSpec((BGROUP, TM, q.shape[-1]), lambda b: (b, NT_LAT, 0)),
                     pl.BlockSpec((BGROUP, TM, k.shape[-1]), lambda b: (b, NT_LAT, 0)),
                     pl.BlockSpec((BGROUP, vt.shape[1], TM), lambda b: (b, 0, NT_LAT))]
    widths = [heads * dv for heads, _, _, dv in _CTX_MIXERS]
    return pl.pallas_call(
        _ctx_attn_body,
        grid=(BATCH // BGROUP,),
        in_specs=in_specs,
        out_specs=[pl.BlockSpec((BGROUP, w, TM), lambda b: (b, 0, 0)) for w in widths],
        out_shape=[jax.ShapeDtypeStruct((BATCH, w, TM), BF16) for w in widths],
        compiler_params=_params(1),
        name="ctx_attn",
    )(*arrays)


def _pipe_step(prod, cons, n_chunk, dv):
    ones = jnp.ones((BF16_SUBLANES, KEY_CHUNK), BF16)
    m_cons = [m_ref[0:1, :] for _, _, m_ref in cons]
    m_new = [None] * len(prod)
    acc = [None] * len(cons)
    for c in range(n_chunk):
        rows = slice(c * KEY_CHUNK, (c + 1) * KEY_CHUNK)
        for d, (qg, k_chunk, bias_chunk, s_ref, _) in enumerate(prod):
            st = _dot_nt(k_chunk(c), qg)
            bias = None if bias_chunk is None else bias_chunk(c)
            if bias is not None:
                st = st + bias
            s_ref[rows, :] = st
            mc = jnp.max(st, axis=0, keepdims=True)
            m_new[d] = mc if m_new[d] is None else jnp.maximum(m_new[d], mc)
        for d, (vt_chunk, s_ref, _) in enumerate(cons):
            p = jnp.exp2(s_ref[rows, :] - m_cons[d]).astype(BF16)
            part = _bdot(jnp.concatenate([vt_chunk(c), ones], axis=0), p)
            acc[d] = part if acc[d] is None else acc[d] + part
    for d, (_, _, _, _, m_ref) in enumerate(prod):
        m_ref[0:1, :] = m_new[d]
    return [a[:dv] * (1.0 / a[dv:dv + 1]) for a in acc]


def _pipe_run(step, scratch, n_dots):
    i = pl.program_id(0)
    s = [scratch[slot * n_dots:(slot + 1) * n_dots] for slot in range(2)]
    m = [scratch[(2 + slot) * n_dots:(3 + slot) * n_dots] for slot in range(2)]

    @pl.when(i == 0)
    def _():
        for ref in s[1] + m[1]:
            ref[...] = jnp.zeros(ref.shape, F32)

    pl.when(i % 2 == 0)(functools.partial(step, 0, 1, s, m))
    pl.when(i % 2 == 1)(functools.partial(step, 1, 0, s, m))


N_CHAINS = BATCH * NT_LAT


def _prod_chain(i):
    c = jnp.minimum(i, N_CHAINS - 1)
    return c // NT_LAT, c % NT_LAT, 0


def _cons_chain(i):
    c = jnp.maximum(i - 1, 0)
    return c // NT_LAT, c % NT_LAT, 0


def _pipe_scratch(nk, m_len, n_dots):
    return ([pltpu.VMEM((nk, m_len), F32)] * (2 * n_dots)
            + [pltpu.VMEM((8, m_len), F32)] * (2 * n_dots))


def _gqa_body(q_ref, k_ref, vt_ref, o_ref, *scratch):
    def run(sp, sc, s, m):
        q = q_ref[...]
        prod, cons = [], []
        for g in range(GQA_KV_HEADS):
            qg = jnp.concatenate([q[:, (g * GQA_GRP + j) * HEAD_DIM:(g * GQA_GRP + j + 1) * HEAD_DIM]
                                  for j in range(GQA_GRP)], axis=0)

            def k_chunk(c, g=g):
                return k_ref[c * KEY_CHUNK:(c + 1) * KEY_CHUNK, g * HEAD_DIM:(g + 1) * HEAD_DIM]

            def vt_chunk(c, g=g):
                return vt_ref[g * HEAD_DIM:(g + 1) * HEAD_DIM, c * KEY_CHUNK:(c + 1) * KEY_CHUNK]

            prod.append((qg, k_chunk, None, s[sp][g], m[sp][g]))
            cons.append((vt_chunk, s[sc][g], m[sc][g]))
        outs = _pipe_step(prod, cons, TOK // KEY_CHUNK, HEAD_DIM)
        o = jnp.concatenate([ot[:, j * TM:(j + 1) * TM] for ot in outs for j in range(GQA_GRP)], axis=0)
        o_ref[...] = o.astype(BF16)

    _pipe_run(run, scratch, GQA_KV_HEADS)


def _full_key_attn(body, q, k, vt, *, w_out, n_dots, m_len, name):
    return pl.pallas_call(
        body,
        grid=(N_CHAINS + 1,),
        in_specs=[
            pl.BlockSpec((None, TM, q.shape[-1]), _prod_chain),
            pl.BlockSpec((None, TOK, k.shape[-1]), lambda i: (_prod_chain(i)[0], 0, 0)),
            pl.BlockSpec((None, vt.shape[1], TOK), lambda i: (_cons_chain(i)[0], 0, 0)),
        ],
        out_specs=pl.BlockSpec((None, w_out, TM), lambda i: (_cons_chain(i)[0], 0, _cons_chain(i)[1])),
        out_shape=jax.ShapeDtypeStruct((BATCH, w_out, SEQ), BF16),
        scratch_shapes=_pipe_scratch(TOK, m_len, n_dots),
        compiler_params=_params(1),
        name=name,
    )(q, k, vt)


def _gqa_attn(q, k, vt):
    return _full_key_attn(_gqa_body, q, k, vt, w_out=GQA_Q_WIDTH, n_dots=GQA_KV_HEADS, m_len=GQA_GRP * TM,
                          name="gqa_attn")


def _mla_body(q_ref, k_ref, vt_ref, o_ref, *scratch):
    def run(sp, sc, s, m):
        prod, cons = [], []
        for d in range(MLA_HEADS):
            def k_chunk(c, d=d):
                return k_ref[c * KEY_CHUNK:(c + 1) * KEY_CHUNK, d * MLA_PAD:(d + 1) * MLA_PAD]

            def vt_chunk(c, d=d):
                return vt_ref[d * MLA_V:(d + 1) * MLA_V, c * KEY_CHUNK:(c + 1) * KEY_CHUNK]

            prod.append((q_ref[:, d * MLA_PAD:(d + 1) * MLA_PAD], k_chunk, None, s[sp][d], m[sp][d]))
            cons.append((vt_chunk, s[sc][d], m[sc][d]))
        outs = _pipe_step(prod, cons, TOK // KEY_CHUNK, MLA_V)
        o_ref[...] = jnp.concatenate(outs, axis=0).astype(BF16)

    _pipe_run(run, scratch, MLA_HEADS)


def _mla_attn(q, k, vt):
    return _full_key_attn(_mla_body, q, k, vt, w_out=MLA_V_WIDTH, n_dots=MLA_HEADS, m_len=TM, name="mla_attn")


def _na_start_block(t):
    return jnp.clip(t - 1, 0, NT_LAT - NA_KBLK)


def _na_body(q_ref, k0_ref, k1_ref, k2_ref, kc_ref, v0_ref, v1_ref, v2_ref, vc_ref, bias_ref, o_ref, *scratch):
    k_blocks = (k0_ref, k1_ref, k2_ref, kc_ref)
    v_blocks = (v0_ref, v1_ref, v2_ref, vc_ref)

    def run(sp, sc, s, m):
        prod, cons = [], []
        for d in range(NA_HEADS):
            def k_chunk(c, d=d):
                return k_blocks[c][:, d * HEAD_DIM:(d + 1) * HEAD_DIM]

            def bias_chunk(c, d=d):
                return None if c == NA_KBLK else bias_ref[d, c * KEY_CHUNK:(c + 1) * KEY_CHUNK, :]

            def vt_chunk(c, d=d):
                return v_blocks[c][d * HEAD_DIM:(d + 1) * HEAD_DIM, :]

            prod.append((q_ref[:, d * HEAD_DIM:(d + 1) * HEAD_DIM], k_chunk, bias_chunk, s[sp][d], m[sp][d]))
            cons.append((vt_chunk, s[sc][d], m[sc][d]))
        outs = _pipe_step(prod, cons, NA_NKEY // KEY_CHUNK, HEAD_DIM)
        o_ref[...] = jnp.concatenate(outs, axis=0).astype(BF16)

    _pipe_run(run, scratch, NA_HEADS)


def _na_attn(q, k, vt, bias, layer):
    def k_spec(j):
        def index(i):
            b, t, _ = _prod_chain(i)
            return b, (NT_LAT if j == NA_KBLK else _na_start_block(t) + j), 0
        return pl.BlockSpec((None, TM, NA_WIDTH), index)

    def vt_spec(j):
        def index(i):
            b, t, _ = _cons_chain(i)
            return b, 0, (NT_LAT if j == NA_KBLK else _na_start_block(t) + j)
        return pl.BlockSpec((None, NA_WIDTH, TM), index)

    def bias_index(i):
        t = _prod_chain(i)[1]
        return layer, 0, jnp.where(t == 0, 0, jnp.where(t == NT_LAT - 1, 2, 1)), 0, 0

    return pl.pallas_call(
        _na_body,
        grid=(N_CHAINS + 1,),
        in_specs=[pl.BlockSpec((None, TM, NA_WIDTH), _prod_chain)]
                 + [k_spec(j) for j in range(NA_KBLK + 1)] + [vt_spec(j) for j in range(NA_KBLK + 1)]
                 + [pl.BlockSpec((None, NA_HEADS, None, NA_NLOCAL, TM), bias_index)],
        out_specs=pl.BlockSpec((None, NA_WIDTH, TM), lambda i: (_cons_chain(i)[0], 0, _cons_chain(i)[1])),
        out_shape=jax.ShapeDtypeStruct((BATCH, NA_WIDTH, SEQ), BF16),
        scratch_shapes=_pipe_scratch(NA_NKEY, TM, NA_HEADS),
        compiler_params=_params(1),
        name="na_attn",
    )(q, *([k] * (NA_KBLK + 1)), *([vt] * (NA_KBLK + 1)), bias)


def _na_bias_table(rel_bias):
    lead = rel_bias.shape[:-2]
    rel_bias = rel_bias.reshape((-1,) + rel_bias.shape[-2:])
    nh = rel_bias.shape[0]
    cols = np.arange(GRID_W)
    c0 = np.clip(cols - NA_COLS // 2, 0, GRID_W - NA_COLS)
    col_in = (cols[None, :] >= c0[:, None]) & (cols[None, :] < c0[:, None] + NA_COLS)
    n_dr = 2 * NA_ROWS - 1
    pad = GRID_W - NA_COLS
    rbp = jnp.pad(rel_bias.astype(F32) * LOG2_E, ((0, 0), (0, 0), (pad, pad)), mode="edge")
    tiles = jnp.stack([rbp[:, :, GRID_W - 1 - cq:2 * GRID_W - 1 - cq] for cq in range(GRID_W)], axis=3)
    tiles = jnp.where(col_in.T[None, None], tiles, NEG_BIG)
    tiles = jnp.concatenate([tiles, jnp.full((nh, 1, GRID_W, GRID_W), NEG_BIG, F32)], axis=1)
    dr_blk = np.full((3, NA_QROWS, NA_KROWS), n_dr, np.int32)
    for cls, tile in enumerate((0, 2, NT_LAT - 1)):
        start = int(np.clip(tile - 1, 0, NT_LAT - NA_KBLK)) * NA_QROWS
        for qr in range(NA_QROWS):
            r = tile * NA_QROWS + qr
            r0 = int(np.clip(r - NA_ROWS // 2, 0, ROWS - NA_ROWS))
            for kj in range(NA_KROWS):
                kr = start + kj
                if r0 <= kr < r0 + NA_ROWS:
                    dr_blk[cls, qr, kj] = kr - r + NA_ROWS - 1
    local = jnp.take(tiles, jnp.asarray(dr_blk.reshape(-1)), axis=1)
    local = local.reshape(nh, 3, NA_QROWS, NA_KROWS, GRID_W, GRID_W).transpose(0, 1, 3, 4, 2, 5)
    return local.reshape(lead + (3, NA_NLOCAL, TM))


def _merge_body(h_ref, mod_ref, g_ref, wgate_ref, wa_ref, wb_ref, wm_ref, wo_ref, *rest, split):
    o_ref = rest[-1]
    for r in range(BGROUP):
        branch, pos = [], 0
        for has_ctx in split:
            if has_ctx:
                is_ctx = pl.program_id(0) >= NT_LAT
                branch.append(jnp.where(is_ctx, rest[pos + 1][r], rest[pos][r]))
            else:
                branch.append(rest[pos][r])
            pos += 2 if has_ctx else 1
        h = h_ref[r]
        shift, scale, gate = _mod_vectors(mod_ref, r)
        n = _rmsnorm(h, g_ref[...]) * (1.0 + scale) + shift
        gates =_sigmoid(_bdot(n.astype(BF16), wgate_ref[...]))
        y = (gates[:, :D_MODEL] * _dot_tn(branch[0], wa_ref[...])
             + gates[:, D_MODEL:2 * D_MODEL] * _dot_tn(branch[1], wb_ref[...])
             + gates[:, 2 * D_MODEL:] * _dot_tn(branch[2], wm_ref[...]))
        o_ref[r] = h + gate * _bdot(y.astype(BF16), wo_ref[...])


def _merge(h, mod, layer, g_norm, w_gate, branches, wa, wb, wm, wo, *, n_tiles):
    split = tuple(isinstance(br, tuple) for br in branches)
    arrays, specs = [], []
    for br in branches:
        if isinstance(br, tuple):
            lat, ctx = br
            arrays += [lat, ctx]
            specs += [pl.BlockSpec((BGROUP, lat.shape[1], TM), lambda t, b: (b, 0, jnp.minimum(t, NT_LAT - 1))),
                      pl.BlockSpec((BGROUP, ctx.shape[1], TM), lambda t, b: (b, 0, 0))]
        else:
            arrays.append(br)
            specs.append(pl.BlockSpec((BGROUP, br.shape[1], TM), lambda t, b: (b, 0, t)))
    return pl.pallas_call(
        functools.partial(_merge_body, split=split),
        grid=(n_tiles, BATCH // BGROUP),
        in_specs=[_tok_spec(D_MODEL), _mod_spec(layer, 1)]
                 + [_layer_spec(a, layer) for a in (g_norm, w_gate, wa, wb, wm, wo)] + specs,
        out_specs=_tok_spec(D_MODEL),
        out_shape=jax.ShapeDtypeStruct((BATCH, n_tiles * TM, D_MODEL), F32),
        compiler_params=_params(2),
        name="gated_merge",
    )(h, mod, g_norm, w_gate, wa, wb, wm, wo, *arrays)


def _rope_tables():
    t = np.arange(SEQ)
    row = (t // GRID_W).astype(np.float32)[:, None]
    col = (t % GRID_W).astype(np.float32)[:, None]

    def head_tables(d_rot):
        half = d_rot // 2
        freqs = np.float32(ROPE_THETA) ** (-np.arange(0, half, 2, dtype=np.float32) / np.float32(half))
        r, c = row * freqs, col * freqs
        ang = np.concatenate([r, r, c, c], axis=-1).astype(np.float32)
        first = (np.arange(d_rot) % half) < (half // 2)
        cos, sin = np.cos(ang), np.sin(ang)
        zero = np.zeros_like(sin)
        return cos, np.where(first, -sin, zero), np.where(first, zero, sin)

    def place(tabs, reps, lo, group):
        out = []
        for k, tab in enumerate(tabs):
            fill = 1.0 if k == 0 else 0.0
            d_rot = tab.shape[-1]
            g = np.concatenate([np.full((SEQ, lo), fill, np.float32), tab,
                                np.full((SEQ, group - lo - d_rot), fill, np.float32)], axis=-1)
            g = np.tile(g, (1, reps))
            ctx = np.full((CTX_LEN, reps * group), fill, np.float32)
            out.append(jnp.asarray(np.concatenate([g, ctx], axis=0), F32))
        return out

    tb = head_tables(HEAD_DIM)
    tm = head_tables(MLA_ROPE)
    return (place(tb, GQA_HEADS, 0, HEAD_DIM)
            + place(tm, MLA_HEADS, MLA_NOPE, MLA_PAD)
            + place(tm, 1, 0, LANES))


def _seg_matrix(width):
    idx = np.arange(width) // HEAD_DIM
    return jnp.asarray((idx[:, None] == idx[None, :]).astype(np.float32) / HEAD_DIM, BF16)


def _place_matrix():
    m = np.zeros((LANES, MLA_QK_WIDTH), np.float32)
    for hd in range(MLA_HEADS):
        for j in range(MLA_ROPE):
            m[j, hd * MLA_PAD + MLA_NOPE + j] = 1.0
    return jnp.asarray(m, BF16)


def _pad_heads(w, n_heads, per_head, keep_lo, keep_hi):
    d, k = w.shape[:2]
    wh = w.reshape(d, k, n_heads, per_head)[..., keep_lo:keep_hi]
    wh = jnp.pad(wh, ((0, 0), (0, 0), (0, 0), (0, MLA_PAD - (keep_hi - keep_lo))))
    return wh.reshape(d, k, n_heads * MLA_PAD)


def kernel(x, c, ctx, c_ctx, w_ada, b_ada, ffn1_norm, ffn1_w_gate, ffn1_w_up, ffn1_w_down, mix_norm, w_in,
           na_rel_bias, gqa_q_norm, gqa_k_norm, mla_q_norm, mla_kv_norm, mla_w_uq, mla_w_ukv,
           w_branch_a, w_branch_b, w_branch_c, w_out, ffn2_norm, ffn2_w_gate, ffn2_w_up, ffn2_w_down,
           final_norm):
    assert x.shape == (BATCH, SEQ, D_MODEL) and ctx.shape == (BATCH, CTX_LEN, D_MODEL)

    c_all = jnp.concatenate([c, jnp.tile(c_ctx[None, :], (BGROUP, 1)),
                             jnp.zeros((MOD_ROWS - BATCH - BGROUP, D_MODEL), F32)], axis=0)
    mod_all = _ada(c_all, w_ada, b_ada)

    tables = _rope_tables()
    seg_q, seg_k, place = _seg_matrix(2 * LANES), _seg_matrix(GQA_KV_WIDTH), _place_matrix()
    fin = final_norm.reshape(1, D_MODEL)

    def row(v):
        return v[:, None, :]

    w_qkv = w_in[:, :, :QKV_WIDTH].astype(BF16)
    w_gate = w_in[:, :, N_QKV_COLS:].astype(BF16)
    w_uq = _pad_heads(mla_w_uq, MLA_HEADS, MLA_NOPE + MLA_ROPE, 0, MLA_NOPE + MLA_ROPE).astype(BF16)
    w_uk = _pad_heads(mla_w_ukv, MLA_HEADS, MLA_NOPE + MLA_V, 0, MLA_NOPE)
    w_uv = mla_w_ukv.reshape(DEPTH, MLA_KV_RANK, MLA_HEADS, MLA_NOPE + MLA_V)[..., MLA_NOPE:]
    w_ukv = jnp.concatenate([w_uk, w_uv.reshape(DEPTH, MLA_KV_RANK, MLA_V_WIDTH)], axis=-1).astype(BF16)
    consts = (seg_q, seg_k, row(jnp.tile(gqa_q_norm, (1, GQA_HEADS))), row(jnp.tile(gqa_k_norm, (1, GQA_KV_HEADS))),
              row(mla_q_norm), row(mla_kv_norm), w_uq, w_ukv, place)
    ffn1 = (row(ffn1_norm), ffn1_w_gate.astype(BF16), ffn1_w_up.astype(BF16), ffn1_w_down.astype(BF16))
    ffn2 = (row(ffn2_norm), ffn2_w_gate.astype(BF16), ffn2_w_up.astype(BF16), ffn2_w_down.astype(BF16))
    branch_w = (w_branch_a.astype(BF16), w_branch_b.astype(BF16), w_branch_c.astype(BF16), w_out.astype(BF16))
    na_bias = _na_bias_table(na_rel_bias)
    g_mix = row(mix_norm)

    h = (x, ctx)
    for i in range(DEPTH):
        last = i == DEPTH - 1
        h = _ffn(h, mod_all, i, 0, *ffn1, fin, n_tiles=NT_ALL, final=False)
        qa, ka, va, qb, kb, vb, qm, km, vm = _proj(h, mod_all, i, g_mix, w_qkv, consts, tables)

        oa = _na_attn(qa, ka, va, na_bias, i)
        ob = _gqa_attn(qb, kb, vb)
        om = _mla_attn(qm, km, vm)
        if not last:
            oa_c, ob_c, om_c = _ctx_attn(((qa, ka, va), (qb, kb, vb), (qm, km, vm)))
            oa, ob, om = (oa, oa_c), (ob, ob_c), (om, om_c)

        n_tiles = NT_LAT if last else NT_ALL
        h = _merge(h, mod_all, i, g_mix, w_gate, (oa, ob, om), *branch_w, n_tiles=n_tiles)
        h = _ffn(h, mod_all, i, 2, *ffn2, fin, n_tiles=n_tiles, final=last)
    return h
```

```python
import functools

import numpy as np
import jax
import jax.numpy as jnp
from jax import lax
from jax.experimental import pallas as pl
from jax.experimental.pallas import tpu as pltpu

D_MODEL = 1024
BATCH = 32
SEQ = 2048
DEPTH = 2
CTX_LEN = 256
GRID_W = 64
HEAD_DIM = 64
NA_HEADS = 4
NA_ROWS = 8
NA_COLS = 16
GQA_HEADS = 8
GQA_KV_HEADS = 2
MLA_HEADS = 4
MLA_Q_RANK = 256
MLA_KV_RANK = 128
MLA_NOPE = 64
MLA_ROPE = 32
MLA_V = 64
N_MOD = 9
D_FF = ((8 * D_MODEL // 3 + 127) // 128) * 128
ROPE_THETA = 10000.0
EPS = 1e-6
NEG_BIG = -1e30

NA_WIDTH = NA_HEADS * HEAD_DIM
GQA_Q_WIDTH = GQA_HEADS * HEAD_DIM
GQA_KV_WIDTH = GQA_KV_HEADS * HEAD_DIM
GQA_GRP = GQA_HEADS // GQA_KV_HEADS
MLA_V_WIDTH = MLA_HEADS * MLA_V
GATE_WIDTH = 3 * D_MODEL
ROWS = SEQ // GRID_W

LANES = 128
SUBLANES = 8
BF16_SUBLANES = 16
LOG2_E = 1.4426950408889634
TOK = SEQ + CTX_LEN
TM = 256
NT_ALL = TOK // TM
NT_LAT = SEQ // TM
BGROUP = 4
MOD_ROWS = 40
CTX_ROW = BATCH
ADA_BLOCK = 1152
MLA_PAD = LANES
MLA_QK_WIDTH = MLA_HEADS * MLA_PAD
N_QKV_COLS = 1952
QKV_WIDTH = 2048
NA_QROWS = TM // GRID_W
NA_KROWS = 12
NA_KBLK = NA_KROWS * GRID_W // TM
NA_NLOCAL = NA_KROWS * GRID_W
NA_NKEY = NA_NLOCAL + CTX_LEN
KEY_CHUNK = 256
VMEM_LIMIT = 56 * 1024 * 1024

F32 = jnp.float32
BF16 = jnp.bfloat16

_OFF_AQ, _OFF_AK, _OFF_AV = 0, 256, 512
_OFF_BQ, _OFF_BK, _OFF_BV = 768, 1280, 1408
_OFF_CQ, _OFF_CKV, _OFF_CKR = 1536, 1792, 1920


def _params(n_axes):
    return pltpu.CompilerParams(dimension_semantics=("arbitrary",) * n_axes,
                                vmem_limit_bytes=VMEM_LIMIT)


def _const_spec(shape):
    nd = len(shape)
    return pl.BlockSpec(shape, lambda *_: (0,) * nd, pipeline_mode=pl.Buffered(1))


def _layer_spec(arr, layer):
    nd = arr.ndim - 1
    return pl.BlockSpec((None,) + arr.shape[1:], lambda *_: (layer,) + (0,) * nd, pipeline_mode=pl.Buffered(1))


def _mod_spec(layer, sub):
    return pl.BlockSpec((None, SUBLANES, 3 * D_MODEL),
                        lambda t, b: (layer, jnp.where(t >= NT_LAT, CTX_ROW, b * BGROUP) // SUBLANES, sub))


def _mod_vectors(mod_ref, r, t=None, b=None):
    t = pl.program_id(0) if t is None else t
    b = pl.program_id(1) if b is None else b
    first = jnp.where(t >= NT_LAT, CTX_ROW, b * BGROUP) % SUBLANES
    row = mod_ref[pl.ds(first + r, 1), :]
    return row[:, :D_MODEL], row[:, D_MODEL:2 * D_MODEL], row[:, 2 * D_MODEL:]


def _tok_spec(width):
    return pl.BlockSpec((BGROUP, TM, width), lambda t, b: (b, t, 0))


def _sigmoid(x):
    return 1.0 / (1.0 + jnp.exp(-x))


def _rmsnorm(x, g):
    return (x * lax.rsqrt(jnp.mean(x * x, axis=-1, keepdims=True) + EPS)) * g


def _bdot(a, b):
    return jnp.dot(a, b, preferred_element_type=F32)


def _dot_tn(a, b):
    return lax.dot_general(a, b, (((0,), (0,)), ((), ())), preferred_element_type=F32)


def _dot_nt(a, b):
    return lax.dot_general(a, b, (((1,), (1,)), ((), ())), preferred_element_type=F32)


def _ada_body(c_ref, w_ref, b_ref, o_ref):
    c = c_ref[...]
    s = c * _sigmoid(c)
    o_ref[...] = _bdot(s.astype(BF16), w_ref[...].astype(BF16)) + b_ref[...]


def _ada(c_all, w_ada, b_ada):
    nblk = (N_MOD * D_MODEL) // ADA_BLOCK
    return pl.pallas_call(
        _ada_body,
        grid=(DEPTH, nblk),
        in_specs=[
            pl.BlockSpec((MOD_ROWS, D_MODEL), lambda i, j: (0, 0)),
            pl.BlockSpec((None, D_MODEL, ADA_BLOCK), lambda i, j: (i, 0, j)),
            pl.BlockSpec((None, 1, ADA_BLOCK), lambda i, j: (i, 0, j)),
        ],
        out_specs=pl.BlockSpec((None, MOD_ROWS, ADA_BLOCK), lambda i, j: (i, 0, j)),
        out_shape=jax.ShapeDtypeStruct((DEPTH, MOD_ROWS, N_MOD * D_MODEL), F32),
        compiler_params=_params(2),
        name="ada_mod",
    )(c_all, w_ada, b_ada.reshape(DEPTH, 1, N_MOD * D_MODEL))


def _ffn_body(*refs, final, split):
    if split:
        h_ref, hc_ref, mod_ref, g_ref, wg_ref, wu_ref, wd_ref, fin_ref, o_ref = refs
    else:
        h_ref, mod_ref, g_ref, wg_ref, wu_ref, wd_ref, fin_ref, o_ref = refs
    for r in range(BGROUP):
        h = h_ref[r]
        if split:
            h = jnp.where(pl.program_id(0) >= NT_LAT, hc_ref[r], h)
        shift, scale, gate = _mod_vectors(mod_ref, r)
        n = _rmsnorm(h, g_ref[...]) * (1.0 + scale) + shift
        nb = n.astype(BF16)
        g = _bdot(nb, wg_ref[...])
        u = _bdot(nb, wu_ref[...])
        a = (g * _sigmoid(g)) * u
        d = _bdot(a.astype(BF16), wd_ref[...])
        out = h + 0.5 * gate * d
        if final:
            out = _rmsnorm(out, fin_ref[...])
        o_ref[r] = out


def _ffn(h, mod, layer, sub, g_norm, wg, wu, wd, fin, *, n_tiles, final):
    split = isinstance(h, tuple)
    if split:
        h_arrays = list(h)
        h_specs = [pl.BlockSpec((BGROUP, TM, D_MODEL), lambda t, b: (b, jnp.minimum(t, NT_LAT - 1), 0)),
                   pl.BlockSpec((BGROUP, TM, D_MODEL), lambda t, b: (b, 0, 0))]
    else:
        h_arrays, h_specs = [h], [_tok_spec(D_MODEL)]
    return pl.pallas_call(
        functools.partial(_ffn_body, final=final, split=split),
        grid=(n_tiles, BATCH // BGROUP),
        in_specs=h_specs + [_mod_spec(layer, sub)] + [_layer_spec(a, layer) for a in (g_norm, wg, wu, wd)]
                 + [_const_spec((1, D_MODEL))],
        out_specs=_tok_spec(D_MODEL),
        out_shape=jax.ShapeDtypeStruct((BATCH, n_tiles * TM, D_MODEL), F32),
        compiler_params=_params(2),
        name="half_ffn",
    )(*h_arrays, mod, g_norm, wg, wu, wd, fin)


def _rope(x, cos, sin_a, sin_b, half):
    w = x.shape[-1]
    return x * cos + pltpu.roll(x, w - half, 1) * sin_a + pltpu.roll(x, half, 1) * sin_b


def _seg_mean_sq(x, seg):
    w = seg.shape[0]
    parts = []
    for lo_col in range(0, x.shape[-1], w):
        x2 = x[:, lo_col:lo_col + w]
        x2 = x2 * x2
        hi = x2.astype(BF16)
        lo = (x2 - hi.astype(F32)).astype(BF16)
        parts.append(_bdot(hi, seg) + _bdot(lo, seg))
    return parts[0] if len(parts) == 1 else jnp.concatenate(parts, axis=1)


_PROJ_GROUPS = NT_ALL * (BATCH // BGROUP)


def _proj_prod(i):
    c = jnp.minimum(i, _PROJ_GROUPS - 1)
    return c // (BATCH // BGROUP), c % (BATCH // BGROUP)


def _proj_cons(i):
    c = jnp.maximum(i - 1, 0)
    return c // (BATCH // BGROUP), c % (BATCH // BGROUP)


def _proj_body(h_ref, mod_ref, g_ref, w_ref, segq_ref, segk_ref, qn_ref, kn_ref, mqn_ref, mkvn_ref,
               wuq_ref, wukv_ref, place_ref,
               cb_ref, sab_ref, sbb_ref, cm_ref, sam_ref, sbm_ref, ck_ref, sak_ref, sbk_ref,
               qa_ref, ka_ref, va_ref, qb_ref, kb_ref, vb_ref, qm_ref, km_ref, vm_ref, z0_scr, z1_scr):
    i = pl.program_id(0)

    @pl.when(i == 0)
    def _():
        z1_scr[...] = jnp.zeros(z1_scr.shape, F32)

    def step(z_prod, z_cons):
        t, b = _proj_prod(i)
        for r in range(BGROUP):
            shift, mscale, _ = _mod_vectors(mod_ref, r, t, b)
            n = _rmsnorm(h_ref[r], g_ref[...]) * (1.0 + mscale) + shift
            z_prod[r] = _bdot(n.astype(BF16), w_ref[...])
        _proj_mixers(z_cons, segq_ref, segk_ref, qn_ref, kn_ref, mqn_ref, mkvn_ref, wuq_ref, wukv_ref, place_ref,
                     cb_ref, sab_ref, sbb_ref, cm_ref, sam_ref, sbm_ref, ck_ref, sak_ref, sbk_ref,
                     qa_ref, ka_ref, va_ref, qb_ref, kb_ref, vb_ref, qm_ref, km_ref, vm_ref)

    pl.when(i % 2 == 0)(functools.partial(step, z0_scr, z1_scr))
    pl.when(i % 2 == 1)(functools.partial(step, z1_scr, z0_scr))


def _proj_mixers(z_ref, segq_ref, segk_ref, qn_ref, kn_ref, mqn_ref, mkvn_ref, wuq_ref, wukv_ref, place_ref,
                 cb_ref, sab_ref, sbb_ref, cm_ref, sam_ref, sbm_ref, ck_ref, sak_ref, sbk_ref,
                 qa_ref, ka_ref, va_ref, qb_ref, kb_ref, vb_ref, qm_ref, km_ref, vm_ref):
    scale = LOG2_E * HEAD_DIM ** -0.5
    mla_scale = LOG2_E * (MLA_NOPE + MLA_ROPE) ** -0.5
    cb, sab, sbb = cb_ref[...], sab_ref[...], sbb_ref[...]
    for r in range(BGROUP):
        z = z_ref[r]
        qa_ref[r] = (z[:, _OFF_AQ:_OFF_AQ + NA_WIDTH] * scale).astype(BF16)
        ka_ref[r] = z[:, _OFF_AK:_OFF_AK + NA_WIDTH].astype(BF16)
        va_ref[r] = z[:, _OFF_AV:_OFF_AV + NA_WIDTH].T.astype(BF16)

        bq = z[:, _OFF_BQ:_OFF_BQ + GQA_Q_WIDTH]
        bq = (bq * lax.rsqrt(_seg_mean_sq(bq, segq_ref[...]) + EPS)) * qn_ref[...]
        qb_ref[r] = (_rope(bq, cb, sab, sbb, HEAD_DIM // 4) * scale).astype(BF16)
        bk = z[:, _OFF_BK:_OFF_BK + GQA_KV_WIDTH]
        bk = (bk * lax.rsqrt(_seg_mean_sq(bk, segk_ref[...]) + EPS)) * kn_ref[...]
        kb_ref[r] = _rope(bk, cb[:, :GQA_KV_WIDTH], sab[:, :GQA_KV_WIDTH], sbb[:, :GQA_KV_WIDTH],
                          HEAD_DIM // 4).astype(BF16)
        vb_ref[r] = z[:, _OFF_BV:_OFF_BV + GQA_KV_WIDTH].T.astype(BF16)

        cq = _rmsnorm(z[:, _OFF_CQ:_OFF_CQ + MLA_Q_RANK], mqn_ref[...])
        q_lat = _bdot(cq.astype(BF16), wuq_ref[...])
        qm_ref[r] = (_rope(q_lat, cm_ref[...], sam_ref[...], sbm_ref[...], MLA_ROPE // 4)
                     * mla_scale).astype(BF16)
        ckv = _rmsnorm(z[:, _OFF_CKV:_OFF_CKV + MLA_KV_RANK], mkvn_ref[...])
        kv_lat = _bdot(ckv.astype(BF16), wukv_ref[...])
        vm_ref[r] = kv_lat[:, MLA_QK_WIDTH:].T.astype(BF16)
        lane = lax.broadcasted_iota(jnp.int32, (1, LANES), 1)
        kr = jnp.where(lane < MLA_ROPE, z[:, _OFF_CKR:_OFF_CKR + LANES], 0.0)
        kr = _rope(kr, ck_ref[...], sak_ref[...], sbk_ref[...], MLA_ROPE // 4).astype(BF16)
        km_ref[r] = (kv_lat[:, :MLA_QK_WIDTH] + _bdot(kr, place_ref[...])).astype(BF16)


def _proj(h, mod, layer, g_norm, w_qkv, consts, tables):
    widths = (NA_WIDTH, NA_WIDTH, NA_WIDTH, GQA_Q_WIDTH, GQA_KV_WIDTH, GQA_KV_WIDTH,
              MLA_QK_WIDTH, MLA_QK_WIDTH, MLA_V_WIDTH)
    is_value = [k % 3 == 2 for k in range(len(widths))]
    tab_specs = [pl.BlockSpec((TM, t.shape[-1]), lambda i: (_proj_cons(i)[0], 0)) for t in tables]

    def mod_index(i):
        t, b = _proj_prod(i)
        return layer, jnp.where(t >= NT_LAT, CTX_ROW, b * BGROUP) // SUBLANES, 1

    def out_spec(is_val, w):
        if is_val:
            return pl.BlockSpec((BGROUP, w, TM), lambda i: (_proj_cons(i)[1], 0, _proj_cons(i)[0]))
        return pl.BlockSpec((BGROUP, TM, w), lambda i: (_proj_cons(i)[1], _proj_cons(i)[0], 0))

    return pl.pallas_call(
        _proj_body,
        grid=(_PROJ_GROUPS + 1,),
        in_specs=[pl.BlockSpec((BGROUP, TM, D_MODEL), lambda i: (_proj_prod(i)[1], _proj_prod(i)[0], 0)),
                  pl.BlockSpec((None, SUBLANES, 3 * D_MODEL), mod_index),
                  _layer_spec(g_norm, layer), _layer_spec(w_qkv, layer)]
                 + [_layer_spec(c, layer) if c.ndim == 3 else _const_spec(c.shape) for c in consts] + tab_specs,
        out_specs=[out_spec(v, w) for v, w in zip(is_value, widths)],
        out_shape=[jax.ShapeDtypeStruct((BATCH, w, TOK) if v else (BATCH, TOK, w), BF16)
                   for v, w in zip(is_value, widths)],
        scratch_shapes=[pltpu.VMEM((BGROUP, TM, QKV_WIDTH), F32)] * 2,
        compiler_params=_params(1),
        name="mix_proj",
    )(h, mod, g_norm, w_qkv, *consts, *tables)


_CTX_MIXERS = ((NA_HEADS, 1, HEAD_DIM, HEAD_DIM), (GQA_HEADS, GQA_GRP, HEAD_DIM, HEAD_DIM),
               (MLA_HEADS, 1, MLA_PAD, MLA_V))


def _ctx_attn_body(*refs):
    ones = jnp.ones((BF16_SUBLANES, TM), BF16)
    n_mix = len(_CTX_MIXERS)
    for r in range(BGROUP):
        for mix, (heads, grp, dqk, dv) in enumerate(_CTX_MIXERS):
            q_ref, k_ref, vt_ref = refs[3 * mix:3 * mix + 3]
            outs = []
            for hd in range(heads):
                g = hd // grp
                st = _dot_nt(k_ref[r, :, g * dqk:(g + 1) * dqk], q_ref[r, :, hd * dqk:(hd + 1) * dqk])
                p = jnp.exp2(st - jnp.max(st, axis=0, keepdims=True)).astype(BF16)
                acc = _bdot(jnp.concatenate([vt_ref[r, g * dv:(g + 1) * dv, :], ones], axis=0), p)
                outs.append(acc[:dv] * (1.0 / acc[dv:dv + 1]))
            refs[3 * n_mix + mix][r] = jnp.concatenate(outs, axis=0).astype(BF16)


def _ctx_attn(qkv):
    in_specs, arrays = [], []
    for q, k, vt in qkv:
        arrays += [q, k, vt]
        in_specs += [pl.BlockSpec((BGROUP, TM, q.shape[-1]), lambda b: (b, NT_LAT, 0)),
                     pl.BlockSpec((BGROUP, TM, k.shape[-1]), lambda b: (b, NT_LAT, 0)),
                     pl.BlockSpec((BGROUP, vt.shape[1], TM), lambda b: (b, 0, NT_LAT))]
    widths = [heads * dv for heads, _, _, dv in _CTX_MIXERS]
    return pl.pallas_call(
        _ctx_attn_body,
        grid=(BATCH // BGROUP,),
        in_specs=in_specs,
        out_specs=[pl.BlockSpec((BGROUP, w, TM), lambda b: (b, 0, 0)) for w in widths],
        out_shape=[jax.ShapeDtypeStruct((BATCH, w, TM), BF16) for w in widths],
        compiler_params=_params(1),
        name="ctx_attn",
    )(*arrays)


def _pipe_step(prod, cons, n_chunk, dv):
    ones = jnp.ones((BF16_SUBLANES, KEY_CHUNK), BF16)
    m_cons = [m_ref[0:1, :] for _, _, m_ref in cons]
    m_new = [None] * len(prod)
    acc = [None] * len(cons)
    for c in range(n_chunk):
        rows = slice(c * KEY_CHUNK, (c + 1) * KEY_CHUNK)
        for d, (qg, k_chunk, bias_chunk, s_ref, _) in enumerate(prod):
            st = _dot_nt(k_chunk(c), qg)
            bias = None if bias_chunk is None else bias_chunk(c)
            if bias is not None:
                st = st + bias
            s_ref[rows, :] = st
            mc = jnp.max(st, axis=0, keepdims=True)
            m_new[d] = mc if m_new[d] is None else jnp.maximum(m_new[d], mc)
        for d, (vt_chunk, s_ref, _) in enumerate(cons):
            p = jnp.exp2(s_ref[rows, :] - m_cons[d]).astype(BF16)
            part = _bdot(jnp.concatenate([vt_chunk(c), ones], axis=0), p)
            acc[d] = part if acc[d] is None else acc[d] + part
    for d, (_, _, _, _, m_ref) in enumerate(prod):
        m_ref[0:1, :] = m_new[d]
    return [a[:dv] * (1.0 / a[dv:dv + 1]) for a in acc]


def _pipe_run(step, scratch, n_dots):
    i = pl.program_id(0)
    s = [scratch[slot * n_dots:(slot + 1) * n_dots] for slot in range(2)]
    m = [scratch[(2 + slot) * n_dots:(3 + slot) * n_dots] for slot in range(2)]

    @pl.when(i == 0)
    def _():
        for ref in s[1] + m[1]:
            ref[...] = jnp.zeros(ref.shape, F32)

    pl.when(i % 2 == 0)(functools.partial(step, 0, 1, s, m))
    pl.when(i % 2 == 1)(functools.partial(step, 1, 0, s, m))


N_CHAINS = BATCH * NT_LAT


def _prod_chain(i):
    c = jnp.minimum(i, N_CHAINS - 1)
    return c // NT_LAT, c % NT_LAT, 0


def _cons_chain(i):
    c = jnp.maximum(i - 1, 0)
    return c // NT_LAT, c % NT_LAT, 0


def _pipe_scratch(nk, m_len, n_dots):
    return ([pltpu.VMEM((nk, m_len), F32)] * (2 * n_dots)
            + [pltpu.VMEM((8, m_len), F32)] * (2 * n_dots))


def _gqa_body(q_ref, k_ref, vt_ref, o_ref, *scratch):
    def run(sp, sc, s, m):
        q = q_ref[...]
        prod, cons = [], []
        for g in range(GQA_KV_HEADS):
            qg = jnp.concatenate([q[:, (g * GQA_GRP + j) * HEAD_DIM:(g * GQA_GRP + j + 1) * HEAD_DIM]
                                  for j in range(GQA_GRP)], axis=0)

            def k_chunk(c, g=g):
                return k_ref[c * KEY_CHUNK:(c + 1) * KEY_CHUNK, g * HEAD_DIM:(g + 1) * HEAD_DIM]

            def vt_chunk(c, g=g):
                return vt_ref[g * HEAD_DIM:(g + 1) * HEAD_DIM, c * KEY_CHUNK:(c + 1) * KEY_CHUNK]

            prod.append((qg, k_chunk, None, s[sp][g], m[sp][g]))
            cons.append((vt_chunk, s[sc][g], m[sc][g]))
        outs = _pipe_step(prod, cons, TOK // KEY_CHUNK, HEAD_DIM)
        o = jnp.concatenate([ot[:, j * TM:(j + 1) * TM] for ot in outs for j in range(GQA_GRP)], axis=0)
        o_ref[...] = o.astype(BF16)

    _pipe_run(run, scratch, GQA_KV_HEADS)


def _full_key_attn(body, q, k, vt, *, w_out, n_dots, m_len, name):
    return pl.pallas_call(
        body,
        grid=(N_CHAINS + 1,),
        in_specs=[
            pl.BlockSpec((None, TM, q.shape[-1]), _prod_chain),
            pl.BlockSpec((None, TOK, k.shape[-1]), lambda i: (_prod_chain(i)[0], 0, 0)),
            pl.BlockSpec((None, vt.shape[1], TOK), lambda i: (_cons_chain(i)[0], 0, 0)),
        ],
        out_specs=pl.BlockSpec((None, w_out, TM), lambda i: (_cons_chain(i)[0], 0, _cons_chain(i)[1])),
        out_shape=jax.ShapeDtypeStruct((BATCH, w_out, SEQ), BF16),
        scratch_shapes=_pipe_scratch(TOK, m_len, n_dots),
        compiler_params=_params(1),
        name=name,
    )(q, k, vt)


def _gqa_attn(q, k, vt):
    return _full_key_attn(_gqa_body, q, k, vt, w_out=GQA_Q_WIDTH, n_dots=GQA_KV_HEADS, m_len=GQA_GRP * TM,
                          name="gqa_attn")


def _mla_body(q_ref, k_ref, vt_ref, o_ref, *scratch):
    def run(sp, sc, s, m):
        prod, cons = [], []
        for d in range(MLA_HEADS):
            def k_chunk(c, d=d):
                return k_ref[c * KEY_CHUNK:(c + 1) * KEY_CHUNK, d * MLA_PAD:(d + 1) * MLA_PAD]

            def vt_chunk(c, d=d):
                return vt_ref[d * MLA_V:(d + 1) * MLA_V, c * KEY_CHUNK:(c + 1) * KEY_CHUNK]

            prod.append((q_ref[:, d * MLA_PAD:(d + 1) * MLA_PAD], k_chunk, None, s[sp][d], m[sp][d]))
            cons.append((vt_chunk, s[sc][d], m[sc][d]))
        outs = _pipe_step(prod, cons, TOK // KEY_CHUNK, MLA_V)
        o_ref[...] = jnp.concatenate(outs, axis=0).astype(BF16)

    _pipe_run(run, scratch, MLA_HEADS)


def _mla_attn(q, k, vt):
    return _full_key_attn(_mla_body, q, k, vt, w_out=MLA_V_WIDTH, n_dots=MLA_HEADS, m_len=TM, name="mla_attn")


def _na_start_block(t):
    return jnp.clip(t - 1, 0, NT_LAT - NA_KBLK)


def _na_body(q_ref, k0_ref, k1_ref, k2_ref, kc_ref, v0_ref, v1_ref, v2_ref, vc_ref, bias_ref, o_ref, *scratch):
    k_blocks = (k0_ref, k1_ref, k2_ref, kc_ref)
    v_blocks = (v0_ref, v1_ref, v2_ref, vc_ref)

    def run(sp, sc, s, m):
        prod, cons = [], []
        for d in range(NA_HEADS):
            def k_chunk(c, d=d):
                return k_blocks[c][:, d * HEAD_DIM:(d + 1) * HEAD_DIM]

            def bias_chunk(c, d=d):
                return None if c == NA_KBLK else bias_ref[d, c * KEY_CHUNK:(c + 1) * KEY_CHUNK, :]

            def vt_chunk(c, d=d):
                return v_blocks[c][d * HEAD_DIM:(d + 1) * HEAD_DIM, :]

            prod.append((q_ref[:, d * HEAD_DIM:(d + 1) * HEAD_DIM], k_chunk, bias_chunk, s[sp][d], m[sp][d]))
            cons.append((vt_chunk, s[sc][d], m[sc][d]))
        outs = _pipe_step(prod, cons, NA_NKEY // KEY_CHUNK, HEAD_DIM)
        o_ref[...] = jnp.concatenate(outs, axis=0).astype(BF16)

    _pipe_run(run, scratch, NA_HEADS)


def _na_attn(q, k, vt, bias, layer):
    def k_spec(j):
        def index(i):
            b, t, _ = _prod_chain(i)
            return b, (NT_LAT if j == NA_KBLK else _na_start_block(t) + j), 0
        return pl.BlockSpec((None, TM, NA_WIDTH), index)

    def vt_spec(j):
        def index(i):
            b, t, _ = _cons_chain(i)
            return b, 0, (NT_LAT if j == NA_KBLK else _na_start_block(t) + j)
        return pl.BlockSpec((None, NA_WIDTH, TM), index)

    def bias_index(i):
        t = _prod_chain(i)[1]
        return layer, 0, jnp.where(t == 0, 0, jnp.where(t == NT_LAT - 1, 2, 1)), 0, 0

    return pl.pallas_call(
        _na_body,
        grid=(N_CHAINS + 1,),
        in_specs=[pl.BlockSpec((None, TM, NA_WIDTH), _prod_chain)]
                 + [k_spec(j) for j in range(NA_KBLK + 1)] + [vt_spec(j) for j in range(NA_KBLK + 1)]
                 + [pl.BlockSpec((None, NA_HEADS, None, NA_NLOCAL, TM), bias_index)],
        out_specs=pl.BlockSpec((None, NA_WIDTH, TM), lambda i: (_cons_chain(i)[0], 0, _cons_chain(i)[1])),
        out_shape=jax.ShapeDtypeStruct((BATCH, NA_WIDTH, SEQ), BF16),
        scratch_shapes=_pipe_scratch(NA_NKEY, TM, NA_HEADS),
        compiler_params=_params(1),
        name="na_attn",
    )(q, *([k] * (NA_KBLK + 1)), *([vt] * (NA_KBLK + 1)), bias)


def _na_bias_table(rel_bias):
    lead = rel_bias.shape[:-2]
    rel_bias = rel_bias.reshape((-1,) + rel_bias.shape[-2:])
    nh = rel_bias.shape[0]
    cols = np.arange(GRID_W)
    c0 = np.clip(cols - NA_COLS // 2, 0, GRID_W - NA_COLS)
    col_in = (cols[None, :] >= c0[:, None]) & (cols[None, :] < c0[:, None] + NA_COLS)
    n_dr = 2 * NA_ROWS - 1
    pad = GRID_W - NA_COLS
    rbp = jnp.pad(rel_bias.astype(F32) * LOG2_E, ((0, 0), (0, 0), (pad, pad)), mode="edge")
    tiles = jnp.stack([rbp[:, :, GRID_W - 1 - cq:2 * GRID_W - 1 - cq] for cq in range(GRID_W)], axis=3)
    tiles = jnp.where(col_in.T[None, None], tiles, NEG_BIG)
    tiles = jnp.concatenate([tiles, jnp.full((nh, 1, GRID_W, GRID_W), NEG_BIG, F32)], axis=1)
    dr_blk = np.full((3, NA_QROWS, NA_KROWS), n_dr, np.int32)
    for cls, tile in enumerate((0, 2, NT_LAT - 1)):
        start = int(np.clip(tile - 1, 0, NT_LAT - NA_KBLK)) * NA_QROWS
        for qr in range(NA_QROWS):
            r = tile * NA_QROWS + qr
            r0 = int(np.clip(r - NA_ROWS // 2, 0, ROWS - NA_ROWS))
            for kj in range(NA_KROWS):
                kr = start + kj
                if r0 <= kr < r0 + NA_ROWS:
                    dr_blk[cls, qr, kj] = kr - r + NA_ROWS - 1
    local = jnp.take(tiles, jnp.asarray(dr_blk.reshape(-1)), axis=1)
    local = local.reshape(nh, 3, NA_QROWS, NA_KROWS, GRID_W, GRID_W).transpose(0, 1, 3, 4, 2, 5)
    return local.reshape(lead + (3, NA_NLOCAL, TM))


def _merge_body(h_ref, mod_ref, g_ref, wgate_ref, wa_ref, wb_ref, wm_ref, wo_ref, *rest, split):
    o_ref = rest[-1]
    for r in range(BGROUP):
        branch, pos = [], 0
        for has_ctx in split:
            if has_ctx:
                is_ctx = pl.program_id(0) >= NT_LAT
                branch.append(jnp.where(is_ctx, rest[pos + 1][r], rest[pos][r]))
            else:
                branch.append(rest[pos][r])
            pos += 2 if has_ctx else 1
        h = h_ref[r]
        shift, scale, gate = _mod_vectors(mod_ref, r)
        n = _rmsnorm(h, g_ref[...]) * (1.0 + scale) + shift
        gates =_sigmoid(_bdot(n.astype(BF16), wgate_ref[...]))
        y = (gates[:, :D_MODEL] * _dot_tn(branch[0], wa_ref[...])
             + gates[:, D_MODEL:2 * D_MODEL] * _dot_tn(branch[1], wb_ref[...])
             + gates[:, 2 * D_MODEL:] * _dot_tn(branch[2], wm_ref[...]))
        o_ref[r] = h + gate * _bdot(y.astype(BF16), wo_ref[...])


def _merge(h, mod, layer, g_norm, w_gate, branches, wa, wb, wm, wo, *, n_tiles):
    split = tuple(isinstance(br, tuple) for br in branches)
    arrays, specs = [], []
    for br in branches:
        if isinstance(br, tuple):
            lat, ctx = br
            arrays += [lat, ctx]
            specs += [pl.BlockSpec((BGROUP, lat.shape[1], TM), lambda t, b: (b, 0, jnp.minimum(t, NT_LAT - 1))),
                      pl.BlockSpec((BGROUP, ctx.shape[1], TM), lambda t, b: (b, 0, 0))]
        else:
            arrays.append(br)
            specs.append(pl.BlockSpec((BGROUP, br.shape[1], TM), lambda t, b: (b, 0, t)))
    return pl.pallas_call(
        functools.partial(_merge_body, split=split),
        grid=(n_tiles, BATCH // BGROUP),
        in_specs=[_tok_spec(D_MODEL), _mod_spec(layer, 1)]
                 + [_layer_spec(a, layer) for a in (g_norm, w_gate, wa, wb, wm, wo)] + specs,
        out_specs=_tok_spec(D_MODEL),
        out_shape=jax.ShapeDtypeStruct((BATCH, n_tiles * TM, D_MODEL), F32),
        compiler_params=_params(2),
        name="gated_merge",
    )(h, mod, g_norm, w_gate, wa, wb, wm, wo, *arrays)


def _rope_tables():
    t = np.arange(SEQ)
    row = (t // GRID_W).astype(np.float32)[:, None]
    col = (t % GRID_W).astype(np.float32)[:, None]

    def head_tables(d_rot):
        half = d_rot // 2
        freqs = np.float32(ROPE_THETA) ** (-np.arange(0, half, 2, dtype=np.float32) / np.float32(half))
        r, c = row * freqs, col * freqs
        ang = np.concatenate([r, r, c, c], axis=-1).astype(np.float32)
        first = (np.arange(d_rot) % half) < (half // 2)
        cos, sin = np.cos(ang), np.sin(ang)
        zero = np.zeros_like(sin)
        return cos, np.where(first, -sin, zero), np.where(first, zero, sin)

    def place(tabs, reps, lo, group):
        out = []
        for k, tab in enumerate(tabs):
            fill = 1.0 if k == 0 else 0.0
            d_rot = tab.shape[-1]
            g = np.concatenate([np.full((SEQ, lo), fill, np.float32), tab,
                                np.full((SEQ, group - lo - d_rot), fill, np.float32)], axis=-1)
            g = np.tile(g, (1, reps))
            ctx = np.full((CTX_LEN, reps * group), fill, np.float32)
            out.append(jnp.asarray(np.concatenate([g, ctx], axis=0), F32))
        return out

    tb = head_tables(HEAD_DIM)
    tm = head_tables(MLA_ROPE)
    return (place(tb, GQA_HEADS, 0, HEAD_DIM)
            + place(tm, MLA_HEADS, MLA_NOPE, MLA_PAD)
            + place(tm, 1, 0, LANES))


def _seg_matrix(width):
    idx = np.arange(width) // HEAD_DIM
    return jnp.asarray((idx[:, None] == idx[None, :]).astype(np.float32) / HEAD_DIM, BF16)


def _place_matrix():
    m = np.zeros((LANES, MLA_QK_WIDTH), np.float32)
    for hd in range(MLA_HEADS):
        for j in range(MLA_ROPE):
            m[j, hd * MLA_PAD + MLA_NOPE + j] = 1.0
    return jnp.asarray(m, BF16)


def _pad_heads(w, n_heads, per_head, keep_lo, keep_hi):
    d, k = w.shape[:2]
    wh = w.reshape(d, k, n_heads, per_head)[..., keep_lo:keep_hi]
    wh = jnp.pad(wh, ((0, 0), (0, 0), (0, 0), (0, MLA_PAD - (keep_hi - keep_lo))))
    return wh.reshape(d, k, n_heads * MLA_PAD)


def kernel(x, c, ctx, c_ctx, w_ada, b_ada, ffn1_norm, ffn1_w_gate, ffn1_w_up, ffn1_w_down, mix_norm, w_in,
           na_rel_bias, gqa_q_norm, gqa_k_norm, mla_q_norm, mla_kv_norm, mla_w_uq, mla_w_ukv,
           w_branch_a, w_branch_b, w_branch_c, w_out, ffn2_norm, ffn2_w_gate, ffn2_w_up, ffn2_w_down,
           final_norm):
    assert x.shape == (BATCH, SEQ, D_MODEL) and ctx.shape == (BATCH, CTX_LEN, D_MODEL)

    c_all = jnp.concatenate([c, jnp.tile(c_ctx[None, :], (BGROUP, 1)),
                             jnp.zeros((MOD_ROWS - BATCH - BGROUP, D_MODEL), F32)], axis=0)
    mod_all = _ada(c_all, w_ada, b_ada)

    tables = _rope_tables()
    seg_q, seg_k, place = _seg_matrix(2 * LANES), _seg_matrix(GQA_KV_WIDTH), _place_matrix()
    fin = final_norm.reshape(1, D_MODEL)

    def row(v):
        return v[:, None, :]

    w_qkv = w_in[:, :, :QKV_WIDTH].astype(BF16)
    w_gate = w_in[:, :, N_QKV_COLS:].astype(BF16)
    w_uq = _pad_heads(mla_w_uq, MLA_HEADS, MLA_NOPE + MLA_ROPE, 0, MLA_NOPE + MLA_ROPE).astype(BF16)
    w_uk = _pad_heads(mla_w_ukv, MLA_HEADS, MLA_NOPE + MLA_V, 0, MLA_NOPE)
    w_uv = mla_w_ukv.reshape(DEPTH, MLA_KV_RANK, MLA_HEADS, MLA_NOPE + MLA_V)[..., MLA_NOPE:]
    w_ukv = jnp.concatenate([w_uk, w_uv.reshape(DEPTH, MLA_KV_RANK, MLA_V_WIDTH)], axis=-1).astype(BF16)
    consts = (seg_q, seg_k, row(jnp.tile(gqa_q_norm, (1, GQA_HEADS))), row(jnp.tile(gqa_k_norm, (1, GQA_KV_HEADS))),
              row(mla_q_norm), row(mla_kv_norm), w_uq, w_ukv, place)
    ffn1 = (row(ffn1_norm), ffn1_w_gate.astype(BF16), ffn1_w_up.astype(BF16), ffn1_w_down.astype(BF16))
    ffn2 = (row(ffn2_norm), ffn2_w_gate.astype(BF16), ffn2_w_up.astype(BF16), ffn2_w_down.astype(BF16))
    branch_w = (w_branch_a.astype(BF16), w_branch_b.astype(BF16), w_branch_c.astype(BF16), w_out.astype(BF16))
    na_bias = _na_bias_table(na_rel_bias)
    g_mix = row(mix_norm)

    h = (x, ctx)
    for i in range(DEPTH):
        last = i == DEPTH - 1
        h = _ffn(h, mod_all, i, 0, *ffn1, fin, n_tiles=NT_ALL, final=False)
        qa, ka, va, qb, kb, vb, qm, km, vm = _proj(h, mod_all, i, g_mix, w_qkv, consts, tables)

        oa = _na_attn(qa, ka, va, na_bias, i)
        ob = _gqa_attn(qb, kb, vb)
        om = _mla_attn(qm, km, vm)
        if not last:
            oa_c, ob_c, om_c = _ctx_attn(((qa, ka, va), (qb, kb, vb), (qm, km, vm)))
            oa, ob, om = (oa, oa_c), (ob, ob_c), (om, om_c)

        n_tiles = NT_LAT if last else NT_ALL
        h = _merge(h, mod_all, i, g_mix, w_gate, (oa, ob, om), *branch_w, n_tiles=n_tiles)
        h = _ffn(h, mod_all, i, 2, *ffn2, fin, n_tiles=n_tiles, final=last)
    return h
```

```python
import functools

import numpy as np
import jax
import jax.numpy as jnp
from jax import lax
from jax.experimental import pallas as pl
from jax.experimental.pallas import tpu as pltpu

D_MODEL = 1024
BATCH = 32
SEQ = 2048
DEPTH = 2
CTX_LEN = 256
GRID_W = 64
HEAD_DIM = 64
NA_HEADS = 4
NA_ROWS = 8
NA_COLS = 16
GQA_HEADS = 8
GQA_KV_HEADS = 2
MLA_HEADS = 4
MLA_Q_RANK = 256
MLA_KV_RANK = 128
MLA_NOPE = 64
MLA_ROPE = 32
MLA_V = 64
N_MOD = 9
D_FF = ((8 * D_MODEL // 3 + 127) // 128) * 128
ROPE_THETA = 10000.0
EPS = 1e-6
NEG_BIG = -1e30

NA_WIDTH = NA_HEADS * HEAD_DIM
GQA_Q_WIDTH = GQA_HEADS * HEAD_DIM
GQA_KV_WIDTH = GQA_KV_HEADS * HEAD_DIM
GQA_GRP = GQA_HEADS // GQA_KV_HEADS
MLA_V_WIDTH = MLA_HEADS * MLA_V
GATE_WIDTH = 3 * D_MODEL
ROWS = SEQ // GRID_W

LANES = 128
SUBLANES = 8
BF16_SUBLANES = 16
LOG2_E = 1.4426950408889634
TOK = SEQ + CTX_LEN
TM = 256
NT_ALL = TOK // TM
NT_LAT = SEQ // TM
BGROUP = 4
MOD_ROWS = 40
CTX_ROW = BATCH
ADA_BLOCK = 1152
MLA_PAD = LANES
MLA_QK_WIDTH = MLA_HEADS * MLA_PAD
N_QKV_COLS = 1952
QKV_WIDTH = 2048
NA_QROWS = TM // GRID_W
NA_KROWS = 12
NA_KBLK = NA_KROWS * GRID_W // TM
NA_NLOCAL = NA_KROWS * GRID_W
NA_NKEY = NA_NLOCAL + CTX_LEN
KEY_CHUNK = 256
NA_GROUP = 4
VMEM_LIMIT = 56 * 1024 * 1024

F32 = jnp.float32
BF16 = jnp.bfloat16

_OFF_AQ, _OFF_AK, _OFF_AV = 0, 256, 512
_OFF_BQ, _OFF_BK, _OFF_BV = 768, 1280, 1408
_OFF_CQ, _OFF_CKV, _OFF_CKR = 1536, 1792, 1920


def _params(n_axes):
    return pltpu.CompilerParams(dimension_semantics=("arbitrary",) * n_axes,
                                vmem_limit_bytes=VMEM_LIMIT)


def _const_spec(shape):
    nd = len(shape)
    return pl.BlockSpec(shape, lambda *_: (0,) * nd, pipeline_mode=pl.Buffered(1))


def _layer_spec(arr, layer):
    nd = arr.ndim - 1
    return pl.BlockSpec((None,) + arr.shape[1:], lambda *_: (layer,) + (0,) * nd, pipeline_mode=pl.Buffered(1))


def _mod_spec(layer, sub):
    return pl.BlockSpec((None, SUBLANES, 3 * D_MODEL),
                        lambda t, b: (layer, jnp.where(t >= NT_LAT, CTX_ROW, b * BGROUP) // SUBLANES, sub))


def _mod_vectors(mod_ref, r, t=None, b=None):
    t = pl.program_id(0) if t is None else t
    b = pl.program_id(1) if b is None else b
    first = jnp.where(t >= NT_LAT, CTX_ROW, b * BGROUP) % SUBLANES
    row = mod_ref[pl.ds(first + r, 1), :]
    return row[:, :D_MODEL], row[:, D_MODEL:2 * D_MODEL], row[:, 2 * D_MODEL:]


def _tok_spec(width):
    return pl.BlockSpec((BGROUP, TM, width), lambda t, b: (b, t, 0))


def _sigmoid(x):
    return 1.0 / (1.0 + jnp.exp(-x))


def _rmsnorm(x, g):
    return (x * lax.rsqrt(jnp.mean(x * x, axis=-1, keepdims=True) + EPS)) * g


def _bdot(a, b):
    return jnp.dot(a, b, preferred_element_type=F32)


def _dot_tn(a, b):
    return lax.dot_general(a, b, (((0,), (0,)), ((), ())), preferred_element_type=F32)


def _dot_nt(a, b):
    return lax.dot_general(a, b, (((1,), (1,)), ((), ())), preferred_element_type=F32)


def _ada_body(c_ref, w_ref, b_ref, o_ref):
    c = c_ref[...]
    s = c * _sigmoid(c)
    o_ref[...] = _bdot(s.astype(BF16), w_ref[...].astype(BF16)) + b_ref[...]


def _ada(c_all, w_ada, b_ada):
    nblk = (N_MOD * D_MODEL) // ADA_BLOCK
    return pl.pallas_call(
        _ada_body,
        grid=(DEPTH, nblk),
        in_specs=[
            pl.BlockSpec((MOD_ROWS, D_MODEL), lambda i, j: (0, 0)),
            pl.BlockSpec((None, D_MODEL, ADA_BLOCK), lambda i, j: (i, 0, j)),
            pl.BlockSpec((None, 1, ADA_BLOCK), lambda i, j: (i, 0, j)),
        ],
        out_specs=pl.BlockSpec((None, MOD_ROWS, ADA_BLOCK), lambda i, j: (i, 0, j)),
        out_shape=jax.ShapeDtypeStruct((DEPTH, MOD_ROWS, N_MOD * D_MODEL), F32),
        compiler_params=_params(2),
        name="ada_mod",
    )(c_all, w_ada, b_ada.reshape(DEPTH, 1, N_MOD * D_MODEL))


def _ffn_body(*refs, final, split):
    if split:
        h_ref, hc_ref, mod_ref, g_ref, wg_ref, wu_ref, wd_ref, fin_ref, o_ref = refs
    else:
        h_ref, mod_ref, g_ref, wg_ref, wu_ref, wd_ref, fin_ref, o_ref = refs
    for r in range(BGROUP):
        h = h_ref[r]
        if split:
            h = jnp.where(pl.program_id(0) >= NT_LAT, hc_ref[r], h)
        shift, scale, gate = _mod_vectors(mod_ref, r)
        n = _rmsnorm(h, g_ref[...]) * (1.0 + scale) + shift
        nb = n.astype(BF16)
        g = _bdot(nb, wg_ref[...])
        u = _bdot(nb, wu_ref[...])
        a = (g * _sigmoid(g)) * u
        d = _bdot(a.astype(BF16), wd_ref[...])
        out = h + 0.5 * gate * d
        if final:
            out = _rmsnorm(out, fin_ref[...])
        o_ref[r] = out


def _ffn(h, mod, layer, sub, g_norm, wg, wu, wd, fin, *, n_tiles, final):
    split = isinstance(h, tuple)
    if split:
        h_arrays = list(h)
        h_specs = [pl.BlockSpec((BGROUP, TM, D_MODEL), lambda t, b: (b, jnp.minimum(t, NT_LAT - 1), 0)),
                   pl.BlockSpec((BGROUP, TM, D_MODEL), lambda t, b: (b, 0, 0))]
    else:
        h_arrays, h_specs = [h], [_tok_spec(D_MODEL)]
    return pl.pallas_call(
        functools.partial(_ffn_body, final=final, split=split),
        grid=(n_tiles, BATCH // BGROUP),
        in_specs=h_specs + [_mod_spec(layer, sub)] + [_layer_spec(a, layer) for a in (g_norm, wg, wu, wd)]
                 + [_const_spec((1, D_MODEL))],
        out_specs=_tok_spec(D_MODEL),
        out_shape=jax.ShapeDtypeStruct((BATCH, n_tiles * TM, D_MODEL), F32),
        compiler_params=_params(2),
        name="half_ffn",
    )(*h_arrays, mod, g_norm, wg, wu, wd, fin)


def _rope(x, cos, sin_a, sin_b, half):
    w = x.shape[-1]
    return x * cos + pltpu.roll(x, w - half, 1) * sin_a + pltpu.roll(x, half, 1) * sin_b


def _seg_mean_sq(x, seg):
    w = seg.shape[0]
    parts = []
    for lo_col in range(0, x.shape[-1], w):
        x2 = x[:, lo_col:lo_col + w]
        x2 = x2 * x2
        hi = x2.astype(BF16)
        lo = (x2 - hi.astype(F32)).astype(BF16)
        parts.append(_bdot(hi, seg) + _bdot(lo, seg))
    return parts[0] if len(parts) == 1 else jnp.concatenate(parts, axis=1)


_PROJ_GROUPS = NT_ALL * (BATCH // BGROUP)


def _proj_prod(i):
    c = jnp.minimum(i, _PROJ_GROUPS - 1)
    return c // (BATCH // BGROUP), c % (BATCH // BGROUP)


def _proj_cons(i):
    c = jnp.maximum(i - 1, 0)
    return c // (BATCH // BGROUP), c % (BATCH // BGROUP)


def _proj_body(h_ref, mod_ref, g_ref, w_ref, segq_ref, segk_ref, qn_ref, kn_ref, mqn_ref, mkvn_ref,
               wuq_ref, wukv_ref, place_ref,
               cb_ref, sab_ref, sbb_ref, cm_ref, sam_ref, sbm_ref, ck_ref, sak_ref, sbk_ref,
               qa_ref, ka_ref, va_ref, qb_ref, kb_ref, vb_ref, qm_ref, km_ref, vm_ref, z0_scr, z1_scr):
    i = pl.program_id(0)

    @pl.when(i == 0)
    def _():
        z1_scr[...] = jnp.zeros(z1_scr.shape, F32)

    def step(z_prod, z_cons):
        t, b = _proj_prod(i)
        for r in range(BGROUP):
            shift, mscale, _ = _mod_vectors(mod_ref, r, t, b)
            n = _rmsnorm(h_ref[r], g_ref[...]) * (1.0 + mscale) + shift
            z_prod[r] = _bdot(n.astype(BF16), w_ref[...])
        _proj_mixers(z_cons, segq_ref, segk_ref, qn_ref, kn_ref, mqn_ref, mkvn_ref, wuq_ref, wukv_ref, place_ref,
                     cb_ref, sab_ref, sbb_ref, cm_ref, sam_ref, sbm_ref, ck_ref, sak_ref, sbk_ref,
                     qa_ref, ka_ref, va_ref, qb_ref, kb_ref, vb_ref, qm_ref, km_ref, vm_ref)

    pl.when(i % 2 == 0)(functools.partial(step, z0_scr, z1_scr))
    pl.when(i % 2 == 1)(functools.partial(step, z1_scr, z0_scr))


def _proj_mixers(z_ref, segq_ref, segk_ref, qn_ref, kn_ref, mqn_ref, mkvn_ref, wuq_ref, wukv_ref, place_ref,
                 cb_ref, sab_ref, sbb_ref, cm_ref, sam_ref, sbm_ref, ck_ref, sak_ref, sbk_ref,
                 qa_ref, ka_ref, va_ref, qb_ref, kb_ref, vb_ref, qm_ref, km_ref, vm_ref):
    scale = LOG2_E * HEAD_DIM ** -0.5
    mla_scale = LOG2_E * (MLA_NOPE + MLA_ROPE) ** -0.5
    cb, sab, sbb = cb_ref[...], sab_ref[...], sbb_ref[...]
    for r in range(BGROUP):
        z = z_ref[r]
        qa_ref[r] = (z[:, _OFF_AQ:_OFF_AQ + NA_WIDTH] * scale).astype(BF16)
        ka_ref[r] = z[:, _OFF_AK:_OFF_AK + NA_WIDTH].astype(BF16)
        va_ref[r] = z[:, _OFF_AV:_OFF_AV + NA_WIDTH].T.astype(BF16)

        bq = z[:, _OFF_BQ:_OFF_BQ + GQA_Q_WIDTH]
        bq = (bq * lax.rsqrt(_seg_mean_sq(bq, segq_ref[...]) + EPS)) * qn_ref[...]
        qb_ref[r] = (_rope(bq, cb, sab, sbb, HEAD_DIM // 4) * scale).astype(BF16)
        bk = z[:, _OFF_BK:_OFF_BK + GQA_KV_WIDTH]
        bk = (bk * lax.rsqrt(_seg_mean_sq(bk, segk_ref[...]) + EPS)) * kn_ref[...]
        kb_ref[r] = _rope(bk, cb[:, :GQA_KV_WIDTH], sab[:, :GQA_KV_WIDTH], sbb[:, :GQA_KV_WIDTH],
                          HEAD_DIM // 4).astype(BF16)
        vb_ref[r] = z[:, _OFF_BV:_OFF_BV + GQA_KV_WIDTH].T.astype(BF16)

        cq = _rmsnorm(z[:, _OFF_CQ:_OFF_CQ + MLA_Q_RANK], mqn_ref[...])
        q_lat = _bdot(cq.astype(BF16), wuq_ref[...])
        qm_ref[r] = (_rope(q_lat, cm_ref[...], sam_ref[...], sbm_ref[...], MLA_ROPE // 4)
                     * mla_scale).astype(BF16)
        ckv = _rmsnorm(z[:, _OFF_CKV:_OFF_CKV + MLA_KV_RANK], mkvn_ref[...])
        kv_lat = _bdot(ckv.astype(BF16), wukv_ref[...])
        vm_ref[r] = kv_lat[:, MLA_QK_WIDTH:].T.astype(BF16)
        lane = lax.broadcasted_iota(jnp.int32, (1, LANES), 1)
        kr = jnp.where(lane < MLA_ROPE, z[:, _OFF_CKR:_OFF_CKR + LANES], 0.0)
        kr = _rope(kr, ck_ref[...], sak_ref[...], sbk_ref[...], MLA_ROPE // 4).astype(BF16)
        km_ref[r] = (kv_lat[:, :MLA_QK_WIDTH] + _bdot(kr, place_ref[...])).astype(BF16)


def _proj(h, mod, layer, g_norm, w_qkv, consts, tables):
    widths = (NA_WIDTH, NA_WIDTH, NA_WIDTH, GQA_Q_WIDTH, GQA_KV_WIDTH, GQA_KV_WIDTH,
              MLA_QK_WIDTH, MLA_QK_WIDTH, MLA_V_WIDTH)
    is_value = [k % 3 == 2 for k in range(len(widths))]
    tab_specs = [pl.BlockSpec((TM, t.shape[-1]), lambda i: (_proj_cons(i)[0], 0)) for t in tables]

    def mod_index(i):
        t, b = _proj_prod(i)
        return layer, jnp.where(t >= NT_LAT, CTX_ROW, b * BGROUP) // SUBLANES, 1

    def out_spec(is_val, w):
        if is_val:
            return pl.BlockSpec((BGROUP, w, TM), lambda i: (_proj_cons(i)[1], 0, _proj_cons(i)[0]))
        return pl.BlockSpec((BGROUP, TM, w), lambda i: (_proj_cons(i)[1], _proj_cons(i)[0], 0))

    return pl.pallas_call(
        _proj_body,
        grid=(_PROJ_GROUPS + 1,),
        in_specs=[pl.BlockSpec((BGROUP, TM, D_MODEL), lambda i: (_proj_prod(i)[1], _proj_prod(i)[0], 0)),
                  pl.BlockSpec((None, SUBLANES, 3 * D_MODEL), mod_index),
                  _layer_spec(g_norm, layer), _layer_spec(w_qkv, layer)]
                 + [_layer_spec(c, layer) if c.ndim == 3 else _const_spec(c.shape) for c in consts] + tab_specs,
        out_specs=[out_spec(v, w) for v, w in zip(is_value, widths)],
        out_shape=[jax.ShapeDtypeStruct((BATCH, w, TOK) if v else (BATCH, TOK, w), BF16)
                   for v, w in zip(is_value, widths)],
        scratch_shapes=[pltpu.VMEM((BGROUP, TM, QKV_WIDTH), F32)] * 2,
        compiler_params=_params(1),
        name="mix_proj",
    )(h, mod, g_norm, w_qkv, *consts, *tables)


_CTX_MIXERS = ((NA_HEADS, 1, HEAD_DIM, HEAD_DIM), (GQA_HEADS, GQA_GRP, HEAD_DIM, HEAD_DIM),
               (MLA_HEADS, 1, MLA_PAD, MLA_V))


def _ctx_attn_body(*refs):
    ones = jnp.ones((BF16_SUBLANES, TM), BF16)
    n_mix = len(_CTX_MIXERS)
    for r in range(BGROUP):
        for mix, (heads, grp, dqk, dv) in enumerate(_CTX_MIXERS):
            q_ref, k_ref, vt_ref = refs[3 * mix:3 * mix + 3]
            outs = []
            for hd in range(heads):
                g = hd // grp
                st = _dot_nt(k_ref[r, :, g * dqk:(g + 1) * dqk], q_ref[r, :, hd * dqk:(hd + 1) * dqk])
                p = jnp.exp2(st - jnp.max(st, axis=0, keepdims=True)).astype(BF16)
                acc = _bdot(jnp.concatenate([vt_ref[r, g * dv:(g + 1) * dv, :], ones], axis=0), p)
                outs.append(acc[:dv] * (1.0 / acc[dv:dv + 1]))
            refs[3 * n_mix + mix][r] = jnp.concatenate(outs, axis=0).astype(BF16)


def _ctx_attn(qkv):
    in_specs, arrays = [], []
    for q, k, vt in qkv:
        arrays += [q, k, vt]
        in_specs += [pl.BlockSpec((BGROUP, TM, q.shape[-1]), lambda b: (b, NT_LAT, 0)),
                     pl.BlockSpec((BGROUP, TM, k.shape[-1]), lambda b: (b, NT_LAT, 0)),
                     pl.BlockSpec((BGROUP, vt.shape[1], TM), lambda b: (b, 0, NT_LAT))]
    widths = [heads * dv for heads, _, _, dv in _CTX_MIXERS]
    return pl.pallas_call(
        _ctx_attn_body,
        grid=(BATCH // BGROUP,),
        in_specs=in_specs,
        out_specs=[pl.BlockSpec((BGROUP, w, TM), lambda b: (b, 0, 0)) for w in widths],
        out_shape=[jax.ShapeDtypeStruct((BATCH, w, TM), BF16) for w in widths],
        compiler_params=_params(1),
        name="ctx_attn",
    )(*arrays)


def _pipe_step(prod, cons, n_chunk, dv):
    ones = jnp.ones((BF16_SUBLANES, KEY_CHUNK), BF16)
    m_cons = [m_ref[0:1, :] for _, _, m_ref in cons]
    m_new = [None] * len(prod)
    acc = [None] * len(cons)
    for c in range(n_chunk):
        rows = slice(c * KEY_CHUNK, (c + 1) * KEY_CHUNK)
        for d, (qg, k_chunk, bias_chunk, s_ref, _) in enumerate(prod):
            st = _dot_nt(k_chunk(c), qg)
            bias = None if bias_chunk is None else bias_chunk(c)
            if bias is not None:
                st = st + bias
            s_ref[rows, :] = st
            mc = jnp.max(st, axis=0, keepdims=True)
            m_new[d] = mc if m_new[d] is None else jnp.maximum(m_new[d], mc)
        for d, (vt_chunk, s_ref, _) in enumerate(cons):
            p = jnp.exp2(s_ref[rows, :] - m_cons[d]).astype(BF16)
            part = _bdot(jnp.concatenate([vt_chunk(c), ones], axis=0), p)
            acc[d] = part if acc[d] is None else acc[d] + part
    for d, (_, _, _, _, m_ref) in enumerate(prod):
        m_ref[0:1, :] = m_new[d]
    return [a[:dv] * (1.0 / a[dv:dv + 1]) for a in acc]


def _pipe_run(step, scratch, n_dots):
    i = pl.program_id(0)
    s = [scratch[slot * n_dots:(slot + 1) * n_dots] for slot in range(2)]
    m = [scratch[(2 + slot) * n_dots:(3 + slot) * n_dots] for slot in range(2)]

    @pl.when(i == 0)
    def _():
        for ref in s[1] + m[1]:
            ref[...] = jnp.zeros(ref.shape, F32)

    pl.when(i % 2 == 0)(functools.partial(step, 0, 1, s, m))
    pl.when(i % 2 == 1)(functools.partial(step, 1, 0, s, m))


def _n_chains(ag):
    return (BATCH // ag) * NT_LAT


def _prod_chain(i, ag=1):
    c = jnp.minimum(i, _n_chains(ag) - 1)
    return c // NT_LAT, c % NT_LAT, 0


def _cons_chain(i, ag=1):
    c = jnp.maximum(i - 1, 0)
    return c // NT_LAT, c % NT_LAT, 0


def _pipe_scratch(nk, m_len, n_dots):
    return ([pltpu.VMEM((nk, m_len), F32)] * (2 * n_dots)
            + [pltpu.VMEM((8, m_len), F32)] * (2 * n_dots))


def _gqa_body(q_ref, k_ref, vt_ref, o_ref, *scratch):
    def run(sp, sc, s, m):
        q = q_ref[...]
        prod, cons = [], []
        for g in range(GQA_KV_HEADS):
            qg = jnp.concatenate([q[:, (g * GQA_GRP + j) * HEAD_DIM:(g * GQA_GRP + j + 1) * HEAD_DIM]
                                  for j in range(GQA_GRP)], axis=0)

            def k_chunk(c, g=g):
                return k_ref[c * KEY_CHUNK:(c + 1) * KEY_CHUNK, g * HEAD_DIM:(g + 1) * HEAD_DIM]

            def vt_chunk(c, g=g):
                return vt_ref[g * HEAD_DIM:(g + 1) * HEAD_DIM, c * KEY_CHUNK:(c + 1) * KEY_CHUNK]

            prod.append((qg, k_chunk, None, s[sp][g], m[sp][g]))
            cons.append((vt_chunk, s[sc][g], m[sc][g]))
        outs = _pipe_step(prod, cons, TOK // KEY_CHUNK, HEAD_DIM)
        o = jnp.concatenate([ot[:, j * TM:(j + 1) * TM] for ot in outs for j in range(GQA_GRP)], axis=0)
        o_ref[...] = o.astype(BF16)

    _pipe_run(run, scratch, GQA_KV_HEADS)


def _full_key_attn(body, q, k, vt, *, w_out, n_dots, m_len, name):
    return pl.pallas_call(
        body,
        grid=(_n_chains(1) + 1,),
        in_specs=[
            pl.BlockSpec((None, TM, q.shape[-1]), _prod_chain),
            pl.BlockSpec((None, TOK, k.shape[-1]), lambda i: (_prod_chain(i)[0], 0, 0)),
            pl.BlockSpec((None, vt.shape[1], TOK), lambda i: (_cons_chain(i)[0], 0, 0)),
        ],
        out_specs=pl.BlockSpec((None, w_out, TM), lambda i: (_cons_chain(i)[0], 0, _cons_chain(i)[1])),
        out_shape=jax.ShapeDtypeStruct((BATCH, w_out, SEQ), BF16),
        scratch_shapes=_pipe_scratch(TOK, m_len, n_dots),
        compiler_params=_params(1),
        name=name,
    )(q, k, vt)


def _gqa_attn(q, k, vt):
    return _full_key_attn(_gqa_body, q, k, vt, w_out=GQA_Q_WIDTH, n_dots=GQA_KV_HEADS, m_len=GQA_GRP * TM,
                          name="gqa_attn")


def _mla_body(q_ref, k_ref, vt_ref, o_ref, *scratch):
    def run(sp, sc, s, m):
        prod, cons = [], []
        for d in range(MLA_HEADS):
            def k_chunk(c, d=d):
                return k_ref[c * KEY_CHUNK:(c + 1) * KEY_CHUNK, d * MLA_PAD:(d + 1) * MLA_PAD]

            def vt_chunk(c, d=d):
                return vt_ref[d * MLA_V:(d + 1) * MLA_V, c * KEY_CHUNK:(c + 1) * KEY_CHUNK]

            prod.append((q_ref[:, d * MLA_PAD:(d + 1) * MLA_PAD], k_chunk, None, s[sp][d], m[sp][d]))
            cons.append((vt_chunk, s[sc][d], m[sc][d]))
        outs = _pipe_step(prod, cons, TOK // KEY_CHUNK, MLA_V)
        o_ref[...] = jnp.concatenate(outs, axis=0).astype(BF16)

    _pipe_run(run, scratch, MLA_HEADS)


def _mla_attn(q, k, vt):
    return _full_key_attn(_mla_body, q, k, vt, w_out=MLA_V_WIDTH, n_dots=MLA_HEADS, m_len=TM, name="mla_attn")


def _na_start_block(t):
    return jnp.clip(t - 1, 0, NT_LAT - NA_KBLK)


def _na_body(q_ref, k0_ref, k1_ref, k2_ref, kc_ref, v0_ref, v1_ref, v2_ref, vc_ref, bias_ref, o_ref, *scratch):
    k_blocks = (k0_ref, k1_ref, k2_ref, kc_ref)
    v_blocks = (v0_ref, v1_ref, v2_ref, vc_ref)

    def run(sp, sc, s, m):
        prod, cons = [], []
        for a in range(NA_GROUP):
            for d in range(NA_HEADS):
                def k_chunk(c, a=a, d=d):
                    return k_blocks[c][a, :, d * HEAD_DIM:(d + 1) * HEAD_DIM]

                def bias_chunk(c, d=d):
                    return None if c == NA_KBLK else bias_ref[d, c * KEY_CHUNK:(c + 1) * KEY_CHUNK, :]

                def vt_chunk(c, a=a, d=d):
                    return v_blocks[c][a, d * HEAD_DIM:(d + 1) * HEAD_DIM, :]

                n = a * NA_HEADS + d
                prod.append((q_ref[a, :, d * HEAD_DIM:(d + 1) * HEAD_DIM], k_chunk, bias_chunk, s[sp][n], m[sp][n]))
                cons.append((vt_chunk, s[sc][n], m[sc][n]))
        outs = _pipe_step(prod, cons, NA_NKEY // KEY_CHUNK, HEAD_DIM)
        for a in range(NA_GROUP):
            o_ref[a] = jnp.concatenate(outs[a * NA_HEADS:(a + 1) * NA_HEADS], axis=0).astype(BF16)

    _pipe_run(run, scratch, NA_GROUP * NA_HEADS)


def _na_attn(q, k, vt, bias, layer):
    def prod(i):
        return _prod_chain(i, NA_GROUP)

    def cons(i):
        return _cons_chain(i, NA_GROUP)

    def k_spec(j):
        def index(i):
            b, t, _ = prod(i)
            return b, (NT_LAT if j == NA_KBLK else _na_start_block(t) + j), 0
        return pl.BlockSpec((NA_GROUP, TM, NA_WIDTH), index)

    def vt_spec(j):
        def index(i):
            b, t, _ = cons(i)
            return b, 0, (NT_LAT if j == NA_KBLK else _na_start_block(t) + j)
        return pl.BlockSpec((NA_GROUP, NA_WIDTH, TM), index)

    def bias_index(i):
        t = prod(i)[1]
        return layer, 0, jnp.where(t == 0, 0, jnp.where(t == NT_LAT - 1, 2, 1)), 0, 0

    return pl.pallas_call(
        _na_body,
        grid=(_n_chains(NA_GROUP) + 1,),
        in_specs=[pl.BlockSpec((NA_GROUP, TM, NA_WIDTH), prod)]
                 + [k_spec(j) for j in range(NA_KBLK + 1)] + [vt_spec(j) for j in range(NA_KBLK + 1)]
                 + [pl.BlockSpec((None, NA_HEADS, None, NA_NLOCAL, TM), bias_index)],
        out_specs=pl.BlockSpec((NA_GROUP, NA_WIDTH, TM), lambda i: (cons(i)[0], 0, cons(i)[1])),
        out_shape=jax.ShapeDtypeStruct((BATCH, NA_WIDTH, SEQ), BF16),
        scratch_shapes=_pipe_scratch(NA_NKEY, TM, NA_GROUP * NA_HEADS),
        compiler_params=_params(1),
        name="na_attn",
    )(q, *([k] * (NA_KBLK + 1)), *([vt] * (NA_KBLK + 1)), bias)


def _na_bias_table(rel_bias):
    lead = rel_bias.shape[:-2]
    rel_bias = rel_bias.reshape((-1,) + rel_bias.shape[-2:])
    nh = rel_bias.shape[0]
    cols = np.arange(GRID_W)
    c0 = np.clip(cols - NA_COLS // 2, 0, GRID_W - NA_COLS)
    col_in = (cols[None, :] >= c0[:, None]) & (cols[None, :] < c0[:, None] + NA_COLS)
    n_dr = 2 * NA_ROWS - 1
    pad = GRID_W - NA_COLS
    rbp = jnp.pad(rel_bias.astype(F32) * LOG2_E, ((0, 0), (0, 0), (pad, pad)), mode="edge")
    tiles = jnp.stack([rbp[:, :, GRID_W - 1 - cq:2 * GRID_W - 1 - cq] for cq in range(GRID_W)], axis=3)
    tiles = jnp.where(col_in.T[None, None], tiles, NEG_BIG)
    tiles = jnp.concatenate([tiles, jnp.full((nh, 1, GRID_W, GRID_W), NEG_BIG, F32)], axis=1)
    dr_blk = np.full((3, NA_QROWS, NA_KROWS), n_dr, np.int32)
    for cls, tile in enumerate((0, 2, NT_LAT - 1)):
        start = int(np.clip(tile - 1, 0, NT_LAT - NA_KBLK)) * NA_QROWS
        for qr in range(NA_QROWS):
            r = tile * NA_QROWS + qr
            r0 = int(np.clip(r - NA_ROWS // 2, 0, ROWS - NA_ROWS))
            for kj in range(NA_KROWS):
                kr = start + kj
                if r0 <= kr < r0 + NA_ROWS:
                    dr_blk[cls, qr, kj] = kr - r + NA_ROWS - 1
    local = jnp.take(tiles, jnp.asarray(dr_blk.reshape(-1)), axis=1)
    local = local.reshape(nh, 3, NA_QROWS, NA_KROWS, GRID_W, GRID_W).transpose(0, 1, 3, 4, 2, 5)
    return local.reshape(lead + (3, NA_NLOCAL, TM))


def _merge_body(h_ref, mod_ref, g_ref, wgate_ref, wa_ref, wb_ref, wm_ref, wo_ref, *rest, split):
    o_ref = rest[-1]
    for r in range(BGROUP):
        branch, pos = [], 0
        for has_ctx in split:
            if has_ctx:
                is_ctx = pl.program_id(0) >= NT_LAT
                branch.append(jnp.where(is_ctx, rest[pos + 1][r], rest[pos][r]))
            else:
                branch.append(rest[pos][r])
            pos += 2 if has_ctx else 1
        h = h_ref[r]
        shift, scale, gate = _mod_vectors(mod_ref, r)
        n = _rmsnorm(h, g_ref[...]) * (1.0 + scale) + shift
        gates =_sigmoid(_bdot(n.astype(BF16), wgate_ref[...]))
        y = (gates[:, :D_MODEL] * _dot_tn(branch[0], wa_ref[...])
             + gates[:, D_MODEL:2 * D_MODEL] * _dot_tn(branch[1], wb_ref[...])
             + gates[:, 2 * D_MODEL:] * _dot_tn(branch[2], wm_ref[...]))
        o_ref[r] = h + gate * _bdot(y.astype(BF16), wo_ref[...])


def _merge(h, mod, layer, g_norm, w_gate, branches, wa, wb, wm, wo, *, n_tiles):
    split = tuple(isinstance(br, tuple) for br in branches)
    arrays, specs = [], []
    for br in branches:
        if isinstance(br, tuple):
            lat, ctx = br
            arrays += [lat, ctx]
            specs += [pl.BlockSpec((BGROUP, lat.shape[1], TM), lambda t, b: (b, 0, jnp.minimum(t, NT_LAT - 1))),
                      pl.BlockSpec((BGROUP, ctx.shape[1], TM), lambda t, b: (b, 0, 0))]
        else:
            arrays.append(br)
            specs.append(pl.BlockSpec((BGROUP, br.shape[1], TM), lambda t, b: (b, 0, t)))
    return pl.pallas_call(
        functools.partial(_merge_body, split=split),
        grid=(n_tiles, BATCH // BGROUP),
        in_specs=[_tok_spec(D_MODEL), _mod_spec(layer, 1)]
                 + [_layer_spec(a, layer) for a in (g_norm, w_gate, wa, wb, wm, wo)] + specs,
        out_specs=_tok_spec(D_MODEL),
        out_shape=jax.ShapeDtypeStruct((BATCH, n_tiles * TM, D_MODEL), F32),
        compiler_params=_params(2),
        name="gated_merge",
    )(h, mod, g_norm, w_gate, wa, wb, wm, wo, *arrays)


def _rope_tables():
    t = np.arange(SEQ)
    row = (t // GRID_W).astype(np.float32)[:, None]
    col = (t % GRID_W).astype(np.float32)[:, None]

    def head_tables(d_rot):
        half = d_rot // 2
        freqs = np.float32(ROPE_THETA) ** (-np.arange(0, half, 2, dtype=np.float32) / np.float32(half))
        r, c = row * freqs, col * freqs
        ang = np.concatenate([r, r, c, c], axis=-1).astype(np.float32)
        first = (np.arange(d_rot) % half) < (half // 2)
        cos, sin = np.cos(ang), np.sin(ang)
        zero = np.zeros_like(sin)
        return cos, np.where(first, -sin, zero), np.where(first, zero, sin)

    def place(tabs, reps, lo, group):
        out = []
        for k, tab in enumerate(tabs):
            fill = 1.0 if k == 0 else 0.0
            d_rot = tab.shape[-1]
            g = np.concatenate([np.full((SEQ, lo), fill, np.float32), tab,
                                np.full((SEQ, group - lo - d_rot), fill, np.float32)], axis=-1)
            g = np.tile(g, (1, reps))
            ctx = np.full((CTX_LEN, reps * group), fill, np.float32)
            out.append(jnp.asarray(np.concatenate([g, ctx], axis=0), F32))
        return out

    tb = head_tables(HEAD_DIM)
    tm = head_tables(MLA_ROPE)
    return (place(tb, GQA_HEADS, 0, HEAD_DIM)
            + place(tm, MLA_HEADS, MLA_NOPE, MLA_PAD)
            + place(tm, 1, 0, LANES))


def _seg_matrix(width):
    idx = np.arange(width) // HEAD_DIM
    return jnp.asarray((idx[:, None] == idx[None, :]).astype(np.float32) / HEAD_DIM, BF16)


def _place_matrix():
    m = np.zeros((LANES, MLA_QK_WIDTH), np.float32)
    for hd in range(MLA_HEADS):
        for j in range(MLA_ROPE):
            m[j, hd * MLA_PAD + MLA_NOPE + j] = 1.0
    return jnp.asarray(m, BF16)


def _pad_heads(w, n_heads, per_head, keep_lo, keep_hi):
    d, k = w.shape[:2]
    wh = w.reshape(d, k, n_heads, per_head)[..., keep_lo:keep_hi]
    wh = jnp.pad(wh, ((0, 0), (0, 0), (0, 0), (0, MLA_PAD - (keep_hi - keep_lo))))
    return wh.reshape(d, k, n_heads * MLA_PAD)


def kernel(x, c, ctx, c_ctx, w_ada, b_ada, ffn1_norm, ffn1_w_gate, ffn1_w_up, ffn1_w_down, mix_norm, w_in,
           na_rel_bias, gqa_q_norm, gqa_k_norm, mla_q_norm, mla_kv_norm, mla_w_uq, mla_w_ukv,
           w_branch_a, w_branch_b, w_branch_c, w_out, ffn2_norm, ffn2_w_gate, ffn2_w_up, ffn2_w_down,
           final_norm):
    assert x.shape == (BATCH, SEQ, D_MODEL) and ctx.shape == (BATCH, CTX_LEN, D_MODEL)

    c_all = jnp.concatenate([c, jnp.tile(c_ctx[None, :], (BGROUP, 1)),
                             jnp.zeros((MOD_ROWS - BATCH - BGROUP, D_MODEL), F32)], axis=0)
    mod_all = _ada(c_all, w_ada, b_ada)

    tables = _rope_tables()
    seg_q, seg_k, place = _seg_matrix(2 * LANES), _seg_matrix(GQA_KV_WIDTH), _place_matrix()
    fin = final_norm.reshape(1, D_MODEL)

    def row(v):
        return v[:, None, :]

    w_qkv = w_in[:, :, :QKV_WIDTH].astype(BF16)
    w_gate = w_in[:, :, N_QKV_COLS:].astype(BF16)
    w_uq = _pad_heads(mla_w_uq, MLA_HEADS, MLA_NOPE + MLA_ROPE, 0, MLA_NOPE + MLA_ROPE).astype(BF16)
    w_uk = _pad_heads(mla_w_ukv, MLA_HEADS, MLA_NOPE + MLA_V, 0, MLA_NOPE)
    w_uv = mla_w_ukv.reshape(DEPTH, MLA_KV_RANK, MLA_HEADS, MLA_NOPE + MLA_V)[..., MLA_NOPE:]
    w_ukv = jnp.concatenate([w_uk, w_uv.reshape(DEPTH, MLA_KV_RANK, MLA_V_WIDTH)], axis=-1).astype(BF16)
    consts = (seg_q, seg_k, row(jnp.tile(gqa_q_norm, (1, GQA_HEADS))), row(jnp.tile(gqa_k_norm, (1, GQA_KV_HEADS))),
              row(mla_q_norm), row(mla_kv_norm), w_uq, w_ukv, place)
    ffn1 = (row(ffn1_norm), ffn1_w_gate.astype(BF16), ffn1_w_up.astype(BF16), ffn1_w_down.astype(BF16))
    ffn2 = (row(ffn2_norm), ffn2_w_gate.astype(BF16), ffn2_w_up.astype(BF16), ffn2_w_down.astype(BF16))
    branch_w = (w_branch_a.astype(BF16), w_branch_b.astype(BF16), w_branch_c.astype(BF16), w_out.astype(BF16))
    na_bias = _na_bias_table(na_rel_bias)
    g_mix = row(mix_norm)

    h = (x, ctx)
    for i in range(DEPTH):
        last = i == DEPTH - 1
        h = _ffn(h, mod_all, i, 0, *ffn1, fin, n_tiles=NT_ALL, final=False)
        qa, ka, va, qb, kb, vb, qm, km, vm = _proj(h, mod_all, i, g_mix, w_qkv, consts, tables)

        oa = _na_attn(qa, ka, va, na_bias, i)
        ob = _gqa_attn(qb, kb, vb)
        om = _mla_attn(qm, km, vm)
        if not last:
            oa_c, ob_c, om_c = _ctx_attn(((qa, ka, va), (qb, kb, vb), (qm, km, vm)))
            oa, ob, om = (oa, oa_c), (ob, ob_c), (om, om_c)

        n_tiles = NT_LAT if last else NT_ALL
        h = _merge(h, mod_all, i, g_mix, w_gate, (oa, ob, om), *branch_w, n_tiles=n_tiles)
        h = _ffn(h, mod_all, i, 2, *ffn2, fin, n_tiles=n_tiles, final=last)
    return h
```

```python
import functools

import numpy as np
import jax
import jax.numpy as jnp
from jax import lax
from jax.experimental import pallas as pl
from jax.experimental.pallas import tpu as pltpu

D_MODEL = 1024
BATCH = 32
SEQ = 2048
DEPTH = 2
CTX_LEN = 256
GRID_W = 64
HEAD_DIM = 64
NA_HEADS = 4
NA_ROWS = 8
NA_COLS = 16
GQA_HEADS = 8
GQA_KV_HEADS = 2
MLA_HEADS = 4
MLA_Q_RANK = 256
MLA_KV_RANK = 128
MLA_NOPE = 64
MLA_ROPE = 32
MLA_V = 64
N_MOD = 9
ROPE_THETA = 10000.0
EPS = 1e-6
NEG_BIG = -1e30

NA_WIDTH = NA_HEADS * HEAD_DIM
GQA_Q_WIDTH = GQA_HEADS * HEAD_DIM
GQA_KV_WIDTH = GQA_KV_HEADS * HEAD_DIM
GQA_GRP = GQA_HEADS // GQA_KV_HEADS
MLA_V_WIDTH = MLA_HEADS * MLA_V
ROWS = SEQ // GRID_W

LANES = 128
SUBLANES = 8
BF16_SUBLANES = 16
LOG2_E = 1.4426950408889634
TOK = SEQ + CTX_LEN
TM = 256
NT_ALL = TOK // TM
NT_LAT = SEQ // TM
BGROUP = 4
MOD_ROWS = 40
CTX_ROW = BATCH
ADA_BLOCK = 1152
MLA_PAD = LANES
MLA_QK_WIDTH = MLA_HEADS * MLA_PAD
N_QKV_COLS = 1952
QKV_WIDTH = 2048
NA_QROWS = TM // GRID_W
NA_KROWS = 12
NA_KBLK = NA_KROWS * GRID_W // TM
NA_NLOCAL = NA_KROWS * GRID_W
NA_NKEY = NA_NLOCAL + CTX_LEN
KEY_CHUNK = 256
NA_GROUP = 4
VMEM_LIMIT = 56 * 1024 * 1024

F32 = jnp.float32
BF16 = jnp.bfloat16

_OFF_AQ, _OFF_AK, _OFF_AV = 0, 256, 512
_OFF_BQ, _OFF_BK, _OFF_BV = 768, 1280, 1408
_OFF_CQ, _OFF_CKV, _OFF_CKR = 1536, 1792, 1920


def _params(n_axes):
    return pltpu.CompilerParams(dimension_semantics=("arbitrary",) * n_axes,
                                vmem_limit_bytes=VMEM_LIMIT)


def _const_spec(shape):
    nd = len(shape)
    return pl.BlockSpec(shape, lambda *_: (0,) * nd, pipeline_mode=pl.Buffered(1))


def _layer_spec(arr, layer):
    nd = arr.ndim - 1
    return pl.BlockSpec((None,) + arr.shape[1:], lambda *_: (layer,) + (0,) * nd, pipeline_mode=pl.Buffered(1))


def _mod_spec(layer, sub):
    return pl.BlockSpec((None, SUBLANES, 3 * D_MODEL),
                        lambda t, b: (layer, jnp.where(t >= NT_LAT, CTX_ROW, b * BGROUP) // SUBLANES, sub))


def _mod_vectors(mod_ref, r, t=None, b=None):
    t = pl.program_id(0) if t is None else t
    b = pl.program_id(1) if b is None else b
    first = jnp.where(t >= NT_LAT, CTX_ROW, b * BGROUP) % SUBLANES
    row = mod_ref[pl.ds(first + r, 1), :]
    return row[:, :D_MODEL], row[:, D_MODEL:2 * D_MODEL], row[:, 2 * D_MODEL:]


def _tok_spec(width):
    return pl.BlockSpec((BGROUP, TM, width), lambda t, b: (b, t, 0))


def _sigmoid(x):
    return 1.0 / (1.0 + jnp.exp(-x))


def _rmsnorm(x, g):
    return (x * lax.rsqrt(jnp.mean(x * x, axis=-1, keepdims=True) + EPS)) * g


def _bdot(a, b):
    return jnp.dot(a, b, preferred_element_type=F32)


def _dot_tn(a, b):
    return lax.dot_general(a, b, (((0,), (0,)), ((), ())), preferred_element_type=F32)


def _dot_nt(a, b):
    return lax.dot_general(a, b, (((1,), (1,)), ((), ())), preferred_element_type=F32)


def _ada_body(c_ref, w_ref, b_ref, o_ref):
    c = c_ref[...]
    s = c * _sigmoid(c)
    o_ref[...] = _bdot(s.astype(BF16), w_ref[...].astype(BF16)) + b_ref[...]


def _ada(c_all, w_ada, b_ada):
    nblk = (N_MOD * D_MODEL) // ADA_BLOCK
    return pl.pallas_call(
        _ada_body,
        grid=(DEPTH, nblk),
        in_specs=[
            pl.BlockSpec((MOD_ROWS, D_MODEL), lambda i, j: (0, 0)),
            pl.BlockSpec((None, D_MODEL, ADA_BLOCK), lambda i, j: (i, 0, j)),
            pl.BlockSpec((None, 1, ADA_BLOCK), lambda i, j: (i, 0, j)),
        ],
        out_specs=pl.BlockSpec((None, MOD_ROWS, ADA_BLOCK), lambda i, j: (i, 0, j)),
        out_shape=jax.ShapeDtypeStruct((DEPTH, MOD_ROWS, N_MOD * D_MODEL), F32),
        compiler_params=_params(2),
        name="ada_mod",
    )(c_all, w_ada, b_ada.reshape(DEPTH, 1, N_MOD * D_MODEL))


def _ffn_body(*refs, final, split):
    if split:
        h_ref, hc_ref, mod_ref, g_ref, wg_ref, wu_ref, wd_ref, fin_ref, o_ref = refs
    else:
        h_ref, mod_ref, g_ref, wg_ref, wu_ref, wd_ref, fin_ref, o_ref = refs
    for r in range(BGROUP):
        h = h_ref[r]
        if split:
            h = jnp.where(pl.program_id(0) >= NT_LAT, hc_ref[r], h)
        shift, scale, gate = _mod_vectors(mod_ref, r)
        n = _rmsnorm(h, g_ref[...]) * (1.0 + scale) + shift
        nb = n.astype(BF16)
        g = _bdot(nb, wg_ref[...])
        u = _bdot(nb, wu_ref[...])
        a = (g * _sigmoid(g)) * u
        d = _bdot(a.astype(BF16), wd_ref[...])
        out = h + 0.5 * gate * d
        if final:
            out = _rmsnorm(out, fin_ref[...])
        o_ref[r] = out


def _ffn(h, mod, layer, sub, g_norm, wg, wu, wd, fin, *, n_tiles, final):
    split = isinstance(h, tuple)
    if split:
        h_arrays = list(h)
        h_specs = [pl.BlockSpec((BGROUP, TM, D_MODEL), lambda t, b: (b, jnp.minimum(t, NT_LAT - 1), 0)),
                   pl.BlockSpec((BGROUP, TM, D_MODEL), lambda t, b: (b, 0, 0))]
    else:
        h_arrays, h_specs = [h], [_tok_spec(D_MODEL)]
    return pl.pallas_call(
        functools.partial(_ffn_body, final=final, split=split),
        grid=(n_tiles, BATCH // BGROUP),
        in_specs=h_specs + [_mod_spec(layer, sub)] + [_layer_spec(a, layer) for a in (g_norm, wg, wu, wd)]
                 + [_const_spec((1, D_MODEL))],
        out_specs=_tok_spec(D_MODEL),
        out_shape=jax.ShapeDtypeStruct((BATCH, n_tiles * TM, D_MODEL), F32),
        compiler_params=_params(2),
        name="half_ffn",
    )(*h_arrays, mod, g_norm, wg, wu, wd, fin)


def _rope(x, cos, sin, half):
    w = x.shape[-1]
    lane = lax.broadcasted_iota(jnp.int32, (1, w), 1)
    rot = jnp.where((lane & (2 * half - 1)) < half, pltpu.roll(x, w - half, 1), pltpu.roll(x, half, 1))
    return x * cos + rot * sin


def _seg_mean_sq(x, seg):
    w = seg.shape[0]
    parts = []
    for lo_col in range(0, x.shape[-1], w):
        x2 = x[:, lo_col:lo_col + w]
        x2 = x2 * x2
        hi = x2.astype(BF16)
        lo = (x2 - hi.astype(F32)).astype(BF16)
        parts.append(_bdot(hi, seg) + _bdot(lo, seg))
    return parts[0] if len(parts) == 1 else jnp.concatenate(parts, axis=1)


_PROJ_GROUPS = NT_ALL * (BATCH // BGROUP)


def _proj_prod(i):
    c = jnp.minimum(i, _PROJ_GROUPS - 1)
    return c // (BATCH // BGROUP), c % (BATCH // BGROUP)


def _proj_cons(i):
    c = jnp.maximum(i - 1, 0)
    return c // (BATCH // BGROUP), c % (BATCH // BGROUP)


def _proj_body(h_ref, mod_ref, g_ref, w_ref, segq_ref, segk_ref, qn_ref, kn_ref, mqn_ref, mkvn_ref,
               wuq_ref, wukv_ref, place_ref,
               cb_ref, sb_ref, cm_ref, sm_ref, ck_ref, sk_ref,
               qa_ref, ka_ref, va_ref, qb_ref, kb_ref, vb_ref, qm_ref, km_ref, vm_ref, z0_scr, z1_scr):
    i = pl.program_id(0)

    @pl.when(i == 0)
    def _():
        z1_scr[...] = jnp.zeros(z1_scr.shape, F32)

    def step(z_prod, z_cons):
        t, b = _proj_prod(i)
        for r in range(BGROUP):
            shift, mscale, _ = _mod_vectors(mod_ref, r, t, b)
            n = _rmsnorm(h_ref[r], g_ref[...]) * (1.0 + mscale) + shift
            z_prod[r] = _bdot(n.astype(BF16), w_ref[...])
        _proj_mixers(z_cons, segq_ref, segk_ref, qn_ref, kn_ref, mqn_ref, mkvn_ref, wuq_ref, wukv_ref, place_ref,
                     cb_ref, sb_ref, cm_ref, sm_ref, ck_ref, sk_ref,
                     qa_ref, ka_ref, va_ref, qb_ref, kb_ref, vb_ref, qm_ref, km_ref, vm_ref)

    pl.when(i % 2 == 0)(functools.partial(step, z0_scr, z1_scr))
    pl.when(i % 2 == 1)(functools.partial(step, z1_scr, z0_scr))


def _proj_mixers(z_ref, segq_ref, segk_ref, qn_ref, kn_ref, mqn_ref, mkvn_ref, wuq_ref, wukv_ref, place_ref,
                 cb_ref, sb_ref, cm_ref, sm_ref, ck_ref, sk_ref,
                 qa_ref, ka_ref, va_ref, qb_ref, kb_ref, vb_ref, qm_ref, km_ref, vm_ref):
    scale = LOG2_E * HEAD_DIM ** -0.5
    mla_scale = LOG2_E * (MLA_NOPE + MLA_ROPE) ** -0.5
    cb, sb = cb_ref[...], sb_ref[...]
    for r in range(BGROUP):
        z = z_ref[r]
        qa_ref[r] = (z[:, _OFF_AQ:_OFF_AQ + NA_WIDTH] * scale).astype(BF16)
        ka_ref[r] = z[:, _OFF_AK:_OFF_AK + NA_WIDTH].astype(BF16)
        va_ref[r] = z[:, _OFF_AV:_OFF_AV + NA_WIDTH].T.astype(BF16)

        bq = z[:, _OFF_BQ:_OFF_BQ + GQA_Q_WIDTH]
        bq = (bq * lax.rsqrt(_seg_mean_sq(bq, segq_ref[...]) + EPS)) * qn_ref[...]
        qb_ref[r] = (_rope(bq, cb, sb, HEAD_DIM // 4) * scale).astype(BF16)
        bk = z[:, _OFF_BK:_OFF_BK + GQA_KV_WIDTH]
        bk = (bk * lax.rsqrt(_seg_mean_sq(bk, segk_ref[...]) + EPS)) * kn_ref[...]
        kb_ref[r] = _rope(bk, cb[:, :GQA_KV_WIDTH], sb[:, :GQA_KV_WIDTH], HEAD_DIM // 4).astype(BF16)
        vb_ref[r] = z[:, _OFF_BV:_OFF_BV + GQA_KV_WIDTH].T.astype(BF16)

        cq = _rmsnorm(z[:, _OFF_CQ:_OFF_CQ + MLA_Q_RANK], mqn_ref[...])
        q_lat = _bdot(cq.astype(BF16), wuq_ref[...])
        qm_ref[r] = (_rope(q_lat, cm_ref[...], sm_ref[...], MLA_ROPE // 4) * mla_scale).astype(BF16)
        ckv = _rmsnorm(z[:, _OFF_CKV:_OFF_CKV + MLA_KV_RANK], mkvn_ref[...])
        kv_lat = _bdot(ckv.astype(BF16), wukv_ref[...])
        vm_ref[r] = kv_lat[:, MLA_QK_WIDTH:].T.astype(BF16)
        lane = lax.broadcasted_iota(jnp.int32, (1, LANES), 1)
        kr = jnp.where(lane < MLA_ROPE, z[:, _OFF_CKR:_OFF_CKR + LANES], 0.0)
        kr = _rope(kr, ck_ref[...], sk_ref[...], MLA_ROPE // 4).astype(BF16)
        km_ref[r] = (kv_lat[:, :MLA_QK_WIDTH] + _bdot(kr, place_ref[...])).astype(BF16)


def _proj(h, mod, layer, g_norm, w_qkv, consts, tables):
    widths = (NA_WIDTH, NA_WIDTH, NA_WIDTH, GQA_Q_WIDTH, GQA_KV_WIDTH, GQA_KV_WIDTH,
              MLA_QK_WIDTH, MLA_QK_WIDTH, MLA_V_WIDTH)
    is_value = [k % 3 == 2 for k in range(len(widths))]
    tab_specs = [pl.BlockSpec((TM, t.shape[-1]), lambda i: (_proj_cons(i)[0], 0)) for t in tables]

    def mod_index(i):
        t, b = _proj_prod(i)
        return layer, jnp.where(t >= NT_LAT, CTX_ROW, b * BGROUP) // SUBLANES, 1

    def out_spec(is_val, w):
        if is_val:
            return pl.BlockSpec((BGROUP, w, TM), lambda i: (_proj_cons(i)[1], 0, _proj_cons(i)[0]))
        return pl.BlockSpec((BGROUP, TM, w), lambda i: (_proj_cons(i)[1], _proj_cons(i)[0], 0))

    return pl.pallas_call(
        _proj_body,
        grid=(_PROJ_GROUPS + 1,),
        in_specs=[pl.BlockSpec((BGROUP, TM, D_MODEL), lambda i: (_proj_prod(i)[1], _proj_prod(i)[0], 0)),
                  pl.BlockSpec((None, SUBLANES, 3 * D_MODEL), mod_index),
                  _layer_spec(g_norm, layer), _layer_spec(w_qkv, layer)]
                 + [_layer_spec(c, layer) if c.ndim == 3 else _const_spec(c.shape) for c in consts] + tab_specs,
        out_specs=[out_spec(v, w) for v, w in zip(is_value, widths)],
        out_shape=[jax.ShapeDtypeStruct((BATCH, w, TOK) if v else (BATCH, TOK, w), BF16)
                   for v, w in zip(is_value, widths)],
        scratch_shapes=[pltpu.VMEM((BGROUP, TM, QKV_WIDTH), F32)] * 2,
        compiler_params=_params(1),
        name="mix_proj",
    )(h, mod, g_norm, w_qkv, *consts, *tables)


_CTX_MIXERS = ((NA_HEADS, 1, HEAD_DIM, HEAD_DIM), (GQA_HEADS, GQA_GRP, HEAD_DIM, HEAD_DIM),
               (MLA_HEADS, 1, MLA_PAD, MLA_V))


def _ctx_attn_body(*refs):
    ones = jnp.ones((BF16_SUBLANES, TM), BF16)
    n_mix = len(_CTX_MIXERS)
    for r in range(BGROUP):
        for mix, (heads, grp, dqk, dv) in enumerate(_CTX_MIXERS):
            q_ref, k_ref, vt_ref = refs[3 * mix:3 * mix + 3]
            outs = []
            for hd in range(heads):
                g = hd // grp
                st = _dot_nt(k_ref[r, :, g * dqk:(g + 1) * dqk], q_ref[r, :, hd * dqk:(hd + 1) * dqk])
                p = jnp.exp2(st - jnp.max(st, axis=0, keepdims=True)).astype(BF16)
                acc = _bdot(jnp.concatenate([vt_ref[r, g * dv:(g + 1) * dv, :], ones], axis=0), p)
                outs.append(acc[:dv] * (1.0 / acc[dv:dv + 1]))
            refs[3 * n_mix + mix][r] = jnp.concatenate(outs, axis=0).astype(BF16)


def _ctx_attn(qkv):
    in_specs, arrays = [], []
    for q, k, vt in qkv:
        arrays += [q, k, vt]
        in_specs += [pl.BlockSpec((BGROUP, TM, q.shape[-1]), lambda b: (b, NT_LAT, 0)),
                     pl.BlockSpec((BGROUP, TM, k.shape[-1]), lambda b: (b, NT_LAT, 0)),
                     pl.BlockSpec((BGROUP, vt.shape[1], TM), lambda b: (b, 0, NT_LAT))]
    widths = [heads * dv for heads, _, _, dv in _CTX_MIXERS]
    return pl.pallas_call(
        _ctx_attn_body,
        grid=(BATCH // BGROUP,),
        in_specs=in_specs,
        out_specs=[pl.BlockSpec((BGROUP, w, TM), lambda b: (b, 0, 0)) for w in widths],
        out_shape=[jax.ShapeDtypeStruct((BATCH, w, TM), BF16) for w in widths],
        compiler_params=_params(1),
        name="ctx_attn",
    )(*arrays)


def _pipe_step(prod, cons, n_chunk, dv):
    ones = jnp.ones((BF16_SUBLANES, KEY_CHUNK), BF16)
    m_cons = [m_ref[0:1, :] for _, _, m_ref in cons]
    m_new = [None] * len(prod)
    acc = [None] * len(cons)
    for c in range(n_chunk):
        rows = slice(c * KEY_CHUNK, (c + 1) * KEY_CHUNK)
        for d, (qg, k_chunk, bias_chunk, s_ref, _) in enumerate(prod):
            st = _dot_nt(k_chunk(c), qg)
            bias = None if bias_chunk is None else bias_chunk(c)
            if bias is not None:
                st = st + bias
            s_ref[rows, :] = st
            mc = jnp.max(st, axis=0, keepdims=True)
            m_new[d] = mc if m_new[d] is None else jnp.maximum(m_new[d], mc)
        for d, (vt_chunk, s_ref, _) in enumerate(cons):
            p = jnp.exp2(s_ref[rows, :] - m_cons[d]).astype(BF16)
            part = _bdot(jnp.concatenate([vt_chunk(c), ones], axis=0), p)
            acc[d] = part if acc[d] is None else acc[d] + part
    for d, (_, _, _, _, m_ref) in enumerate(prod):
        m_ref[0:1, :] = m_new[d]
    return [a[:dv] * (1.0 / a[dv:dv + 1]) for a in acc]


def _pipe_run(step, scratch, n_dots):
    i = pl.program_id(0)
    s = [scratch[slot * n_dots:(slot + 1) * n_dots] for slot in range(2)]
    m = [scratch[(2 + slot) * n_dots:(3 + slot) * n_dots] for slot in range(2)]

    @pl.when(i == 0)
    def _():
        for ref in s[1] + m[1]:
            ref[...] = jnp.zeros(ref.shape, F32)

    pl.when(i % 2 == 0)(functools.partial(step, 0, 1, s, m))
    pl.when(i % 2 == 1)(functools.partial(step, 1, 0, s, m))


def _n_chains(ag):
    return (BATCH // ag) * NT_LAT


def _prod_chain(i, ag=1):
    c = jnp.minimum(i, _n_chains(ag) - 1)
    return c // NT_LAT, c % NT_LAT, 0


def _cons_chain(i, ag=1):
    c = jnp.maximum(i - 1, 0)
    return c // NT_LAT, c % NT_LAT, 0


def _pipe_scratch(nk, m_len, n_dots):
    return ([pltpu.VMEM((nk, m_len), F32)] * (2 * n_dots)
            + [pltpu.VMEM((8, m_len), F32)] * (2 * n_dots))


def _gqa_body(q_ref, k_ref, vt_ref, o_ref, *scratch):
    def run(sp, sc, s, m):
        q = q_ref[...]
        prod, cons = [], []
        for g in range(GQA_KV_HEADS):
            qg = jnp.concatenate([q[:, (g * GQA_GRP + j) * HEAD_DIM:(g * GQA_GRP + j + 1) * HEAD_DIM]
                                  for j in range(GQA_GRP)], axis=0)

            def k_chunk(c, g=g):
                return k_ref[c * KEY_CHUNK:(c + 1) * KEY_CHUNK, g * HEAD_DIM:(g + 1) * HEAD_DIM]

            def vt_chunk(c, g=g):
                return vt_ref[g * HEAD_DIM:(g + 1) * HEAD_DIM, c * KEY_CHUNK:(c + 1) * KEY_CHUNK]

            prod.append((qg, k_chunk, None, s[sp][g], m[sp][g]))
            cons.append((vt_chunk, s[sc][g], m[sc][g]))
        outs = _pipe_step(prod, cons, TOK // KEY_CHUNK, HEAD_DIM)
        o = jnp.concatenate([ot[:, j * TM:(j + 1) * TM] for ot in outs for j in range(GQA_GRP)], axis=0)
        o_ref[...] = o.astype(BF16)

    _pipe_run(run, scratch, GQA_KV_HEADS)


def _full_key_attn(body, q, k, vt, *, w_out, n_dots, m_len, name):
    return pl.pallas_call(
        body,
        grid=(_n_chains(1) + 1,),
        in_specs=[
            pl.BlockSpec((None, TM, q.shape[-1]), _prod_chain),
            pl.BlockSpec((None, TOK, k.shape[-1]), lambda i: (_prod_chain(i)[0], 0, 0)),
            pl.BlockSpec((None, vt.shape[1], TOK), lambda i: (_cons_chain(i)[0], 0, 0)),
        ],
        out_specs=pl.BlockSpec((None, w_out, TM), lambda i: (_cons_chain(i)[0], 0, _cons_chain(i)[1])),
        out_shape=jax.ShapeDtypeStruct((BATCH, w_out, SEQ), BF16),
        scratch_shapes=_pipe_scratch(TOK, m_len, n_dots),
        compiler_params=_params(1),
        name=name,
    )(q, k, vt)


def _gqa_attn(q, k, vt):
    return _full_key_attn(_gqa_body, q, k, vt, w_out=GQA_Q_WIDTH, n_dots=GQA_KV_HEADS, m_len=GQA_GRP * TM,
                          name="gqa_attn")


def _mla_body(q_ref, k_ref, vt_ref, o_ref, *scratch):
    def run(sp, sc, s, m):
        prod, cons = [], []
        for d in range(MLA_HEADS):
            def k_chunk(c, d=d):
                return k_ref[c * KEY_CHUNK:(c + 1) * KEY_CHUNK, d * MLA_PAD:(d + 1) * MLA_PAD]

            def vt_chunk(c, d=d):
                return vt_ref[d * MLA_V:(d + 1) * MLA_V, c * KEY_CHUNK:(c + 1) * KEY_CHUNK]

            prod.append((q_ref[:, d * MLA_PAD:(d + 1) * MLA_PAD], k_chunk, None, s[sp][d], m[sp][d]))
            cons.append((vt_chunk, s[sc][d], m[sc][d]))
        outs = _pipe_step(prod, cons, TOK // KEY_CHUNK, MLA_V)
        o_ref[...] = jnp.concatenate(outs, axis=0).astype(BF16)

    _pipe_run(run, scratch, MLA_HEADS)


def _mla_attn(q, k, vt):
    return _full_key_attn(_mla_body, q, k, vt, w_out=MLA_V_WIDTH, n_dots=MLA_HEADS, m_len=TM, name="mla_attn")


def _na_start_block(t):
    return jnp.clip(t - 1, 0, NT_LAT - NA_KBLK)


def _na_body(q_ref, k0_ref, k1_ref, k2_ref, kc_ref, v0_ref, v1_ref, v2_ref, vc_ref, bias_ref, o_ref, *scratch):
    k_blocks = (k0_ref, k1_ref, k2_ref, kc_ref)
    v_blocks = (v0_ref, v1_ref, v2_ref, vc_ref)

    def run(sp, sc, s, m):
        prod, cons = [], []
        for a in range(NA_GROUP):
            for d in range(NA_HEADS):
                def k_chunk(c, a=a, d=d):
                    return k_blocks[c][a, :, d * HEAD_DIM:(d + 1) * HEAD_DIM]

                def bias_chunk(c, d=d):
                    return None if c == NA_KBLK else bias_ref[d, c * KEY_CHUNK:(c + 1) * KEY_CHUNK, :]

                def vt_chunk(c, a=a, d=d):
                    return v_blocks[c][a, d * HEAD_DIM:(d + 1) * HEAD_DIM, :]

                n = a * NA_HEADS + d
                prod.append((q_ref[a, :, d * HEAD_DIM:(d + 1) * HEAD_DIM], k_chunk, bias_chunk, s[sp][n], m[sp][n]))
                cons.append((vt_chunk, s[sc][n], m[sc][n]))
        outs = _pipe_step(prod, cons, NA_NKEY // KEY_CHUNK, HEAD_DIM)
        for a in range(NA_GROUP):
            o_ref[a] = jnp.concatenate(outs[a * NA_HEADS:(a + 1) * NA_HEADS], axis=0).astype(BF16)

    _pipe_run(run, scratch, NA_GROUP * NA_HEADS)


def _na_attn(q, k, vt, bias, layer):
    def prod(i):
        return _prod_chain(i, NA_GROUP)

    def cons(i):
        return _cons_chain(i, NA_GROUP)

    def k_spec(j):
        def index(i):
            b, t, _ = prod(i)
            return b, (NT_LAT if j == NA_KBLK else _na_start_block(t) + j), 0
        return pl.BlockSpec((NA_GROUP, TM, NA_WIDTH), index)

    def vt_spec(j):
        def index(i):
            b, t, _ = cons(i)
            return b, 0, (NT_LAT if j == NA_KBLK else _na_start_block(t) + j)
        return pl.BlockSpec((NA_GROUP, NA_WIDTH, TM), index)

    def bias_index(i):
        t = prod(i)[1]
        return layer, 0, jnp.where(t == 0, 0, jnp.where(t == NT_LAT - 1, 2, 1)), 0, 0

    return pl.pallas_call(
        _na_body,
        grid=(_n_chains(NA_GROUP) + 1,),
        in_specs=[pl.BlockSpec((NA_GROUP, TM, NA_WIDTH), prod)]
                 + [k_spec(j) for j in range(NA_KBLK + 1)] + [vt_spec(j) for j in range(NA_KBLK + 1)]
                 + [pl.BlockSpec((None, NA_HEADS, None, NA_NLOCAL, TM), bias_index)],
        out_specs=pl.BlockSpec((NA_GROUP, NA_WIDTH, TM), lambda i: (cons(i)[0], 0, cons(i)[1])),
        out_shape=jax.ShapeDtypeStruct((BATCH, NA_WIDTH, SEQ), BF16),
        scratch_shapes=_pipe_scratch(NA_NKEY, TM, NA_GROUP * NA_HEADS),
        compiler_params=_params(1),
        name="na_attn",
    )(q, *([k] * (NA_KBLK + 1)), *([vt] * (NA_KBLK + 1)), bias)


def _na_bias_table(rel_bias):
    lead = rel_bias.shape[:-2]
    rel_bias = rel_bias.reshape((-1,) + rel_bias.shape[-2:])
    nh = rel_bias.shape[0]
    cols = np.arange(GRID_W)
    c0 = np.clip(cols - NA_COLS // 2, 0, GRID_W - NA_COLS)
    col_in = (cols[None, :] >= c0[:, None]) & (cols[None, :] < c0[:, None] + NA_COLS)
    n_dr = 2 * NA_ROWS - 1
    pad = GRID_W - NA_COLS
    rbp = jnp.pad(rel_bias.astype(F32) * LOG2_E, ((0, 0), (0, 0), (pad, pad)), mode="edge")
    tiles = jnp.stack([rbp[:, :, GRID_W - 1 - cq:2 * GRID_W - 1 - cq] for cq in range(GRID_W)], axis=3)
    tiles = jnp.where(col_in.T[None, None], tiles, NEG_BIG)
    tiles = jnp.concatenate([tiles, jnp.full((nh, 1, GRID_W, GRID_W), NEG_BIG, F32)], axis=1)
    dr_blk = np.full((3, NA_QROWS, NA_KROWS), n_dr, np.int32)
    for cls, tile in enumerate((0, 2, NT_LAT - 1)):
        start = int(np.clip(tile - 1, 0, NT_LAT - NA_KBLK)) * NA_QROWS
        for qr in range(NA_QROWS):
            r = tile * NA_QROWS + qr
            r0 = int(np.clip(r - NA_ROWS // 2, 0, ROWS - NA_ROWS))
            for kj in range(NA_KROWS):
                kr = start + kj
                if r0 <= kr < r0 + NA_ROWS:
                    dr_blk[cls, qr, kj] = kr - r + NA_ROWS - 1
    local = jnp.take(tiles, jnp.asarray(dr_blk.reshape(-1)), axis=1)
    local = local.reshape(nh, 3, NA_QROWS, NA_KROWS, GRID_W, GRID_W).transpose(0, 1, 3, 4, 2, 5)
    return local.reshape(lead + (3, NA_NLOCAL, TM))


def _merge_body(h_ref, mod_ref, g_ref, wgate_ref, wa_ref, wb_ref, wm_ref, wo_ref, *rest, split):
    o_ref = rest[-1]
    for r in range(BGROUP):
        branch, pos = [], 0
        for has_ctx in split:
            if has_ctx:
                is_ctx = pl.program_id(0) >= NT_LAT
                branch.append(jnp.where(is_ctx, rest[pos + 1][r], rest[pos][r]))
            else:
                branch.append(rest[pos][r])
            pos += 2 if has_ctx else 1
        h = h_ref[r]
        shift, scale, gate = _mod_vectors(mod_ref, r)
        n = _rmsnorm(h, g_ref[...]) * (1.0 + scale) + shift
        gates =_sigmoid(_bdot(n.astype(BF16), wgate_ref[...]))
        y = (gates[:, :D_MODEL] * _dot_tn(branch[0], wa_ref[...])
             + gates[:, D_MODEL:2 * D_MODEL] * _dot_tn(branch[1], wb_ref[...])
             + gates[:, 2 * D_MODEL:] * _dot_tn(branch[2], wm_ref[...]))
        o_ref[r] = h + gate * _bdot(y.astype(BF16), wo_ref[...])


def _merge(h, mod, layer, g_norm, w_gate, branches, wa, wb, wm, wo, *, n_tiles):
    split = tuple(isinstance(br, tuple) for br in branches)
    arrays, specs = [], []
    for br in branches:
        if isinstance(br, tuple):
            lat, ctx = br
            arrays += [lat, ctx]
            specs += [pl.BlockSpec((BGROUP, lat.shape[1], TM), lambda t, b: (b, 0, jnp.minimum(t, NT_LAT - 1))),
                      pl.BlockSpec((BGROUP, ctx.shape[1], TM), lambda t, b: (b, 0, 0))]
        else:
            arrays.append(br)
            specs.append(pl.BlockSpec((BGROUP, br.shape[1], TM), lambda t, b: (b, 0, t)))
    return pl.pallas_call(
        functools.partial(_merge_body, split=split),
        grid=(n_tiles, BATCH // BGROUP),
        in_specs=[_tok_spec(D_MODEL), _mod_spec(layer, 1)]
                 + [_layer_spec(a, layer) for a in (g_norm, w_gate, wa, wb, wm, wo)] + specs,
        out_specs=_tok_spec(D_MODEL),
        out_shape=jax.ShapeDtypeStruct((BATCH, n_tiles * TM, D_MODEL), F32),
        compiler_params=_params(2),
        name="gated_merge",
    )(h, mod, g_norm, w_gate, wa, wb, wm, wo, *arrays)


def _rope_tables():
    t = np.arange(SEQ)
    row = (t // GRID_W).astype(np.float32)[:, None]
    col = (t % GRID_W).astype(np.float32)[:, None]

    def head_tables(d_rot):
        half = d_rot // 2
        freqs = np.float32(ROPE_THETA) ** (-np.arange(0, half, 2, dtype=np.float32) / np.float32(half))
        r, c = row * freqs, col * freqs
        ang = np.concatenate([r, r, c, c], axis=-1).astype(np.float32)
        first = (np.arange(d_rot) % half) < (half // 2)
        cos, sin = np.cos(ang), np.sin(ang)
        return cos, np.where(first, -sin, sin)

    def place(tabs, reps, lo, group):
        out = []
        for k, tab in enumerate(tabs):
            fill = 1.0 if k == 0 else 0.0
            d_rot = tab.shape[-1]
            g = np.concatenate([np.full((SEQ, lo), fill, np.float32), tab,
                                np.full((SEQ, group - lo - d_rot), fill, np.float32)], axis=-1)
            g = np.tile(g, (1, reps))
            ctx = np.full((CTX_LEN, reps * group), fill, np.float32)
            out.append(jnp.asarray(np.concatenate([g, ctx], axis=0), F32))
        return out

    tb = head_tables(HEAD_DIM)
    tm = head_tables(MLA_ROPE)
    return (place(tb, GQA_HEADS, 0, HEAD_DIM)
            + place(tm, MLA_HEADS, MLA_NOPE, MLA_PAD)
            + place(tm, 1, 0, LANES))


def _seg_matrix(width):
    idx = np.arange(width) // HEAD_DIM
    return jnp.asarray((idx[:, None] == idx[None, :]).astype(np.float32) / HEAD_DIM, BF16)


def _place_matrix():
    m = np.zeros((LANES, MLA_QK_WIDTH), np.float32)
    for hd in range(MLA_HEADS):
        for j in range(MLA_ROPE):
            m[j, hd * MLA_PAD + MLA_NOPE + j] = 1.0
    return jnp.asarray(m, BF16)


def _pad_heads(w, n_heads, per_head, keep_lo, keep_hi):
    d, k = w.shape[:2]
    wh = w.reshape(d, k, n_heads, per_head)[..., keep_lo:keep_hi]
    wh = jnp.pad(wh, ((0, 0), (0, 0), (0, 0), (0, MLA_PAD - (keep_hi - keep_lo))))
    return wh.reshape(d, k, n_heads * MLA_PAD)


def kernel(x, c, ctx, c_ctx, w_ada, b_ada, ffn1_norm, ffn1_w_gate, ffn1_w_up, ffn1_w_down, mix_norm, w_in,
           na_rel_bias, gqa_q_norm, gqa_k_norm, mla_q_norm, mla_kv_norm, mla_w_uq, mla_w_ukv,
           w_branch_a, w_branch_b, w_branch_c, w_out, ffn2_norm, ffn2_w_gate, ffn2_w_up, ffn2_w_down,
           final_norm):
    assert x.shape == (BATCH, SEQ, D_MODEL) and ctx.shape == (BATCH, CTX_LEN, D_MODEL)

    c_all = jnp.concatenate([c, jnp.tile(c_ctx[None, :], (BGROUP, 1)),
                             jnp.zeros((MOD_ROWS - BATCH - BGROUP, D_MODEL), F32)], axis=0)
    mod_all = _ada(c_all, w_ada, b_ada)

    tables = _rope_tables()
    seg_q, seg_k, place = _seg_matrix(2 * LANES), _seg_matrix(GQA_KV_WIDTH), _place_matrix()
    fin = final_norm.reshape(1, D_MODEL)

    def row(v):
        return v[:, None, :]

    w_qkv = w_in[:, :, :QKV_WIDTH].astype(BF16)
    w_gate = w_in[:, :, N_QKV_COLS:].astype(BF16)
    w_uq = _pad_heads(mla_w_uq, MLA_HEADS, MLA_NOPE + MLA_ROPE, 0, MLA_NOPE + MLA_ROPE).astype(BF16)
    w_uk = _pad_heads(mla_w_ukv, MLA_HEADS, MLA_NOPE + MLA_V, 0, MLA_NOPE)
    w_uv = mla_w_ukv.reshape(DEPTH, MLA_KV_RANK, MLA_HEADS, MLA_NOPE + MLA_V)[..., MLA_NOPE:]
    w_ukv = jnp.concatenate([w_uk, w_uv.reshape(DEPTH, MLA_KV_RANK, MLA_V_WIDTH)], axis=-1).astype(BF16)
    consts = (seg_q, seg_k, row(jnp.tile(gqa_q_norm, (1, GQA_HEADS))), row(jnp.tile(gqa_k_norm, (1, GQA_KV_HEADS))),
              row(mla_q_norm), row(mla_kv_norm), w_uq, w_ukv, place)
    ffn1 = (row(ffn1_norm), ffn1_w_gate.astype(BF16), ffn1_w_up.astype(BF16), ffn1_w_down.astype(BF16))
    ffn2 = (row(ffn2_norm), ffn2_w_gate.astype(BF16), ffn2_w_up.astype(BF16), ffn2_w_down.astype(BF16))
    branch_w = (w_branch_a.astype(BF16), w_branch_b.astype(BF16), w_branch_c.astype(BF16), w_out.astype(BF16))
    na_bias = _na_bias_table(na_rel_bias)
    g_mix = row(mix_norm)

    h = (x, ctx)
    for i in range(DEPTH):
        last = i == DEPTH - 1
        h = _ffn(h, mod_all, i, 0, *ffn1, fin, n_tiles=NT_ALL, final=False)
        qa, ka, va, qb, kb, vb, qm, km, vm = _proj(h, mod_all, i, g_mix, w_qkv, consts, tables)

        oa = _na_attn(qa, ka, va, na_bias, i)
        ob = _gqa_attn(qb, kb, vb)
        om = _mla_attn(qm, km, vm)
        if not last:
            oa_c, ob_c, om_c = _ctx_attn(((qa, ka, va), (qb, kb, vb), (qm, km, vm)))
            oa, ob, om = (oa, oa_c), (ob, ob_c), (om, om_c)

        n_tiles = NT_LAT if last else NT_ALL
        h = _merge(h, mod_all, i, g_mix, w_gate, (oa, ob, om), *branch_w, n_tiles=n_tiles)
        h = _ffn(h, mod_all, i, 2, *ffn2, fin, n_tiles=n_tiles, final=last)
    return h
```

```python
import functools

import numpy as np
import jax
import jax.numpy as jnp
from jax import lax
from jax.experimental import pallas as pl
from jax.experimental.pallas import tpu as pltpu

D_MODEL = 1024
BATCH = 32
SEQ = 2048
DEPTH = 2
CTX_LEN = 256
GRID_W = 64
HEAD_DIM = 64
NA_HEADS = 4
NA_ROWS = 8
NA_COLS = 16
GQA_HEADS = 8
GQA_KV_HEADS = 2
MLA_HEADS = 4
MLA_Q_RANK = 256
MLA_KV_RANK = 128
MLA_NOPE = 64
MLA_ROPE = 32
MLA_V = 64
N_MOD = 9
ROPE_THETA = 10000.0
EPS = 1e-6
NEG_BIG = -1e30

NA_WIDTH = NA_HEADS * HEAD_DIM
GQA_Q_WIDTH = GQA_HEADS * HEAD_DIM
GQA_KV_WIDTH = GQA_KV_HEADS * HEAD_DIM
GQA_GRP = GQA_HEADS // GQA_KV_HEADS
MLA_V_WIDTH = MLA_HEADS * MLA_V
ROWS = SEQ // GRID_W

LANES = 128
SUBLANES = 8
BF16_SUBLANES = 16
LOG2_E = 1.4426950408889634
TOK = SEQ + CTX_LEN
TM = 256
NT_ALL = TOK // TM
NT_LAT = SEQ // TM
BGROUP = 4
MOD_ROWS = 40
CTX_ROW = BATCH
ADA_BLOCK = 1152
MLA_PAD = LANES
MLA_QK_WIDTH = MLA_HEADS * MLA_PAD
N_QKV_COLS = 1952
QKV_WIDTH = 2048
NA_QROWS = TM // GRID_W
NA_KROWS = 12
NA_KBLK = NA_KROWS * GRID_W // TM
NA_NLOCAL = NA_KROWS * GRID_W
NA_NKEY = NA_NLOCAL + CTX_LEN
KEY_CHUNK = 256
NA_GROUP = 4
MLA_TILES = 2
VMEM_LIMIT = 56 * 1024 * 1024

F32 = jnp.float32
BF16 = jnp.bfloat16

_OFF_AQ, _OFF_AK, _OFF_AV = 0, 256, 512
_OFF_BQ, _OFF_BK, _OFF_BV = 768, 1280, 1408
_OFF_CQ, _OFF_CKV, _OFF_CKR = 1536, 1792, 1920


def _params(n_axes):
    return pltpu.CompilerParams(dimension_semantics=("arbitrary",) * n_axes,
                                vmem_limit_bytes=VMEM_LIMIT)


def _const_spec(shape):
    nd = len(shape)
    return pl.BlockSpec(shape, lambda *_: (0,) * nd, pipeline_mode=pl.Buffered(1))


def _layer_spec(arr, layer):
    nd = arr.ndim - 1
    return pl.BlockSpec((None,) + arr.shape[1:], lambda *_: (layer,) + (0,) * nd, pipeline_mode=pl.Buffered(1))


def _mod_spec(layer, sub):
    return pl.BlockSpec((None, SUBLANES, 3 * D_MODEL),
                        lambda t, b: (layer, jnp.where(t >= NT_LAT, CTX_ROW, b * BGROUP) // SUBLANES, sub))


def _mod_vectors(mod_ref, r, t=None, b=None):
    t = pl.program_id(0) if t is None else t
    b = pl.program_id(1) if b is None else b
    first = jnp.where(t >= NT_LAT, CTX_ROW, b * BGROUP) % SUBLANES
    row = mod_ref[pl.ds(first + r, 1), :]
    return row[:, :D_MODEL], row[:, D_MODEL:2 * D_MODEL], row[:, 2 * D_MODEL:]


def _tok_spec(width):
    return pl.BlockSpec((BGROUP, TM, width), lambda t, b: (b, t, 0))


def _sigmoid(x):
    return 1.0 / (1.0 + jnp.exp(-x))


def _rmsnorm(x, g):
    return (x * lax.rsqrt(jnp.mean(x * x, axis=-1, keepdims=True) + EPS)) * g


def _bdot(a, b):
    return jnp.dot(a, b, preferred_element_type=F32)


def _dot_tn(a, b):
    return lax.dot_general(a, b, (((0,), (0,)), ((), ())), preferred_element_type=F32)


def _dot_nt(a, b):
    return lax.dot_general(a, b, (((1,), (1,)), ((), ())), preferred_element_type=F32)


def _ada_body(c_ref, w_ref, b_ref, o_ref):
    c = c_ref[...]
    s = c * _sigmoid(c)
    o_ref[...] = _bdot(s.astype(BF16), w_ref[...].astype(BF16)) + b_ref[...]


def _ada(c_all, w_ada, b_ada):
    nblk = (N_MOD * D_MODEL) // ADA_BLOCK
    return pl.pallas_call(
        _ada_body,
        grid=(DEPTH, nblk),
        in_specs=[
            pl.BlockSpec((MOD_ROWS, D_MODEL), lambda i, j: (0, 0)),
            pl.BlockSpec((None, D_MODEL, ADA_BLOCK), lambda i, j: (i, 0, j)),
            pl.BlockSpec((None, 1, ADA_BLOCK), lambda i, j: (i, 0, j)),
        ],
        out_specs=pl.BlockSpec((None, MOD_ROWS, ADA_BLOCK), lambda i, j: (i, 0, j)),
        out_shape=jax.ShapeDtypeStruct((DEPTH, MOD_ROWS, N_MOD * D_MODEL), F32),
        compiler_params=_params(2),
        name="ada_mod",
    )(c_all, w_ada, b_ada.reshape(DEPTH, 1, N_MOD * D_MODEL))


def _ffn_body(*refs, final, split):
    if split:
        h_ref, hc_ref, mod_ref, g_ref, wg_ref, wu_ref, wd_ref, fin_ref, o_ref = refs
    else:
        h_ref, mod_ref, g_ref, wg_ref, wu_ref, wd_ref, fin_ref, o_ref = refs
    for r in range(BGROUP):
        h = h_ref[r]
        if split:
            h = jnp.where(pl.program_id(0) >= NT_LAT, hc_ref[r], h)
        shift, scale, gate = _mod_vectors(mod_ref, r)
        n = _rmsnorm(h, g_ref[...]) * (1.0 + scale) + shift
        nb = n.astype(BF16)
        g = _bdot(nb, wg_ref[...])
        u = _bdot(nb, wu_ref[...])
        a = (g * _sigmoid(g)) * u
        d = _bdot(a.astype(BF16), wd_ref[...])
        out = h + 0.5 * gate * d
        if final:
            out = _rmsnorm(out, fin_ref[...])
        o_ref[r] = out


def _ffn(h, mod, layer, sub, g_norm, wg, wu, wd, fin, *, n_tiles, final):
    split = isinstance(h, tuple)
    if split:
        h_arrays = list(h)
        h_specs = [pl.BlockSpec((BGROUP, TM, D_MODEL), lambda t, b: (b, jnp.minimum(t, NT_LAT - 1), 0)),
                   pl.BlockSpec((BGROUP, TM, D_MODEL), lambda t, b: (b, 0, 0))]
    else:
        h_arrays, h_specs = [h], [_tok_spec(D_MODEL)]
    return pl.pallas_call(
        functools.partial(_ffn_body, final=final, split=split),
        grid=(n_tiles, BATCH // BGROUP),
        in_specs=h_specs + [_mod_spec(layer, sub)] + [_layer_spec(a, layer) for a in (g_norm, wg, wu, wd)]
                 + [_const_spec((1, D_MODEL))],
        out_specs=_tok_spec(D_MODEL),
        out_shape=jax.ShapeDtypeStruct((BATCH, n_tiles * TM, D_MODEL), F32),
        compiler_params=_params(2),
        name="half_ffn",
    )(*h_arrays, mod, g_norm, wg, wu, wd, fin)


def _rope(x, cos, sin, half):
    w = x.shape[-1]
    lane = lax.broadcasted_iota(jnp.int32, (1, w), 1)
    rot = jnp.where((lane & (2 * half - 1)) < half, pltpu.roll(x, w - half, 1), pltpu.roll(x, half, 1))
    return x * cos + rot * sin


def _seg_mean_sq(x, seg):
    w = seg.shape[0]
    parts = []
    for lo_col in range(0, x.shape[-1], w):
        x2 = x[:, lo_col:lo_col + w]
        x2 = x2 * x2
        hi = x2.astype(BF16)
        lo = (x2 - hi.astype(F32)).astype(BF16)
        parts.append(_bdot(hi, seg) + _bdot(lo, seg))
    return parts[0] if len(parts) == 1 else jnp.concatenate(parts, axis=1)


_PROJ_GROUPS = NT_ALL * (BATCH // BGROUP)


def _proj_prod(i):
    c = jnp.minimum(i, _PROJ_GROUPS - 1)
    return c // (BATCH // BGROUP), c % (BATCH // BGROUP)


def _proj_cons(i):
    c = jnp.maximum(i - 1, 0)
    return c // (BATCH // BGROUP), c % (BATCH // BGROUP)


def _proj_body(h_ref, mod_ref, g_ref, w_ref, segq_ref, segk_ref, qn_ref, kn_ref, mqn_ref, mkvn_ref,
               wuq_ref, wukv_ref, place_ref,
               cb_ref, sb_ref, cm_ref, sm_ref, ck_ref, sk_ref,
               qa_ref, ka_ref, va_ref, qb_ref, kb_ref, vb_ref, qm_ref, km_ref, vm_ref, z0_scr, z1_scr):
    i = pl.program_id(0)

    @pl.when(i == 0)
    def _():
        z1_scr[...] = jnp.zeros(z1_scr.shape, F32)

    def step(z_prod, z_cons):
        t, b = _proj_prod(i)
        for r in range(BGROUP):
            shift, mscale, _ = _mod_vectors(mod_ref, r, t, b)
            n = _rmsnorm(h_ref[r], g_ref[...]) * (1.0 + mscale) + shift
            z_prod[r] = _bdot(n.astype(BF16), w_ref[...])
        _proj_mixers(z_cons, segq_ref, segk_ref, qn_ref, kn_ref, mqn_ref, mkvn_ref, wuq_ref, wukv_ref, place_ref,
                     cb_ref, sb_ref, cm_ref, sm_ref, ck_ref, sk_ref,
                     qa_ref, ka_ref, va_ref, qb_ref, kb_ref, vb_ref, qm_ref, km_ref, vm_ref)

    pl.when(i % 2 == 0)(functools.partial(step, z0_scr, z1_scr))
    pl.when(i % 2 == 1)(functools.partial(step, z1_scr, z0_scr))


def _proj_mixers(z_ref, segq_ref, segk_ref, qn_ref, kn_ref, mqn_ref, mkvn_ref, wuq_ref, wukv_ref, place_ref,
                 cb_ref, sb_ref, cm_ref, sm_ref, ck_ref, sk_ref,
                 qa_ref, ka_ref, va_ref, qb_ref, kb_ref, vb_ref, qm_ref, km_ref, vm_ref):
    scale = LOG2_E * HEAD_DIM ** -0.5
    mla_scale = LOG2_E * (MLA_NOPE + MLA_ROPE) ** -0.5
    cb, sb = cb_ref[...], sb_ref[...]
    for r in range(BGROUP):
        z = z_ref[r]
        qa_ref[r] = (z[:, _OFF_AQ:_OFF_AQ + NA_WIDTH] * scale).astype(BF16)
        ka_ref[r] = z[:, _OFF_AK:_OFF_AK + NA_WIDTH].astype(BF16)
        va_ref[r] = z[:, _OFF_AV:_OFF_AV + NA_WIDTH].T.astype(BF16)

        bq = z[:, _OFF_BQ:_OFF_BQ + GQA_Q_WIDTH]
        bq = (bq * lax.rsqrt(_seg_mean_sq(bq, segq_ref[...]) + EPS)) * qn_ref[...]
        qb_ref[r] = (_rope(bq, cb, sb, HEAD_DIM // 4) * scale).astype(BF16)
        bk = z[:, _OFF_BK:_OFF_BK + GQA_KV_WIDTH]
        bk = (bk * lax.rsqrt(_seg_mean_sq(bk, segk_ref[...]) + EPS)) * kn_ref[...]
        kb_ref[r] = _rope(bk, cb[:, :GQA_KV_WIDTH], sb[:, :GQA_KV_WIDTH], HEAD_DIM // 4).astype(BF16)
        vb_ref[r] = z[:, _OFF_BV:_OFF_BV + GQA_KV_WIDTH].T.astype(BF16)

        cq = _rmsnorm(z[:, _OFF_CQ:_OFF_CQ + MLA_Q_RANK], mqn_ref[...])
        q_lat = _bdot(cq.astype(BF16), wuq_ref[...])
        qm_ref[r] = (_rope(q_lat, cm_ref[...], sm_ref[...], MLA_ROPE // 4) * mla_scale).astype(BF16)
        ckv = _rmsnorm(z[:, _OFF_CKV:_OFF_CKV + MLA_KV_RANK], mkvn_ref[...])
        kv_lat = _bdot(ckv.astype(BF16), wukv_ref[...])
        vm_ref[r] = kv_lat[:, MLA_QK_WIDTH:].T.astype(BF16)
        lane = lax.broadcasted_iota(jnp.int32, (1, LANES), 1)
        kr = jnp.where(lane < MLA_ROPE, z[:, _OFF_CKR:_OFF_CKR + LANES], 0.0)
        kr = _rope(kr, ck_ref[...], sk_ref[...], MLA_ROPE // 4).astype(BF16)
        km_ref[r] = (kv_lat[:, :MLA_QK_WIDTH] + _bdot(kr, place_ref[...])).astype(BF16)


def _proj(h, mod, layer, g_norm, w_qkv, consts, tables):
    widths = (NA_WIDTH, NA_WIDTH, NA_WIDTH, GQA_Q_WIDTH, GQA_KV_WIDTH, GQA_KV_WIDTH,
              MLA_QK_WIDTH, MLA_QK_WIDTH, MLA_V_WIDTH)
    is_value = [k % 3 == 2 for k in range(len(widths))]
    tab_specs = [pl.BlockSpec((TM, t.shape[-1]), lambda i: (_proj_cons(i)[0], 0)) for t in tables]

    def mod_index(i):
        t, b = _proj_prod(i)
        return layer, jnp.where(t >= NT_LAT, CTX_ROW, b * BGROUP) // SUBLANES, 1

    def out_spec(is_val, w):
        if is_val:
            return pl.BlockSpec((BGROUP, w, TM), lambda i: (_proj_cons(i)[1], 0, _proj_cons(i)[0]))
        return pl.BlockSpec((BGROUP, TM, w), lambda i: (_proj_cons(i)[1], _proj_cons(i)[0], 0))

    return pl.pallas_call(
        _proj_body,
        grid=(_PROJ_GROUPS + 1,),
        in_specs=[pl.BlockSpec((BGROUP, TM, D_MODEL), lambda i: (_proj_prod(i)[1], _proj_prod(i)[0], 0)),
                  pl.BlockSpec((None, SUBLANES, 3 * D_MODEL), mod_index),
                  _layer_spec(g_norm, layer), _layer_spec(w_qkv, layer)]
                 + [_layer_spec(c, layer) if c.ndim == 3 else _const_spec(c.shape) for c in consts] + tab_specs,
        out_specs=[out_spec(v, w) for v, w in zip(is_value, widths)],
        out_shape=[jax.ShapeDtypeStruct((BATCH, w, TOK) if v else (BATCH, TOK, w), BF16)
                   for v, w in zip(is_value, widths)],
        scratch_shapes=[pltpu.VMEM((BGROUP, TM, QKV_WIDTH), F32)] * 2,
        compiler_params=_params(1),
        name="mix_proj",
    )(h, mod, g_norm, w_qkv, *consts, *tables)


_CTX_MIXERS = ((NA_HEADS, 1, HEAD_DIM, HEAD_DIM), (GQA_HEADS, GQA_GRP, HEAD_DIM, HEAD_DIM),
               (MLA_HEADS, 1, MLA_PAD, MLA_V))


def _ctx_attn_body(*refs):
    ones = jnp.ones((BF16_SUBLANES, TM), BF16)
    n_mix = len(_CTX_MIXERS)
    for r in range(BGROUP):
        for mix, (heads, grp, dqk, dv) in enumerate(_CTX_MIXERS):
            q_ref, k_ref, vt_ref = refs[3 * mix:3 * mix + 3]
            outs = []
            for hd in range(heads):
                g = hd // grp
                st = _dot_nt(k_ref[r, :, g * dqk:(g + 1) * dqk], q_ref[r, :, hd * dqk:(hd + 1) * dqk])
                p = jnp.exp2(st - jnp.max(st, axis=0, keepdims=True)).astype(BF16)
                acc = _bdot(jnp.concatenate([vt_ref[r, g * dv:(g + 1) * dv, :], ones], axis=0), p)
                outs.append(acc[:dv] * (1.0 / acc[dv:dv + 1]))
            refs[3 * n_mix + mix][r] = jnp.concatenate(outs, axis=0).astype(BF16)


def _ctx_attn(qkv):
    in_specs, arrays = [], []
    for q, k, vt in qkv:
        arrays += [q, k, vt]
        in_specs += [pl.BlockSpec((BGROUP, TM, q.shape[-1]), lambda b: (b, NT_LAT, 0)),
                     pl.BlockSpec((BGROUP, TM, k.shape[-1]), lambda b: (b, NT_LAT, 0)),
                     pl.BlockSpec((BGROUP, vt.shape[1], TM), lambda b: (b, 0, NT_LAT))]
    widths = [heads * dv for heads, _, _, dv in _CTX_MIXERS]
    return pl.pallas_call(
        _ctx_attn_body,
        grid=(BATCH // BGROUP,),
        in_specs=in_specs,
        out_specs=[pl.BlockSpec((BGROUP, w, TM), lambda b: (b, 0, 0)) for w in widths],
        out_shape=[jax.ShapeDtypeStruct((BATCH, w, TM), BF16) for w in widths],
        compiler_params=_params(1),
        name="ctx_attn",
    )(*arrays)


def _pipe_step(prod, cons, n_chunk, dv):
    ones = jnp.ones((BF16_SUBLANES, KEY_CHUNK), BF16)
    m_cons = [m_ref[0:1, :] for _, _, m_ref in cons]
    m_new = [None] * len(prod)
    acc = [None] * len(cons)
    for c in range(n_chunk):
        rows = slice(c * KEY_CHUNK, (c + 1) * KEY_CHUNK)
        for d, (qg, k_chunk, bias_chunk, s_ref, _) in enumerate(prod):
            st = _dot_nt(k_chunk(c), qg)
            bias = None if bias_chunk is None else bias_chunk(c)
            if bias is not None:
                st = st + bias
            s_ref[rows, :] = st
            mc = jnp.max(st, axis=0, keepdims=True)
            m_new[d] = mc if m_new[d] is None else jnp.maximum(m_new[d], mc)
        for d, (vt_chunk, s_ref, _) in enumerate(cons):
            p = jnp.exp2(s_ref[rows, :] - m_cons[d]).astype(BF16)
            part = _bdot(jnp.concatenate([vt_chunk(c), ones], axis=0), p)
            acc[d] = part if acc[d] is None else acc[d] + part
    for d, (_, _, _, _, m_ref) in enumerate(prod):
        m_ref[0:1, :] = m_new[d]
    return [a[:dv] * (1.0 / a[dv:dv + 1]) for a in acc]


def _pipe_run(step, scratch, n_dots):
    i = pl.program_id(0)
    s = [scratch[slot * n_dots:(slot + 1) * n_dots] for slot in range(2)]
    m = [scratch[(2 + slot) * n_dots:(3 + slot) * n_dots] for slot in range(2)]

    @pl.when(i == 0)
    def _():
        for ref in s[1] + m[1]:
            ref[...] = jnp.zeros(ref.shape, F32)

    pl.when(i % 2 == 0)(functools.partial(step, 0, 1, s, m))
    pl.when(i % 2 == 1)(functools.partial(step, 1, 0, s, m))


def _n_chains(ag):
    return (BATCH // ag) * NT_LAT


def _prod_chain(i, ag=1):
    c = jnp.minimum(i, _n_chains(ag) - 1)
    return c // NT_LAT, c % NT_LAT, 0


def _cons_chain(i, ag=1):
    c = jnp.maximum(i - 1, 0)
    return c // NT_LAT, c % NT_LAT, 0


def _pipe_scratch(nk, m_len, n_dots):
    return ([pltpu.VMEM((nk, m_len), F32)] * (2 * n_dots)
            + [pltpu.VMEM((8, m_len), F32)] * (2 * n_dots))


def _gqa_body(q_ref, k_ref, vt_ref, o_ref, *scratch):
    def run(sp, sc, s, m):
        q = q_ref[...]
        prod, cons = [], []
        for g in range(GQA_KV_HEADS):
            qg = jnp.concatenate([q[:, (g * GQA_GRP + j) * HEAD_DIM:(g * GQA_GRP + j + 1) * HEAD_DIM]
                                  for j in range(GQA_GRP)], axis=0)

            def k_chunk(c, g=g):
                return k_ref[c * KEY_CHUNK:(c + 1) * KEY_CHUNK, g * HEAD_DIM:(g + 1) * HEAD_DIM]

            def vt_chunk(c, g=g):
                return vt_ref[g * HEAD_DIM:(g + 1) * HEAD_DIM, c * KEY_CHUNK:(c + 1) * KEY_CHUNK]

            prod.append((qg, k_chunk, None, s[sp][g], m[sp][g]))
            cons.append((vt_chunk, s[sc][g], m[sc][g]))
        outs = _pipe_step(prod, cons, TOK // KEY_CHUNK, HEAD_DIM)
        o = jnp.concatenate([ot[:, j * TM:(j + 1) * TM] for ot in outs for j in range(GQA_GRP)], axis=0)
        o_ref[...] = o.astype(BF16)

    _pipe_run(run, scratch, GQA_KV_HEADS)


def _full_key_attn(body, q, k, vt, *, tq, w_out, n_dots, m_len, name):
    nt = SEQ // tq

    def prod(i):
        c = jnp.minimum(i, BATCH * nt - 1)
        return c // nt, c % nt

    def cons(i):
        c = jnp.maximum(i - 1, 0)
        return c // nt, c % nt

    return pl.pallas_call(
        body,
        grid=(BATCH * nt + 1,),
        in_specs=[
            pl.BlockSpec((None, tq, q.shape[-1]), lambda i: prod(i) + (0,)),
            pl.BlockSpec((None, TOK, k.shape[-1]), lambda i: (prod(i)[0], 0, 0)),
            pl.BlockSpec((None, vt.shape[1], TOK), lambda i: (cons(i)[0], 0, 0)),
        ],
        out_specs=pl.BlockSpec((None, w_out, tq), lambda i: (cons(i)[0], 0, cons(i)[1])),
        out_shape=jax.ShapeDtypeStruct((BATCH, w_out, SEQ), BF16),
        scratch_shapes=_pipe_scratch(TOK, m_len, n_dots),
        compiler_params=_params(1),
        name=name,
    )(q, k, vt)


def _gqa_attn(q, k, vt):
    return _full_key_attn(_gqa_body, q, k, vt, tq=TM, w_out=GQA_Q_WIDTH, n_dots=GQA_KV_HEADS,
                          m_len=GQA_GRP * TM, name="gqa_attn")


def _mla_body(q_ref, k_ref, vt_ref, o_ref, *scratch):
    def run(sp, sc, s, m):
        prod, cons = [], []
        for a in range(MLA_TILES):
            for d in range(MLA_HEADS):
                def k_chunk(c, d=d):
                    return k_ref[c * KEY_CHUNK:(c + 1) * KEY_CHUNK, d * MLA_PAD:(d + 1) * MLA_PAD]

                def vt_chunk(c, d=d):
                    return vt_ref[d * MLA_V:(d + 1) * MLA_V, c * KEY_CHUNK:(c + 1) * KEY_CHUNK]

                n = a * MLA_HEADS + d
                prod.append((q_ref[a * TM:(a + 1) * TM, d * MLA_PAD:(d + 1) * MLA_PAD], k_chunk, None,
                             s[sp][n], m[sp][n]))
                cons.append((vt_chunk, s[sc][n], m[sc][n]))
        outs = _pipe_step(prod, cons, TOK // KEY_CHUNK, MLA_V)
        for a in range(MLA_TILES):
            o_ref[:, a * TM:(a + 1) * TM] = jnp.concatenate(outs[a * MLA_HEADS:(a + 1) * MLA_HEADS],
                                                            axis=0).astype(BF16)

    _pipe_run(run, scratch, MLA_TILES * MLA_HEADS)


def _mla_attn(q, k, vt):
    return _full_key_attn(_mla_body, q, k, vt, tq=MLA_TILES * TM, w_out=MLA_V_WIDTH,
                          n_dots=MLA_TILES * MLA_HEADS, m_len=TM, name="mla_attn")


def _na_start_block(t):
    return jnp.clip(t - 1, 0, NT_LAT - NA_KBLK)


def _na_body(q_ref, k0_ref, k1_ref, k2_ref, kc_ref, v0_ref, v1_ref, v2_ref, vc_ref, bias_ref, o_ref, *scratch):
    k_blocks = (k0_ref, k1_ref, k2_ref, kc_ref)
    v_blocks = (v0_ref, v1_ref, v2_ref, vc_ref)

    def run(sp, sc, s, m):
        prod, cons = [], []
        for a in range(NA_GROUP):
            for d in range(NA_HEADS):
                def k_chunk(c, a=a, d=d):
                    return k_blocks[c][a, :, d * HEAD_DIM:(d + 1) * HEAD_DIM]

                def bias_chunk(c, d=d):
                    return None if c == NA_KBLK else bias_ref[d, c * KEY_CHUNK:(c + 1) * KEY_CHUNK, :]

                def vt_chunk(c, a=a, d=d):
                    return v_blocks[c][a, d * HEAD_DIM:(d + 1) * HEAD_DIM, :]

                n = a * NA_HEADS + d
                prod.append((q_ref[a, :, d * HEAD_DIM:(d + 1) * HEAD_DIM], k_chunk, bias_chunk, s[sp][n], m[sp][n]))
                cons.append((vt_chunk, s[sc][n], m[sc][n]))
        outs = _pipe_step(prod, cons, NA_NKEY // KEY_CHUNK, HEAD_DIM)
        for a in range(NA_GROUP):
            o_ref[a] = jnp.concatenate(outs[a * NA_HEADS:(a + 1) * NA_HEADS], axis=0).astype(BF16)

    _pipe_run(run, scratch, NA_GROUP * NA_HEADS)


def _na_attn(q, k, vt, bias, layer):
    def prod(i):
        return _prod_chain(i, NA_GROUP)

    def cons(i):
        return _cons_chain(i, NA_GROUP)

    def k_spec(j):
        def index(i):
            b, t, _ = prod(i)
            return b, (NT_LAT if j == NA_KBLK else _na_start_block(t) + j), 0
        return pl.BlockSpec((NA_GROUP, TM, NA_WIDTH), index)

    def vt_spec(j):
        def index(i):
            b, t, _ = cons(i)
            return b, 0, (NT_LAT if j == NA_KBLK else _na_start_block(t) + j)
        return pl.BlockSpec((NA_GROUP, NA_WIDTH, TM), index)

    def bias_index(i):
        t = prod(i)[1]
        return layer, 0, jnp.where(t == 0, 0, jnp.where(t == NT_LAT - 1, 2, 1)), 0, 0

    return pl.pallas_call(
        _na_body,
        grid=(_n_chains(NA_GROUP) + 1,),
        in_specs=[pl.BlockSpec((NA_GROUP, TM, NA_WIDTH), prod)]
                 + [k_spec(j) for j in range(NA_KBLK + 1)] + [vt_spec(j) for j in range(NA_KBLK + 1)]
                 + [pl.BlockSpec((None, NA_HEADS, None, NA_NLOCAL, TM), bias_index)],
        out_specs=pl.BlockSpec((NA_GROUP, NA_WIDTH, TM), lambda i: (cons(i)[0], 0, cons(i)[1])),
        out_shape=jax.ShapeDtypeStruct((BATCH, NA_WIDTH, SEQ), BF16),
        scratch_shapes=_pipe_scratch(NA_NKEY, TM, NA_GROUP * NA_HEADS),
        compiler_params=_params(1),
        name="na_attn",
    )(q, *([k] * (NA_KBLK + 1)), *([vt] * (NA_KBLK + 1)), bias)


def _na_bias_table(rel_bias):
    lead = rel_bias.shape[:-2]
    rel_bias = rel_bias.reshape((-1,) + rel_bias.shape[-2:])
    nh = rel_bias.shape[0]
    cols = np.arange(GRID_W)
    c0 = np.clip(cols - NA_COLS // 2, 0, GRID_W - NA_COLS)
    col_in = (cols[None, :] >= c0[:, None]) & (cols[None, :] < c0[:, None] + NA_COLS)
    n_dr = 2 * NA_ROWS - 1
    pad = GRID_W - NA_COLS
    rbp = jnp.pad(rel_bias.astype(F32) * LOG2_E, ((0, 0), (0, 0), (pad, pad)), mode="edge")
    tiles = jnp.stack([rbp[:, :, GRID_W - 1 - cq:2 * GRID_W - 1 - cq] for cq in range(GRID_W)], axis=3)
    tiles = jnp.where(col_in.T[None, None], tiles, NEG_BIG)
    tiles = jnp.concatenate([tiles, jnp.full((nh, 1, GRID_W, GRID_W), NEG_BIG, F32)], axis=1)
    dr_blk = np.full((3, NA_QROWS, NA_KROWS), n_dr, np.int32)
    for cls, tile in enumerate((0, 2, NT_LAT - 1)):
        start = int(np.clip(tile - 1, 0, NT_LAT - NA_KBLK)) * NA_QROWS
        for qr in range(NA_QROWS):
            r = tile * NA_QROWS + qr
            r0 = int(np.clip(r - NA_ROWS // 2, 0, ROWS - NA_ROWS))
            for kj in range(NA_KROWS):
                kr = start + kj
                if r0 <= kr < r0 + NA_ROWS:
                    dr_blk[cls, qr, kj] = kr - r + NA_ROWS - 1
    local = jnp.take(tiles, jnp.asarray(dr_blk.reshape(-1)), axis=1)
    local = local.reshape(nh, 3, NA_QROWS, NA_KROWS, GRID_W, GRID_W).transpose(0, 1, 3, 4, 2, 5)
    return local.reshape(lead + (3, NA_NLOCAL, TM))


def _merge_body(h_ref, mod_ref, g_ref, wgate_ref, wa_ref, wb_ref, wm_ref, wo_ref, *rest, split):
    o_ref = rest[-1]
    for r in range(BGROUP):
        branch, pos = [], 0
        for has_ctx in split:
            if has_ctx:
                is_ctx = pl.program_id(0) >= NT_LAT
                branch.append(jnp.where(is_ctx, rest[pos + 1][r], rest[pos][r]))
            else:
                branch.append(rest[pos][r])
            pos += 2 if has_ctx else 1
        h = h_ref[r]
        shift, scale, gate = _mod_vectors(mod_ref, r)
        n = _rmsnorm(h, g_ref[...]) * (1.0 + scale) + shift
        gates =_sigmoid(_bdot(n.astype(BF16), wgate_ref[...]))
        y = (gates[:, :D_MODEL] * _dot_tn(branch[0], wa_ref[...])
             + gates[:, D_MODEL:2 * D_MODEL] * _dot_tn(branch[1], wb_ref[...])
             + gates[:, 2 * D_MODEL:] * _dot_tn(branch[2], wm_ref[...]))
        o_ref[r] = h + gate * _bdot(y.astype(BF16), wo_ref[...])


def _merge(h, mod, layer, g_norm, w_gate, branches, wa, wb, wm, wo, *, n_tiles):
    split = tuple(isinstance(br, tuple) for br in branches)
    arrays, specs = [], []
    for br in branches:
        if isinstance(br, tuple):
            lat, ctx = br
            arrays += [lat, ctx]
            specs += [pl.BlockSpec((BGROUP, lat.shape[1], TM), lambda t, b: (b, 0, jnp.minimum(t, NT_LAT - 1))),
                      pl.BlockSpec((BGROUP, ctx.shape[1], TM), lambda t, b: (b, 0, 0))]
        else:
            arrays.append(br)
            specs.append(pl.BlockSpec((BGROUP, br.shape[1], TM), lambda t, b: (b, 0, t)))
    return pl.pallas_call(
        functools.partial(_merge_body, split=split),
        grid=(n_tiles, BATCH // BGROUP),
        in_specs=[_tok_spec(D_MODEL), _mod_spec(layer, 1)]
                 + [_layer_spec(a, layer) for a in (g_norm, w_gate, wa, wb, wm, wo)] + specs,
        out_specs=_tok_spec(D_MODEL),
        out_shape=jax.ShapeDtypeStruct((BATCH, n_tiles * TM, D_MODEL), F32),
        compiler_params=_params(2),
        name="gated_merge",
    )(h, mod, g_norm, w_gate, wa, wb, wm, wo, *arrays)


def _rope_tables():
    t = np.arange(SEQ)
    row = (t // GRID_W).astype(np.float32)[:, None]
    col = (t % GRID_W).astype(np.float32)[:, None]

    def head_tables(d_rot):
        half = d_rot // 2
        freqs = np.float32(ROPE_THETA) ** (-np.arange(0, half, 2, dtype=np.float32) / np.float32(half))
        r, c = row * freqs, col * freqs
        ang = np.concatenate([r, r, c, c], axis=-1).astype(np.float32)
        first = (np.arange(d_rot) % half) < (half // 2)
        cos, sin = np.cos(ang), np.sin(ang)
        return cos, np.where(first, -sin, sin)

    def place(tabs, reps, lo, group):
        out = []
        for k, tab in enumerate(tabs):
            fill = 1.0 if k == 0 else 0.0
            d_rot = tab.shape[-1]
            g = np.concatenate([np.full((SEQ, lo), fill, np.float32), tab,
                                np.full((SEQ, group - lo - d_rot), fill, np.float32)], axis=-1)
            g = np.tile(g, (1, reps))
            ctx = np.full((CTX_LEN, reps * group), fill, np.float32)
            out.append(jnp.asarray(np.concatenate([g, ctx], axis=0), F32))
        return out

    tb = head_tables(HEAD_DIM)
    tm = head_tables(MLA_ROPE)
    return (place(tb, GQA_HEADS, 0, HEAD_DIM)
            + place(tm, MLA_HEADS, MLA_NOPE, MLA_PAD)
            + place(tm, 1, 0, LANES))


def _seg_matrix(width):
    idx = np.arange(width) // HEAD_DIM
    return jnp.asarray((idx[:, None] == idx[None, :]).astype(np.float32) / HEAD_DIM, BF16)


def _place_matrix():
    m = np.zeros((LANES, MLA_QK_WIDTH), np.float32)
    for hd in range(MLA_HEADS):
        for j in range(MLA_ROPE):
            m[j, hd * MLA_PAD + MLA_NOPE + j] = 1.0
    return jnp.asarray(m, BF16)


def _pad_heads(w, n_heads, per_head, keep_lo, keep_hi):
    d, k = w.shape[:2]
    wh = w.reshape(d, k, n_heads, per_head)[..., keep_lo:keep_hi]
    wh = jnp.pad(wh, ((0, 0), (0, 0), (0, 0), (0, MLA_PAD - (keep_hi - keep_lo))))
    return wh.reshape(d, k, n_heads * MLA_PAD)


def kernel(x, c, ctx, c_ctx, w_ada, b_ada, ffn1_norm, ffn1_w_gate, ffn1_w_up, ffn1_w_down, mix_norm, w_in,
           na_rel_bias, gqa_q_norm, gqa_k_norm, mla_q_norm, mla_kv_norm, mla_w_uq, mla_w_ukv,
           w_branch_a, w_branch_b, w_branch_c, w_out, ffn2_norm, ffn2_w_gate, ffn2_w_up, ffn2_w_down,
           final_norm):
    assert x.shape == (BATCH, SEQ, D_MODEL) and ctx.shape == (BATCH, CTX_LEN, D_MODEL)

    c_all = jnp.concatenate([c, jnp.tile(c_ctx[None, :], (BGROUP, 1)),
                             jnp.zeros((MOD_ROWS - BATCH - BGROUP, D_MODEL), F32)], axis=0)
    mod_all = _ada(c_all, w_ada, b_ada)

    tables = _rope_tables()
    seg_q, seg_k, place = _seg_matrix(2 * LANES), _seg_matrix(GQA_KV_WIDTH), _place_matrix()
    fin = final_norm.reshape(1, D_MODEL)

    def row(v):
        return v[:, None, :]

    w_qkv = w_in[:, :, :QKV_WIDTH].astype(BF16)
    w_gate = w_in[:, :, N_QKV_COLS:].astype(BF16)
    w_uq = _pad_heads(mla_w_uq, MLA_HEADS, MLA_NOPE + MLA_ROPE, 0, MLA_NOPE + MLA_ROPE).astype(BF16)
    w_uk = _pad_heads(mla_w_ukv, MLA_HEADS, MLA_NOPE + MLA_V, 0, MLA_NOPE)
    w_uv = mla_w_ukv.reshape(DEPTH, MLA_KV_RANK, MLA_HEADS, MLA_NOPE + MLA_V)[..., MLA_NOPE:]
    w_ukv = jnp.concatenate([w_uk, w_uv.reshape(DEPTH, MLA_KV_RANK, MLA_V_WIDTH)], axis=-1).astype(BF16)
    consts = (seg_q, seg_k, row(jnp.tile(gqa_q_norm, (1, GQA_HEADS))), row(jnp.tile(gqa_k_norm, (1, GQA_KV_HEADS))),
              row(mla_q_norm), row(mla_kv_norm), w_uq, w_ukv, place)
    ffn1 = (row(ffn1_norm), ffn1_w_gate.astype(BF16), ffn1_w_up.astype(BF16), ffn1_w_down.astype(BF16))
    ffn2 = (row(ffn2_norm), ffn2_w_gate.astype(BF16), ffn2_w_up.astype(BF16), ffn2_w_down.astype(BF16))
    branch_w = (w_branch_a.astype(BF16), w_branch_b.astype(BF16), w_branch_c.astype(BF16), w_out.astype(BF16))
    na_bias = _na_bias_table(na_rel_bias)
    g_mix = row(mix_norm)

    h = (x, ctx)
    for i in range(DEPTH):
        last = i == DEPTH - 1
        h = _ffn(h, mod_all, i, 0, *ffn1, fin, n_tiles=NT_ALL, final=False)
        qa, ka, va, qb, kb, vb, qm, km, vm = _proj(h, mod_all, i, g_mix, w_qkv, consts, tables)

        oa = _na_attn(qa, ka, va, na_bias, i)
        ob = _gqa_attn(qb, kb, vb)
        om = _mla_attn(qm, km, vm)
        if not last:
            oa_c, ob_c, om_c = _ctx_attn(((qa, ka, va), (qb, kb, vb), (qm, km, vm)))
            oa, ob, om = (oa, oa_c), (ob, ob_c), (om, om_c)

        n_tiles = NT_LAT if last else NT_ALL
        h = _merge(h, mod_all, i, g_mix, w_gate, (oa, ob, om), *branch_w, n_tiles=n_tiles)
        h = _ffn(h, mod_all, i, 2, *ffn2, fin, n_tiles=n_tiles, final=last)
    return h
```

```python
import functools

import numpy as np
import jax
import jax.numpy as jnp
from jax import lax
from jax.experimental import pallas as pl
from jax.experimental.pallas import tpu as pltpu

D_MODEL = 1024
BATCH = 32
SEQ = 2048
DEPTH = 2
CTX_LEN = 256
GRID_W = 64
HEAD_DIM = 64
NA_HEADS = 4
NA_ROWS = 8
NA_COLS = 16
GQA_HEADS = 8
GQA_KV_HEADS = 2
MLA_HEADS = 4
MLA_Q_RANK = 256
MLA_KV_RANK = 128
MLA_NOPE = 64
MLA_ROPE = 32
MLA_V = 64
N_MOD = 9
ROPE_THETA = 10000.0
EPS = 1e-6
NEG_BIG = -1e30

NA_WIDTH = NA_HEADS * HEAD_DIM
GQA_Q_WIDTH = GQA_HEADS * HEAD_DIM
GQA_KV_WIDTH = GQA_KV_HEADS * HEAD_DIM
GQA_GRP = GQA_HEADS // GQA_KV_HEADS
MLA_V_WIDTH = MLA_HEADS * MLA_V
ROWS = SEQ // GRID_W

LANES = 128
SUBLANES = 8
BF16_SUBLANES = 16
LOG2_E = 1.4426950408889634
TOK = SEQ + CTX_LEN
TM = 256
NT_ALL = TOK // TM
NT_LAT = SEQ // TM
BGROUP = 4
MOD_ROWS = 40
CTX_ROW = BATCH
ADA_BLOCK = 1152
MLA_PAD = LANES
MLA_QK_WIDTH = MLA_HEADS * MLA_PAD
N_QKV_COLS = 1952
QKV_WIDTH = 2048
NA_QROWS = TM // GRID_W
NA_KROWS = 12
NA_KBLK = NA_KROWS * GRID_W // TM
NA_NLOCAL = NA_KROWS * GRID_W
NA_NKEY = NA_NLOCAL + CTX_LEN
KEY_CHUNK = 256
NA_GROUP = 4
VMEM_LIMIT = 56 * 1024 * 1024

F32 = jnp.float32
BF16 = jnp.bfloat16

_OFF_AQ, _OFF_AK, _OFF_AV = 0, 256, 512
_OFF_BQ, _OFF_BK, _OFF_BV = 768, 1280, 1408
_OFF_CQ, _OFF_CKV, _OFF_CKR = 1536, 1792, 1920


def _params(n_axes):
    return pltpu.CompilerParams(dimension_semantics=("arbitrary",) * n_axes,
                                vmem_limit_bytes=VMEM_LIMIT)


def _const_spec(shape):
    nd = len(shape)
    return pl.BlockSpec(shape, lambda *_: (0,) * nd, pipeline_mode=pl.Buffered(1))


def _layer_spec(arr, layer):
    nd = arr.ndim - 1
    return pl.BlockSpec((None,) + arr.shape[1:], lambda *_: (layer,) + (0,) * nd, pipeline_mode=pl.Buffered(1))


def _mod_spec(layer, sub):
    return pl.BlockSpec((None, SUBLANES, 3 * D_MODEL),
                        lambda t, b: (layer, jnp.where(t >= NT_LAT, CTX_ROW, b * BGROUP) // SUBLANES, sub))


def _mod_vectors(mod_ref, r, t=None, b=None):
    t = pl.program_id(0) if t is None else t
    b = pl.program_id(1) if b is None else b
    first = jnp.where(t >= NT_LAT, CTX_ROW, b * BGROUP) % SUBLANES
    row = mod_ref[pl.ds(first + r, 1), :]
    return row[:, :D_MODEL], row[:, D_MODEL:2 * D_MODEL], row[:, 2 * D_MODEL:]


def _tok_spec(width):
    return pl.BlockSpec((BGROUP, TM, width), lambda t, b: (b, t, 0))


def _sigmoid(x):
    return 1.0 / (1.0 + jnp.exp(-x))


def _rmsnorm(x, g):
    return (x * lax.rsqrt(jnp.mean(x * x, axis=-1, keepdims=True) + EPS)) * g


def _bdot(a, b):
    return jnp.dot(a, b, preferred_element_type=F32)


def _dot_tn(a, b):
    return lax.dot_general(a, b, (((0,), (0,)), ((), ())), preferred_element_type=F32)


def _dot_nt(a, b):
    return lax.dot_general(a, b, (((1,), (1,)), ((), ())), preferred_element_type=F32)


def _ada_body(c_ref, w_ref, b_ref, o_ref):
    c = c_ref[...]
    s = c * _sigmoid(c)
    o_ref[...] = _bdot(s.astype(BF16), w_ref[...].astype(BF16)) + b_ref[...]


def _ada(c_all, w_ada, b_ada):
    nblk = (N_MOD * D_MODEL) // ADA_BLOCK
    return pl.pallas_call(
        _ada_body,
        grid=(DEPTH, nblk),
        in_specs=[
            pl.BlockSpec((MOD_ROWS, D_MODEL), lambda i, j: (0, 0)),
            pl.BlockSpec((None, D_MODEL, ADA_BLOCK), lambda i, j: (i, 0, j)),
            pl.BlockSpec((None, 1, ADA_BLOCK), lambda i, j: (i, 0, j)),
        ],
        out_specs=pl.BlockSpec((None, MOD_ROWS, ADA_BLOCK), lambda i, j: (i, 0, j)),
        out_shape=jax.ShapeDtypeStruct((DEPTH, MOD_ROWS, N_MOD * D_MODEL), F32),
        compiler_params=_params(2),
        name="ada_mod",
    )(c_all, w_ada, b_ada.reshape(DEPTH, 1, N_MOD * D_MODEL))


def _ffn_body(*refs, final, split):
    if split:
        h_ref, hc_ref, mod_ref, g_ref, wg_ref, wu_ref, wd_ref, fin_ref, o_ref = refs
    else:
        h_ref, mod_ref, g_ref, wg_ref, wu_ref, wd_ref, fin_ref, o_ref = refs
    for r in range(BGROUP):
        h = h_ref[r]
        if split:
            h = jnp.where(pl.program_id(0) >= NT_LAT, hc_ref[r], h)
        shift, scale, gate = _mod_vectors(mod_ref, r)
        n = _rmsnorm(h, g_ref[...]) * (1.0 + scale) + shift
        nb = n.astype(BF16)
        g = _bdot(nb, wg_ref[...])
        u = _bdot(nb, wu_ref[...])
        a = (g * _sigmoid(g)) * u
        d = _bdot(a.astype(BF16), wd_ref[...])
        out = h + 0.5 * gate * d
        if final:
            out = _rmsnorm(out, fin_ref[...])
        o_ref[r] = out


def _ffn(h, mod, layer, sub, g_norm, wg, wu, wd, fin, *, n_tiles, final):
    split = isinstance(h, tuple)
    if split:
        h_arrays = list(h)
        h_specs = [pl.BlockSpec((BGROUP, TM, D_MODEL), lambda t, b: (b, jnp.minimum(t, NT_LAT - 1), 0)),
                   pl.BlockSpec((BGROUP, TM, D_MODEL), lambda t, b: (b, 0, 0))]
    else:
        h_arrays, h_specs = [h], [_tok_spec(D_MODEL)]
    return pl.pallas_call(
        functools.partial(_ffn_body, final=final, split=split),
        grid=(n_tiles, BATCH // BGROUP),
        in_specs=h_specs + [_mod_spec(layer, sub)] + [_layer_spec(a, layer) for a in (g_norm, wg, wu, wd)]
                 + [_const_spec((1, D_MODEL))],
        out_specs=_tok_spec(D_MODEL),
        out_shape=jax.ShapeDtypeStruct((BATCH, n_tiles * TM, D_MODEL), F32),
        compiler_params=_params(2),
        name="half_ffn",
    )(*h_arrays, mod, g_norm, wg, wu, wd, fin)


def _rope(x, cos, sin, half):
    w = x.shape[-1]
    lane = lax.broadcasted_iota(jnp.int32, (1, w), 1)
    rot = jnp.where((lane & (2 * half - 1)) < half, pltpu.roll(x, w - half, 1), pltpu.roll(x, half, 1))
    return x * cos + rot * sin


def _seg_mean_sq(x, seg):
    w = seg.shape[0]
    parts = []
    for lo_col in range(0, x.shape[-1], w):
        x2 = x[:, lo_col:lo_col + w]
        x2 = x2 * x2
        hi = x2.astype(BF16)
        lo = (x2 - hi.astype(F32)).astype(BF16)
        parts.append(_bdot(hi, seg) + _bdot(lo, seg))
    return parts[0] if len(parts) == 1 else jnp.concatenate(parts, axis=1)


_PROJ_GROUPS = NT_ALL * (BATCH // BGROUP)


def _proj_prod(i):
    c = jnp.minimum(i, _PROJ_GROUPS - 1)
    return c // (BATCH // BGROUP), c % (BATCH // BGROUP)


def _proj_cons(i):
    c = jnp.maximum(i - 1, 0)
    return c // (BATCH // BGROUP), c % (BATCH // BGROUP)


def _proj_body(h_ref, mod_ref, g_ref, w_ref, segq_ref, segk_ref, qn_ref, kn_ref, mqn_ref, mkvn_ref,
               wuq_ref, wukv_ref, place_ref,
               cb_ref, sb_ref, cm_ref, sm_ref, ck_ref, sk_ref,
               qa_ref, ka_ref, va_ref, qb_ref, kb_ref, vb_ref, qm_ref, km_ref, vm_ref, z0_scr, z1_scr):
    i = pl.program_id(0)

    @pl.when(i == 0)
    def _():
        z1_scr[...] = jnp.zeros(z1_scr.shape, F32)

    def step(z_prod, z_cons):
        t, b = _proj_prod(i)
        for r in range(BGROUP):
            shift, mscale, _ = _mod_vectors(mod_ref, r, t, b)
            n = _rmsnorm(h_ref[r], g_ref[...]) * (1.0 + mscale) + shift
            z_prod[r] = _bdot(n.astype(BF16), w_ref[...])
        _proj_mixers(z_cons, segq_ref, segk_ref, qn_ref, kn_ref, mqn_ref, mkvn_ref, wuq_ref, wukv_ref, place_ref,
                     cb_ref, sb_ref, cm_ref, sm_ref, ck_ref, sk_ref,
                     qa_ref, ka_ref, va_ref, qb_ref, kb_ref, vb_ref, qm_ref, km_ref, vm_ref)

    pl.when(i % 2 == 0)(functools.partial(step, z0_scr, z1_scr))
    pl.when(i % 2 == 1)(functools.partial(step, z1_scr, z0_scr))


def _proj_mixers(z_ref, segq_ref, segk_ref, qn_ref, kn_ref, mqn_ref, mkvn_ref, wuq_ref, wukv_ref, place_ref,
                 cb_ref, sb_ref, cm_ref, sm_ref, ck_ref, sk_ref,
                 qa_ref, ka_ref, va_ref, qb_ref, kb_ref, vb_ref, qm_ref, km_ref, vm_ref):
    scale = LOG2_E * HEAD_DIM ** -0.5
    mla_scale = LOG2_E * (MLA_NOPE + MLA_ROPE) ** -0.5
    cb, sb = cb_ref[...], sb_ref[...]
    for r in range(BGROUP):
        z = z_ref[r]
        qa_ref[r] = (z[:, _OFF_AQ:_OFF_AQ + NA_WIDTH] * scale).astype(BF16)
        ka_ref[r] = z[:, _OFF_AK:_OFF_AK + NA_WIDTH].astype(BF16)
        va_ref[r] = z[:, _OFF_AV:_OFF_AV + NA_WIDTH].T.astype(BF16)

        bq = z[:, _OFF_BQ:_OFF_BQ + GQA_Q_WIDTH]
        bq = (bq * lax.rsqrt(_seg_mean_sq(bq, segq_ref[...]) + EPS)) * qn_ref[...]
        qb_ref[r] = (_rope(bq, cb, sb, HEAD_DIM // 4) * scale).astype(BF16)
        bk = z[:, _OFF_BK:_OFF_BK + GQA_KV_WIDTH]
        bk = (bk * lax.rsqrt(_seg_mean_sq(bk, segk_ref[...]) + EPS)) * kn_ref[...]
        kb_ref[r] = _rope(bk, cb[:, :GQA_KV_WIDTH], sb[:, :GQA_KV_WIDTH], HEAD_DIM // 4).astype(BF16)
        vb_ref[r] = z[:, _OFF_BV:_OFF_BV + GQA_KV_WIDTH].T.astype(BF16)

        cq = _rmsnorm(z[:, _OFF_CQ:_OFF_CQ + MLA_Q_RANK], mqn_ref[...])
        q_lat = _bdot(cq.astype(BF16), wuq_ref[...])
        qm_ref[r] = (_rope(q_lat, cm_ref[...], sm_ref[...], MLA_ROPE // 4) * mla_scale).astype(BF16)
        ckv = _rmsnorm(z[:, _OFF_CKV:_OFF_CKV + MLA_KV_RANK], mkvn_ref[...])
        kv_lat = _bdot(ckv.astype(BF16), wukv_ref[...])
        vm_ref[r] = kv_lat[:, MLA_QK_WIDTH:].T.astype(BF16)
        lane = lax.broadcasted_iota(jnp.int32, (1, LANES), 1)
        kr = jnp.where(lane < MLA_ROPE, z[:, _OFF_CKR:_OFF_CKR + LANES], 0.0)
        kr = _rope(kr, ck_ref[...], sk_ref[...], MLA_ROPE // 4).astype(BF16)
        km_ref[r] = (kv_lat[:, :MLA_QK_WIDTH] + _bdot(kr, place_ref[...])).astype(BF16)


def _proj(h, mod, layer, g_norm, w_qkv, consts, tables):
    widths = (NA_WIDTH, NA_WIDTH, NA_WIDTH, GQA_Q_WIDTH, GQA_KV_WIDTH, GQA_KV_WIDTH,
              MLA_QK_WIDTH, MLA_QK_WIDTH, MLA_V_WIDTH)
    is_value = [k % 3 == 2 for k in range(len(widths))]
    tab_specs = [pl.BlockSpec((TM, t.shape[-1]), lambda i: (_proj_cons(i)[0], 0)) for t in tables]

    def mod_index(i):
        t, b = _proj_prod(i)
        return layer, jnp.where(t >= NT_LAT, CTX_ROW, b * BGROUP) // SUBLANES, 1

    def out_spec(is_val, w):
        if is_val:
            return pl.BlockSpec((BGROUP, w, TM), lambda i: (_proj_cons(i)[1], 0, _proj_cons(i)[0]))
        return pl.BlockSpec((BGROUP, TM, w), lambda i: (_proj_cons(i)[1], _proj_cons(i)[0], 0))

    return pl.pallas_call(
        _proj_body,
        grid=(_PROJ_GROUPS + 1,),
        in_specs=[pl.BlockSpec((BGROUP, TM, D_MODEL), lambda i: (_proj_prod(i)[1], _proj_prod(i)[0], 0)),
                  pl.BlockSpec((None, SUBLANES, 3 * D_MODEL), mod_index),
                  _layer_spec(g_norm, layer), _layer_spec(w_qkv, layer)]
                 + [_layer_spec(c, layer) if c.ndim == 3 else _const_spec(c.shape) for c in consts] + tab_specs,
        out_specs=[out_spec(v, w) for v, w in zip(is_value, widths)],
        out_shape=[jax.ShapeDtypeStruct((BATCH, w, TOK) if v else (BATCH, TOK, w), BF16)
                   for v, w in zip(is_value, widths)],
        scratch_shapes=[pltpu.VMEM((BGROUP, TM, QKV_WIDTH), F32)] * 2,
        compiler_params=_params(1),
        name="mix_proj",
    )(h, mod, g_norm, w_qkv, *consts, *tables)


_CTX_MIXERS = ((NA_HEADS, 1, HEAD_DIM, HEAD_DIM), (GQA_HEADS, GQA_GRP, HEAD_DIM, HEAD_DIM),
               (MLA_HEADS, 1, MLA_PAD, MLA_V))


def _ctx_attn_body(*refs):
    ones = jnp.ones((BF16_SUBLANES, TM), BF16)
    n_mix = len(_CTX_MIXERS)
    for r in range(BGROUP):
        for mix, (heads, grp, dqk, dv) in enumerate(_CTX_MIXERS):
            q_ref, k_ref, vt_ref = refs[3 * mix:3 * mix + 3]
            outs = []
            for hd in range(heads):
                g = hd // grp
                st = _dot_nt(k_ref[r, :, g * dqk:(g + 1) * dqk], q_ref[r, :, hd * dqk:(hd + 1) * dqk])
                p = jnp.exp2(st - jnp.max(st, axis=0, keepdims=True)).astype(BF16)
                acc = _bdot(jnp.concatenate([vt_ref[r, g * dv:(g + 1) * dv, :], ones], axis=0), p)
                outs.append(acc[:dv] * (1.0 / acc[dv:dv + 1]))
            refs[3 * n_mix + mix][r] = jnp.concatenate(outs, axis=0).astype(BF16)


def _ctx_attn(qkv):
    in_specs, arrays = [], []
    for q, k, vt in qkv:
        arrays += [q, k, vt]
        in_specs += [pl.BlockSpec((BGROUP, TM, q.shape[-1]), lambda b: (b, NT_LAT, 0)),
                     pl.BlockSpec((BGROUP, TM, k.shape[-1]), lambda b: (b, NT_LAT, 0)),
                     pl.BlockSpec((BGROUP, vt.shape[1], TM), lambda b: (b, 0, NT_LAT))]
    widths = [heads * dv for heads, _, _, dv in _CTX_MIXERS]
    return pl.pallas_call(
        _ctx_attn_body,
        grid=(BATCH // BGROUP,),
        in_specs=in_specs,
        out_specs=[pl.BlockSpec((BGROUP, w, TM), lambda b: (b, 0, 0)) for w in widths],
        out_shape=[jax.ShapeDtypeStruct((BATCH, w, TM), BF16) for w in widths],
        compiler_params=_params(1),
        name="ctx_attn",
    )(*arrays)


def _pipe_step(prod, cons, n_chunk, dv):
    ones = jnp.ones((BF16_SUBLANES, KEY_CHUNK), BF16)
    m_cons = [m_ref[0:1, :] for _, _, m_ref in cons]
    m_new = [None] * len(prod)
    acc = [None] * len(cons)
    for c in range(n_chunk):
        rows = slice(c * KEY_CHUNK, (c + 1) * KEY_CHUNK)
        for d, (qg, k_chunk, bias_chunk, s_ref, _) in enumerate(prod):
            st = _dot_nt(k_chunk(c), qg)
            bias = None if bias_chunk is None else bias_chunk(c)
            if bias is not None:
                st = st + bias
            s_ref[rows, :] = st
            mc = jnp.max(st, axis=0, keepdims=True)
            m_new[d] = mc if m_new[d] is None else jnp.maximum(m_new[d], mc)
        for d, (vt_chunk, s_ref, _) in enumerate(cons):
            p = jnp.exp2(s_ref[rows, :] - m_cons[d]).astype(BF16)
            part = _bdot(jnp.concatenate([vt_chunk(c), ones], axis=0), p)
            acc[d] = part if acc[d] is None else acc[d] + part
    for d, (_, _, _, _, m_ref) in enumerate(prod):
        m_ref[0:1, :] = m_new[d]
    return [a[:dv] * (1.0 / a[dv:dv + 1]) for a in acc]


def _pipe_run(step, scratch, n_dots):
    i = pl.program_id(0)
    s = [scratch[slot * n_dots:(slot + 1) * n_dots] for slot in range(2)]
    m = [scratch[(2 + slot) * n_dots:(3 + slot) * n_dots] for slot in range(2)]

    @pl.when(i == 0)
    def _():
        for ref in s[1] + m[1]:
            ref[...] = jnp.zeros(ref.shape, F32)

    pl.when(i % 2 == 0)(functools.partial(step, 0, 1, s, m))
    pl.when(i % 2 == 1)(functools.partial(step, 1, 0, s, m))


def _n_chains(ag):
    return (BATCH // ag) * NT_LAT


def _prod_chain(i, ag=1):
    c = jnp.minimum(i, _n_chains(ag) - 1)
    return c // NT_LAT, c % NT_LAT, 0


def _cons_chain(i, ag=1):
    c = jnp.maximum(i - 1, 0)
    return c // NT_LAT, c % NT_LAT, 0


def _pipe_scratch(nk, m_len, n_dots):
    return ([pltpu.VMEM((nk, m_len), F32)] * (2 * n_dots)
            + [pltpu.VMEM((8, m_len), F32)] * (2 * n_dots))


def _gqa_body(q_ref, k_ref, vt_ref, o_ref, *scratch):
    def run(sp, sc, s, m):
        q = q_ref[...]
        prod, cons = [], []
        for g in range(GQA_KV_HEADS):
            qg = jnp.concatenate([q[:, (g * GQA_GRP + j) * HEAD_DIM:(g * GQA_GRP + j + 1) * HEAD_DIM]
                                  for j in range(GQA_GRP)], axis=0)

            def k_chunk(c, g=g):
                return k_ref[c * KEY_CHUNK:(c + 1) * KEY_CHUNK, g * HEAD_DIM:(g + 1) * HEAD_DIM]

            def vt_chunk(c, g=g):
                return vt_ref[g * HEAD_DIM:(g + 1) * HEAD_DIM, c * KEY_CHUNK:(c + 1) * KEY_CHUNK]

            prod.append((qg, k_chunk, None, s[sp][g], m[sp][g]))
            cons.append((vt_chunk, s[sc][g], m[sc][g]))
        outs = _pipe_step(prod, cons, TOK // KEY_CHUNK, HEAD_DIM)
        o = jnp.concatenate([ot[:, j * TM:(j + 1) * TM] for ot in outs for j in range(GQA_GRP)], axis=0)
        o_ref[...] = o.astype(BF16)

    _pipe_run(run, scratch, GQA_KV_HEADS)


def _full_key_attn(body, q, k, vt, *, w_out, n_dots, m_len, name):
    return pl.pallas_call(
        body,
        grid=(_n_chains(1) + 1,),
        in_specs=[
            pl.BlockSpec((None, TM, q.shape[-1]), _prod_chain),
            pl.BlockSpec((None, TOK, k.shape[-1]), lambda i: (_prod_chain(i)[0], 0, 0)),
            pl.BlockSpec((None, vt.shape[1], TOK), lambda i: (_cons_chain(i)[0], 0, 0)),
        ],
        out_specs=pl.BlockSpec((None, w_out, TM), lambda i: (_cons_chain(i)[0], 0, _cons_chain(i)[1])),
        out_shape=jax.ShapeDtypeStruct((BATCH, w_out, SEQ), BF16),
        scratch_shapes=_pipe_scratch(TOK, m_len, n_dots),
        compiler_params=_params(1),
        name=name,
    )(q, k, vt)


def _gqa_attn(q, k, vt):
    return _full_key_attn(_gqa_body, q, k, vt, w_out=GQA_Q_WIDTH, n_dots=GQA_KV_HEADS, m_len=GQA_GRP * TM,
                          name="gqa_attn")


def _mla_body(q_ref, k_ref, vt_ref, o_ref, *scratch):
    def run(sp, sc, s, m):
        prod, cons = [], []
        for d in range(MLA_HEADS):
            def k_chunk(c, d=d):
                return k_ref[c * KEY_CHUNK:(c + 1) * KEY_CHUNK, d * MLA_PAD:(d + 1) * MLA_PAD]

            def vt_chunk(c, d=d):
                return vt_ref[d * MLA_V:(d + 1) * MLA_V, c * KEY_CHUNK:(c + 1) * KEY_CHUNK]

            prod.append((q_ref[:, d * MLA_PAD:(d + 1) * MLA_PAD], k_chunk, None, s[sp][d], m[sp][d]))
            cons.append((vt_chunk, s[sc][d], m[sc][d]))
        outs = _pipe_step(prod, cons, TOK // KEY_CHUNK, MLA_V)
        o_ref[...] = jnp.concatenate(outs, axis=0).astype(BF16)

    _pipe_run(run, scratch, MLA_HEADS)


def _mla_attn(q, k, vt):
    return _full_key_attn(_mla_body, q, k, vt, w_out=MLA_V_WIDTH, n_dots=MLA_HEADS, m_len=TM, name="mla_attn")


def _na_start_block(t):
    return jnp.clip(t - 1, 0, NT_LAT - NA_KBLK)


def _na_body(q_ref, k0_ref, k1_ref, k2_ref, kc_ref, v0_ref, v1_ref, v2_ref, vc_ref, bias_ref, o_ref, *scratch):
    k_blocks = (k0_ref, k1_ref, k2_ref, kc_ref)
    v_blocks = (v0_ref, v1_ref, v2_ref, vc_ref)

    def run(sp, sc, s, m):
        prod, cons = [], []
        for a in range(NA_GROUP):
            for d in range(NA_HEADS):
                def k_chunk(c, a=a, d=d):
                    return k_blocks[c][a, :, d * HEAD_DIM:(d + 1) * HEAD_DIM]

                def bias_chunk(c, d=d):
                    return None if c == NA_KBLK else bias_ref[d, c * KEY_CHUNK:(c + 1) * KEY_CHUNK, :]

                def vt_chunk(c, a=a, d=d):
                    return v_blocks[c][a, d * HEAD_DIM:(d + 1) * HEAD_DIM, :]

                n = a * NA_HEADS + d
                prod.append((q_ref[a, :, d * HEAD_DIM:(d + 1) * HEAD_DIM], k_chunk, bias_chunk, s[sp][n], m[sp][n]))
                cons.append((vt_chunk, s[sc][n], m[sc][n]))
        outs = _pipe_step(prod, cons, NA_NKEY // KEY_CHUNK, HEAD_DIM)
        for a in range(NA_GROUP):
            o_ref[a] = jnp.concatenate(outs[a * NA_HEADS:(a + 1) * NA_HEADS], axis=0).astype(BF16)

    _pipe_run(run, scratch, NA_GROUP * NA_HEADS)


def _na_attn(q, k, vt, bias, layer):
    def prod(i):
        return _prod_chain(i, NA_GROUP)

    def cons(i):
        return _cons_chain(i, NA_GROUP)

    def k_spec(j):
        def index(i):
            b, t, _ = prod(i)
            return b, (NT_LAT if j == NA_KBLK else _na_start_block(t) + j), 0
        return pl.BlockSpec((NA_GROUP, TM, NA_WIDTH), index)

    def vt_spec(j):
        def index(i):
            b, t, _ = cons(i)
            return b, 0, (NT_LAT if j == NA_KBLK else _na_start_block(t) + j)
        return pl.BlockSpec((NA_GROUP, NA_WIDTH, TM), index)

    def bias_index(i):
        t = prod(i)[1]
        return layer, 0, jnp.where(t == 0, 0, jnp.where(t == NT_LAT - 1, 2, 1)), 0, 0

    return pl.pallas_call(
        _na_body,
        grid=(_n_chains(NA_GROUP) + 1,),
        in_specs=[pl.BlockSpec((NA_GROUP, TM, NA_WIDTH), prod)]
                 + [k_spec(j) for j in range(NA_KBLK + 1)] + [vt_spec(j) for j in range(NA_KBLK + 1)]
                 + [pl.BlockSpec((None, NA_HEADS, None, NA_NLOCAL, TM), bias_index)],
        out_specs=pl.BlockSpec((NA_GROUP, NA_WIDTH, TM), lambda i: (cons(i)[0], 0, cons(i)[1])),
        out_shape=jax.ShapeDtypeStruct((BATCH, NA_WIDTH, SEQ), BF16),
        scratch_shapes=_pipe_scratch(NA_NKEY, TM, NA_GROUP * NA_HEADS),
        compiler_params=_params(1),
        name="na_attn",
    )(q, *([k] * (NA_KBLK + 1)), *([vt] * (NA_KBLK + 1)), bias)


def _na_bias_table(rel_bias):
    lead = rel_bias.shape[:-2]
    rel_bias = rel_bias.reshape((-1,) + rel_bias.shape[-2:])
    nh = rel_bias.shape[0]
    cols = np.arange(GRID_W)
    c0 = np.clip(cols - NA_COLS // 2, 0, GRID_W - NA_COLS)
    col_in = (cols[None, :] >= c0[:, None]) & (cols[None, :] < c0[:, None] + NA_COLS)
    n_dr = 2 * NA_ROWS - 1
    pad = GRID_W - NA_COLS
    rbp = jnp.pad(rel_bias.astype(F32) * LOG2_E, ((0, 0), (0, 0), (pad, pad)), mode="edge")
    tiles = jnp.stack([rbp[:, :, GRID_W - 1 - cq:2 * GRID_W - 1 - cq] for cq in range(GRID_W)], axis=3)
    tiles = jnp.where(col_in.T[None, None], tiles, NEG_BIG).reshape(nh, n_dr * GRID_W, GRID_W)
    classes = []
    for tile in (0, 2, NT_LAT - 1):
        start = int(np.clip(tile - 1, 0, NT_LAT - NA_KBLK)) * NA_QROWS
        blocks = []
        for qr in range(NA_QROWS):
            r = tile * NA_QROWS + qr
            r0 = int(np.clip(r - NA_ROWS // 2, 0, ROWS - NA_ROWS))
            kj0, dr0 = r0 - start, r0 - r + NA_ROWS - 1
            run = tiles[:, dr0 * GRID_W:(dr0 + NA_ROWS) * GRID_W, :]
            blocks.append(jnp.pad(run, ((0, 0), (kj0 * GRID_W, (NA_KROWS - NA_ROWS - kj0) * GRID_W), (0, 0)),
                                  constant_values=NEG_BIG))
        classes.append(jnp.concatenate(blocks, axis=-1))
    return jnp.stack(classes, axis=1).reshape(lead + (3, NA_NLOCAL, TM))


def _merge_body(h_ref, mod_ref, g_ref, wgate_ref, wa_ref, wb_ref, wm_ref, wo_ref, *rest, split):
    o_ref = rest[-1]
    for r in range(BGROUP):
        branch, pos = [], 0
        for has_ctx in split:
            if has_ctx:
                is_ctx = pl.program_id(0) >= NT_LAT
                branch.append(jnp.where(is_ctx, rest[pos + 1][r], rest[pos][r]))
            else:
                branch.append(rest[pos][r])
            pos += 2 if has_ctx else 1
        h = h_ref[r]
        shift, scale, gate = _mod_vectors(mod_ref, r)
        n = _rmsnorm(h, g_ref[...]) * (1.0 + scale) + shift
        gates =_sigmoid(_bdot(n.astype(BF16), wgate_ref[...]))
        y = (gates[:, :D_MODEL] * _dot_tn(branch[0], wa_ref[...])
             + gates[:, D_MODEL:2 * D_MODEL] * _dot_tn(branch[1], wb_ref[...])
             + gates[:, 2 * D_MODEL:] * _dot_tn(branch[2], wm_ref[...]))
        o_ref[r] = h + gate * _bdot(y.astype(BF16), wo_ref[...])


def _merge(h, mod, layer, g_norm, w_gate, branches, wa, wb, wm, wo, *, n_tiles):
    split = tuple(isinstance(br, tuple) for br in branches)
    arrays, specs = [], []
    for br in branches:
        if isinstance(br, tuple):
            lat, ctx = br
            arrays += [lat, ctx]
            specs += [pl.BlockSpec((BGROUP, lat.shape[1], TM), lambda t, b: (b, 0, jnp.minimum(t, NT_LAT - 1))),
                      pl.BlockSpec((BGROUP, ctx.shape[1], TM), lambda t, b: (b, 0, 0))]
        else:
            arrays.append(br)
            specs.append(pl.BlockSpec((BGROUP, br.shape[1], TM), lambda t, b: (b, 0, t)))
    return pl.pallas_call(
        functools.partial(_merge_body, split=split),
        grid=(n_tiles, BATCH // BGROUP),
        in_specs=[_tok_spec(D_MODEL), _mod_spec(layer, 1)]
                 + [_layer_spec(a, layer) for a in (g_norm, w_gate, wa, wb, wm, wo)] + specs,
        out_specs=_tok_spec(D_MODEL),
        out_shape=jax.ShapeDtypeStruct((BATCH, n_tiles * TM, D_MODEL), F32),
        compiler_params=_params(2),
        name="gated_merge",
    )(h, mod, g_norm, w_gate, wa, wb, wm, wo, *arrays)


def _rope_tables():
    t = np.arange(SEQ)
    row = (t // GRID_W).astype(np.float32)[:, None]
    col = (t % GRID_W).astype(np.float32)[:, None]

    def head_tables(d_rot):
        half = d_rot // 2
        freqs = np.float32(ROPE_THETA) ** (-np.arange(0, half, 2, dtype=np.float32) / np.float32(half))
        r, c = row * freqs, col * freqs
        ang = np.concatenate([r, r, c, c], axis=-1).astype(np.float32)
        first = (np.arange(d_rot) % half) < (half // 2)
        cos, sin = np.cos(ang), np.sin(ang)
        return cos, np.where(first, -sin, sin)

    def place(tabs, reps, lo, group):
        out = []
        for k, tab in enumerate(tabs):
            fill = 1.0 if k == 0 else 0.0
            d_rot = tab.shape[-1]
            g = np.concatenate([np.full((SEQ, lo), fill, np.float32), tab,
                                np.full((SEQ, group - lo - d_rot), fill, np.float32)], axis=-1)
            g = np.tile(g, (1, reps))
            ctx = np.full((CTX_LEN, reps * group), fill, np.float32)
            out.append(jnp.asarray(np.concatenate([g, ctx], axis=0), F32))
        return out

    tb = head_tables(HEAD_DIM)
    tm = head_tables(MLA_ROPE)
    return (place(tb, GQA_HEADS, 0, HEAD_DIM)
            + place(tm, MLA_HEADS, MLA_NOPE, MLA_PAD)
            + place(tm, 1, 0, LANES))


def _seg_matrix(width):
    idx = np.arange(width) // HEAD_DIM
    return jnp.asarray((idx[:, None] == idx[None, :]).astype(np.float32) / HEAD_DIM, BF16)


def _place_matrix():
    m = np.zeros((LANES, MLA_QK_WIDTH), np.float32)
    for hd in range(MLA_HEADS):
        for j in range(MLA_ROPE):
            m[j, hd * MLA_PAD + MLA_NOPE + j] = 1.0
    return jnp.asarray(m, BF16)


def _pad_heads(w, n_heads, per_head, keep_lo, keep_hi):
    d, k = w.shape[:2]
    wh = w.reshape(d, k, n_heads, per_head)[..., keep_lo:keep_hi]
    wh = jnp.pad(wh, ((0, 0), (0, 0), (0, 0), (0, MLA_PAD - (keep_hi - keep_lo))))
    return wh.reshape(d, k, n_heads * MLA_PAD)


def kernel(x, c, ctx, c_ctx, w_ada, b_ada, ffn1_norm, ffn1_w_gate, ffn1_w_up, ffn1_w_down, mix_norm, w_in,
           na_rel_bias, gqa_q_norm, gqa_k_norm, mla_q_norm, mla_kv_norm, mla_w_uq, mla_w_ukv,
           w_branch_a, w_branch_b, w_branch_c, w_out, ffn2_norm, ffn2_w_gate, ffn2_w_up, ffn2_w_down,
           final_norm):
    assert x.shape == (BATCH, SEQ, D_MODEL) and ctx.shape == (BATCH, CTX_LEN, D_MODEL)

    c_all = jnp.concatenate([c, jnp.tile(c_ctx[None, :], (BGROUP, 1)),
                             jnp.zeros((MOD_ROWS - BATCH - BGROUP, D_MODEL), F32)], axis=0)
    mod_all = _ada(c_all, w_ada, b_ada)

    tables = _rope_tables()
    seg_q, seg_k, place = _seg_matrix(2 * LANES), _seg_matrix(GQA_KV_WIDTH), _place_matrix()
    fin = final_norm.reshape(1, D_MODEL)

    def row(v):
        return v[:, None, :]

    w_qkv = w_in[:, :, :QKV_WIDTH].astype(BF16)
    w_gate = w_in[:, :, N_QKV_COLS:].astype(BF16)
    w_uq = _pad_heads(mla_w_uq, MLA_HEADS, MLA_NOPE + MLA_ROPE, 0, MLA_NOPE + MLA_ROPE).astype(BF16)
    w_uk = _pad_heads(mla_w_ukv, MLA_HEADS, MLA_NOPE + MLA_V, 0, MLA_NOPE)
    w_uv = mla_w_ukv.reshape(DEPTH, MLA_KV_RANK, MLA_HEADS, MLA_NOPE + MLA_V)[..., MLA_NOPE:]
    w_ukv = jnp.concatenate([w_uk, w_uv.reshape(DEPTH, MLA_KV_RANK, MLA_V_WIDTH)], axis=-1).astype(BF16)
    consts = (seg_q, seg_k, row(jnp.tile(gqa_q_norm, (1, GQA_HEADS))), row(jnp.tile(gqa_k_norm, (1, GQA_KV_HEADS))),
              row(mla_q_norm), row(mla_kv_norm), w_uq, w_ukv, place)
    ffn1 = (row(ffn1_norm), ffn1_w_gate.astype(BF16), ffn1_w_up.astype(BF16), ffn1_w_down.astype(BF16))
    ffn2 = (row(ffn2_norm), ffn2_w_gate.astype(BF16), ffn2_w_up.astype(BF16), ffn2_w_down.astype(BF16))
    branch_w = (w_branch_a.astype(BF16), w_branch_b.astype(BF16), w_branch_c.astype(BF16), w_out.astype(BF16))
    na_bias = _na_bias_table(na_rel_bias)
    g_mix = row(mix_norm)

    h = (x, ctx)
    for i in range(DEPTH):
        last = i == DEPTH - 1
        h = _ffn(h, mod_all, i, 0, *ffn1, fin, n_tiles=NT_ALL, final=False)
        qa, ka, va, qb, kb, vb, qm, km, vm = _proj(h, mod_all, i, g_mix, w_qkv, consts, tables)

        oa = _na_attn(qa, ka, va, na_bias, i)
        ob = _gqa_attn(qb, kb, vb)
        om = _mla_attn(qm, km, vm)
        if not last:
            oa_c, ob_c, om_c = _ctx_attn(((qa, ka, va), (qb, kb, vb), (qm, km, vm)))
            oa, ob, om = (oa, oa_c), (ob, ob_c), (om, om_c)

        n_tiles = NT_LAT if last else NT_ALL
        h = _merge(h, mod_all, i, g_mix, w_gate, (oa, ob, om), *branch_w, n_tiles=n_tiles)
        h = _ffn(h, mod_all, i, 2, *ffn2, fin, n_tiles=n_tiles, final=last)
    return h
```

```python
import functools

import numpy as np
import jax
import jax.numpy as jnp
from jax import lax
from jax.experimental import pallas as pl
from jax.experimental.pallas import tpu as pltpu

D_MODEL = 1024
BATCH = 32
SEQ = 2048
DEPTH = 2
CTX_LEN = 256
GRID_W = 64
HEAD_DIM = 64
NA_HEADS = 4
NA_ROWS = 8
NA_COLS = 16
GQA_HEADS = 8
GQA_KV_HEADS = 2
MLA_HEADS = 4
MLA_Q_RANK = 256
MLA_KV_RANK = 128
MLA_NOPE = 64
MLA_ROPE = 32
MLA_V = 64
N_MOD = 9
ROPE_THETA = 10000.0
EPS = 1e-6
NEG_BIG = -1e30

NA_WIDTH = NA_HEADS * HEAD_DIM
GQA_Q_WIDTH = GQA_HEADS * HEAD_DIM
GQA_KV_WIDTH = GQA_KV_HEADS * HEAD_DIM
GQA_GRP = GQA_HEADS // GQA_KV_HEADS
MLA_V_WIDTH = MLA_HEADS * MLA_V
ROWS = SEQ // GRID_W

LANES = 128
SUBLANES = 8
BF16_SUBLANES = 16
LOG2_E = 1.4426950408889634
TOK = SEQ + CTX_LEN
TM = 256
NT_ALL = TOK // TM
NT_LAT = SEQ // TM
BGROUP = 4
MOD_ROWS = 40
CTX_ROW = BATCH
ADA_BLOCK = 1152
MLA_PAD = LANES
MLA_QK_WIDTH = MLA_HEADS * MLA_PAD
N_QKV_COLS = 1952
QKV_WIDTH = 2048
NA_QROWS = TM // GRID_W
NA_KROWS = 12
NA_KBLK = NA_KROWS * GRID_W // TM
NA_NLOCAL = NA_KROWS * GRID_W
NA_NKEY = NA_NLOCAL + CTX_LEN
KEY_CHUNK = 256
NA_GROUP = 4
VMEM_LIMIT = 56 * 1024 * 1024

F32 = jnp.float32
BF16 = jnp.bfloat16

_OFF_AQ, _OFF_AK, _OFF_AV = 0, 256, 512
_OFF_BQ, _OFF_BK, _OFF_BV = 768, 1280, 1408
_OFF_CQ, _OFF_CKV, _OFF_CKR = 1536, 1792, 1920


def _params(n_axes):
    return pltpu.CompilerParams(dimension_semantics=("arbitrary",) * n_axes,
                                vmem_limit_bytes=VMEM_LIMIT)


def _const_spec(shape):
    nd = len(shape)
    return pl.BlockSpec(shape, lambda *_: (0,) * nd, pipeline_mode=pl.Buffered(1))


def _layer_spec(arr, layer):
    nd = arr.ndim - 1
    return pl.BlockSpec((None,) + arr.shape[1:], lambda *_: (layer,) + (0,) * nd, pipeline_mode=pl.Buffered(1))


def _mod_spec(layer, sub):
    return pl.BlockSpec((None, SUBLANES, 3 * D_MODEL),
                        lambda t, b: (layer, jnp.where(t >= NT_LAT, CTX_ROW, b * BGROUP) // SUBLANES, sub))


def _mod_vectors(mod_ref, r, t=None, b=None):
    t = pl.program_id(0) if t is None else t
    b = pl.program_id(1) if b is None else b
    first = jnp.where(t >= NT_LAT, CTX_ROW, b * BGROUP) % SUBLANES
    row = mod_ref[pl.ds(first + r, 1), :]
    return row[:, :D_MODEL], row[:, D_MODEL:2 * D_MODEL], row[:, 2 * D_MODEL:]


def _tok_spec(width):
    return pl.BlockSpec((BGROUP, TM, width), lambda t, b: (b, t, 0))


def _sigmoid(x):
    return 1.0 / (1.0 + jnp.exp(-x))


def _rmsnorm(x, g):
    return (x * lax.rsqrt(jnp.mean(x * x, axis=-1, keepdims=True) + EPS)) * g


def _bdot(a, b):
    return jnp.dot(a, b, preferred_element_type=F32)


def _dot_tn(a, b):
    return lax.dot_general(a, b, (((0,), (0,)), ((), ())), preferred_element_type=F32)


def _dot_nt(a, b):
    return lax.dot_general(a, b, (((1,), (1,)), ((), ())), preferred_element_type=F32)


def _ada_body(c_ref, w_ref, b_ref, o_ref):
    c = c_ref[...]
    s = c * _sigmoid(c)
    o_ref[...] = _bdot(s.astype(BF16), w_ref[...].astype(BF16)) + b_ref[...]


def _ada(c_all, w_ada, b_ada):
    nblk = (N_MOD * D_MODEL) // ADA_BLOCK
    return pl.pallas_call(
        _ada_body,
        grid=(DEPTH, nblk),
        in_specs=[
            pl.BlockSpec((MOD_ROWS, D_MODEL), lambda i, j: (0, 0)),
            pl.BlockSpec((None, D_MODEL, ADA_BLOCK), lambda i, j: (i, 0, j)),
            pl.BlockSpec((None, 1, ADA_BLOCK), lambda i, j: (i, 0, j)),
        ],
        out_specs=pl.BlockSpec((None, MOD_ROWS, ADA_BLOCK), lambda i, j: (i, 0, j)),
        out_shape=jax.ShapeDtypeStruct((DEPTH, MOD_ROWS, N_MOD * D_MODEL), F32),
        compiler_params=_params(2),
        name="ada_mod",
    )(c_all, w_ada, b_ada.reshape(DEPTH, 1, N_MOD * D_MODEL))


def _ffn_body(*refs, final, split):
    if split:
        h_ref, hc_ref, mod_ref, g_ref, wg_ref, wu_ref, wd_ref, fin_ref, o_ref = refs
    else:
        h_ref, mod_ref, g_ref, wg_ref, wu_ref, wd_ref, fin_ref, o_ref = refs
    for r in range(BGROUP):
        h = h_ref[r]
        if split:
            h = jnp.where(pl.program_id(0) >= NT_LAT, hc_ref[r], h)
        shift, scale, gate = _mod_vectors(mod_ref, r)
        n = _rmsnorm(h, g_ref[...]) * (1.0 + scale) + shift
        nb = n.astype(BF16)
        g = _bdot(nb, wg_ref[...])
        u = _bdot(nb, wu_ref[...])
        a = (g * _sigmoid(g)) * u
        d = _bdot(a.astype(BF16), wd_ref[...])
        out = h + 0.5 * gate * d
        if final:
            out = _rmsnorm(out, fin_ref[...])
        o_ref[r] = out


def _ffn(h, mod, layer, sub, g_norm, wg, wu, wd, fin, *, n_tiles, final):
    split = isinstance(h, tuple)
    if split:
        h_arrays = list(h)
        h_specs = [pl.BlockSpec((BGROUP, TM, D_MODEL), lambda t, b: (b, jnp.minimum(t, NT_LAT - 1), 0)),
                   pl.BlockSpec((BGROUP, TM, D_MODEL), lambda t, b: (b, 0, 0))]
    else:
        h_arrays, h_specs = [h], [_tok_spec(D_MODEL)]
    return pl.pallas_call(
        functools.partial(_ffn_body, final=final, split=split),
        grid=(n_tiles, BATCH // BGROUP),
        in_specs=h_specs + [_mod_spec(layer, sub)] + [_layer_spec(a, layer) for a in (g_norm, wg, wu, wd)]
                 + [_const_spec((1, D_MODEL))],
        out_specs=_tok_spec(D_MODEL),
        out_shape=jax.ShapeDtypeStruct((BATCH, n_tiles * TM, D_MODEL), F32),
        compiler_params=_params(2),
        name="half_ffn",
    )(*h_arrays, mod, g_norm, wg, wu, wd, fin)


def _rope(x, cos, sin, half):
    w = x.shape[-1]
    lane = lax.broadcasted_iota(jnp.int32, (1, w), 1)
    rot = jnp.where((lane & (2 * half - 1)) < half, pltpu.roll(x, w - half, 1), pltpu.roll(x, half, 1))
    return x * cos + rot * sin


def _seg_mean_sq(x, seg):
    w = seg.shape[0]
    parts = []
    for lo_col in range(0, x.shape[-1], w):
        x2 = x[:, lo_col:lo_col + w]
        x2 = x2 * x2
        hi = x2.astype(BF16)
        lo = (x2 - hi.astype(F32)).astype(BF16)
        parts.append(_bdot(hi, seg) + _bdot(lo, seg))
    return parts[0] if len(parts) == 1 else jnp.concatenate(parts, axis=1)


_PROJ_GROUPS = NT_ALL * (BATCH // BGROUP)


def _proj_prod(i):
    c = jnp.minimum(i, _PROJ_GROUPS - 1)
    return c // (BATCH // BGROUP), c % (BATCH // BGROUP)


def _proj_cons(i):
    c = jnp.maximum(i - 1, 0)
    return c // (BATCH // BGROUP), c % (BATCH // BGROUP)


def _proj_body(h_ref, mod_ref, g_ref, w_ref, segq_ref, segk_ref, qn_ref, kn_ref, mqn_ref, mkvn_ref,
               wuq_ref, wukv_ref, place_ref,
               cb_ref, sb_ref, cm_ref, sm_ref, ck_ref, sk_ref,
               qa_ref, ka_ref, va_ref, qb_ref, kb_ref, vb_ref, qm_ref, km_ref, vm_ref, z0_scr, z1_scr):
    i = pl.program_id(0)

    @pl.when(i == 0)
    def _():
        z1_scr[...] = jnp.zeros(z1_scr.shape, F32)

    def step(z_prod, z_cons):
        t, b = _proj_prod(i)
        for r in range(BGROUP):
            shift, mscale, _ = _mod_vectors(mod_ref, r, t, b)
            n = _rmsnorm(h_ref[r], g_ref[...]) * (1.0 + mscale) + shift
            z_prod[r] = _bdot(n.astype(BF16), w_ref[...])
        _proj_mixers(z_cons, segq_ref, segk_ref, qn_ref, kn_ref, mqn_ref, mkvn_ref, wuq_ref, wukv_ref, place_ref,
                     cb_ref, sb_ref, cm_ref, sm_ref, ck_ref, sk_ref,
                     qa_ref, ka_ref, va_ref, qb_ref, kb_ref, vb_ref, qm_ref, km_ref, vm_ref)

    pl.when(i % 2 == 0)(functools.partial(step, z0_scr, z1_scr))
    pl.when(i % 2 == 1)(functools.partial(step, z1_scr, z0_scr))


def _proj_mixers(z_ref, segq_ref, segk_ref, qn_ref, kn_ref, mqn_ref, mkvn_ref, wuq_ref, wukv_ref, place_ref,
                 cb_ref, sb_ref, cm_ref, sm_ref, ck_ref, sk_ref,
                 qa_ref, ka_ref, va_ref, qb_ref, kb_ref, vb_ref, qm_ref, km_ref, vm_ref):
    scale = LOG2_E * HEAD_DIM ** -0.5
    mla_scale = LOG2_E * (MLA_NOPE + MLA_ROPE) ** -0.5
    cb, sb = cb_ref[...], sb_ref[...]
    for r in range(BGROUP):
        z = z_ref[r]
        qa_ref[r] = (z[:, _OFF_AQ:_OFF_AQ + NA_WIDTH] * scale).astype(BF16)
        ka_ref[r] = z[:, _OFF_AK:_OFF_AK + NA_WIDTH].astype(BF16)
        va_ref[r] = z[:, _OFF_AV:_OFF_AV + NA_WIDTH].T.astype(BF16)

        bq = z[:, _OFF_BQ:_OFF_BQ + GQA_Q_WIDTH]
        bq = (bq * lax.rsqrt(_seg_mean_sq(bq, segq_ref[...]) + EPS)) * qn_ref[...]
        qb_ref[r] = (_rope(bq, cb, sb, HEAD_DIM // 4) * scale).astype(BF16)
        bk = z[:, _OFF_BK:_OFF_BK + GQA_KV_WIDTH]
        bk = (bk * lax.rsqrt(_seg_mean_sq(bk, segk_ref[...]) + EPS)) * kn_ref[...]
        kb_ref[r] = _rope(bk, cb[:, :GQA_KV_WIDTH], sb[:, :GQA_KV_WIDTH], HEAD_DIM // 4).astype(BF16)
        vb_ref[r] = z[:, _OFF_BV:_OFF_BV + GQA_KV_WIDTH].T.astype(BF16)

        cq = _rmsnorm(z[:, _OFF_CQ:_OFF_CQ + MLA_Q_RANK], mqn_ref[...])
        q_lat = _bdot(cq.astype(BF16), wuq_ref[...])
        qm_ref[r] = (_rope(q_lat, cm_ref[...], sm_ref[...], MLA_ROPE // 4) * mla_scale).astype(BF16)
        ckv = _rmsnorm(z[:, _OFF_CKV:_OFF_CKV + MLA_KV_RANK], mkvn_ref[...])
        kv_lat = _bdot(ckv.astype(BF16), wukv_ref[...])
        vm_ref[r] = kv_lat[:, MLA_QK_WIDTH:].T.astype(BF16)
        lane = lax.broadcasted_iota(jnp.int32, (1, LANES), 1)
        kr = jnp.where(lane < MLA_ROPE, z[:, _OFF_CKR:_OFF_CKR + LANES], 0.0)
        kr = _rope(kr, ck_ref[...], sk_ref[...], MLA_ROPE // 4).astype(BF16)
        km_ref[r] = (kv_lat[:, :MLA_QK_WIDTH] + _bdot(kr, place_ref[...])).astype(BF16)


def _proj(h, mod, layer, g_norm, w_qkv, consts, tables):
    widths = (NA_WIDTH, NA_WIDTH, NA_WIDTH, GQA_Q_WIDTH, GQA_KV_WIDTH, GQA_KV_WIDTH,
              MLA_QK_WIDTH, MLA_QK_WIDTH, MLA_V_WIDTH)
    is_value = [k % 3 == 2 for k in range(len(widths))]
    tab_specs = [pl.BlockSpec((TM, t.shape[-1]), lambda i: (_proj_cons(i)[0], 0)) for t in tables]

    def mod_index(i):
        t, b = _proj_prod(i)
        return layer, jnp.where(t >= NT_LAT, CTX_ROW, b * BGROUP) // SUBLANES, 1

    def out_spec(is_val, w):
        if is_val:
            return pl.BlockSpec((BGROUP, w, TM), lambda i: (_proj_cons(i)[1], 0, _proj_cons(i)[0]))
        return pl.BlockSpec((BGROUP, TM, w), lambda i: (_proj_cons(i)[1], _proj_cons(i)[0], 0))

    return pl.pallas_call(
        _proj_body,
        grid=(_PROJ_GROUPS + 1,),
        in_specs=[pl.BlockSpec((BGROUP, TM, D_MODEL), lambda i: (_proj_prod(i)[1], _proj_prod(i)[0], 0)),
                  pl.BlockSpec((None, SUBLANES, 3 * D_MODEL), mod_index),
                  _layer_spec(g_norm, layer), _layer_spec(w_qkv, layer)]
                 + [_layer_spec(c, layer) if c.ndim == 3 else _const_spec(c.shape) for c in consts] + tab_specs,
        out_specs=[out_spec(v, w) for v, w in zip(is_value, widths)],
        out_shape=[jax.ShapeDtypeStruct((BATCH, w, TOK) if v else (BATCH, TOK, w), BF16)
                   for v, w in zip(is_value, widths)],
        scratch_shapes=[pltpu.VMEM((BGROUP, TM, QKV_WIDTH), F32)] * 2,
        compiler_params=_params(1),
        name="mix_proj",
    )(h, mod, g_norm, w_qkv, *consts, *tables)


_CTX_MIXERS = ((NA_HEADS, 1, HEAD_DIM, HEAD_DIM), (GQA_HEADS, GQA_GRP, HEAD_DIM, HEAD_DIM),
               (MLA_HEADS, 1, MLA_PAD, MLA_V))


def _ctx_attn_body(*refs):
    ones = jnp.ones((BF16_SUBLANES, TM), BF16)
    n_mix = len(_CTX_MIXERS)
    for r in range(BGROUP):
        for mix, (heads, grp, dqk, dv) in enumerate(_CTX_MIXERS):
            q_ref, k_ref, vt_ref = refs[3 * mix:3 * mix + 3]
            outs = []
            for hd in range(heads):
                g = hd // grp
                st = _dot_nt(k_ref[r, :, g * dqk:(g + 1) * dqk], q_ref[r, :, hd * dqk:(hd + 1) * dqk])
                p = jnp.exp2(st - jnp.max(st, axis=0, keepdims=True)).astype(BF16)
                acc = _bdot(jnp.concatenate([vt_ref[r, g * dv:(g + 1) * dv, :], ones], axis=0), p)
                outs.append(acc[:dv] * (1.0 / acc[dv:dv + 1]))
            refs[3 * n_mix + mix][r] = jnp.concatenate(outs, axis=0).astype(BF16)


def _ctx_attn(qkv):
    in_specs, arrays = [], []
    for q, k, vt in qkv:
        arrays += [q, k, vt]
        in_specs += [pl.BlockSpec((BGROUP, TM, q.shape[-1]), lambda b: (b, NT_LAT, 0)),
                     pl.BlockSpec((BGROUP, TM, k.shape[-1]), lambda b: (b, NT_LAT, 0)),
                     pl.BlockSpec((BGROUP, vt.shape[1], TM), lambda b: (b, 0, NT_LAT))]
    widths = [heads * dv for heads, _, _, dv in _CTX_MIXERS]
    return pl.pallas_call(
        _ctx_attn_body,
        grid=(BATCH // BGROUP,),
        in_specs=in_specs,
        out_specs=[pl.BlockSpec((BGROUP, w, TM), lambda b: (b, 0, 0)) for w in widths],
        out_shape=[jax.ShapeDtypeStruct((BATCH, w, TM), BF16) for w in widths],
        compiler_params=_params(1),
        name="ctx_attn",
    )(*arrays)


def _pipe_step(prod, cons, n_chunk, dv):
    ones = jnp.ones((BF16_SUBLANES, KEY_CHUNK), BF16)
    m_cons = [m_ref[0:1, :] for _, _, m_ref in cons]
    m_new = [None] * len(prod)
    acc = [None] * len(cons)
    for c in range(n_chunk):
        rows = slice(c * KEY_CHUNK, (c + 1) * KEY_CHUNK)
        for d, (qg, k_chunk, bias_chunk, s_ref, _) in enumerate(prod):
            st = _dot_nt(k_chunk(c), qg)
            bias = None if bias_chunk is None else bias_chunk(c)
            if bias is not None:
                st = st + bias
            s_ref[rows, :] = st
            mc = jnp.max(st, axis=0, keepdims=True)
            m_new[d] = mc if m_new[d] is None else jnp.maximum(m_new[d], mc)
        for d, (vt_chunk, s_ref, _) in enumerate(cons):
            p = jnp.exp2(s_ref[rows, :] - m_cons[d]).astype(BF16)
            part = _bdot(jnp.concatenate([vt_chunk(c), ones], axis=0), p)
            acc[d] = part if acc[d] is None else acc[d] + part
    for d, (_, _, _, _, m_ref) in enumerate(prod):
        m_ref[0:1, :] = m_new[d]
    return [a[:dv] * (1.0 / a[dv:dv + 1]) for a in acc]


def _pipe_run(step, scratch, n_dots):
    i = pl.program_id(0)
    s = [scratch[slot * n_dots:(slot + 1) * n_dots] for slot in range(2)]
    m = [scratch[(2 + slot) * n_dots:(3 + slot) * n_dots] for slot in range(2)]

    @pl.when(i == 0)
    def _():
        for ref in s[1] + m[1]:
            ref[...] = jnp.zeros(ref.shape, F32)

    pl.when(i % 2 == 0)(functools.partial(step, 0, 1, s, m))
    pl.when(i % 2 == 1)(functools.partial(step, 1, 0, s, m))


def _n_chains(ag):
    return (BATCH // ag) * NT_LAT


def _prod_chain(i, ag=1):
    c = jnp.minimum(i, _n_chains(ag) - 1)
    return c // NT_LAT, c % NT_LAT, 0


def _cons_chain(i, ag=1):
    c = jnp.maximum(i - 1, 0)
    return c // NT_LAT, c % NT_LAT, 0


def _pipe_scratch(nk, m_len, n_dots):
    return ([pltpu.VMEM((nk, m_len), F32)] * (2 * n_dots)
            + [pltpu.VMEM((8, m_len), F32)] * (2 * n_dots))


def _gqa_body(q_ref, k_ref, vt_ref, o_ref, *scratch):
    def run(sp, sc, s, m):
        q = q_ref[...]
        prod, cons = [], []
        for g in range(GQA_KV_HEADS):
            qg = jnp.concatenate([q[:, (g * GQA_GRP + j) * HEAD_DIM:(g * GQA_GRP + j + 1) * HEAD_DIM]
                                  for j in range(GQA_GRP)], axis=0)

            def k_chunk(c, g=g):
                return k_ref[c * KEY_CHUNK:(c + 1) * KEY_CHUNK, g * HEAD_DIM:(g + 1) * HEAD_DIM]

            def vt_chunk(c, g=g):
                return vt_ref[g * HEAD_DIM:(g + 1) * HEAD_DIM, c * KEY_CHUNK:(c + 1) * KEY_CHUNK]

            prod.append((qg, k_chunk, None, s[sp][g], m[sp][g]))
            cons.append((vt_chunk, s[sc][g], m[sc][g]))
        outs = _pipe_step(prod, cons, TOK // KEY_CHUNK, HEAD_DIM)
        o = jnp.concatenate([ot[:, j * TM:(j + 1) * TM] for ot in outs for j in range(GQA_GRP)], axis=0)
        o_ref[...] = o.astype(BF16)

    _pipe_run(run, scratch, GQA_KV_HEADS)


def _full_key_attn(body, q, k, vt, *, w_out, n_dots, m_len, name):
    return pl.pallas_call(
        body,
        grid=(_n_chains(1) + 1,),
        in_specs=[
            pl.BlockSpec((None, TM, q.shape[-1]), _prod_chain),
            pl.BlockSpec((None, TOK, k.shape[-1]), lambda i: (_prod_chain(i)[0], 0, 0)),
            pl.BlockSpec((None, vt.shape[1], TOK), lambda i: (_cons_chain(i)[0], 0, 0)),
        ],
        out_specs=pl.BlockSpec((None, w_out, TM), lambda i: (_cons_chain(i)[0], 0, _cons_chain(i)[1])),
        out_shape=jax.ShapeDtypeStruct((BATCH, w_out, SEQ), BF16),
        scratch_shapes=_pipe_scratch(TOK, m_len, n_dots),
        compiler_params=_params(1),
        name=name,
    )(q, k, vt)


def _gqa_attn(q, k, vt):
    return _full_key_attn(_gqa_body, q, k, vt, w_out=GQA_Q_WIDTH, n_dots=GQA_KV_HEADS, m_len=GQA_GRP * TM,
                          name="gqa_attn")


def _mla_body(q_ref, k_ref, vt_ref, o_ref, *scratch):
    def run(sp, sc, s, m):
        prod, cons = [], []
        for d in range(MLA_HEADS):
            def k_chunk(c, d=d):
                return k_ref[c * KEY_CHUNK:(c + 1) * KEY_CHUNK, d * MLA_PAD:(d + 1) * MLA_PAD]

            def vt_chunk(c, d=d):
                return vt_ref[d * MLA_V:(d + 1) * MLA_V, c * KEY_CHUNK:(c + 1) * KEY_CHUNK]

            prod.append((q_ref[:, d * MLA_PAD:(d + 1) * MLA_PAD], k_chunk, None, s[sp][d], m[sp][d]))
            cons.append((vt_chunk, s[sc][d], m[sc][d]))
        outs = _pipe_step(prod, cons, TOK // KEY_CHUNK, MLA_V)
        o_ref[...] = jnp.concatenate(outs, axis=0).astype(BF16)

    _pipe_run(run, scratch, MLA_HEADS)


def _mla_attn(q, k, vt):
    return _full_key_attn(_mla_body, q, k, vt, w_out=MLA_V_WIDTH, n_dots=MLA_HEADS, m_len=TM, name="mla_attn")


def _na_start_block(t):
    return jnp.clip(t - 1, 0, NT_LAT - NA_KBLK)


def _na_body(q_ref, k0_ref, k1_ref, k2_ref, kc_ref, v0_ref, v1_ref, v2_ref, vc_ref, bias_ref, o_ref, *scratch):
    k_blocks = (k0_ref, k1_ref, k2_ref, kc_ref)
    v_blocks = (v0_ref, v1_ref, v2_ref, vc_ref)

    def run(sp, sc, s, m):
        prod, cons = [], []
        for a in range(NA_GROUP):
            for d in range(NA_HEADS):
                def k_chunk(c, a=a, d=d):
                    return k_blocks[c][a, :, d * HEAD_DIM:(d + 1) * HEAD_DIM]

                def bias_chunk(c, d=d):
                    return None if c == NA_KBLK else bias_ref[d, c * KEY_CHUNK:(c + 1) * KEY_CHUNK, :]

                def vt_chunk(c, a=a, d=d):
                    return v_blocks[c][a, d * HEAD_DIM:(d + 1) * HEAD_DIM, :]

                n = a * NA_HEADS + d
                prod.append((q_ref[a, :, d * HEAD_DIM:(d + 1) * HEAD_DIM], k_chunk, bias_chunk, s[sp][n], m[sp][n]))
                cons.append((vt_chunk, s[sc][n], m[sc][n]))
        outs = _pipe_step(prod, cons, NA_NKEY // KEY_CHUNK, HEAD_DIM)
        for a in range(NA_GROUP):
            o_ref[a] = jnp.concatenate(outs[a * NA_HEADS:(a + 1) * NA_HEADS], axis=0).astype(BF16)

    _pipe_run(run, scratch, NA_GROUP * NA_HEADS)


def _na_attn(q, k, vt, bias, layer):
    def prod(i):
        return _prod_chain(i, NA_GROUP)

    def cons(i):
        return _cons_chain(i, NA_GROUP)

    def k_spec(j):
        def index(i):
            b, t, _ = prod(i)
            return b, (NT_LAT if j == NA_KBLK else _na_start_block(t) + j), 0
        return pl.BlockSpec((NA_GROUP, TM, NA_WIDTH), index)

    def vt_spec(j):
        def index(i):
            b, t, _ = cons(i)
            return b, 0, (NT_LAT if j == NA_KBLK else _na_start_block(t) + j)
        return pl.BlockSpec((NA_GROUP, NA_WIDTH, TM), index)

    def bias_index(i):
        t = prod(i)[1]
        return layer, 0, jnp.where(t == 0, 0, jnp.where(t == NT_LAT - 1, 2, 1)), 0, 0

    return pl.pallas_call(
        _na_body,
        grid=(_n_chains(NA_GROUP) + 1,),
        in_specs=[pl.BlockSpec((NA_GROUP, TM, NA_WIDTH), prod)]
                 + [k_spec(j) for j in range(NA_KBLK + 1)] + [vt_spec(j) for j in range(NA_KBLK + 1)]
                 + [pl.BlockSpec((None, NA_HEADS, None, NA_NLOCAL, TM), bias_index)],
        out_specs=pl.BlockSpec((NA_GROUP, NA_WIDTH, TM), lambda i: (cons(i)[0], 0, cons(i)[1])),
        out_shape=jax.ShapeDtypeStruct((BATCH, NA_WIDTH, SEQ), BF16),
        scratch_shapes=_pipe_scratch(NA_NKEY, TM, NA_GROUP * NA_HEADS),
        compiler_params=_params(1),
        name="na_attn",
    )(q, *([k] * (NA_KBLK + 1)), *([vt] * (NA_KBLK + 1)), bias)


def _na_bias_table(rel_bias):
    lead = rel_bias.shape[:-2]
    rel_bias = rel_bias.reshape((-1,) + rel_bias.shape[-2:])
    nh = rel_bias.shape[0]
    cols = np.arange(GRID_W)
    c0 = np.clip(cols - NA_COLS // 2, 0, GRID_W - NA_COLS)
    col_in = (cols[None, :] >= c0[:, None]) & (cols[None, :] < c0[:, None] + NA_COLS)
    n_dr = 2 * NA_ROWS - 1
    pad = GRID_W - NA_COLS
    rbp = jnp.pad(rel_bias.astype(F32) * LOG2_E, ((0, 0), (0, 0), (pad, pad)), mode="edge")
    tiles = jnp.stack([rbp[:, :, GRID_W - 1 - cq:2 * GRID_W - 1 - cq] for cq in range(GRID_W)], axis=3)
    tiles = jnp.where(col_in.T[None, None], tiles, NEG_BIG)
    tiles = jnp.concatenate([tiles, jnp.full((nh, 1, GRID_W, GRID_W), NEG_BIG, F32)], axis=1)
    dr_blk = np.full((3, NA_QROWS, NA_KROWS), n_dr, np.int32)
    for cls, tile in enumerate((0, 2, NT_LAT - 1)):
        start = int(np.clip(tile - 1, 0, NT_LAT - NA_KBLK)) * NA_QROWS
        for qr in range(NA_QROWS):
            r = tile * NA_QROWS + qr
            r0 = int(np.clip(r - NA_ROWS // 2, 0, ROWS - NA_ROWS))
            for kj in range(NA_KROWS):
                kr = start + kj
                if r0 <= kr < r0 + NA_ROWS:
                    dr_blk[cls, qr, kj] = kr - r + NA_ROWS - 1
    local = jnp.take(tiles, jnp.asarray(dr_blk.reshape(-1)), axis=1)
    local = local.reshape(nh, 3, NA_QROWS, NA_KROWS, GRID_W, GRID_W).transpose(0, 1, 3, 4, 2, 5)
    return local.reshape(lead + (3, NA_NLOCAL, TM))


def _merge_body(h_ref, mod_ref, g_ref, wgate_ref, wa_ref, wb_ref, wm_ref, wo_ref, *rest, split):
    o_ref = rest[-1]
    for r in range(BGROUP):
        branch, pos = [], 0
        for has_ctx in split:
            if has_ctx:
                is_ctx = pl.program_id(0) >= NT_LAT
                branch.append(jnp.where(is_ctx, rest[pos + 1][r], rest[pos][r]))
            else:
                branch.append(rest[pos][r])
            pos += 2 if has_ctx else 1
        h = h_ref[r]
        shift, scale, gate = _mod_vectors(mod_ref, r)
        n = _rmsnorm(h, g_ref[...]) * (1.0 + scale) + shift
        gates =_sigmoid(_bdot(n.astype(BF16), wgate_ref[...]))
        y = (gates[:, :D_MODEL] * _dot_tn(branch[0], wa_ref[...])
             + gates[:, D_MODEL:2 * D_MODEL] * _dot_tn(branch[1], wb_ref[...])
             + gates[:, 2 * D_MODEL:] * _dot_tn(branch[2], wm_ref[...]))
        o_ref[r] = h + gate * _bdot(y.astype(BF16), wo_ref[...])


def _merge(h, mod, layer, g_norm, w_gate, branches, wa, wb, wm, wo, *, n_tiles):
    split = tuple(isinstance(br, tuple) for br in branches)
    arrays, specs = [], []
    for br in branches:
        if isinstance(br, tuple):
            lat, ctx = br
            arrays += [lat, ctx]
            specs += [pl.BlockSpec((BGROUP, lat.shape[1], TM), lambda t, b: (b, 0, jnp.minimum(t, NT_LAT - 1))),
                      pl.BlockSpec((BGROUP, ctx.shape[1], TM), lambda t, b: (b, 0, 0))]
        else:
            arrays.append(br)
            specs.append(pl.BlockSpec((BGROUP, br.shape[1], TM), lambda t, b: (b, 0, t)))
    return pl.pallas_call(
        functools.partial(_merge_body, split=split),
        grid=(n_tiles, BATCH // BGROUP),
        in_specs=[_tok_spec(D_MODEL), _mod_spec(layer, 1)]
                 + [_layer_spec(a, layer) for a in (g_norm, w_gate, wa, wb, wm, wo)] + specs,
        out_specs=_tok_spec(D_MODEL),
        out_shape=jax.ShapeDtypeStruct((BATCH, n_tiles * TM, D_MODEL), F32),
        compiler_params=_params(2),
        name="gated_merge",
    )(h, mod, g_norm, w_gate, wa, wb, wm, wo, *arrays)


def _rope_tables():
    t = np.arange(SEQ)
    row = (t // GRID_W).astype(np.float32)[:, None]
    col = (t % GRID_W).astype(np.float32)[:, None]

    def head_tables(d_rot):
        half = d_rot // 2
        freqs = np.float32(ROPE_THETA) ** (-np.arange(0, half, 2, dtype=np.float32) / np.float32(half))
        r, c = row * freqs, col * freqs
        ang = np.concatenate([r, r, c, c], axis=-1).astype(np.float32)
        first = (np.arange(d_rot) % half) < (half // 2)
        cos, sin = np.cos(ang), np.sin(ang)
        return cos, np.where(first, -sin, sin)

    def place(tabs, reps, lo, group):
        out = []
        for k, tab in enumerate(tabs):
            fill = 1.0 if k == 0 else 0.0
            d_rot = tab.shape[-1]
            g = np.concatenate([np.full((SEQ, lo), fill, np.float32), tab,
                                np.full((SEQ, group - lo - d_rot), fill, np.float32)], axis=-1)
            g = np.tile(g, (1, reps))
            ctx = np.full((CTX_LEN, reps * group), fill, np.float32)
            out.append(jnp.asarray(np.concatenate([g, ctx], axis=0), F32))
        return out

    tb = head_tables(HEAD_DIM)
    tm = head_tables(MLA_ROPE)
    return (place(tb, GQA_HEADS, 0, HEAD_DIM)
            + place(tm, MLA_HEADS, MLA_NOPE, MLA_PAD)
            + place(tm, 1, 0, LANES))


def _seg_matrix(width):
    idx = np.arange(width) // HEAD_DIM
    return jnp.asarray((idx[:, None] == idx[None, :]).astype(np.float32) / HEAD_DIM, BF16)


def _place_matrix():
    m = np.zeros((LANES, MLA_QK_WIDTH), np.float32)
    for hd in range(MLA_HEADS):
        for j in range(MLA_ROPE):
            m[j, hd * MLA_PAD + MLA_NOPE + j] = 1.0
    return jnp.asarray(m, BF16)


def _pad_heads(w, n_heads, per_head, keep_lo, keep_hi):
    d, k = w.shape[:2]
    wh = w.reshape(d, k, n_heads, per_head)[..., keep_lo:keep_hi]
    wh = jnp.pad(wh, ((0, 0), (0, 0), (0, 0), (0, MLA_PAD - (keep_hi - keep_lo))))
    return wh.reshape(d, k, n_heads * MLA_PAD)


def kernel(x, c, ctx, c_ctx, w_ada, b_ada, ffn1_norm, ffn1_w_gate, ffn1_w_up, ffn1_w_down, mix_norm, w_in,
           na_rel_bias, gqa_q_norm, gqa_k_norm, mla_q_norm, mla_kv_norm, mla_w_uq, mla_w_ukv,
           w_branch_a, w_branch_b, w_branch_c, w_out, ffn2_norm, ffn2_w_gate, ffn2_w_up, ffn2_w_down,
           final_norm):
    assert x.shape == (BATCH, SEQ, D_MODEL) and ctx.shape == (BATCH, CTX_LEN, D_MODEL)

    c_all = jnp.concatenate([c, jnp.tile(c_ctx[None, :], (BGROUP, 1)),
                             jnp.zeros((MOD_ROWS - BATCH - BGROUP, D_MODEL), F32)], axis=0)
    mod_all = _ada(c_all, w_ada, b_ada)

    tables = _rope_tables()
    seg_q, seg_k, place = _seg_matrix(2 * LANES), _seg_matrix(GQA_KV_WIDTH), _place_matrix()
    fin = final_norm.reshape(1, D_MODEL)

    def row(v):
        return v[:, None, :]

    w_qkv = w_in[:, :, :QKV_WIDTH].astype(BF16)
    w_gate = w_in[:, :, N_QKV_COLS:].astype(BF16)
    w_uq = _pad_heads(mla_w_uq, MLA_HEADS, MLA_NOPE + MLA_ROPE, 0, MLA_NOPE + MLA_ROPE).astype(BF16)
    w_uk = _pad_heads(mla_w_ukv, MLA_HEADS, MLA_NOPE + MLA_V, 0, MLA_NOPE)
    w_uv = mla_w_ukv.reshape(DEPTH, MLA_KV_RANK, MLA_HEADS, MLA_NOPE + MLA_V)[..., MLA_NOPE:]
    w_ukv = jnp.concatenate([w_uk, w_uv.reshape(DEPTH, MLA_KV_RANK, MLA_V_WIDTH)], axis=-1).astype(BF16)
    consts = (seg_q, seg_k, row(jnp.tile(gqa_q_norm, (1, GQA_HEADS))), row(jnp.tile(gqa_k_norm, (1, GQA_KV_HEADS))),
              row(mla_q_norm), row(mla_kv_norm), w_uq, w_ukv, place)
    ffn1 = (row(ffn1_norm), ffn1_w_gate.astype(BF16), ffn1_w_up.astype(BF16), ffn1_w_down.astype(BF16))
    ffn2 = (row(ffn2_norm), ffn2_w_gate.astype(BF16), ffn2_w_up.astype(BF16), ffn2_w_down.astype(BF16))
    branch_w = (w_branch_a.astype(BF16), w_branch_b.astype(BF16), w_branch_c.astype(BF16), w_out.astype(BF16))
    na_bias = _na_bias_table(na_rel_bias)
    g_mix = row(mix_norm)

    h = (x, ctx)
    for i in range(DEPTH):
        last = i == DEPTH - 1
        h = _ffn(h, mod_all, i, 0, *ffn1, fin, n_tiles=NT_ALL, final=False)
        qa, ka, va, qb, kb, vb, qm, km, vm = _proj(h, mod_all, i, g_mix, w_qkv, consts, tables)

        oa = _na_attn(qa, ka, va, na_bias, i)
        ob = _gqa_attn(qb, kb, vb)
        om = _mla_attn(qm, km, vm)
        if not last:
            oa_c, ob_c, om_c = _ctx_attn(((qa, ka, va), (qb, kb, vb), (qm, km, vm)))
            oa, ob, om = (oa, oa_c), (ob, ob_c), (om, om_c)

        n_tiles = NT_LAT if last else NT_ALL
        h = _merge(h, mod_all, i, g_mix, w_gate, (oa, ob, om), *branch_w, n_tiles=n_tiles)
        h = _ffn(h, mod_all, i, 2, *ffn2, fin, n_tiles=n_tiles, final=last)
    return h
```
